```python
import jax, jax.numpy as jnp
from jax import lax
import numpy as np

D_MODEL = 1024
BATCH = 8
SEQ = 4096
DEPTH = 2

EXPAND = 2
D_INNER = EXPAND * D_MODEL
D_POOL = D_INNER // 2
D_SB = D_INNER - D_POOL
POOL_WINDOWS = (2, 4, 8, 16)
N_POOL_GROUPS = len(POOL_WINDOWS)
POOL_GROUP = D_POOL // N_POOL_GROUPS
SB_HEAD_DIM = 64
SB_HEADS = D_SB // SB_HEAD_DIM
SB_BLOCK = 128
CONV_WIDTH = 3
D_CONV = D_INNER
D_IN_EVEN = D_POOL + 3 * D_SB + D_INNER
D_IN_ODD = 3 * D_CONV + D_INNER
N_EVEN = (DEPTH + 1) // 2
N_ODD = DEPTH // 2
EPS = 1e-6

kernel_name = "hybrid_pool_stickbreak_shortconv_adaln"


def rmsnorm(x, g):
    xf = x.astype(jnp.float32)
    y = xf * lax.rsqrt(jnp.mean(xf * xf, axis=-1, keepdims=True) + EPS)
    return (y * g.astype(jnp.float32)).astype(x.dtype)


def adaln_params(c, w, b):
    m = jax.nn.silu(c) @ w + b
    shift, scale, gate = jnp.split(m, 3, axis=-1)
    return shift[:, None, :], scale[:, None, :], gate[:, None, :]


def pool_mixer(u, w_grp, scale):
    bsz, s, _ = u.shape
    uf = u.astype(jnp.float32)
    cs = jnp.cumsum(uf, axis=1)
    pos = jnp.arange(s, dtype=jnp.float32)
    outs = []
    for gi, w in enumerate(POOL_WINDOWS):
        sl = slice(gi * POOL_GROUP, (gi + 1) * POOL_GROUP)
        cg = cs[..., sl]
        prev = jnp.pad(cg, ((0, 0), (w, 0), (0, 0)))[:, :s]
        count = jnp.minimum(pos + 1.0, float(w))[None, :, None]
        outs.append((cg - prev) / count - uf[..., sl])
    p = jnp.stack(outs, axis=2).astype(u.dtype)
    y = jnp.einsum('bsgc,gcd->bsgd', p, w_grp).reshape(bsz, s, D_POOL)
    return y * scale


def stick_breaking_attention(q, k, v):
    bsz, s, _, _ = q.shape
    qh, kh, vh = (jnp.swapaxes(t, 1, 2) for t in (q, k, v))
    inv_sqrt = 1.0 / np.sqrt(SB_HEAD_DIM).astype(np.float32)
    outs = []
    for blk in range(s // SB_BLOCK):
        q0 = blk * SB_BLOCK
        end = q0 + SB_BLOCK
        qb = qh[:, :, q0:end]
        kb = kh[:, :, :end]
        vb = vh[:, :, :end]
        z = jnp.einsum('bhqd,bhkd->bhqk', qb, kb).astype(jnp.float32) * inv_sqrt
        q_pos = jnp.arange(q0, end)[:, None]
        k_pos = jnp.arange(end)[None, :]
        causal = k_pos < q_pos
        log_fail = jnp.where(causal, -jax.nn.softplus(z), 0.0)
        after = lax.cumsum(log_fail, axis=3, reverse=True) - log_fail
        a = jnp.where(causal, jnp.exp(jax.nn.log_sigmoid(z) + after), 0.0)
        outs.append(jnp.einsum('bhqk,bhkd->bhqd', a.astype(vb.dtype), vb))
    o = jnp.concatenate(outs, axis=2)
    return jnp.swapaxes(o, 1, 2).reshape(bsz, s, SB_HEADS * SB_HEAD_DIM)


def even_mixer(h, w_in, pool_w, pool_scale, w_out):
    bsz, s, _ = h.shape
    proj = h @ w_in
    u_pool, q, k, v, gate = jnp.split(
        proj, [D_POOL, D_POOL + D_SB, D_POOL + 2 * D_SB, D_POOL + 3 * D_SB], axis=-1)
    y_pool = pool_mixer(u_pool, pool_w, pool_scale)
    shp = (bsz, s, SB_HEADS, SB_HEAD_DIM)
    y_sb = stick_breaking_attention(q.reshape(shp), k.reshape(shp), v.reshape(shp))
    y = jnp.concatenate([y_pool, y_sb], axis=-1) * jax.nn.silu(gate)
    return y @ w_out


def odd_mixer(h, w_in, conv_w, conv_b, w_out):
    s = h.shape[1]
    proj = h @ w_in
    gb, gc, u, gate = jnp.split(proj, [D_CONV, 2 * D_CONV, 3 * D_CONV], axis=-1)
    u = gc * u
    up = jnp.pad(u, ((0, 0), (CONV_WIDTH - 1, 0), (0, 0)))
    conv = conv_b + sum(up[:, j:j + s] * conv_w[j] for j in range(CONV_WIDTH))
    y = gb * conv * jax.nn.silu(gate)
    return y @ w_out


def _fwd_setup_inputs(seed: int = 0) -> dict:
    key = jax.random.key(seed)
    ks = jax.random.split(key, 16)
    f32 = jnp.float32
    D = D_MODEL
    nrm = lambda k, shp, sc: jax.random.normal(k, shp, f32) * sc
    return {
        "x": nrm(ks[0], (BATCH, SEQ, D), 1.0),
        "c": nrm(ks[1], (BATCH, D), 1.0),
        "norm_g": 1.0 + nrm(ks[2], (DEPTH, D), 0.02),
        "ada_w": nrm(ks[3], (DEPTH, D, 3 * D), 0.1 * D ** -0.5),
        "ada_b": nrm(ks[4], (DEPTH, 3 * D), 0.01),
        "even_w_in": nrm(ks[5], (N_EVEN, D, D_IN_EVEN), D ** -0.5),
        "pool_w": nrm(ks[6], (N_EVEN, N_POOL_GROUPS, POOL_GROUP, POOL_GROUP), POOL_GROUP ** -0.5),
        "pool_scale": 1.0 + nrm(ks[7], (N_EVEN, D_POOL), 0.02),
        "even_w_out": nrm(ks[8], (N_EVEN, D_INNER, D), D_INNER ** -0.5),
        "odd_w_in": nrm(ks[9], (N_ODD, D, D_IN_ODD), D ** -0.5),
        "conv_w": nrm(ks[10], (N_ODD, CONV_WIDTH, D_CONV), CONV_WIDTH ** -0.5),
        "conv_b": nrm(ks[11], (N_ODD, D_CONV), 0.01),
        "odd_w_out": nrm(ks[12], (N_ODD, D_INNER, D), D_INNER ** -0.5),
        "final_g": 1.0 + nrm(ks[13], (D,), 0.02),
    }


def _fwd_reference(x, c, norm_g, ada_w, ada_b, even_w_in, pool_w, pool_scale, even_w_out,
              odd_w_in, conv_w, conv_b, odd_w_out, final_g):
    for i in range(DEPTH):
        shift, scale, gate = adaln_params(c, ada_w[i], ada_b[i])
        h = rmsnorm(x, norm_g[i]) * (1.0 + scale) + shift
        j = i // 2
        if i % 2 == 0:
            y = even_mixer(h, even_w_in[j], pool_w[j], pool_scale[j], even_w_out[j])
        else:
            y = odd_mixer(h, odd_w_in[j], conv_w[j], conv_b[j], odd_w_out[j])
        x = x + ((1.0 + gate) * y).astype(x.dtype)
    return rmsnorm(x, final_g)


import jax as _jax
import jax.numpy as _jnp

TWIN_FORMAT = 'train_step'
FWD_PARAMS = ['x', 'c', 'norm_g', 'ada_w', 'ada_b', 'even_w_in', 'pool_w', 'pool_scale', 'even_w_out', 'odd_w_in', 'conv_w', 'conv_b', 'odd_w_out', 'final_g']
TWIN_WEIGHTS = ['norm_g', 'ada_w', 'ada_b', 'even_w_in', 'pool_w', 'pool_scale', 'even_w_out', 'odd_w_in', 'conv_w', 'conv_b', 'odd_w_out', 'final_g']
TWIN_DIFF_INPUT = 'x'
TWIN_INPUTS = ['x', 'c', 'norm_g', 'ada_w', 'ada_b', 'even_w_in', 'pool_w', 'pool_scale', 'even_w_out', 'odd_w_in', 'conv_w', 'conv_b', 'odd_w_out', 'final_g', 'loss_target', 'm_norm_g', 'm_ada_w', 'm_ada_b', 'm_even_w_in', 'm_pool_w', 'm_pool_scale', 'm_even_w_out', 'm_odd_w_in', 'm_conv_w', 'm_conv_b', 'm_odd_w_out', 'm_final_g', 'v_norm_g', 'v_ada_w', 'v_ada_b', 'v_even_w_in', 'v_pool_w', 'v_pool_scale', 'v_even_w_out', 'v_odd_w_in', 'v_conv_w', 'v_conv_b', 'v_odd_w_out', 'v_final_g']
TWIN_OUTPUTS = ['loss', 'grad_x', 'grad_norm_g', 'grad_ada_w', 'grad_ada_b', 'grad_even_w_in', 'grad_pool_w', 'grad_pool_scale', 'grad_even_w_out', 'grad_odd_w_in', 'grad_conv_w', 'grad_conv_b', 'grad_odd_w_out', 'grad_final_g', 'delta_norm_g', 'delta_ada_w', 'delta_ada_b', 'delta_even_w_in', 'delta_pool_w', 'delta_pool_scale', 'delta_even_w_out', 'delta_odd_w_in', 'delta_conv_w', 'delta_conv_b', 'delta_odd_w_out', 'delta_final_g', 'new_m_norm_g', 'new_m_ada_w', 'new_m_ada_b', 'new_m_even_w_in', 'new_m_pool_w', 'new_m_pool_scale', 'new_m_even_w_out', 'new_m_odd_w_in', 'new_m_conv_w', 'new_m_conv_b', 'new_m_odd_w_out', 'new_m_final_g', 'new_v_norm_g', 'new_v_ada_w', 'new_v_ada_b', 'new_v_even_w_in', 'new_v_pool_w', 'new_v_pool_scale', 'new_v_even_w_out', 'new_v_odd_w_in', 'new_v_conv_w', 'new_v_conv_b', 'new_v_odd_w_out', 'new_v_final_g']
TWIN_LEAF_KINDS = {'loss': 'loss', 'grad_x': 'grad_x', 'grad_norm_g': 'grad_w', 'grad_ada_w': 'grad_w', 'grad_ada_b': 'grad_w', 'grad_even_w_in': 'grad_w', 'grad_pool_w': 'grad_w', 'grad_pool_scale': 'grad_w', 'grad_even_w_out': 'grad_w', 'grad_odd_w_in': 'grad_w', 'grad_conv_w': 'grad_w', 'grad_conv_b': 'grad_w', 'grad_odd_w_out': 'grad_w', 'grad_final_g': 'grad_w', 'delta_norm_g': 'delta_w', 'delta_ada_w': 'delta_w', 'delta_ada_b': 'delta_w', 'delta_even_w_in': 'delta_w', 'delta_pool_w': 'delta_w', 'delta_pool_scale': 'delta_w', 'delta_even_w_out': 'delta_w', 'delta_odd_w_in': 'delta_w', 'delta_conv_w': 'delta_w', 'delta_conv_b': 'delta_w', 'delta_odd_w_out': 'delta_w', 'delta_final_g': 'delta_w', 'new_m_norm_g': 'new_m', 'new_m_ada_w': 'new_m', 'new_m_ada_b': 'new_m', 'new_m_even_w_in': 'new_m', 'new_m_pool_w': 'new_m', 'new_m_pool_scale': 'new_m', 'new_m_even_w_out': 'new_m', 'new_m_odd_w_in': 'new_m', 'new_m_conv_w': 'new_m', 'new_m_conv_b': 'new_m', 'new_m_odd_w_out': 'new_m', 'new_m_final_g': 'new_m', 'new_v_norm_g': 'new_v', 'new_v_ada_w': 'new_v', 'new_v_ada_b': 'new_v', 'new_v_even_w_in': 'new_v', 'new_v_pool_w': 'new_v', 'new_v_pool_scale': 'new_v', 'new_v_even_w_out': 'new_v', 'new_v_odd_w_in': 'new_v', 'new_v_conv_w': 'new_v', 'new_v_conv_b': 'new_v', 'new_v_odd_w_out': 'new_v', 'new_v_final_g': 'new_v'}


def _forward(args):
    return _fwd_reference(*[args[k] for k in FWD_PARAMS])


def _output_shape():
    out = _jax.eval_shape(lambda: _forward(_fwd_setup_inputs(0)))
    return out.shape, out.dtype

N_MICROBATCH = 1
ADAM_LR = 0.001
ADAM_B1 = 0.9
ADAM_B2 = 0.999
ADAM_EPS = 1e-08
ADAM_WD = 0.01
ADAM_STEP = 10
PER_EXAMPLE_BATCH_AXIS = {'x': 0, 'c': 0, 'loss_target': 0}
SHARED_INPUTS = []
_WEIGHT_DTYPES = {'norm_g': _jnp.float32, 'ada_w': _jnp.float32, 'ada_b': _jnp.float32, 'even_w_in': _jnp.float32, 'pool_w': _jnp.float32, 'pool_scale': _jnp.float32, 'even_w_out': _jnp.float32, 'odd_w_in': _jnp.float32, 'conv_w': _jnp.float32, 'conv_b': _jnp.float32, 'odd_w_out': _jnp.float32, 'final_g': _jnp.float32}
MOMENT_SCALE = {'norm_g': 1.611326e-01, 'ada_w': 9.091169e-02, 'ada_b': 1.596370e-01, 'even_w_in': 5.909116e-02, 'pool_w': 7.932609e-02, 'pool_scale': 7.621019e-02, 'even_w_out': 9.763946e-02, 'odd_w_in': 6.073915e-02, 'conv_w': 6.051559e-02, 'conv_b': 6.085640e-02, 'odd_w_out': 8.490298e-02, 'final_g': 3.198509e+01}


def _to_microbatches(a, axis):
    t = _jnp.moveaxis(a, axis, 0)
    t = t.reshape((N_MICROBATCH, t.shape[0] // N_MICROBATCH) + t.shape[1:])
    return _jnp.moveaxis(t, 1, axis + 1)


def setup_inputs(seed: int = 0) -> dict:
    inp = _fwd_setup_inputs(seed)
    key = _jax.random.fold_in(_jax.random.key(seed), 7919)
    shape, _ = _output_shape()
    out = dict(inp)
    out["loss_target"] = _jax.random.normal(_jax.random.fold_in(key, 0), shape, _jnp.float32)
    for i, name in enumerate(TWIN_WEIGHTS):
        w = inp[name].astype(_jnp.float32)
        if MOMENT_SCALE is None:
            s = _jnp.sqrt(_jnp.mean(_jnp.square(w)) + 1e-30)
        else:
            s = MOMENT_SCALE[name]
        km, kv = _jax.random.split(_jax.random.fold_in(key, i + 1))
        out[name] = w
        out["m_" + name] = s * _jax.random.normal(km, w.shape, _jnp.float32)
        out["v_" + name] = (s * s) * _jax.random.uniform(kv, w.shape, _jnp.float32, 0.5, 1.5)
    if N_MICROBATCH > 1:
        for name, axis in PER_EXAMPLE_BATCH_AXIS.items():
            out[name] = _to_microbatches(out[name], axis)
    return {'x': out['x'], 'c': out['c'], 'norm_g': out['norm_g'], 'ada_w': out['ada_w'], 'ada_b': out['ada_b'], 'even_w_in': out['even_w_in'], 'pool_w': out['pool_w'], 'pool_scale': out['pool_scale'], 'even_w_out': out['even_w_out'], 'odd_w_in': out['odd_w_in'], 'conv_w': out['conv_w'], 'conv_b': out['conv_b'], 'odd_w_out': out['odd_w_out'], 'final_g': out['final_g'], 'loss_target': out['loss_target'], 'm_norm_g': out['m_norm_g'], 'm_ada_w': out['m_ada_w'], 'm_ada_b': out['m_ada_b'], 'm_even_w_in': out['m_even_w_in'], 'm_pool_w': out['m_pool_w'], 'm_pool_scale': out['m_pool_scale'], 'm_even_w_out': out['m_even_w_out'], 'm_odd_w_in': out['m_odd_w_in'], 'm_conv_w': out['m_conv_w'], 'm_conv_b': out['m_conv_b'], 'm_odd_w_out': out['m_odd_w_out'], 'm_final_g': out['m_final_g'], 'v_norm_g': out['v_norm_g'], 'v_ada_w': out['v_ada_w'], 'v_ada_b': out['v_ada_b'], 'v_even_w_in': out['v_even_w_in'], 'v_pool_w': out['v_pool_w'], 'v_pool_scale': out['v_pool_scale'], 'v_even_w_out': out['v_even_w_out'], 'v_odd_w_in': out['v_odd_w_in'], 'v_conv_w': out['v_conv_w'], 'v_conv_b': out['v_conv_b'], 'v_odd_w_out': out['v_odd_w_out'], 'v_final_g': out['v_final_g']}


def _loss(weights, diff, rest, loss_target):
    with _jax.named_scope("forward"):
        args = {**rest, TWIN_DIFF_INPUT: diff, **{k: w.astype(_WEIGHT_DTYPES[k]) for k, w in weights.items()}}
        y = _forward(args)
    with _jax.named_scope("loss_head"):
        err = _jnp.square(y.astype(_jnp.float32) - loss_target)
        return 0.5 * _jnp.sum(_jnp.mean(err, axis=-1)) if err.ndim else 0.5 * err


def _adamw(w, g, m, v):
    m = ADAM_B1 * m + (1.0 - ADAM_B1) * g
    v = ADAM_B2 * v + (1.0 - ADAM_B2) * _jnp.square(g)
    m_hat = m / (1.0 - ADAM_B1 ** ADAM_STEP)
    v_hat = v / (1.0 - ADAM_B2 ** ADAM_STEP)
    delta = -ADAM_LR * (m_hat / (_jnp.sqrt(v_hat) + ADAM_EPS) + ADAM_WD * w)
    return delta, m, v


def reference(x, c, norm_g, ada_w, ada_b, even_w_in, pool_w, pool_scale, even_w_out, odd_w_in, conv_w, conv_b, odd_w_out, final_g, loss_target, m_norm_g, m_ada_w, m_ada_b, m_even_w_in, m_pool_w, m_pool_scale, m_even_w_out, m_odd_w_in, m_conv_w, m_conv_b, m_odd_w_out, m_final_g, v_norm_g, v_ada_w, v_ada_b, v_even_w_in, v_pool_w, v_pool_scale, v_even_w_out, v_odd_w_in, v_conv_w, v_conv_b, v_odd_w_out, v_final_g):
    given = dict(x=x, c=c, norm_g=norm_g, ada_w=ada_w, ada_b=ada_b, even_w_in=even_w_in, pool_w=pool_w, pool_scale=pool_scale, even_w_out=even_w_out, odd_w_in=odd_w_in, conv_w=conv_w, conv_b=conv_b, odd_w_out=odd_w_out, final_g=final_g, loss_target=loss_target, m_norm_g=m_norm_g, m_ada_w=m_ada_w, m_ada_b=m_ada_b, m_even_w_in=m_even_w_in, m_pool_w=m_pool_w, m_pool_scale=m_pool_scale, m_even_w_out=m_even_w_out, m_odd_w_in=m_odd_w_in, m_conv_w=m_conv_w, m_conv_b=m_conv_b, m_odd_w_out=m_odd_w_out, m_final_g=m_final_g, v_norm_g=v_norm_g, v_ada_w=v_ada_w, v_ada_b=v_ada_b, v_even_w_in=v_even_w_in, v_pool_w=v_pool_w, v_pool_scale=v_pool_scale, v_even_w_out=v_even_w_out, v_odd_w_in=v_odd_w_in, v_conv_w=v_conv_w, v_conv_b=v_conv_b, v_odd_w_out=v_odd_w_out, v_final_g=v_final_g)
    weights = {n: given[n] for n in TWIN_WEIGHTS}
    shared = {n: given[n] for n in SHARED_INPUTS}
    per_example = {n: given[n] for n in ['x', 'c']}
    grad_fn = _jax.value_and_grad(_loss, argnums=(0, 1))

    def one_microbatch(ex, loss_target):
        ex = dict(ex)
        diff = ex.pop(TWIN_DIFF_INPUT)
        return grad_fn(weights, diff, {**shared, **ex}, loss_target)

    if N_MICROBATCH == 1:
        loss, (grad_w, grad_x) = one_microbatch(per_example, given["loss_target"])
    else:
        def body(carry, xs):
            loss_sum, grad_sum = carry
            l_k, (gw_k, gx_k) = one_microbatch(xs[0], xs[1])
            with _jax.named_scope("update"):
                return (loss_sum + l_k, _jax.tree.map(_jnp.add, grad_sum, gw_k)), gx_k

        init = (_jnp.zeros((), _jnp.float32), _jax.tree.map(_jnp.zeros_like, weights))
        (loss, grad_w), grad_x = _jax.lax.scan(body, init, (per_example, given["loss_target"]))
    with _jax.named_scope("update"):
        delta_w, new_m, new_v = {}, {}, {}
        for n in TWIN_WEIGHTS:
            delta_w[n], new_m[n], new_v[n] = _adamw(weights[n], grad_w[n], given["m_" + n], given["v_" + n])
    return (loss, grad_x, *[grad_w[n] for n in TWIN_WEIGHTS], *[delta_w[n] for n in TWIN_WEIGHTS],
            *[new_m[n] for n in TWIN_WEIGHTS], *[new_v[n] for n in TWIN_WEIGHTS])
```

```python
import jax
import jax.numpy as jnp
from jax import lax
from jax.experimental import pallas as pl
from jax.experimental.pallas import tpu as pltpu

F32 = jnp.float32
BF16 = jnp.bfloat16
SDS = jax.ShapeDtypeStruct
MESH = pl.DeviceIdType.MESH

N_DEV = 8
D_MODEL = 1024
D_INNER = 2048
D_POOL = 1024
D_SB = 1024
N_GROUPS = 4
POOL_GROUP = 256
HEAD_DIM = 64
LANES = 128
D_IN_EVEN = 6144
D_IN_ODD = 8192
EPS = 1e-6
ADAM_LR = 0.001
ADAM_B1 = 0.9
ADAM_B2 = 0.999
ADAM_EPS = 1e-08
ADAM_WD = 0.01
ADAM_STEP = 10

ROW_TILE = 256
ATT_TILE = 256
HALO = 16
VMEM_LIMIT = 48 * 1024 * 1024
SLAB_ROWS = 32


def _params(n_axes):
    return pltpu.CompilerParams(dimension_semantics=("arbitrary",) * n_axes, vmem_limit_bytes=VMEM_LIMIT)


def _sigmoid(x):
    return 1.0 / (1.0 + jnp.exp(-x))


def _softplus(z):
    return jnp.maximum(z, 0.0) + jnp.log1p(jnp.exp(-jnp.abs(z)))


def _split_bf16(x):
    hi = x.astype(BF16)
    lo = (x - hi.astype(F32)).astype(BF16)
    return hi, lo


def _dot(a, b):
    return jnp.dot(a, b, preferred_element_type=F32)


def _dot_nt(a, b):
    return lax.dot_general(a, b, (((1,), (1,)), ((), ())), preferred_element_type=F32)


def _dot_tn(a, b):
    return lax.dot_general(a, b, (((0,), (0,)), ((), ())), preferred_element_type=F32)


def _mm(name, a, b, *, grid, a_spec, b_spec, o_spec, o_shape, o_dtype, dot, acc_axis=None, acc_shape=None):
    n_acc = grid[acc_axis] if acc_axis is not None else 1

    def body(a_ref, b_ref, o_ref, *scratch):
        prod = dot(a_ref[...], b_ref[...])
        if acc_axis is None:
            o_ref[...] = prod.astype(o_dtype)
        else:
            acc = scratch[0]
            k = pl.program_id(acc_axis)

            @pl.when(k == 0)
            def _():
                acc[...] = prod

            @pl.when(k > 0)
            def _():
                acc[...] += prod

            @pl.when(k == n_acc - 1)
            def _():
                o_ref[...] = acc[...].astype(o_dtype)

    scratch = [] if acc_axis is None else [pltpu.VMEM(acc_shape, F32)]
    return pl.pallas_call(
        body, name=name, grid=grid, in_specs=[a_spec, b_spec], out_specs=o_spec,
        out_shape=SDS(o_shape, o_dtype), scratch_shapes=scratch, compiler_params=_params(len(grid)),
    )(a, b)


def _proj_in(name, h, wg):
    s = h.shape[0]
    cn = wg.shape[2]
    tm = min(s, 512)
    return _mm(name, h, wg, grid=(s // tm, N_DEV),
               a_spec=pl.BlockSpec((tm, D_MODEL), lambda i, d: (i, 0)),
               b_spec=pl.BlockSpec((None, D_MODEL, cn), lambda i, d: (d, 0, 0)),
               o_spec=pl.BlockSpec((tm, cn), lambda i, d: (i, d)),
               o_shape=(s, N_DEV * cn), o_dtype=F32, dot=_dot)


def _proj_out(name, y, w):
    s = y.shape[0]
    tm = min(s, 512)
    return _mm(name, y, w, grid=(s // tm,),
               a_spec=pl.BlockSpec((tm, D_INNER), lambda i: (i, 0)),
               b_spec=pl.BlockSpec((D_INNER, D_MODEL), lambda i: (0, 0)),
               o_spec=pl.BlockSpec((tm, D_MODEL), lambda i: (i, 0)),
               o_shape=(s, D_MODEL), o_dtype=F32, dot=_dot)


def _proj_out_bwd(name, dyo, w):
    s = dyo.shape[0]
    tm = min(s, 512)
    return _mm(name, dyo, w, grid=(s // tm,),
               a_spec=pl.BlockSpec((tm, D_MODEL), lambda i: (i, 0)),
               b_spec=pl.BlockSpec((D_INNER, D_MODEL), lambda i: (0, 0)),
               o_spec=pl.BlockSpec((tm, D_INNER), lambda i: (i, 0)),
               o_shape=(s, D_INNER), o_dtype=F32, dot=_dot_nt)


def _wgrad_out(name, y, dyo):
    s = y.shape[0]
    ts = min(s, 512)
    return _mm(name, y, dyo, grid=(s // ts,),
               a_spec=pl.BlockSpec((ts, D_INNER), lambda k: (k, 0)),
               b_spec=pl.BlockSpec((ts, D_MODEL), lambda k: (k, 0)),
               o_spec=pl.BlockSpec((D_INNER, D_MODEL), lambda k: (0, 0)),
               o_shape=(D_INNER, D_MODEL), o_dtype=BF16, dot=_dot_tn, acc_axis=0, acc_shape=(D_INNER, D_MODEL))


def _proj_in_bwd(name, dproj, wg):
    s = dproj.shape[0]
    cn = wg.shape[2]
    tm = min(s, 512)
    return _mm(name, dproj, wg, grid=(s // tm, N_DEV),
               a_spec=pl.BlockSpec((tm, cn), lambda i, d: (i, d)),
               b_spec=pl.BlockSpec((None, D_MODEL, cn), lambda i, d: (d, 0, 0)),
               o_spec=pl.BlockSpec((tm, D_MODEL), lambda i, d: (i, 0)),
               o_shape=(s, D_MODEL), o_dtype=F32, dot=_dot_nt, acc_axis=1, acc_shape=(tm, D_MODEL))


def _wgrad_in(name, h, dproj):
    s = h.shape[0]
    cn = dproj.shape[1] // N_DEV
    ts = min(s, 512)
    return _mm(name, h, dproj, grid=(N_DEV, s // ts),
               a_spec=pl.BlockSpec((ts, D_MODEL), lambda d, k: (k, 0)),
               b_spec=pl.BlockSpec((ts, cn), lambda d, k: (k, d)),
               o_spec=pl.BlockSpec((None, D_MODEL, cn), lambda d, k: (d, 0, 0)),
               o_shape=(N_DEV, D_MODEL, cn), o_dtype=BF16, dot=_dot_tn, acc_axis=1, acc_shape=(D_MODEL, cn))


def _pool_mm(name, p, wp, dot):
    s = p.shape[0]
    tm = min(s, 512)
    return _mm(name, p, wp, grid=(s // tm, N_GROUPS),
               a_spec=pl.BlockSpec((tm, POOL_GROUP), lambda i, g: (i, g)),
               b_spec=pl.BlockSpec((None, POOL_GROUP, POOL_GROUP), lambda i, g: (g, 0, 0)),
               o_spec=pl.BlockSpec((tm, POOL_GROUP), lambda i, g: (i, g)),
               o_shape=(s, D_POOL), o_dtype=F32, dot=dot)


def _pool_wgrad(name, p, dyp):
    s = p.shape[0]
    ts = min(s, 512)
    return _mm(name, p, dyp, grid=(N_GROUPS, s // ts),
               a_spec=pl.BlockSpec((ts, POOL_GROUP), lambda g, k: (k, g)),
               b_spec=pl.BlockSpec((ts, POOL_GROUP), lambda g, k: (k, g)),
               o_spec=pl.BlockSpec((None, POOL_GROUP, POOL_GROUP), lambda g, k: (g, 0, 0)),
               o_shape=(N_GROUPS, POOL_GROUP, POOL_GROUP), o_dtype=F32, dot=_dot_tn, acc_axis=1,
               acc_shape=(POOL_GROUP, POOL_GROUP))


def _vec_spec():
    return pl.BlockSpec((1, D_MODEL), lambda i: (0, 0))


def _row_spec(width=D_MODEL, col=0):
    return pl.BlockSpec((ROW_TILE, width), lambda i: (i, col))


def _ln_mod(name, x, g, scale, shift):
    s = x.shape[0]

    def body(x_ref, g_ref, sc_ref, sh_ref, h_ref):
        xv = x_ref[...]
        r = lax.rsqrt(jnp.mean(xv * xv, axis=-1, keepdims=True) + EPS)
        n = (xv * r) * g_ref[...]
        h_ref[...] = (n * (1.0 + sc_ref[...]) + sh_ref[...]).astype(BF16)

    return pl.pallas_call(
        body, name=name, grid=(s // ROW_TILE,),
        in_specs=[_row_spec(), _vec_spec(), _vec_spec(), _vec_spec()], out_specs=_row_spec(),
        out_shape=SDS((s, D_MODEL), BF16), compiler_params=_params(1),
    )(x, g, scale, shift)


def _resid_ln_mod(name, x, yo, gate, g, scale, shift):
    s = x.shape[0]

    def body(x_ref, yo_ref, gt_ref, g_ref, sc_ref, sh_ref, xn_ref, h_ref):
        xv = x_ref[...] + (1.0 + gt_ref[...]) * yo_ref[...]
        xn_ref[...] = xv
        r = lax.rsqrt(jnp.mean(xv * xv, axis=-1, keepdims=True) + EPS)
        n = (xv * r) * g_ref[...]
        h_ref[...] = (n * (1.0 + sc_ref[...]) + sh_ref[...]).astype(BF16)

    return pl.pallas_call(
        body, name=name, grid=(s // ROW_TILE,),
        in_specs=[_row_spec(), _row_spec(), _vec_spec(), _vec_spec(), _vec_spec(), _vec_spec()],
        out_specs=[_row_spec(), _row_spec()],
        out_shape=[SDS((s, D_MODEL), F32), SDS((s, D_MODEL), BF16)], compiler_params=_params(1),
    )(x, yo, gate, g, scale, shift)


def _final_loss(name, x1, yo1, gate1, gf, target):
    s = x1.shape[0]

    def body(x_ref, yo_ref, gt_ref, gf_ref, t_ref, dx_ref, dyo_ref, loss_ref, dgf_ref, dgt_ref):
        i = pl.program_id(0)

        @pl.when(i == 0)
        def _():
            loss_ref[...] = jnp.zeros_like(loss_ref)
            dgf_ref[...] = jnp.zeros_like(dgf_ref)
            dgt_ref[...] = jnp.zeros_like(dgt_ref)

        yo = yo_ref[...]
        one_gate = 1.0 + gt_ref[...]
        x2 = x_ref[...] + one_gate * yo
        r = lax.rsqrt(jnp.mean(x2 * x2, axis=-1, keepdims=True) + EPS)
        xn = x2 * r
        gf_v = gf_ref[...]
        err = xn * gf_v - t_ref[...]
        loss_ref[...] += 0.5 * jnp.sum(jnp.mean(err * err, axis=-1, keepdims=True))
        dout = err * (1.0 / D_MODEL)
        dgf_ref[...] += jnp.sum(dout * xn, axis=0, keepdims=True)
        dxn = dout * gf_v
        dx2 = r * (dxn - xn * jnp.mean(dxn * xn, axis=-1, keepdims=True))
        dx_ref[...] = dx2
        dyo_ref[...] = (dx2 * one_gate).astype(BF16)
        dgt_ref[...] += jnp.sum(dx2 * yo, axis=0, keepdims=True)

    return pl.pallas_call(
        body, name=name, grid=(s // ROW_TILE,),
        in_specs=[_row_spec(), _row_spec(), _vec_spec(), _vec_spec(), _row_spec()],
        out_specs=[_row_spec(), _row_spec(), pl.BlockSpec((1, LANES), lambda i: (0, 0)), _vec_spec(), _vec_spec()],
        out_shape=[SDS((s, D_MODEL), F32), SDS((s, D_MODEL), BF16), SDS((1, LANES), F32),
                   SDS((1, D_MODEL), F32), SDS((1, D_MODEL), F32)],
        compiler_params=_params(1),
    )(x1, yo1, gate1, gf, target)


def _ln_mod_bwd(name, dh, x, dx_next, g, scale):
    s = x.shape[0]

    def body(dh_ref, x_ref, dxn_ref, g_ref, sc_ref, dx_ref, dsh_ref, dsc_ref, dg_ref):
        i = pl.program_id(0)

        @pl.when(i == 0)
        def _():
            dsh_ref[...] = jnp.zeros_like(dsh_ref)
            dsc_ref[...] = jnp.zeros_like(dsc_ref)
            dg_ref[...] = jnp.zeros_like(dg_ref)

        dh_v = dh_ref[...]
        xv = x_ref[...]
        g_v = g_ref[...]
        r = lax.rsqrt(jnp.mean(xv * xv, axis=-1, keepdims=True) + EPS)
        xn = xv * r
        dsh_ref[...] += jnp.sum(dh_v, axis=0, keepdims=True)
        dsc_ref[...] += jnp.sum(dh_v * (xn * g_v), axis=0, keepdims=True)
        dn = dh_v * (1.0 + sc_ref[...])
        dg_ref[...] += jnp.sum(dn * xn, axis=0, keepdims=True)
        dxh = dn * g_v
        dx_ref[...] = dxn_ref[...] + r * (dxh - xn * jnp.mean(dxh * xn, axis=-1, keepdims=True))

    return pl.pallas_call(
        body, name=name, grid=(s // ROW_TILE,),
        in_specs=[_row_spec(), _row_spec(), _row_spec(), _vec_spec(), _vec_spec()],
        out_specs=[_row_spec(), _vec_spec(), _vec_spec(), _vec_spec()],
        out_shape=[SDS((s, D_MODEL), F32)] + [SDS((1, D_MODEL), F32)] * 3, compiler_params=_params(1),
    )(dh, x, dx_next, g, scale)


def _resid_bwd(name, dx, yo, gate):
    s = dx.shape[0]

    def body(dx_ref, yo_ref, gt_ref, dyo_ref, dgt_ref):
        i = pl.program_id(0)

        @pl.when(i == 0)
        def _():
            dgt_ref[...] = jnp.zeros_like(dgt_ref)

        dx_v = dx_ref[...]
        dyo_ref[...] = (dx_v * (1.0 + gt_ref[...])).astype(BF16)
        dgt_ref[...] += jnp.sum(dx_v * yo_ref[...], axis=0, keepdims=True)

    return pl.pallas_call(
        body, name=name, grid=(s // ROW_TILE,),
        in_specs=[_row_spec(), _row_spec(), _vec_spec()], out_specs=[_row_spec(), _vec_spec()],
        out_shape=[SDS((s, D_MODEL), BF16), SDS((1, D_MODEL), F32)], compiler_params=_params(1),
    )(dx, yo, gate)


def _window_of(g):
    return jnp.left_shift(2, g)


def _pool_fwd(name, proj0):
    s = proj0.shape[0]
    hb = ROW_TILE // HALO
    ext_rows = ROW_TILE + HALO

    def body(u_ref, halo_ref, p_ref):
        i = pl.program_id(0)
        g = pl.program_id(1)
        u = u_ref[...]
        halo = jnp.where(i == 0, 0.0, halo_ref[...])
        ext = jnp.concatenate([halo, u], axis=0)
        s2 = ext + pltpu.roll(ext, 1, axis=0)
        s4 = s2 + pltpu.roll(s2, 2, axis=0)
        s8 = s4 + pltpu.roll(s4, 4, axis=0)
        s16 = s8 + pltpu.roll(s8, 8, axis=0)
        win = jnp.where(g == 0, s2, jnp.where(g == 1, s4, jnp.where(g == 2, s8, s16)))[HALO:, :]
        t = i * ROW_TILE + lax.broadcasted_iota(jnp.int32, (ROW_TILE, 1), 0)
        cnt = jnp.minimum(t + 1, _window_of(g)).astype(F32)
        p_ref[...] = (win / cnt - u).astype(BF16)

    return pl.pallas_call(
        body, name=name, grid=(s // ROW_TILE, N_GROUPS),
        in_specs=[pl.BlockSpec((ROW_TILE, POOL_GROUP), lambda i, g: (i, g)),
                  pl.BlockSpec((HALO, POOL_GROUP), lambda i, g: (jnp.maximum(i * hb - 1, 0), g))],
        out_specs=pl.BlockSpec((ROW_TILE, POOL_GROUP), lambda i, g: (i, g)),
        out_shape=SDS((s, D_POOL), BF16), compiler_params=_params(2),
    )(proj0, proj0)


def _pool_bwd(name, dp):
    s = dp.shape[0]
    hb = ROW_TILE // HALO
    n_hb = s // HALO
    n_tiles = s // ROW_TILE
    ext_rows = ROW_TILE + HALO

    def body(dp_ref, halo_ref, du_ref):
        i = pl.program_id(0)
        g = pl.program_id(1)
        w = _window_of(g)
        dp_v = dp_ref[...]
        t = i * ROW_TILE + lax.broadcasted_iota(jnp.int32, (ext_rows, 1), 0)
        cnt = jnp.minimum(t + 1, w).astype(F32)
        halo = jnp.where(i == n_tiles - 1, 0.0, halo_ref[...])
        ext = jnp.concatenate([dp_v, halo], axis=0) / cnt
        s2 = ext + pltpu.roll(ext, ext_rows - 1, axis=0)
        s4 = s2 + pltpu.roll(s2, ext_rows - 2, axis=0)
        s8 = s4 + pltpu.roll(s4, ext_rows - 4, axis=0)
        s16 = s8 + pltpu.roll(s8, ext_rows - 8, axis=0)
        win = jnp.where(g == 0, s2, jnp.where(g == 1, s4, jnp.where(g == 2, s8, s16)))[:ROW_TILE, :]
        du_ref[...] = (win - dp_v).astype(BF16)

    return pl.pallas_call(
        body, name=name, grid=(n_tiles, N_GROUPS),
        in_specs=[pl.BlockSpec((ROW_TILE, POOL_GROUP), lambda i, g: (i, g)),
                  pl.BlockSpec((HALO, POOL_GROUP), lambda i, g: (jnp.minimum((i + 1) * hb, n_hb - 1), g))],
        out_specs=pl.BlockSpec((ROW_TILE, POOL_GROUP), lambda i, g: (i, g)),
        out_shape=SDS((s, D_POOL), BF16), compiler_params=_params(2),
    )(dp, dp)


Q_COL = D_POOL // LANES
K_COL = (D_POOL + D_SB) // LANES
V_COL = (D_POOL + 2 * D_SB) // LANES
N_PAIRS = D_SB // LANES
ATT_SCALE = 0.125


def _att_consts():
    r = lax.broadcasted_iota(jnp.int32, (ATT_TILE, ATT_TILE), 0)
    c = lax.broadcasted_iota(jnp.int32, (ATT_TILE, ATT_TILE), 1)
    lane = lax.broadcasted_iota(jnp.int32, (1, LANES), 1)
    m_a = (lane < HEAD_DIM).astype(F32)
    return r, c, (m_a, 1.0 - m_a)


def _attn_fwd(name, proj0):
    s = proj0.shape[0]
    nq = s // ATT_TILE

    def body(q_ref, k_ref, v_ref, o_ref):
        i = pl.program_id(1)
        r, c, masks = _att_consts()
        tri = (r >= c).astype(BF16)
        diff = c - r
        q = q_ref[...]
        qh = [(q * (m * ATT_SCALE)).astype(BF16) for m in masks]

        def step(n, carry):
            kb = i - n
            k0 = pl.multiple_of(kb * ATT_TILE, ATT_TILE)
            kt = k_ref[pl.ds(k0, ATT_TILE), :].astype(BF16)
            vt = v_ref[pl.ds(k0, ATT_TILE), :]
            valid = diff < n * ATT_TILE
            acc = carry[2]
            new_c = []
            for h in range(2):
                z = _dot_nt(qh[h], kt)
                lf = jnp.where(valid, -_softplus(z), 0.0)
                hi, lo = _split_bf16(lf)
                run = _dot(hi, tri) + _dot(lo, tri) + carry[h]
                a = jnp.where(valid, jnp.exp(jnp.minimum(z + run, 0.0)), 0.0)
                acc = acc + _dot(a.astype(BF16), (vt * masks[h]).astype(BF16))
                new_c.append(carry[h] + jnp.sum(lf, axis=1, keepdims=True))
            return new_c[0], new_c[1], acc

        zero_c = jnp.zeros((ATT_TILE, 1), F32)
        out = lax.fori_loop(0, i + 1, step, (zero_c, zero_c, jnp.zeros((ATT_TILE, LANES), F32)))
        o_ref[...] = out[2]

    return pl.pallas_call(
        body, name=name, grid=(N_PAIRS, nq),
        in_specs=[pl.BlockSpec((ATT_TILE, LANES), lambda j, i: (i, Q_COL + j)),
                  pl.BlockSpec((s, LANES), lambda j, i: (0, K_COL + j)),
                  pl.BlockSpec((s, LANES), lambda j, i: (0, V_COL + j))],
        out_specs=pl.BlockSpec((ATT_TILE, LANES), lambda j, i: (i, j)),
        out_shape=SDS((s, D_SB), F32), compiler_params=_params(2),
    )(proj0, proj0, proj0)


def _attn_bwd(name, proj0, o, do):
    s = proj0.shape[0]
    nq = s // ATT_TILE

    def body(q_ref, k_ref, v_ref, o_ref, do_ref, dq_ref, dk_ref, dv_ref):
        i = pl.program_id(1)

        @pl.when(i == 0)
        def _():
            dk_ref[...] = jnp.zeros_like(dk_ref)
            dv_ref[...] = jnp.zeros_like(dv_ref)

        r, c, masks = _att_consts()
        tri = (r >= c).astype(BF16)
        tri_x = (r > c).astype(BF16)
        diff = c - r
        q = q_ref[...]
        do_b = do_ref[...].astype(BF16)
        do_o = do_b.astype(F32) * o_ref[...]
        qh = [(q * (m * ATT_SCALE)).astype(BF16) for m in masks]
        doh = [(do_b * m.astype(BF16)) for m in masks]
        dsum = [jnp.sum(do_o * m, axis=1, keepdims=True) for m in masks]

        def step(n, carry):
            kb = i - n
            k0 = pl.multiple_of(kb * ATT_TILE, ATT_TILE)
            kt = k_ref[pl.ds(k0, ATT_TILE), :]
            kt_b = kt.astype(BF16)
            vt_b = v_ref[pl.ds(k0, ATT_TILE), :].astype(BF16)
            valid = diff < n * ATT_TILE
            dq_acc = carry[4]
            dk_t = jnp.zeros((ATT_TILE, LANES), F32)
            dv_t = jnp.zeros((ATT_TILE, LANES), F32)
            new_c = []
            for h in range(2):
                z = _dot_nt(qh[h], kt_b)
                sp = _softplus(z)
                lf = jnp.where(valid, -sp, 0.0)
                hi, lo = _split_bf16(lf)
                run = _dot(hi, tri) + _dot(lo, tri) + carry[h]
                a_b = jnp.where(valid, jnp.exp(jnp.minimum(z + run, 0.0)), 0.0).astype(BF16)
                d_a = _dot_nt(doh[h], vt_b)
                g = a_b.astype(F32) * d_a
                ghi, glo = _split_bf16(g)
                later = _dot(ghi, tri_x) + _dot(glo, tri_x) + carry[2 + h]
                sig = jnp.exp(jnp.minimum(z - sp, 0.0))
                dz = jnp.where(valid, g - sig * (dsum[h] - later), 0.0).astype(BF16)
                dq_acc = dq_acc + _dot(dz, (kt * (masks[h] * ATT_SCALE)).astype(BF16))
                dk_t = dk_t + _dot_tn(dz, qh[h])
                dv_t = dv_t + _dot_tn(a_b, doh[h])
                new_c.append((carry[h] + jnp.sum(lf, axis=1, keepdims=True),
                              carry[2 + h] + jnp.sum(g, axis=1, keepdims=True)))
            dk_ref[pl.ds(k0, ATT_TILE), :] += dk_t
            dv_ref[pl.ds(k0, ATT_TILE), :] += dv_t
            return new_c[0][0], new_c[1][0], new_c[0][1], new_c[1][1], dq_acc

        zero_c = jnp.zeros((ATT_TILE, 1), F32)
        out = lax.fori_loop(0, i + 1, step, (zero_c, zero_c, zero_c, zero_c, jnp.zeros((ATT_TILE, LANES), F32)))
        dq_ref[...] = out[4]

    tile = pl.BlockSpec((ATT_TILE, LANES), lambda j, i: (i, j))
    full = pl.BlockSpec((s, LANES), lambda j, i: (0, j))
    return pl.pallas_call(
        body, name=name, grid=(N_PAIRS, nq),
        in_specs=[pl.BlockSpec((ATT_TILE, LANES), lambda j, i: (i, Q_COL + j)),
                  pl.BlockSpec((s, LANES), lambda j, i: (0, K_COL + j)),
                  pl.BlockSpec((s, LANES), lambda j, i: (0, V_COL + j)),
                  tile, tile],
        out_specs=[tile, full, full],
        out_shape=[SDS((s, D_SB), F32)] * 3, compiler_params=_params(2),
    )(proj0, proj0, proj0, o, do)


GATE0_COL = (D_POOL + 3 * D_SB) // D_INNER


def _gate_fwd0(name, yp_raw, o, proj0, ps):
    s = o.shape[0]

    def body(yp_ref, o_ref, gt_ref, ps_ref, y_ref):
        gt = gt_ref[...]
        sg = gt * _sigmoid(gt)
        y_ref[:, :D_POOL] = (yp_ref[...] * ps_ref[...] * sg[:, :D_POOL]).astype(BF16)
        y_ref[:, D_POOL:] = (o_ref[...] * sg[:, D_POOL:]).astype(BF16)

    return pl.pallas_call(
        body, name=name, grid=(s // ROW_TILE,),
        in_specs=[_row_spec(), _row_spec(), _row_spec(D_INNER, GATE0_COL), _vec_spec()],
        out_specs=_row_spec(D_INNER),
        out_shape=SDS((s, D_INNER), BF16), compiler_params=_params(1),
    )(yp_raw, o, proj0, ps)


def _dsilu(x):
    sg = _sigmoid(x)
    return sg * (1.0 + x * (1.0 - sg))


def _gate_bwd0(name, dymix, yp_raw, o, proj0, ps):
    s = o.shape[0]

    def body(dy_ref, yp_ref, o_ref, gt_ref, ps_ref, dyp_ref, do_ref, dgt_ref, dps_ref):
        i = pl.program_id(0)

        @pl.when(i == 0)
        def _():
            dps_ref[...] = jnp.zeros_like(dps_ref)

        gt = gt_ref[...]
        dy = dy_ref[...]
        sg = gt * _sigmoid(gt)
        dsg = _dsilu(gt)
        dcat = dy * sg
        yp = yp_ref[...]
        ps_v = ps_ref[...]
        dyp_ref[...] = (dcat[:, :D_POOL] * ps_v).astype(BF16)
        do_ref[...] = dcat[:, D_POOL:]
        dps_ref[...] += jnp.sum(dcat[:, :D_POOL] * yp, axis=0, keepdims=True)
        dgt_ref[:, :D_POOL] = (dy[:, :D_POOL] * (yp * ps_v) * dsg[:, :D_POOL]).astype(BF16)
        dgt_ref[:, D_POOL:] = (dy[:, D_POOL:] * o_ref[...] * dsg[:, D_POOL:]).astype(BF16)

    return pl.pallas_call(
        body, name=name, grid=(s // ROW_TILE,),
        in_specs=[_row_spec(D_INNER), _row_spec(), _row_spec(), _row_spec(D_INNER, GATE0_COL), _vec_spec()],
        out_specs=[_row_spec(), _row_spec(), _row_spec(D_INNER), _vec_spec()],
        out_shape=[SDS((s, D_POOL), BF16), SDS((s, D_SB), F32), SDS((s, D_INNER), BF16), SDS((1, D_POOL), F32)],
        compiler_params=_params(1),
    )(dymix, yp_raw, o, proj0, ps)


CONV_COLS = 512
N_CONV_BLK = D_INNER // CONV_COLS
CONV_HALO = 8


def _conv_fwd(name, proj1, cw, cb):
    s = proj1.shape[0]
    hb = ROW_TILE // CONV_HALO
    ext_rows = ROW_TILE + CONV_HALO

    def body(gb_ref, gc_ref, u_ref, gt_ref, gch_ref, uh_ref, cw_ref, cb_ref, y_ref):
        i = pl.program_id(1)
        uc = gc_ref[...] * u_ref[...]
        halo = jnp.where(i == 0, 0.0, gch_ref[...] * uh_ref[...])
        ext = jnp.concatenate([halo, uc], axis=0)
        uc1 = pltpu.roll(ext, 1, axis=0)[CONV_HALO:, :]
        uc2 = pltpu.roll(ext, 2, axis=0)[CONV_HALO:, :]
        cw_v = cw_ref[...]
        conv = cb_ref[...] + cw_v[0:1, :] * uc2 + cw_v[1:2, :] * uc1 + cw_v[2:3, :] * uc
        gt = gt_ref[...]
        y_ref[...] = (gb_ref[...] * conv * (gt * _sigmoid(gt))).astype(BF16)

    def tile(part):
        return pl.BlockSpec((ROW_TILE, CONV_COLS), lambda b, i: (i, part * N_CONV_BLK + b))

    def halo(part):
        return pl.BlockSpec((CONV_HALO, CONV_COLS), lambda b, i: (jnp.maximum(i * hb - 1, 0), part * N_CONV_BLK + b))

    return pl.pallas_call(
        body, name=name, grid=(N_CONV_BLK, s // ROW_TILE),
        in_specs=[tile(0), tile(1), tile(2), tile(3), halo(1), halo(2),
                  pl.BlockSpec((3, CONV_COLS), lambda b, i: (0, b)), pl.BlockSpec((1, CONV_COLS), lambda b, i: (0, b))],
        out_specs=pl.BlockSpec((ROW_TILE, CONV_COLS), lambda b, i: (i, b)),
        out_shape=SDS((s, D_INNER), BF16), compiler_params=_params(2),
    )(proj1, proj1, proj1, proj1, proj1, proj1, cw, cb)


def _conv_bwd(name, dymix, proj1, cw, cb):
    s = proj1.shape[0]
    hb = ROW_TILE // CONV_HALO
    n_hb = s // CONV_HALO
    n_tiles = s // ROW_TILE
    ext_rows = ROW_TILE + CONV_HALO

    def body(dy_ref, gb_ref, gc_ref, u_ref, gt_ref, gch_ref, uh_ref, dyn_ref, gbn_ref, gtn_ref, cw_ref, cb_ref,
             dgb_ref, dgc_ref, du_ref, dgt_ref, dcw_ref, dcb_ref):
        i = pl.program_id(1)

        @pl.when(i == 0)
        def _():
            dcw_ref[...] = jnp.zeros_like(dcw_ref)
            dcb_ref[...] = jnp.zeros_like(dcb_ref)

        gc = gc_ref[...]
        u = u_ref[...]
        gb = gb_ref[...]
        gt = gt_ref[...]
        dy = dy_ref[...]
        uc = gc * u
        halo = jnp.where(i == 0, 0.0, gch_ref[...] * uh_ref[...])
        ext = jnp.concatenate([halo, uc], axis=0)
        uc1 = pltpu.roll(ext, 1, axis=0)[CONV_HALO:, :]
        uc2 = pltpu.roll(ext, 2, axis=0)[CONV_HALO:, :]
        cw_v = cw_ref[...]
        w0, w1, w2 = cw_v[0:1, :], cw_v[1:2, :], cw_v[2:3, :]
        conv = cb_ref[...] + w0 * uc2 + w1 * uc1 + w2 * uc
        sg = gt * _sigmoid(gt)
        dconv = dy * gb * sg
        gtn = gtn_ref[...]
        dconv_next = jnp.where(i == n_tiles - 1, 0.0, dyn_ref[...] * gbn_ref[...] * (gtn * _sigmoid(gtn)))
        dext = jnp.concatenate([dconv, dconv_next], axis=0)
        dconv_p1 = pltpu.roll(dext, ext_rows - 1, axis=0)[:ROW_TILE, :]
        dconv_p2 = pltpu.roll(dext, ext_rows - 2, axis=0)[:ROW_TILE, :]
        duc = w2 * dconv + w1 * dconv_p1 + w0 * dconv_p2
        dgb_ref[...] = (dy * conv * sg).astype(BF16)
        dgc_ref[...] = (duc * u).astype(BF16)
        du_ref[...] = (duc * gc).astype(BF16)
        dgt_ref[...] = (dy * gb * conv * _dsilu(gt)).astype(BF16)
        dcw_ref[0:1, :] += jnp.sum(dconv * uc2, axis=0, keepdims=True)
        dcw_ref[1:2, :] += jnp.sum(dconv * uc1, axis=0, keepdims=True)
        dcw_ref[2:3, :] += jnp.sum(dconv * uc, axis=0, keepdims=True)
        dcb_ref[...] += jnp.sum(dconv, axis=0, keepdims=True)

    def tile(part):
        return pl.BlockSpec((ROW_TILE, CONV_COLS), lambda b, i: (i, part * N_CONV_BLK + b))

    def prev(part):
        return pl.BlockSpec((CONV_HALO, CONV_COLS), lambda b, i: (jnp.maximum(i * hb - 1, 0), part * N_CONV_BLK + b))

    def nxt(part):
        return pl.BlockSpec((CONV_HALO, CONV_COLS), lambda b, i: (jnp.minimum((i + 1) * hb, n_hb - 1), part * N_CONV_BLK + b))

    out_tile = pl.BlockSpec((ROW_TILE, CONV_COLS), lambda b, i: (i, b))
    return pl.pallas_call(
        body, name=name, grid=(N_CONV_BLK, n_tiles),
        in_specs=[tile(0), tile(0), tile(1), tile(2), tile(3), prev(1), prev(2), nxt(0), nxt(0), nxt(3),
                  pl.BlockSpec((3, CONV_COLS), lambda b, i: (0, b)), pl.BlockSpec((1, CONV_COLS), lambda b, i: (0, b))],
        out_specs=[out_tile, out_tile, out_tile, out_tile,
                   pl.BlockSpec((3, CONV_COLS), lambda b, i: (0, b)), pl.BlockSpec((1, CONV_COLS), lambda b, i: (0, b))],
        out_shape=[SDS((s, D_INNER), BF16)] * 4 + [SDS((3, D_INNER), F32), SDS((1, D_INNER), F32)],
        compiler_params=_params(2),
    )(dymix, proj1, proj1, proj1, proj1, proj1, proj1, dymix, proj1, proj1, cw, cb)


def _place():
    x, y, c = lax.axis_index("x"), lax.axis_index("y"), lax.axis_index("c")
    return x, y, c


def _flip(x, y, c, k):
    fx, fy, fc = (k >> 2) & 1, (k >> 1) & 1, k & 1
    return (1 - x if fx else x, 1 - y if fy else y, 1 - c if fc else c)


def _dev_index(p):
    return 4 * p[0] + 2 * p[1] + p[2]


HBM_SPEC = pl.BlockSpec(memory_space=pltpu.HBM)
VMEM_SPEC = pl.BlockSpec(memory_space=pltpu.VMEM)


def _allgather_weights(shards):
    n_w = len(shards)

    def body(*refs):
        ins, outs = refs[:n_w], refs[n_w:2 * n_w]
        send_sems, recv_sems, local_sems = refs[2 * n_w:]
        x, y, c = _place()
        me, sibling = (x, y, c), (x, y, 1 - c)
        chips = [(1 - x, y), (x, 1 - y), (1 - x, 1 - y)]

        def copy(w, k, block, to, src=None):
            rows = outs[w].at[_dev_index(block)]
            return pltpu.make_async_remote_copy(
                src_ref=rows if src is None else src, dst_ref=rows,
                send_sem=send_sems.at[7 * w + k], recv_sem=recv_sems.at[7 * w + k],
                device_id=to, device_id_type=MESH)

        mine, first, passed = [], [], []
        for w in range(n_w):
            cp = pltpu.make_async_copy(ins[w], outs[w].at[_dev_index(me)], local_sems.at[w])
            cp.start()
            mine.append(cp)
            fw = [copy(w, 0, me, sibling, src=ins[w])]
            fw += [copy(w, 1 + j, me, (*chip, c), src=ins[w]) for j, chip in enumerate(chips)]
            for cp in fw:
                cp.start()
            first += fw
        for w in range(n_w):
            for j, chip in enumerate(chips):
                copy(w, 1 + j, (*chip, c), me).wait_recv()
                cp = copy(w, 4 + j, (*chip, c), sibling)
                cp.start()
                passed.append(cp)
        for w in range(n_w):
            copy(w, 0, sibling, me).wait_recv()
            for j, chip in enumerate(chips):
                copy(w, 4 + j, (*chip, 1 - c), me).wait_recv()
        for cp in first + passed:
            cp.wait_send()
        for cp in mine:
            cp.wait()

    return pl.pallas_call(
        body, name="allgather_weights",
        out_shape=[SDS((N_DEV,) + sh.shape, sh.dtype) for sh in shards],
        in_specs=[HBM_SPEC] * n_w, out_specs=[HBM_SPEC] * n_w,
        scratch_shapes=[pltpu.SemaphoreType.DMA((7 * n_w,)), pltpu.SemaphoreType.DMA((7 * n_w,)),
                        pltpu.SemaphoreType.DMA((n_w,))],
    )(*shards)


def _exchange_wgrads(dws):
    n_w = len(dws)

    def body(*refs):
        ins, outs = refs[:n_w], refs[n_w:2 * n_w]
        send_sems, recv_sems, local_sems = refs[2 * n_w:]
        x, y, c = _place()
        me = (x, y, c)
        my = _dev_index(me)
        started = []
        for w in range(n_w):
            cp = pltpu.make_async_copy(ins[w].at[my], outs[w].at[my], local_sems.at[w])
            cp.start()
            started.append(cp)
            for k in range(1, N_DEV):
                peer = _flip(x, y, c, k)
                cp = pltpu.make_async_remote_copy(
                    src_ref=ins[w].at[_dev_index(peer)], dst_ref=outs[w].at[my],
                    send_sem=send_sems.at[7 * w + k - 1], recv_sem=recv_sems.at[7 * w + k - 1],
                    device_id=peer, device_id_type=MESH)
                cp.start()
                started.append(cp)
        for cp in started:
            cp.wait()

    return pl.pallas_call(
        body, name="exchange_wgrads",
        out_shape=[SDS(dw.shape, dw.dtype) for dw in dws],
        in_specs=[HBM_SPEC] * n_w, out_specs=[HBM_SPEC] * n_w,
        scratch_shapes=[pltpu.SemaphoreType.DMA((7 * n_w,)), pltpu.SemaphoreType.DMA((7 * n_w,)),
                        pltpu.SemaphoreType.DMA((n_w,))],
    )(*dws)


ADA_COLS = 3 * D_MODEL // N_DEV


def _ada_forward(c_row, conv_w, conv_b, ada_w, ada_b):
    cw_cols = conv_w.shape[1]

    def body(c_ref, cw_ref, cb_ref, aw_ref, ab_ref, m_ref, cs_ref, cwf_ref, cbf_ref,
             slab, gath, part, land, send_sems, recv_sems):
        x, y, c = _place()
        my = _dev_index((x, y, c))
        slab[...] = jnp.zeros_like(slab)
        slab[0:1, :] = c_ref[...]
        slab[1:4, 0:cw_cols] = cw_ref[...]
        slab[4:5, 0:cw_cols] = cb_ref[...]
        gath[my] = slab[...]
        sends = []
        for k in range(1, N_DEV):
            peer = _flip(x, y, c, k)
            cp = pltpu.make_async_remote_copy(
                src_ref=slab, dst_ref=gath.at[my], send_sem=send_sems.at[k - 1], recv_sem=recv_sems.at[k - 1],
                device_id=peer, device_id_type=MESH)
            cp.start()
            sends.append(cp)
        for cp in sends:
            cp.wait()
        for d in range(N_DEV):
            c_d = gath[d, 0:1, :]
            cs_ref[d:d + 1, :] = c_d * _sigmoid(c_d)
            cwf_ref[:, d * cw_cols:(d + 1) * cw_cols] = gath[d, 1:4, 0:cw_cols]
            cbf_ref[:, d * cw_cols:(d + 1) * cw_cols] = gath[d, 4:5, 0:cw_cols]
        cs = cs_ref[...]
        part[...] = jnp.zeros_like(part)
        for layer in range(2):
            m_part = jnp.dot(cs, aw_ref[layer], preferred_element_type=F32, precision=lax.Precision.HIGHEST)
            for d in range(N_DEV):
                part[d, layer:layer + 1, :] = m_part[d:d + 1, :]
        land[my] = part[my]
        sends = []
        for k in range(1, N_DEV):
            peer = _flip(x, y, c, k)
            cp = pltpu.make_async_remote_copy(
                src_ref=part.at[_dev_index(peer)], dst_ref=land.at[my],
                send_sem=send_sems.at[6 + k], recv_sem=recv_sems.at[6 + k],
                device_id=peer, device_id_type=MESH)
            cp.start()
            sends.append(cp)
        for cp in sends:
            cp.wait()
        for d in range(N_DEV):
            cols = slice(d * ADA_COLS, (d + 1) * ADA_COLS)
            m_ref[:, cols] = land[d, 0:2, :] + ab_ref[:, cols]

    return pl.pallas_call(
        body, name="ada_forward",
        out_shape=[SDS((2, 3 * D_MODEL), F32), SDS((N_DEV, D_MODEL), F32), SDS((3, N_DEV * cw_cols), F32),
                   SDS((1, N_DEV * cw_cols), F32)],
        in_specs=[VMEM_SPEC] * 5, out_specs=[VMEM_SPEC] * 4,
        scratch_shapes=[pltpu.VMEM((8, D_MODEL), F32), pltpu.VMEM((N_DEV, 8, D_MODEL), F32),
                        pltpu.VMEM((N_DEV, 8, ADA_COLS), F32), pltpu.VMEM((N_DEV, 8, ADA_COLS), F32),
                        pltpu.SemaphoreType.DMA((14,)), pltpu.SemaphoreType.DMA((14,))],
        compiler_params=pltpu.CompilerParams(vmem_limit_bytes=VMEM_LIMIT),
    )(c_row, conv_w, conv_b, ada_w, ada_b)


def _small_grads(slab):
    def body(slab_ref, gath_ref, tot_ref, send_sems, recv_sems):
        x, y, c = _place()
        my = _dev_index((x, y, c))
        gath_ref[my] = slab_ref[...]
        sends = []
        for k in range(1, N_DEV):
            peer = _flip(x, y, c, k)
            cp = pltpu.make_async_remote_copy(
                src_ref=slab_ref, dst_ref=gath_ref.at[my], send_sem=send_sems.at[k - 1], recv_sem=recv_sems.at[k - 1],
                device_id=peer, device_id_type=MESH)
            cp.start()
            sends.append(cp)
        for cp in sends:
            cp.wait()
        tot = gath_ref[0]
        for d in range(1, N_DEV):
            tot = tot + gath_ref[d]
        tot_ref[...] = tot

    return pl.pallas_call(
        body, name="small_grads",
        out_shape=[SDS((N_DEV, SLAB_ROWS, D_MODEL), F32), SDS((SLAB_ROWS, D_MODEL), F32)],
        in_specs=[VMEM_SPEC], out_specs=[VMEM_SPEC] * 2,
        scratch_shapes=[pltpu.SemaphoreType.DMA((7,)), pltpu.SemaphoreType.DMA((7,))],
    )(slab)


def _adamw_math(w, g, m, v):
    m = ADAM_B1 * m + (1.0 - ADAM_B1) * g
    v = ADAM_B2 * v + (1.0 - ADAM_B2) * jnp.square(g)
    m_hat = m / (1.0 - ADAM_B1 ** ADAM_STEP)
    v_hat = v / (1.0 - ADAM_B2 ** ADAM_STEP)
    delta = -ADAM_LR * (m_hat / (jnp.sqrt(v_hat) + ADAM_EPS) + ADAM_WD * w)
    return delta, m, v


def _sum_adamw(name, recv, w, m, v):
    rows, cols = w.shape
    tr = min(rows, 256)

    def body(r_ref, w_ref, m_ref, v_ref, g_ref, d_ref, nm_ref, nv_ref):
        g = r_ref[0].astype(F32)
        for d in range(1, N_DEV):
            g = g + r_ref[d].astype(F32)
        g_ref[...] = g
        d_ref[...], nm_ref[...], nv_ref[...] = _adamw_math(w_ref[...], g, m_ref[...], v_ref[...])

    blk = pl.BlockSpec((tr, cols), lambda i: (i, 0))
    return pl.pallas_call(
        body, name=name, grid=(rows // tr,),
        in_specs=[pl.BlockSpec((N_DEV, tr, cols), lambda i: (0, i, 0)), blk, blk, blk],
        out_specs=[blk] * 4, out_shape=[SDS((rows, cols), F32)] * 4, compiler_params=_params(1),
    )(recv, w, m, v)


def _ada_w_adamw(name, cs_t, dm_cols, w, m, v):
    def body(cs_ref, dm_ref, w_ref, m_ref, v_ref, g_ref, d_ref, nm_ref, nv_ref):
        cs = cs_ref[...]
        dm = dm_ref[...]
        g = cs[:, 0:1] * dm[0:1, :]
        for b in range(1, N_DEV):
            g = g + cs[:, b:b + 1] * dm[b:b + 1, :]
        g_ref[...] = g
        d_ref[...], nm_ref[...], nv_ref[...] = _adamw_math(w_ref[...], g, m_ref[...], v_ref[...])

    blk = pl.BlockSpec((None, D_MODEL, ADA_COLS), lambda l: (l, 0, 0))
    return pl.pallas_call(
        body, name=name, grid=(2,),
        in_specs=[pl.BlockSpec((D_MODEL, N_DEV), lambda l: (0, 0)),
                  pl.BlockSpec((None, N_DEV, ADA_COLS), lambda l: (l, 0, 0)), blk, blk, blk],
        out_specs=[blk] * 4, out_shape=[SDS((2, D_MODEL, ADA_COLS), F32)] * 4, compiler_params=_params(1),
    )(cs_t, dm_cols, w, m, v)


def _small_adamw(name, triples):
    n = len(triples)

    def body(*refs):
        ins, outs = refs[:4 * n], refs[4 * n:]
        for j in range(n):
            w_ref, g_ref, m_ref, v_ref = ins[4 * j:4 * j + 4]
            d, nm, nv = _adamw_math(w_ref[...], g_ref[...], m_ref[...], v_ref[...])
            outs[3 * j][...] = d
            outs[3 * j + 1][...] = nm
            outs[3 * j + 2][...] = nv

    flat = [a for t in triples for a in t]
    return pl.pallas_call(
        body, name=name,
        out_shape=[SDS(t[0].shape, F32) for t in triples for _ in range(3)],
        in_specs=[VMEM_SPEC] * (4 * n), out_specs=[VMEM_SPEC] * (3 * n),
    )(*flat)


def kernel(x, c, norm_g, ada_w, ada_b, even_w_in, pool_w, pool_scale, even_w_out, odd_w_in, conv_w, conv_b, odd_w_out, final_g, loss_target, m_norm_g, m_ada_w, m_ada_b, m_even_w_in, m_pool_w, m_pool_scale, m_even_w_out, m_odd_w_in, m_conv_w, m_conv_b, m_odd_w_out, m_final_g, v_norm_g, v_ada_w, v_ada_b, v_even_w_in, v_pool_w, v_pool_scale, v_even_w_out, v_odd_w_in, v_conv_w, v_conv_b, v_odd_w_out, v_final_g):
    seq = x.shape[1]
    x0 = x[0]
    target = loss_target[0]
    final_g2 = final_g.reshape(1, D_MODEL)

    w_in_e = even_w_in[0]
    w_out_e = even_w_out[0]
    w_in_o = odd_w_in[0]
    w_out_o = odd_w_out[0]
    w_pool = pool_w[0].reshape(N_GROUPS * 32, POOL_GROUP)
    shards = [w.astype(BF16) for w in (w_in_e, w_out_e, w_in_o, w_out_o, w_pool)]
    wg_in_e, wg_out_e, wg_in_o, wg_out_o, wg_pool = _allgather_weights(shards)
    wf_out_e = wg_out_e.reshape(D_INNER, D_MODEL)
    wf_out_o = wg_out_o.reshape(D_INNER, D_MODEL)
    wf_pool = wg_pool.reshape(N_DEV, N_GROUPS, 32, POOL_GROUP).transpose(1, 0, 2, 3).reshape(N_GROUPS, POOL_GROUP, POOL_GROUP)

    m_vec, cs_all, conv_w_full, conv_b_full = _ada_forward(c, conv_w[0], conv_b, ada_w, ada_b)
    shift = [m_vec[l:l + 1, 0:D_MODEL] for l in range(2)]
    scale = [m_vec[l:l + 1, D_MODEL:2 * D_MODEL] for l in range(2)]
    gate = [m_vec[l:l + 1, 2 * D_MODEL:] for l in range(2)]
    ng = [norm_g[l:l + 1] for l in range(2)]

    h0 = _ln_mod("ln_mod0", x0, ng[0], scale[0], shift[0])
    proj0 = _proj_in("proj_in0", h0, wg_in_e)
    p = _pool_fwd("pool_fwd", proj0)
    yp_raw = _pool_mm("pool_mix", p, wf_pool, _dot)
    o = _attn_fwd("attn_fwd", proj0)
    ymix0 = _gate_fwd0("gate_fwd0", yp_raw, o, proj0, pool_scale)
    yo0 = _proj_out("proj_out0", ymix0, wf_out_e)

    x1, h1 = _resid_ln_mod("resid_ln_mod1", x0, yo0, gate[0], ng[1], scale[1], shift[1])
    proj1 = _proj_in("proj_in1", h1, wg_in_o)
    ymix1 = _conv_fwd("conv_fwd", proj1, conv_w_full, conv_b_full)
    yo1 = _proj_out("proj_out1", ymix1, wf_out_o)

    dx2, dyo1, loss_acc, d_final_g, d_gate1 = _final_loss("final_loss", x1, yo1, gate[1], final_g2, target)
    loss = lax.psum(loss_acc[0, 0], ("x", "y", "c"))

    dymix1 = _proj_out_bwd("proj_out1_bwd", dyo1, wf_out_o)
    dw_out_o = _wgrad_out("wgrad_out1", ymix1, dyo1)
    dgb, dgc, du, dgt1, d_conv_w, d_conv_b = _conv_bwd("conv_bwd", dymix1, proj1, conv_w_full, conv_b_full)
    dproj1 = jnp.concatenate([dgb, dgc, du, dgt1], axis=1)
    dh1 = _proj_in_bwd("proj_in1_bwd", dproj1, wg_in_o)
    dw_in_o = _wgrad_in("wgrad_in1", h1, dproj1)
    dx1, d_shift1, d_scale1, d_ng1 = _ln_mod_bwd("ln_mod1_bwd", dh1, x1, dx2, ng[1], scale[1])

    dyo0, d_gate0 = _resid_bwd("resid0_bwd", dx1, yo0, gate[0])
    dymix0 = _proj_out_bwd("proj_out0_bwd", dyo0, wf_out_e)
    dw_out_e = _wgrad_out("wgrad_out0", ymix0, dyo0)
    dyp, do, dgt0, d_pool_scale = _gate_bwd0("gate_bwd0", dymix0, yp_raw, o, proj0, pool_scale)
    dp = _pool_mm("pool_mix_bwd", dyp, wf_pool, _dot_nt)
    dw_pool = _pool_wgrad("pool_wgrad", p, dyp)
    du_pool = _pool_bwd("pool_bwd", dp)
    dq, dk, dv = _attn_bwd("attn_bwd", proj0, o, do)
    dproj0 = jnp.concatenate([du_pool, dq.astype(BF16), dk.astype(BF16), dv.astype(BF16), dgt0], axis=1)
    dh0 = _proj_in_bwd("proj_in0_bwd", dproj0, wg_in_e)
    dw_in_e = _wgrad_in("wgrad_in0", h0, dproj0)
    dx0, d_shift0, d_scale0, d_ng0 = _ln_mod_bwd("ln_mod0_bwd", dh0, x0, dx1, ng[0], scale[0])
    grad_x = dx0[None]

    dw_pool_c = dw_pool.reshape(N_GROUPS, N_DEV, 32, POOL_GROUP).transpose(1, 0, 2, 3).reshape(N_DEV, N_GROUPS * 32, POOL_GROUP).astype(BF16)
    dws = [dw_in_e, dw_out_e.reshape(N_DEV, D_INNER // N_DEV, D_MODEL), dw_in_o,
           dw_out_o.reshape(N_DEV, D_INNER // N_DEV, D_MODEL), dw_pool_c]
    r_in_e, r_out_e, r_in_o, r_out_o, r_pool = _exchange_wgrads(dws)
    big = {}
    big["even_w_in"] = _sum_adamw("adamw_even_w_in", r_in_e, w_in_e, m_even_w_in[0], v_even_w_in[0])
    big["even_w_out"] = _sum_adamw("adamw_even_w_out", r_out_e, w_out_e, m_even_w_out[0], v_even_w_out[0])
    big["odd_w_in"] = _sum_adamw("adamw_odd_w_in", r_in_o, w_in_o, m_odd_w_in[0], v_odd_w_in[0])
    big["odd_w_out"] = _sum_adamw("adamw_odd_w_out", r_out_o, w_out_o, m_odd_w_out[0], v_odd_w_out[0])
    big["pool_w"] = _sum_adamw("adamw_pool_w", r_pool, w_pool, m_pool_w[0].reshape(N_GROUPS * 32, POOL_GROUP),
                               v_pool_w[0].reshape(N_GROUPS * 32, POOL_GROUP))
    big = {k: [a.reshape(shape) for a in v] for (k, v), shape in zip(
        big.items(), [even_w_in.shape, even_w_out.shape, odd_w_in.shape, odd_w_out.shape, pool_w.shape])}

    dm = jnp.concatenate([jnp.concatenate([d_shift0, d_scale0, d_gate0], axis=1),
                          jnp.concatenate([d_shift1, d_scale1, d_gate1], axis=1)], axis=0)
    slab = jnp.zeros((SLAB_ROWS, D_MODEL), F32)
    slab = slab.at[0:6].set(dm.reshape(6, D_MODEL))
    slab = slab.at[8:9].set(d_ng0).at[9:10].set(d_ng1).at[10:11].set(d_pool_scale).at[11:12].set(d_final_g)
    slab = slab.at[16:22].set(d_conv_w.reshape(6, D_MODEL)).at[24:26].set(d_conv_b.reshape(2, D_MODEL))
    gathered, total = _small_grads(slab)
    my = 4 * lax.axis_index("x") + 2 * lax.axis_index("y") + lax.axis_index("c")
    g_ada_b = total[0:6].reshape(2, 3 * D_MODEL)
    g_norm_g = total[8:10]
    g_pool_scale = total[10:11]
    g_final_g = total[11:12]
    cw_cols = conv_w.shape[2]
    g_conv_w = lax.dynamic_slice_in_dim(total[16:22].reshape(3, D_INNER), my * cw_cols, cw_cols, axis=1)
    g_conv_b = lax.dynamic_slice_in_dim(total[24:26].reshape(1, D_INNER), my * cw_cols, cw_cols, axis=1)
    dm_all = gathered[:, 0:6, :].reshape(N_DEV, 2, 3 * D_MODEL)
    dm_cols = lax.dynamic_slice_in_dim(dm_all, my * ADA_COLS, ADA_COLS, axis=2).transpose(1, 0, 2)
    ada = _ada_w_adamw("adamw_ada_w", cs_all.T, dm_cols, ada_w, m_ada_w, v_ada_w)

    small = _small_adamw("adamw_small", [
        (norm_g, g_norm_g, m_norm_g, v_norm_g),
        (ada_b, g_ada_b, m_ada_b, v_ada_b),
        (pool_scale, g_pool_scale, m_pool_scale, v_pool_scale),
        (conv_w[0], g_conv_w, m_conv_w[0], v_conv_w[0]),
        (conv_b, g_conv_b, m_conv_b, v_conv_b),
        (final_g2, g_final_g, m_final_g.reshape(1, D_MODEL), v_final_g.reshape(1, D_MODEL)),
    ])
    small = [small[3 * j:3 * j + 3] for j in range(6)]

    grads = {
        "norm_g": g_norm_g, "ada_w": ada[0], "ada_b": g_ada_b, "even_w_in": big["even_w_in"][0],
        "pool_w": big["pool_w"][0], "pool_scale": g_pool_scale, "even_w_out": big["even_w_out"][0],
        "odd_w_in": big["odd_w_in"][0], "conv_w": g_conv_w.reshape(conv_w.shape), "conv_b": g_conv_b,
        "odd_w_out": big["odd_w_out"][0], "final_g": g_final_g.reshape(D_MODEL),
    }
    rest = []
    for idx in range(3):
        rest += [
            small[0][idx], ada[1 + idx], small[1][idx], big["even_w_in"][1 + idx], big["pool_w"][1 + idx],
            small[2][idx], big["even_w_out"][1 + idx], big["odd_w_in"][1 + idx],
            small[3][idx].reshape(conv_w.shape), small[4][idx], big["odd_w_out"][1 + idx],
            small[5][idx].reshape(D_MODEL),
        ]
    order = ["norm_g", "ada_w", "ada_b", "even_w_in", "pool_w", "pool_scale", "even_w_out", "odd_w_in",
             "conv_w", "conv_b", "odd_w_out", "final_g"]
    return (loss, grad_x, *[grads[n] for n in order], *rest)
```

```python
import jax
import jax.numpy as jnp
from jax import lax
from jax.experimental import pallas as pl
from jax.experimental.pallas import tpu as pltpu

F32 = jnp.float32
BF16 = jnp.bfloat16
SDS = jax.ShapeDtypeStruct
MESH = pl.DeviceIdType.MESH

N_DEV = 8
D_MODEL = 1024
D_INNER = 2048
D_POOL = 1024
D_SB = 1024
N_GROUPS = 4
POOL_GROUP = 256
HEAD_DIM = 64
LANES = 128
D_IN_EVEN = 6144
D_IN_ODD = 8192
EPS = 1e-6
ADAM_LR = 0.001
ADAM_B1 = 0.9
ADAM_B2 = 0.999
ADAM_EPS = 1e-08
ADAM_WD = 0.01
ADAM_STEP = 10

ROW_TILE = 256
ATT_TILE = 256
HALO = 16
VMEM_LIMIT = 48 * 1024 * 1024
SLAB_ROWS = 32


def _params(n_axes):
    return pltpu.CompilerParams(dimension_semantics=("arbitrary",) * n_axes, vmem_limit_bytes=VMEM_LIMIT)


def _sigmoid(x):
    return 1.0 / (1.0 + jnp.exp(-x))


def _softplus(z):
    return jnp.maximum(z, 0.0) + jnp.log1p(jnp.exp(-jnp.abs(z)))


def _split_bf16(x):
    hi = x.astype(BF16)
    lo = (x - hi.astype(F32)).astype(BF16)
    return hi, lo


def _dot(a, b):
    return jnp.dot(a, b, preferred_element_type=F32)


def _dot_nt(a, b):
    return lax.dot_general(a, b, (((1,), (1,)), ((), ())), preferred_element_type=F32)


def _dot_tn(a, b):
    return lax.dot_general(a, b, (((0,), (0,)), ((), ())), preferred_element_type=F32)


def _mm(name, a, b, *, grid, a_spec, b_spec, o_spec, o_shape, o_dtype, dot, acc_axis=None, acc_shape=None):
    n_acc = grid[acc_axis] if acc_axis is not None else 1

    def body(a_ref, b_ref, o_ref, *scratch):
        prod = dot(a_ref[...], b_ref[...])
        if acc_axis is None:
            o_ref[...] = prod.astype(o_dtype)
        else:
            acc = scratch[0]
            k = pl.program_id(acc_axis)

            @pl.when(k == 0)
            def _():
                acc[...] = prod

            @pl.when(k > 0)
            def _():
                acc[...] += prod

            @pl.when(k == n_acc - 1)
            def _():
                o_ref[...] = acc[...].astype(o_dtype)

    scratch = [] if acc_axis is None else [pltpu.VMEM(acc_shape, F32)]
    return pl.pallas_call(
        body, name=name, grid=grid, in_specs=[a_spec, b_spec], out_specs=o_spec,
        out_shape=SDS(o_shape, o_dtype), scratch_shapes=scratch, compiler_params=_params(len(grid)),
    )(a, b)


def _proj_in(name, h, wg):
    s = h.shape[0]
    cn = wg.shape[2]
    tm = min(s, 512)
    return _mm(name, h, wg, grid=(s // tm, N_DEV),
               a_spec=pl.BlockSpec((tm, D_MODEL), lambda i, d: (i, 0)),
               b_spec=pl.BlockSpec((None, D_MODEL, cn), lambda i, d: (d, 0, 0)),
               o_spec=pl.BlockSpec((tm, cn), lambda i, d: (i, d)),
               o_shape=(s, N_DEV * cn), o_dtype=F32, dot=_dot)


def _proj_in_dual(name, h, wg):
    s = h.shape[0]
    cn = wg.shape[2]
    tm = min(s, 512)

    def body(a_ref, b_ref, o_ref, ob_ref):
        prod = _dot(a_ref[...], b_ref[...])
        o_ref[...] = prod
        ob_ref[...] = prod.astype(BF16)

    out = pl.BlockSpec((tm, cn), lambda i, d: (i, d))
    return pl.pallas_call(
        body, name=name, grid=(s // tm, N_DEV),
        in_specs=[pl.BlockSpec((tm, D_MODEL), lambda i, d: (i, 0)),
                  pl.BlockSpec((None, D_MODEL, cn), lambda i, d: (d, 0, 0))],
        out_specs=[out, out], out_shape=[SDS((s, N_DEV * cn), F32), SDS((s, N_DEV * cn), BF16)],
        compiler_params=_params(2),
    )(h, wg)


def _proj_out(name, y, w):
    s = y.shape[0]
    tm = min(s, 512)
    return _mm(name, y, w, grid=(s // tm,),
               a_spec=pl.BlockSpec((tm, D_INNER), lambda i: (i, 0)),
               b_spec=pl.BlockSpec((D_INNER, D_MODEL), lambda i: (0, 0)),
               o_spec=pl.BlockSpec((tm, D_MODEL), lambda i: (i, 0)),
               o_shape=(s, D_MODEL), o_dtype=F32, dot=_dot)


def _proj_out_bwd(name, dyo, w):
    s = dyo.shape[0]
    tm = min(s, 512)
    return _mm(name, dyo, w, grid=(s // tm,),
               a_spec=pl.BlockSpec((tm, D_MODEL), lambda i: (i, 0)),
               b_spec=pl.BlockSpec((D_INNER, D_MODEL), lambda i: (0, 0)),
               o_spec=pl.BlockSpec((tm, D_INNER), lambda i: (i, 0)),
               o_shape=(s, D_INNER), o_dtype=F32, dot=_dot_nt)


def _wgrad_out(name, y, dyo):
    s = y.shape[0]
    ts = min(s, 512)
    return _mm(name, y, dyo, grid=(s // ts,),
               a_spec=pl.BlockSpec((ts, D_INNER), lambda k: (k, 0)),
               b_spec=pl.BlockSpec((ts, D_MODEL), lambda k: (k, 0)),
               o_spec=pl.BlockSpec((D_INNER, D_MODEL), lambda k: (0, 0)),
               o_shape=(D_INNER, D_MODEL), o_dtype=BF16, dot=_dot_tn, acc_axis=0, acc_shape=(D_INNER, D_MODEL))


def _proj_in_bwd(name, dproj, wg):
    s = dproj.shape[0]
    cn = wg.shape[2]
    tm = min(s, 512)
    return _mm(name, dproj, wg, grid=(s // tm, N_DEV),
               a_spec=pl.BlockSpec((tm, cn), lambda i, d: (i, d)),
               b_spec=pl.BlockSpec((None, D_MODEL, cn), lambda i, d: (d, 0, 0)),
               o_spec=pl.BlockSpec((tm, D_MODEL), lambda i, d: (i, 0)),
               o_shape=(s, D_MODEL), o_dtype=F32, dot=_dot_nt, acc_axis=1, acc_shape=(tm, D_MODEL))


def _wgrad_in(name, h, dproj):
    s = h.shape[0]
    cn = dproj.shape[1] // N_DEV
    ts = min(s, 512)
    return _mm(name, h, dproj, grid=(N_DEV, s // ts),
               a_spec=pl.BlockSpec((ts, D_MODEL), lambda d, k: (k, 0)),
               b_spec=pl.BlockSpec((ts, cn), lambda d, k: (k, d)),
               o_spec=pl.BlockSpec((None, D_MODEL, cn), lambda d, k: (d, 0, 0)),
               o_shape=(N_DEV, D_MODEL, cn), o_dtype=BF16, dot=_dot_tn, acc_axis=1, acc_shape=(D_MODEL, cn))


def _pool_mm(name, p, wp, dot):
    s = p.shape[0]
    tm = min(s, 512)
    return _mm(name, p, wp, grid=(s // tm, N_GROUPS),
               a_spec=pl.BlockSpec((tm, POOL_GROUP), lambda i, g: (i, g)),
               b_spec=pl.BlockSpec((None, POOL_GROUP, POOL_GROUP), lambda i, g: (g, 0, 0)),
               o_spec=pl.BlockSpec((tm, POOL_GROUP), lambda i, g: (i, g)),
               o_shape=(s, D_POOL), o_dtype=F32, dot=dot)


def _pool_wgrad(name, p, dyp):
    s = p.shape[0]
    ts = min(s, 512)
    return _mm(name, p, dyp, grid=(N_GROUPS, s // ts),
               a_spec=pl.BlockSpec((ts, POOL_GROUP), lambda g, k: (k, g)),
               b_spec=pl.BlockSpec((ts, POOL_GROUP), lambda g, k: (k, g)),
               o_spec=pl.BlockSpec((None, POOL_GROUP, POOL_GROUP), lambda g, k: (g, 0, 0)),
               o_shape=(N_GROUPS, POOL_GROUP, POOL_GROUP), o_dtype=F32, dot=_dot_tn, acc_axis=1,
               acc_shape=(POOL_GROUP, POOL_GROUP))


def _vec_spec():
    return pl.BlockSpec((1, D_MODEL), lambda i: (0, 0))


def _row_spec(width=D_MODEL, col=0):
    return pl.BlockSpec((ROW_TILE, width), lambda i: (i, col))


def _ln_mod(name, x, g, scale, shift):
    s = x.shape[0]

    def body(x_ref, g_ref, sc_ref, sh_ref, h_ref):
        xv = x_ref[...]
        r = lax.rsqrt(jnp.mean(xv * xv, axis=-1, keepdims=True) + EPS)
        n = (xv * r) * g_ref[...]
        h_ref[...] = (n * (1.0 + sc_ref[...]) + sh_ref[...]).astype(BF16)

    return pl.pallas_call(
        body, name=name, grid=(s // ROW_TILE,),
        in_specs=[_row_spec(), _vec_spec(), _vec_spec(), _vec_spec()], out_specs=_row_spec(),
        out_shape=SDS((s, D_MODEL), BF16), compiler_params=_params(1),
    )(x, g, scale, shift)


def _resid_ln_mod(name, x, yo, gate, g, scale, shift):
    s = x.shape[0]

    def body(x_ref, yo_ref, gt_ref, g_ref, sc_ref, sh_ref, xn_ref, h_ref):
        xv = x_ref[...] + (1.0 + gt_ref[...]) * yo_ref[...]
        xn_ref[...] = xv
        r = lax.rsqrt(jnp.mean(xv * xv, axis=-1, keepdims=True) + EPS)
        n = (xv * r) * g_ref[...]
        h_ref[...] = (n * (1.0 + sc_ref[...]) + sh_ref[...]).astype(BF16)

    return pl.pallas_call(
        body, name=name, grid=(s // ROW_TILE,),
        in_specs=[_row_spec(), _row_spec(), _vec_spec(), _vec_spec(), _vec_spec(), _vec_spec()],
        out_specs=[_row_spec(), _row_spec()],
        out_shape=[SDS((s, D_MODEL), F32), SDS((s, D_MODEL), BF16)], compiler_params=_params(1),
    )(x, yo, gate, g, scale, shift)


def _final_loss(name, x1, yo1, gate1, gf, target):
    s = x1.shape[0]

    def body(x_ref, yo_ref, gt_ref, gf_ref, t_ref, dx_ref, dyo_ref, loss_ref, dgf_ref, dgt_ref):
        i = pl.program_id(0)

        @pl.when(i == 0)
        def _():
            loss_ref[...] = jnp.zeros_like(loss_ref)
            dgf_ref[...] = jnp.zeros_like(dgf_ref)
            dgt_ref[...] = jnp.zeros_like(dgt_ref)

        yo = yo_ref[...]
        one_gate = 1.0 + gt_ref[...]
        x2 = x_ref[...] + one_gate * yo
        r = lax.rsqrt(jnp.mean(x2 * x2, axis=-1, keepdims=True) + EPS)
        xn = x2 * r
        gf_v = gf_ref[...]
        err = xn * gf_v - t_ref[...]
        loss_ref[...] += 0.5 * jnp.sum(jnp.mean(err * err, axis=-1, keepdims=True))
        dout = err * (1.0 / D_MODEL)
        dgf_ref[...] += jnp.sum(dout * xn, axis=0, keepdims=True)
        dxn = dout * gf_v
        dx2 = r * (dxn - xn * jnp.mean(dxn * xn, axis=-1, keepdims=True))
        dx_ref[...] = dx2
        dyo_ref[...] = (dx2 * one_gate).astype(BF16)
        dgt_ref[...] += jnp.sum(dx2 * yo, axis=0, keepdims=True)

    return pl.pallas_call(
        body, name=name, grid=(s // ROW_TILE,),
        in_specs=[_row_spec(), _row_spec(), _vec_spec(), _vec_spec(), _row_spec()],
        out_specs=[_row_spec(), _row_spec(), pl.BlockSpec((1, LANES), lambda i: (0, 0)), _vec_spec(), _vec_spec()],
        out_shape=[SDS((s, D_MODEL), F32), SDS((s, D_MODEL), BF16), SDS((1, LANES), F32),
                   SDS((1, D_MODEL), F32), SDS((1, D_MODEL), F32)],
        compiler_params=_params(1),
    )(x1, yo1, gate1, gf, target)


def _ln_mod_bwd(name, dh, x, dx_next, g, scale):
    s = x.shape[0]

    def body(dh_ref, x_ref, dxn_ref, g_ref, sc_ref, dx_ref, dsh_ref, dsc_ref, dg_ref):
        i = pl.program_id(0)

        @pl.when(i == 0)
        def _():
            dsh_ref[...] = jnp.zeros_like(dsh_ref)
            dsc_ref[...] = jnp.zeros_like(dsc_ref)
            dg_ref[...] = jnp.zeros_like(dg_ref)

        dh_v = dh_ref[...]
        xv = x_ref[...]
        g_v = g_ref[...]
        r = lax.rsqrt(jnp.mean(xv * xv, axis=-1, keepdims=True) + EPS)
        xn = xv * r
        dsh_ref[...] += jnp.sum(dh_v, axis=0, keepdims=True)
        dsc_ref[...] += jnp.sum(dh_v * (xn * g_v), axis=0, keepdims=True)
        dn = dh_v * (1.0 + sc_ref[...])
        dg_ref[...] += jnp.sum(dn * xn, axis=0, keepdims=True)
        dxh = dn * g_v
        dx_ref[...] = dxn_ref[...] + r * (dxh - xn * jnp.mean(dxh * xn, axis=-1, keepdims=True))

    return pl.pallas_call(
        body, name=name, grid=(s // ROW_TILE,),
        in_specs=[_row_spec(), _row_spec(), _row_spec(), _vec_spec(), _vec_spec()],
        out_specs=[_row_spec(), _vec_spec(), _vec_spec(), _vec_spec()],
        out_shape=[SDS((s, D_MODEL), F32)] + [SDS((1, D_MODEL), F32)] * 3, compiler_params=_params(1),
    )(dh, x, dx_next, g, scale)


def _resid_bwd(name, dx, yo, gate):
    s = dx.shape[0]

    def body(dx_ref, yo_ref, gt_ref, dyo_ref, dgt_ref):
        i = pl.program_id(0)

        @pl.when(i == 0)
        def _():
            dgt_ref[...] = jnp.zeros_like(dgt_ref)

        dx_v = dx_ref[...]
        dyo_ref[...] = (dx_v * (1.0 + gt_ref[...])).astype(BF16)
        dgt_ref[...] += jnp.sum(dx_v * yo_ref[...], axis=0, keepdims=True)

    return pl.pallas_call(
        body, name=name, grid=(s // ROW_TILE,),
        in_specs=[_row_spec(), _row_spec(), _vec_spec()], out_specs=[_row_spec(), _vec_spec()],
        out_shape=[SDS((s, D_MODEL), BF16), SDS((1, D_MODEL), F32)], compiler_params=_params(1),
    )(dx, yo, gate)


def _window_of(g):
    return jnp.left_shift(2, g)


def _pool_fwd(name, proj0):
    s = proj0.shape[0]
    hb = ROW_TILE // HALO
    ext_rows = ROW_TILE + HALO

    def body(u_ref, halo_ref, p_ref):
        i = pl.program_id(0)
        g = pl.program_id(1)
        u = u_ref[...]
        halo = jnp.where(i == 0, 0.0, halo_ref[...])
        ext = jnp.concatenate([halo, u], axis=0)
        s2 = ext + pltpu.roll(ext, 1, axis=0)
        s4 = s2 + pltpu.roll(s2, 2, axis=0)
        s8 = s4 + pltpu.roll(s4, 4, axis=0)
        s16 = s8 + pltpu.roll(s8, 8, axis=0)
        win = jnp.where(g == 0, s2, jnp.where(g == 1, s4, jnp.where(g == 2, s8, s16)))[HALO:, :]
        t = i * ROW_TILE + lax.broadcasted_iota(jnp.int32, (ROW_TILE, 1), 0)
        cnt = jnp.minimum(t + 1, _window_of(g)).astype(F32)
        p_ref[...] = (win / cnt - u).astype(BF16)

    return pl.pallas_call(
        body, name=name, grid=(s // ROW_TILE, N_GROUPS),
        in_specs=[pl.BlockSpec((ROW_TILE, POOL_GROUP), lambda i, g: (i, g)),
                  pl.BlockSpec((HALO, POOL_GROUP), lambda i, g: (jnp.maximum(i * hb - 1, 0), g))],
        out_specs=pl.BlockSpec((ROW_TILE, POOL_GROUP), lambda i, g: (i, g)),
        out_shape=SDS((s, D_POOL), BF16), compiler_params=_params(2),
    )(proj0, proj0)


def _pool_bwd(name, dp):
    s = dp.shape[0]
    hb = ROW_TILE // HALO
    n_hb = s // HALO
    n_tiles = s // ROW_TILE
    ext_rows = ROW_TILE + HALO

    def body(dp_ref, halo_ref, du_ref):
        i = pl.program_id(0)
        g = pl.program_id(1)
        w = _window_of(g)
        dp_v = dp_ref[...]
        t = i * ROW_TILE + lax.broadcasted_iota(jnp.int32, (ext_rows, 1), 0)
        cnt = jnp.minimum(t + 1, w).astype(F32)
        halo = jnp.where(i == n_tiles - 1, 0.0, halo_ref[...])
        ext = jnp.concatenate([dp_v, halo], axis=0) / cnt
        s2 = ext + pltpu.roll(ext, ext_rows - 1, axis=0)
        s4 = s2 + pltpu.roll(s2, ext_rows - 2, axis=0)
        s8 = s4 + pltpu.roll(s4, ext_rows - 4, axis=0)
        s16 = s8 + pltpu.roll(s8, ext_rows - 8, axis=0)
        win = jnp.where(g == 0, s2, jnp.where(g == 1, s4, jnp.where(g == 2, s8, s16)))[:ROW_TILE, :]
        du_ref[...] = (win - dp_v).astype(BF16)

    return pl.pallas_call(
        body, name=name, grid=(n_tiles, N_GROUPS),
        in_specs=[pl.BlockSpec((ROW_TILE, POOL_GROUP), lambda i, g: (i, g)),
                  pl.BlockSpec((HALO, POOL_GROUP), lambda i, g: (jnp.minimum((i + 1) * hb, n_hb - 1), g))],
        out_specs=pl.BlockSpec((ROW_TILE, POOL_GROUP), lambda i, g: (i, g)),
        out_shape=SDS((s, D_POOL), BF16), compiler_params=_params(2),
    )(dp, dp)


FWD_HEADS_PER_STEP = 8
BWD_HEADS_PER_STEP = 4
ATT_SCALE = 0.125


def _att_groups(nh):
    lanes = nh * HEAD_DIM
    return lanes, D_SB // lanes, D_POOL // lanes, (D_POOL + D_SB) // lanes, (D_POOL + 2 * D_SB) // lanes


def _att_consts():
    r = lax.broadcasted_iota(jnp.int32, (ATT_TILE, ATT_TILE), 0)
    c = lax.broadcasted_iota(jnp.int32, (ATT_TILE, ATT_TILE), 1)
    first = lax.broadcasted_iota(jnp.int32, (1, LANES), 1) < HEAD_DIM
    return r, c, first


def _pair(x, p):
    return x[:, p * LANES:(p + 1) * LANES]


def _one_head(x, first, hh):
    zero = jnp.zeros_like(x)
    return jnp.where(first, x, zero) if hh == 0 else jnp.where(first, zero, x)


def _neg_softplus(z):
    return -(jnp.maximum(z, 0.0) + jnp.log(1.0 + jnp.exp(-jnp.abs(z))))


def _attn_fwd(name, proj0):
    s = proj0.shape[0]
    nq = s // ATT_TILE
    nh = FWD_HEADS_PER_STEP
    ATT_GROUP, N_ATT_GROUPS, Q_GRP, K_GRP, V_GRP = _att_groups(nh)

    def body(q_ref, k_ref, v_ref, o_ref):
        i = pl.program_id(1)
        r, c, first = _att_consts()
        tri = (r >= c).astype(BF16)
        below = c < r
        q = q_ref[...] * ATT_SCALE
        qh = [_one_head(_pair(q, h // 2), first, h % 2) for h in range(nh)]

        def tile(kb, carry, diagonal):
            k0 = pl.multiple_of(kb * ATT_TILE, ATT_TILE)
            kt = k_ref[pl.ds(k0, ATT_TILE), :]
            vt = v_ref[pl.ds(k0, ATT_TILE), :]
            z = [_dot_nt(qh[h], _pair(kt, h // 2)) for h in range(nh)]
            lf = [_neg_softplus(z[h]) for h in range(nh)]
            if diagonal:
                lf = [jnp.where(below, x, 0.0) for x in lf]
            parts = [_split_bf16(x) for x in lf]
            run = [_dot(parts[h][0], tri) + _dot(parts[h][1], tri) for h in range(nh)]
            a = [jnp.exp(z[h] + run[h] + carry[h]) for h in range(nh)]
            if diagonal:
                a = [jnp.where(below, x, 0.0) for x in a]
            out_acc = [carry[nh + h] + _dot(a[h].astype(BF16), _pair(vt, h // 2)) for h in range(nh)]
            out_c = [carry[h] + jnp.sum(lf[h], axis=1, keepdims=True) for h in range(nh)]
            return tuple(out_c + out_acc)

        init = tuple([jnp.zeros((ATT_TILE, 1), F32)] * nh + [jnp.zeros((ATT_TILE, LANES), F32)] * nh)
        carry = tile(i, init, True)
        carry = lax.fori_loop(1, i + 1, lambda n, cr: tile(i - n, cr, False), carry)
        for p in range(nh // 2):
            o_ref[:, p * LANES:(p + 1) * LANES] = jnp.where(first, carry[nh + 2 * p], carry[nh + 2 * p + 1])

    return pl.pallas_call(
        body, name=name, grid=(N_ATT_GROUPS, nq),
        in_specs=[pl.BlockSpec((ATT_TILE, ATT_GROUP), lambda j, i: (i, Q_GRP + j)),
                  pl.BlockSpec((s, ATT_GROUP), lambda j, i: (0, K_GRP + j)),
                  pl.BlockSpec((s, ATT_GROUP), lambda j, i: (0, V_GRP + j))],
        out_specs=pl.BlockSpec((ATT_TILE, ATT_GROUP), lambda j, i: (i, j)),
        out_shape=SDS((s, D_SB), F32), compiler_params=_params(2),
    )(proj0, proj0, proj0)


def _attn_bwd(name, proj0, o, do):
    s = proj0.shape[0]
    nq = s // ATT_TILE
    nh = BWD_HEADS_PER_STEP
    ATT_GROUP, N_ATT_GROUPS, Q_GRP, K_GRP, V_GRP = _att_groups(nh)

    def body(q_ref, k_ref, v_ref, o_ref, do_ref, dq_ref, dk_ref, dv_ref, dk_acc, dv_acc):
        i = pl.program_id(1)

        @pl.when(i == 0)
        def _():
            dk_acc[...] = jnp.zeros_like(dk_acc)
            dv_acc[...] = jnp.zeros_like(dv_acc)

        r, c, first = _att_consts()
        tri = (r >= c).astype(BF16)
        tri_x = (r > c).astype(BF16)
        below = c < r
        q = q_ref[...] * ATT_SCALE
        do_b = do_ref[...].astype(BF16)
        do_o = do_b.astype(F32) * o_ref[...]
        qh = [_one_head(_pair(q, h // 2), first, h % 2) for h in range(nh)]
        doh = [_one_head(_pair(do_b, h // 2), first, h % 2) for h in range(nh)]
        dsum = [jnp.sum(_one_head(_pair(do_o, h // 2), first, h % 2), axis=1, keepdims=True) for h in range(nh)]

        def tile(kb, carry, diagonal):
            k0 = pl.multiple_of(kb * ATT_TILE, ATT_TILE)
            kt = k_ref[pl.ds(k0, ATT_TILE), :]
            vt = v_ref[pl.ds(k0, ATT_TILE), :]
            hs = range(nh)
            z = [_dot_nt(qh[h], _pair(kt, h // 2)) for h in hs]
            d_a = [_dot_nt(doh[h], _pair(vt, h // 2)) for h in hs]
            lf = [_neg_softplus(z[h]) for h in hs]
            sig = [jnp.exp(z[h] + lf[h]) for h in hs]
            if diagonal:
                lf = [jnp.where(below, x, 0.0) for x in lf]
            parts = [_split_bf16(x) for x in lf]
            run = [_dot(parts[h][0], tri) + _dot(parts[h][1], tri) for h in hs]
            a = [jnp.exp(z[h] + run[h] + carry[h]) for h in hs]
            if diagonal:
                a = [jnp.where(below, x, 0.0) for x in a]
            a_b = [x.astype(BF16) for x in a]
            g = [a_b[h].astype(F32) * d_a[h] for h in hs]
            gparts = [_split_bf16(x) for x in g]
            later = [_dot(gparts[h][0], tri_x) + _dot(gparts[h][1], tri_x) for h in hs]
            dv_t = [_dot_tn(a_b[2 * p], doh[2 * p]) + _dot_tn(a_b[2 * p + 1], doh[2 * p + 1]) for p in range(nh // 2)]
            dz = [g[h] - sig[h] * (dsum[h] - later[h] - carry[nh + h]) for h in hs]
            if diagonal:
                dz = [jnp.where(below, x, 0.0) for x in dz]
            dz = [x.astype(BF16) for x in dz]
            out_dq = [carry[2 * nh + h] + _dot(dz[h], _pair(kt, h // 2)) for h in hs]
            dk_t = [_dot_tn(dz[2 * p], qh[2 * p]) + _dot_tn(dz[2 * p + 1], qh[2 * p + 1]) for p in range(nh // 2)]
            for p in range(nh // 2):
                dk_acc[pl.ds(k0, ATT_TILE), p * LANES:(p + 1) * LANES] += dk_t[p]
                dv_acc[pl.ds(k0, ATT_TILE), p * LANES:(p + 1) * LANES] += dv_t[p]
            out_c1 = [carry[h] + jnp.sum(lf[h], axis=1, keepdims=True) for h in hs]
            out_c2 = [carry[nh + h] + jnp.sum(g[h], axis=1, keepdims=True) for h in hs]
            return tuple(out_c1 + out_c2 + out_dq)

        init = tuple([jnp.zeros((ATT_TILE, 1), F32)] * (2 * nh) + [jnp.zeros((ATT_TILE, LANES), F32)] * nh)
        carry = tile(i, init, True)
        carry = lax.fori_loop(1, i + 1, lambda n, cr: tile(i - n, cr, False), carry)
        for p in range(nh // 2):
            dq_p = jnp.where(first, carry[2 * nh + 2 * p], carry[2 * nh + 2 * p + 1]) * ATT_SCALE
            dq_ref[:, p * LANES:(p + 1) * LANES] = dq_p.astype(BF16)

        @pl.when(i == nq - 1)
        def _():
            dk_ref[...] = dk_acc[...].astype(BF16)
            dv_ref[...] = dv_acc[...].astype(BF16)

    tile_spec = pl.BlockSpec((ATT_TILE, ATT_GROUP), lambda j, i: (i, j))
    full = pl.BlockSpec((s, ATT_GROUP), lambda j, i: (0, j))
    return pl.pallas_call(
        body, name=name, grid=(N_ATT_GROUPS, nq),
        in_specs=[pl.BlockSpec((ATT_TILE, ATT_GROUP), lambda j, i: (i, Q_GRP + j)),
                  pl.BlockSpec((s, ATT_GROUP), lambda j, i: (0, K_GRP + j)),
                  pl.BlockSpec((s, ATT_GROUP), lambda j, i: (0, V_GRP + j)),
                  tile_spec, tile_spec],
        out_specs=[tile_spec, full, full],
        out_shape=[SDS((s, D_SB), BF16)] * 3,
        scratch_shapes=[pltpu.VMEM((s, ATT_GROUP), F32), pltpu.VMEM((s, ATT_GROUP), F32)],
        compiler_params=_params(2),
    )(proj0, proj0, proj0, o, do)


GATE0_COL = (D_POOL + 3 * D_SB) // D_INNER


def _gate_fwd0(name, yp_raw, o, proj0, ps):
    s = o.shape[0]

    def body(yp_ref, o_ref, gt_ref, ps_ref, y_ref):
        gt = gt_ref[...]
        sg = gt * _sigmoid(gt)
        y_ref[:, :D_POOL] = (yp_ref[...] * ps_ref[...] * sg[:, :D_POOL]).astype(BF16)
        y_ref[:, D_POOL:] = (o_ref[...] * sg[:, D_POOL:]).astype(BF16)

    return pl.pallas_call(
        body, name=name, grid=(s // ROW_TILE,),
        in_specs=[_row_spec(), _row_spec(), _row_spec(D_INNER, GATE0_COL), _vec_spec()],
        out_specs=_row_spec(D_INNER),
        out_shape=SDS((s, D_INNER), BF16), compiler_params=_params(1),
    )(yp_raw, o, proj0, ps)


def _dsilu(x):
    sg = _sigmoid(x)
    return sg * (1.0 + x * (1.0 - sg))


def _gate_bwd0(name, dymix, yp_raw, o, proj0, ps):
    s = o.shape[0]

    def body(dy_ref, yp_ref, o_ref, gt_ref, ps_ref, dyp_ref, do_ref, dgt_ref, dps_ref):
        i = pl.program_id(0)

        @pl.when(i == 0)
        def _():
            dps_ref[...] = jnp.zeros_like(dps_ref)

        gt = gt_ref[...]
        dy = dy_ref[...]
        sg = gt * _sigmoid(gt)
        dsg = _dsilu(gt)
        dcat = dy * sg
        yp = yp_ref[...]
        ps_v = ps_ref[...]
        dyp_ref[...] = (dcat[:, :D_POOL] * ps_v).astype(BF16)
        do_ref[...] = dcat[:, D_POOL:]
        dps_ref[...] += jnp.sum(dcat[:, :D_POOL] * yp, axis=0, keepdims=True)
        dgt_ref[:, :D_POOL] = (dy[:, :D_POOL] * (yp * ps_v) * dsg[:, :D_POOL]).astype(BF16)
        dgt_ref[:, D_POOL:] = (dy[:, D_POOL:] * o_ref[...] * dsg[:, D_POOL:]).astype(BF16)

    return pl.pallas_call(
        body, name=name, grid=(s // ROW_TILE,),
        in_specs=[_row_spec(D_INNER), _row_spec(), _row_spec(), _row_spec(D_INNER, GATE0_COL), _vec_spec()],
        out_specs=[_row_spec(), _row_spec(), _row_spec(D_INNER), _vec_spec()],
        out_shape=[SDS((s, D_POOL), BF16), SDS((s, D_SB), F32), SDS((s, D_INNER), BF16), SDS((1, D_POOL), F32)],
        compiler_params=_params(1),
    )(dymix, yp_raw, o, proj0, ps)


CONV_COLS = 512
N_CONV_BLK = D_INNER // CONV_COLS
CONV_HALO = 8


def _conv_fwd(name, proj1, cw, cb):
    s = proj1.shape[0]
    hb = ROW_TILE // CONV_HALO
    ext_rows = ROW_TILE + CONV_HALO

    def body(gb_ref, gc_ref, u_ref, gt_ref, gch_ref, uh_ref, cw_ref, cb_ref, y_ref):
        i = pl.program_id(1)
        uc = gc_ref[...] * u_ref[...]
        halo = jnp.where(i == 0, 0.0, gch_ref[...] * uh_ref[...])
        ext = jnp.concatenate([halo, uc], axis=0)
        uc1 = pltpu.roll(ext, 1, axis=0)[CONV_HALO:, :]
        uc2 = pltpu.roll(ext, 2, axis=0)[CONV_HALO:, :]
        cw_v = cw_ref[...]
        conv = cb_ref[...] + cw_v[0:1, :] * uc2 + cw_v[1:2, :] * uc1 + cw_v[2:3, :] * uc
        gt = gt_ref[...]
        y_ref[...] = (gb_ref[...] * conv * (gt * _sigmoid(gt))).astype(BF16)

    def tile(part):
        return pl.BlockSpec((ROW_TILE, CONV_COLS), lambda b, i: (i, part * N_CONV_BLK + b))

    def halo(part):
        return pl.BlockSpec((CONV_HALO, CONV_COLS), lambda b, i: (jnp.maximum(i * hb - 1, 0), part * N_CONV_BLK + b))

    return pl.pallas_call(
        body, name=name, grid=(N_CONV_BLK, s // ROW_TILE),
        in_specs=[tile(0), tile(1), tile(2), tile(3), halo(1), halo(2),
                  pl.BlockSpec((3, CONV_COLS), lambda b, i: (0, b)), pl.BlockSpec((1, CONV_COLS), lambda b, i: (0, b))],
        out_specs=pl.BlockSpec((ROW_TILE, CONV_COLS), lambda b, i: (i, b)),
        out_shape=SDS((s, D_INNER), BF16), compiler_params=_params(2),
    )(proj1, proj1, proj1, proj1, proj1, proj1, cw, cb)


def _conv_bwd(name, dymix, proj1, cw, cb):
    s = proj1.shape[0]
    hb = ROW_TILE // CONV_HALO
    n_hb = s // CONV_HALO
    n_tiles = s // ROW_TILE
    ext_rows = ROW_TILE + CONV_HALO

    def body(dy_ref, gb_ref, gc_ref, u_ref, gt_ref, gch_ref, uh_ref, dyn_ref, gbn_ref, gtn_ref, cw_ref, cb_ref,
             dgb_ref, dgc_ref, du_ref, dgt_ref, dcw_ref, dcb_ref):
        i = pl.program_id(1)

        @pl.when(i == 0)
        def _():
            dcw_ref[...] = jnp.zeros_like(dcw_ref)
            dcb_ref[...] = jnp.zeros_like(dcb_ref)

        gc = gc_ref[...]
        u = u_ref[...]
        gb = gb_ref[...]
        gt = gt_ref[...]
        dy = dy_ref[...]
        uc = gc * u
        halo = jnp.where(i == 0, 0.0, gch_ref[...] * uh_ref[...])
        ext = jnp.concatenate([halo, uc], axis=0)
        uc1 = pltpu.roll(ext, 1, axis=0)[CONV_HALO:, :]
        uc2 = pltpu.roll(ext, 2, axis=0)[CONV_HALO:, :]
        cw_v = cw_ref[...]
        w0, w1, w2 = cw_v[0:1, :], cw_v[1:2, :], cw_v[2:3, :]
        conv = cb_ref[...] + w0 * uc2 + w1 * uc1 + w2 * uc
        sg = gt * _sigmoid(gt)
        dconv = dy * gb * sg
        gtn = gtn_ref[...]
        dconv_next = jnp.where(i == n_tiles - 1, 0.0, dyn_ref[...] * gbn_ref[...] * (gtn * _sigmoid(gtn)))
        dext = jnp.concatenate([dconv, dconv_next], axis=0)
        dconv_p1 = pltpu.roll(dext, ext_rows - 1, axis=0)[:ROW_TILE, :]
        dconv_p2 = pltpu.roll(dext, ext_rows - 2, axis=0)[:ROW_TILE, :]
        duc = w2 * dconv + w1 * dconv_p1 + w0 * dconv_p2
        dgb_ref[...] = (dy * conv * sg).astype(BF16)
        dgc_ref[...] = (duc * u).astype(BF16)
        du_ref[...] = (duc * gc).astype(BF16)
        dgt_ref[...] = (dy * gb * conv * _dsilu(gt)).astype(BF16)
        dcw_ref[0:1, :] += jnp.sum(dconv * uc2, axis=0, keepdims=True)
        dcw_ref[1:2, :] += jnp.sum(dconv * uc1, axis=0, keepdims=True)
        dcw_ref[2:3, :] += jnp.sum(dconv * uc, axis=0, keepdims=True)
        dcb_ref[...] += jnp.sum(dconv, axis=0, keepdims=True)

    def tile(part):
        return pl.BlockSpec((ROW_TILE, CONV_COLS), lambda b, i: (i, part * N_CONV_BLK + b))

    def prev(part):
        return pl.BlockSpec((CONV_HALO, CONV_COLS), lambda b, i: (jnp.maximum(i * hb - 1, 0), part * N_CONV_BLK + b))

    def nxt(part):
        return pl.BlockSpec((CONV_HALO, CONV_COLS), lambda b, i: (jnp.minimum((i + 1) * hb, n_hb - 1), part * N_CONV_BLK + b))

    out_tile = pl.BlockSpec((ROW_TILE, CONV_COLS), lambda b, i: (i, b))
    return pl.pallas_call(
        body, name=name, grid=(N_CONV_BLK, n_tiles),
        in_specs=[tile(0), tile(0), tile(1), tile(2), tile(3), prev(1), prev(2), nxt(0), nxt(0), nxt(3),
                  pl.BlockSpec((3, CONV_COLS), lambda b, i: (0, b)), pl.BlockSpec((1, CONV_COLS), lambda b, i: (0, b))],
        out_specs=[out_tile, out_tile, out_tile, out_tile,
                   pl.BlockSpec((3, CONV_COLS), lambda b, i: (0, b)), pl.BlockSpec((1, CONV_COLS), lambda b, i: (0, b))],
        out_shape=[SDS((s, D_INNER), BF16)] * 4 + [SDS((3, D_INNER), F32), SDS((1, D_INNER), F32)],
        compiler_params=_params(2),
    )(dymix, proj1, proj1, proj1, proj1, proj1, proj1, dymix, proj1, proj1, cw, cb)


def _place():
    x, y, c = lax.axis_index("x"), lax.axis_index("y"), lax.axis_index("c")
    return x, y, c


def _flip(x, y, c, k):
    fx, fy, fc = (k >> 2) & 1, (k >> 1) & 1, k & 1
    return (1 - x if fx else x, 1 - y if fy else y, 1 - c if fc else c)


def _dev_index(p):
    return 4 * p[0] + 2 * p[1] + p[2]


HBM_SPEC = pl.BlockSpec(memory_space=pltpu.HBM)
VMEM_SPEC = pl.BlockSpec(memory_space=pltpu.VMEM)


def _allgather_weights(shards):
    n_w = len(shards)

    def body(*refs):
        ins, outs = refs[:n_w], refs[n_w:2 * n_w]
        send_sems, recv_sems, local_sems = refs[2 * n_w:]
        x, y, c = _place()
        me, sibling = (x, y, c), (x, y, 1 - c)
        chips = [(1 - x, y), (x, 1 - y), (1 - x, 1 - y)]

        def copy(w, k, block, to, src=None):
            rows = outs[w].at[_dev_index(block)]
            return pltpu.make_async_remote_copy(
                src_ref=rows if src is None else src, dst_ref=rows,
                send_sem=send_sems.at[7 * w + k], recv_sem=recv_sems.at[7 * w + k],
                device_id=to, device_id_type=MESH)

        mine, first, passed = [], [], []
        for w in range(n_w):
            cp = pltpu.make_async_copy(ins[w], outs[w].at[_dev_index(me)], local_sems.at[w])
            cp.start()
            mine.append(cp)
            fw = [copy(w, 0, me, sibling, src=ins[w])]
            fw += [copy(w, 1 + j, me, (*chip, c), src=ins[w]) for j, chip in enumerate(chips)]
            for cp in fw:
                cp.start()
            first += fw
        for w in range(n_w):
            for j, chip in enumerate(chips):
                copy(w, 1 + j, (*chip, c), me).wait_recv()
                cp = copy(w, 4 + j, (*chip, c), sibling)
                cp.start()
                passed.append(cp)
        for w in range(n_w):
            copy(w, 0, sibling, me).wait_recv()
            for j, chip in enumerate(chips):
                copy(w, 4 + j, (*chip, 1 - c), me).wait_recv()
        for cp in first + passed:
            cp.wait_send()
        for cp in mine:
            cp.wait()

    return pl.pallas_call(
        body, name="allgather_weights",
        out_shape=[SDS((N_DEV,) + sh.shape, sh.dtype) for sh in shards],
        in_specs=[HBM_SPEC] * n_w, out_specs=[HBM_SPEC] * n_w,
        scratch_shapes=[pltpu.SemaphoreType.DMA((7 * n_w,)), pltpu.SemaphoreType.DMA((7 * n_w,)),
                        pltpu.SemaphoreType.DMA((n_w,))],
    )(*shards)


def _exchange_wgrads(dws):
    n_w = len(dws)

    def body(*refs):
        ins, outs = refs[:n_w], refs[n_w:2 * n_w]
        send_sems, recv_sems, local_sems = refs[2 * n_w:]
        x, y, c = _place()
        me = (x, y, c)
        my = _dev_index(me)
        started = []
        for w in range(n_w):
            cp = pltpu.make_async_copy(ins[w].at[my], outs[w].at[my], local_sems.at[w])
            cp.start()
            started.append(cp)
            for k in range(1, N_DEV):
                peer = _flip(x, y, c, k)
                cp = pltpu.make_async_remote_copy(
                    src_ref=ins[w].at[_dev_index(peer)], dst_ref=outs[w].at[my],
                    send_sem=send_sems.at[7 * w + k - 1], recv_sem=recv_sems.at[7 * w + k - 1],
                    device_id=peer, device_id_type=MESH)
                cp.start()
                started.append(cp)
        for cp in started:
            cp.wait()

    return pl.pallas_call(
        body, name="exchange_wgrads",
        out_shape=[SDS(dw.shape, dw.dtype) for dw in dws],
        in_specs=[HBM_SPEC] * n_w, out_specs=[HBM_SPEC] * n_w,
        scratch_shapes=[pltpu.SemaphoreType.DMA((7 * n_w,)), pltpu.SemaphoreType.DMA((7 * n_w,)),
                        pltpu.SemaphoreType.DMA((n_w,))],
    )(*dws)


ADA_COLS = 3 * D_MODEL // N_DEV


def _ada_forward(c_row, conv_w, conv_b, ada_w, ada_b):
    cw_cols = conv_w.shape[1]

    def body(c_ref, cw_ref, cb_ref, aw_ref, ab_ref, m_ref, cs_ref, cwf_ref, cbf_ref,
             slab, gath, part, land, send_sems, recv_sems):
        x, y, c = _place()
        my = _dev_index((x, y, c))
        slab[...] = jnp.zeros_like(slab)
        slab[0:1, :] = c_ref[...]
        slab[1:4, 0:cw_cols] = cw_ref[...]
        slab[4:5, 0:cw_cols] = cb_ref[...]
        gath[my] = slab[...]
        sends = []
        for k in range(1, N_DEV):
            peer = _flip(x, y, c, k)
            cp = pltpu.make_async_remote_copy(
                src_ref=slab, dst_ref=gath.at[my], send_sem=send_sems.at[k - 1], recv_sem=recv_sems.at[k - 1],
                device_id=peer, device_id_type=MESH)
            cp.start()
            sends.append(cp)
        for cp in sends:
            cp.wait()
        for d in range(N_DEV):
            c_d = gath[d, 0:1, :]
            cs_ref[d:d + 1, :] = c_d * _sigmoid(c_d)
            cwf_ref[:, d * cw_cols:(d + 1) * cw_cols] = gath[d, 1:4, 0:cw_cols]
            cbf_ref[:, d * cw_cols:(d + 1) * cw_cols] = gath[d, 4:5, 0:cw_cols]
        cs = cs_ref[...]
        part[...] = jnp.zeros_like(part)
        for layer in range(2):
            m_part = jnp.dot(cs, aw_ref[layer], preferred_element_type=F32, precision=lax.Precision.HIGHEST)
            for d in range(N_DEV):
                part[d, layer:layer + 1, :] = m_part[d:d + 1, :]
        land[my] = part[my]
        sends = []
        for k in range(1, N_DEV):
            peer = _flip(x, y, c, k)
            cp = pltpu.make_async_remote_copy(
                src_ref=part.at[_dev_index(peer)], dst_ref=land.at[my],
                send_sem=send_sems.at[6 + k], recv_sem=recv_sems.at[6 + k],
                device_id=peer, device_id_type=MESH)
            cp.start()
            sends.append(cp)
        for cp in sends:
            cp.wait()
        for d in range(N_DEV):
            cols = slice(d * ADA_COLS, (d + 1) * ADA_COLS)
            m_ref[:, cols] = land[d, 0:2, :] + ab_ref[:, cols]

    return pl.pallas_call(
        body, name="ada_forward",
        out_shape=[SDS((2, 3 * D_MODEL), F32), SDS((N_DEV, D_MODEL), F32), SDS((3, N_DEV * cw_cols), F32),
                   SDS((1, N_DEV * cw_cols), F32)],
        in_specs=[VMEM_SPEC] * 5, out_specs=[VMEM_SPEC] * 4,
        scratch_shapes=[pltpu.VMEM((8, D_MODEL), F32), pltpu.VMEM((N_DEV, 8, D_MODEL), F32),
                        pltpu.VMEM((N_DEV, 8, ADA_COLS), F32), pltpu.VMEM((N_DEV, 8, ADA_COLS), F32),
                        pltpu.SemaphoreType.DMA((14,)), pltpu.SemaphoreType.DMA((14,))],
        compiler_params=pltpu.CompilerParams(vmem_limit_bytes=VMEM_LIMIT),
    )(c_row, conv_w, conv_b, ada_w, ada_b)


def _small_grads(slab):
    def body(slab_ref, gath_ref, tot_ref, send_sems, recv_sems):
        x, y, c = _place()
        my = _dev_index((x, y, c))
        gath_ref[my] = slab_ref[...]
        sends = []
        for k in range(1, N_DEV):
            peer = _flip(x, y, c, k)
            cp = pltpu.make_async_remote_copy(
                src_ref=slab_ref, dst_ref=gath_ref.at[my], send_sem=send_sems.at[k - 1], recv_sem=recv_sems.at[k - 1],
                device_id=peer, device_id_type=MESH)
            cp.start()
            sends.append(cp)
        for cp in sends:
            cp.wait()
        tot = gath_ref[0]
        for d in range(1, N_DEV):
            tot = tot + gath_ref[d]
        tot_ref[...] = tot

    return pl.pallas_call(
        body, name="small_grads",
        out_shape=[SDS((N_DEV, SLAB_ROWS, D_MODEL), F32), SDS((SLAB_ROWS, D_MODEL), F32)],
        in_specs=[VMEM_SPEC], out_specs=[VMEM_SPEC] * 2,
        scratch_shapes=[pltpu.SemaphoreType.DMA((7,)), pltpu.SemaphoreType.DMA((7,))],
    )(slab)


def _adamw_math(w, g, m, v):
    m = ADAM_B1 * m + (1.0 - ADAM_B1) * g
    v = ADAM_B2 * v + (1.0 - ADAM_B2) * jnp.square(g)
    m_hat = m / (1.0 - ADAM_B1 ** ADAM_STEP)
    v_hat = v / (1.0 - ADAM_B2 ** ADAM_STEP)
    delta = -ADAM_LR * (m_hat / (jnp.sqrt(v_hat) + ADAM_EPS) + ADAM_WD * w)
    return delta, m, v


def _sum_adamw(name, recv, w, m, v):
    rows, cols = w.shape
    tr = min(rows, 256)

    def body(r_ref, w_ref, m_ref, v_ref, g_ref, d_ref, nm_ref, nv_ref):
        g = r_ref[0].astype(F32)
        for d in range(1, N_DEV):
            g = g + r_ref[d].astype(F32)
        g_ref[...] = g
        d_ref[...], nm_ref[...], nv_ref[...] = _adamw_math(w_ref[...], g, m_ref[...], v_ref[...])

    blk = pl.BlockSpec((tr, cols), lambda i: (i, 0))
    return pl.pallas_call(
        body, name=name, grid=(rows // tr,),
        in_specs=[pl.BlockSpec((N_DEV, tr, cols), lambda i: (0, i, 0)), blk, blk, blk],
        out_specs=[blk] * 4, out_shape=[SDS((rows, cols), F32)] * 4, compiler_params=_params(1),
    )(recv, w, m, v)


def _ada_w_adamw(name, cs_t, dm_cols, w, m, v):
    def body(cs_ref, dm_ref, w_ref, m_ref, v_ref, g_ref, d_ref, nm_ref, nv_ref):
        cs = cs_ref[...]
        dm = dm_ref[...]
        g = cs[:, 0:1] * dm[0:1, :]
        for b in range(1, N_DEV):
            g = g + cs[:, b:b + 1] * dm[b:b + 1, :]
        g_ref[...] = g
        d_ref[...], nm_ref[...], nv_ref[...] = _adamw_math(w_ref[...], g, m_ref[...], v_ref[...])

    blk = pl.BlockSpec((None, D_MODEL, ADA_COLS), lambda l: (l, 0, 0))
    return pl.pallas_call(
        body, name=name, grid=(2,),
        in_specs=[pl.BlockSpec((D_MODEL, N_DEV), lambda l: (0, 0)),
                  pl.BlockSpec((None, N_DEV, ADA_COLS), lambda l: (l, 0, 0)), blk, blk, blk],
        out_specs=[blk] * 4, out_shape=[SDS((2, D_MODEL, ADA_COLS), F32)] * 4, compiler_params=_params(1),
    )(cs_t, dm_cols, w, m, v)


def _small_adamw(name, triples):
    n = len(triples)

    def body(*refs):
        ins, outs = refs[:4 * n], refs[4 * n:]
        for j in range(n):
            w_ref, g_ref, m_ref, v_ref = ins[4 * j:4 * j + 4]
            d, nm, nv = _adamw_math(w_ref[...], g_ref[...], m_ref[...], v_ref[...])
            outs[3 * j][...] = d
            outs[3 * j + 1][...] = nm
            outs[3 * j + 2][...] = nv

    flat = [a for t in triples for a in t]
    return pl.pallas_call(
        body, name=name,
        out_shape=[SDS(t[0].shape, F32) for t in triples for _ in range(3)],
        in_specs=[VMEM_SPEC] * (4 * n), out_specs=[VMEM_SPEC] * (3 * n),
    )(*flat)


def kernel(x, c, norm_g, ada_w, ada_b, even_w_in, pool_w, pool_scale, even_w_out, odd_w_in, conv_w, conv_b, odd_w_out, final_g, loss_target, m_norm_g, m_ada_w, m_ada_b, m_even_w_in, m_pool_w, m_pool_scale, m_even_w_out, m_odd_w_in, m_conv_w, m_conv_b, m_odd_w_out, m_final_g, v_norm_g, v_ada_w, v_ada_b, v_even_w_in, v_pool_w, v_pool_scale, v_even_w_out, v_odd_w_in, v_conv_w, v_conv_b, v_odd_w_out, v_final_g):
    seq = x.shape[1]
    x0 = x[0]
    target = loss_target[0]
    final_g2 = final_g.reshape(1, D_MODEL)

    w_in_e = even_w_in[0]
    w_out_e = even_w_out[0]
    w_in_o = odd_w_in[0]
    w_out_o = odd_w_out[0]
    w_pool = pool_w[0].reshape(N_GROUPS * 32, POOL_GROUP)
    shards = [w.astype(BF16) for w in (w_in_e, w_out_e, w_in_o, w_out_o, w_pool)]
    wg_in_e, wg_out_e, wg_in_o, wg_out_o, wg_pool = _allgather_weights(shards)
    wf_out_e = wg_out_e.reshape(D_INNER, D_MODEL)
    wf_out_o = wg_out_o.reshape(D_INNER, D_MODEL)
    wf_pool = wg_pool.reshape(N_DEV, N_GROUPS, 32, POOL_GROUP).transpose(1, 0, 2, 3).reshape(N_GROUPS, POOL_GROUP, POOL_GROUP)

    m_vec, cs_all, conv_w_full, conv_b_full = _ada_forward(c, conv_w[0], conv_b, ada_w, ada_b)
    shift = [m_vec[l:l + 1, 0:D_MODEL] for l in range(2)]
    scale = [m_vec[l:l + 1, D_MODEL:2 * D_MODEL] for l in range(2)]
    gate = [m_vec[l:l + 1, 2 * D_MODEL:] for l in range(2)]
    ng = [norm_g[l:l + 1] for l in range(2)]

    h0 = _ln_mod("ln_mod0", x0, ng[0], scale[0], shift[0])
    proj0, proj0_b = _proj_in_dual("proj_in0", h0, wg_in_e)
    p = _pool_fwd("pool_fwd", proj0)
    yp_raw = _pool_mm("pool_mix", p, wf_pool, _dot)
    o = _attn_fwd("attn_fwd", proj0_b)
    ymix0 = _gate_fwd0("gate_fwd0", yp_raw, o, proj0, pool_scale)
    yo0 = _proj_out("proj_out0", ymix0, wf_out_e)

    x1, h1 = _resid_ln_mod("resid_ln_mod1", x0, yo0, gate[0], ng[1], scale[1], shift[1])
    proj1 = _proj_in("proj_in1", h1, wg_in_o)
    ymix1 = _conv_fwd("conv_fwd", proj1, conv_w_full, conv_b_full)
    yo1 = _proj_out("proj_out1", ymix1, wf_out_o)

    dx2, dyo1, loss_acc, d_final_g, d_gate1 = _final_loss("final_loss", x1, yo1, gate[1], final_g2, target)
    loss = lax.psum(loss_acc[0, 0], ("x", "y", "c"))

    dymix1 = _proj_out_bwd("proj_out1_bwd", dyo1, wf_out_o)
    dw_out_o = _wgrad_out("wgrad_out1", ymix1, dyo1)
    dgb, dgc, du, dgt1, d_conv_w, d_conv_b = _conv_bwd("conv_bwd", dymix1, proj1, conv_w_full, conv_b_full)
    dproj1 = jnp.concatenate([dgb, dgc, du, dgt1], axis=1)
    dh1 = _proj_in_bwd("proj_in1_bwd", dproj1, wg_in_o)
    dw_in_o = _wgrad_in("wgrad_in1", h1, dproj1)
    dx1, d_shift1, d_scale1, d_ng1 = _ln_mod_bwd("ln_mod1_bwd", dh1, x1, dx2, ng[1], scale[1])

    dyo0, d_gate0 = _resid_bwd("resid0_bwd", dx1, yo0, gate[0])
    dymix0 = _proj_out_bwd("proj_out0_bwd", dyo0, wf_out_e)
    dw_out_e = _wgrad_out("wgrad_out0", ymix0, dyo0)
    dyp, do, dgt0, d_pool_scale = _gate_bwd0("gate_bwd0", dymix0, yp_raw, o, proj0, pool_scale)
    dp = _pool_mm("pool_mix_bwd", dyp, wf_pool, _dot_nt)
    dw_pool = _pool_wgrad("pool_wgrad", p, dyp)
    du_pool = _pool_bwd("pool_bwd", dp)
    dq, dk, dv = _attn_bwd("attn_bwd", proj0_b, o, do)
    dproj0 = jnp.concatenate([du_pool, dq, dk, dv, dgt0], axis=1)
    dh0 = _proj_in_bwd("proj_in0_bwd", dproj0, wg_in_e)
    dw_in_e = _wgrad_in("wgrad_in0", h0, dproj0)
    dx0, d_shift0, d_scale0, d_ng0 = _ln_mod_bwd("ln_mod0_bwd", dh0, x0, dx1, ng[0], scale[0])
    grad_x = dx0[None]

    dw_pool_c = dw_pool.reshape(N_GROUPS, N_DEV, 32, POOL_GROUP).transpose(1, 0, 2, 3).reshape(N_DEV, N_GROUPS * 32, POOL_GROUP).astype(BF16)
    dws = [dw_in_e, dw_out_e.reshape(N_DEV, D_INNER // N_DEV, D_MODEL), dw_in_o,
           dw_out_o.reshape(N_DEV, D_INNER // N_DEV, D_MODEL), dw_pool_c]
    r_in_e, r_out_e, r_in_o, r_out_o, r_pool = _exchange_wgrads(dws)
    big = {}
    big["even_w_in"] = _sum_adamw("adamw_even_w_in", r_in_e, w_in_e, m_even_w_in[0], v_even_w_in[0])
    big["even_w_out"] = _sum_adamw("adamw_even_w_out", r_out_e, w_out_e, m_even_w_out[0], v_even_w_out[0])
    big["odd_w_in"] = _sum_adamw("adamw_odd_w_in", r_in_o, w_in_o, m_odd_w_in[0], v_odd_w_in[0])
    big["odd_w_out"] = _sum_adamw("adamw_odd_w_out", r_out_o, w_out_o, m_odd_w_out[0], v_odd_w_out[0])
    big["pool_w"] = _sum_adamw("adamw_pool_w", r_pool, w_pool, m_pool_w[0].reshape(N_GROUPS * 32, POOL_GROUP),
                               v_pool_w[0].reshape(N_GROUPS * 32, POOL_GROUP))
    big = {k: [a.reshape(shape) for a in v] for (k, v), shape in zip(
        big.items(), [even_w_in.shape, even_w_out.shape, odd_w_in.shape, odd_w_out.shape, pool_w.shape])}

    dm = jnp.concatenate([jnp.concatenate([d_shift0, d_scale0, d_gate0], axis=1),
                          jnp.concatenate([d_shift1, d_scale1, d_gate1], axis=1)], axis=0)
    slab = jnp.zeros((SLAB_ROWS, D_MODEL), F32)
    slab = slab.at[0:6].set(dm.reshape(6, D_MODEL))
    slab = slab.at[8:9].set(d_ng0).at[9:10].set(d_ng1).at[10:11].set(d_pool_scale).at[11:12].set(d_final_g)
    slab = slab.at[16:22].set(d_conv_w.reshape(6, D_MODEL)).at[24:26].set(d_conv_b.reshape(2, D_MODEL))
    gathered, total = _small_grads(slab)
    my = 4 * lax.axis_index("x") + 2 * lax.axis_index("y") + lax.axis_index("c")
    g_ada_b = total[0:6].reshape(2, 3 * D_MODEL)
    g_norm_g = total[8:10]
    g_pool_scale = total[10:11]
    g_final_g = total[11:12]
    cw_cols = conv_w.shape[2]
    g_conv_w = lax.dynamic_slice_in_dim(total[16:22].reshape(3, D_INNER), my * cw_cols, cw_cols, axis=1)
    g_conv_b = lax.dynamic_slice_in_dim(total[24:26].reshape(1, D_INNER), my * cw_cols, cw_cols, axis=1)
    dm_all = gathered[:, 0:6, :].reshape(N_DEV, 2, 3 * D_MODEL)
    dm_cols = lax.dynamic_slice_in_dim(dm_all, my * ADA_COLS, ADA_COLS, axis=2).transpose(1, 0, 2)
    ada = _ada_w_adamw("adamw_ada_w", cs_all.T, dm_cols, ada_w, m_ada_w, v_ada_w)

    small = _small_adamw("adamw_small", [
        (norm_g, g_norm_g, m_norm_g, v_norm_g),
        (ada_b, g_ada_b, m_ada_b, v_ada_b),
        (pool_scale, g_pool_scale, m_pool_scale, v_pool_scale),
        (conv_w[0], g_conv_w, m_conv_w[0], v_conv_w[0]),
        (conv_b, g_conv_b, m_conv_b, v_conv_b),
        (final_g2, g_final_g, m_final_g.reshape(1, D_MODEL), v_final_g.reshape(1, D_MODEL)),
    ])
    small = [small[3 * j:3 * j + 3] for j in range(6)]

    grads = {
        "norm_g": g_norm_g, "ada_w": ada[0], "ada_b": g_ada_b, "even_w_in": big["even_w_in"][0],
        "pool_w": big["pool_w"][0], "pool_scale": g_pool_scale, "even_w_out": big["even_w_out"][0],
        "odd_w_in": big["odd_w_in"][0], "conv_w": g_conv_w.reshape(conv_w.shape), "conv_b": g_conv_b,
        "odd_w_out": big["odd_w_out"][0], "final_g": g_final_g.reshape(D_MODEL),
    }
    rest = []
    for idx in range(3):
        rest += [
            small[0][idx], ada[1 + idx], small[1][idx], big["even_w_in"][1 + idx], big["pool_w"][1 + idx],
            small[2][idx], big["even_w_out"][1 + idx], big["odd_w_in"][1 + idx],
            small[3][idx].reshape(conv_w.shape), small[4][idx], big["odd_w_out"][1 + idx],
            small[5][idx].reshape(D_MODEL),
        ]
    order = ["norm_g", "ada_w", "ada_b", "even_w_in", "pool_w", "pool_scale", "even_w_out", "odd_w_in",
             "conv_w", "conv_b", "odd_w_out", "final_g"]
    return (loss, grad_x, *[grads[n] for n in order], *rest)
```

```python
import jax
import jax.numpy as jnp
from jax import lax
from jax.experimental import pallas as pl
from jax.experimental.pallas import tpu as pltpu

F32 = jnp.float32
BF16 = jnp.bfloat16
SDS = jax.ShapeDtypeStruct
MESH = pl.DeviceIdType.MESH

N_DEV = 8
D_MODEL = 1024
D_INNER = 2048
D_POOL = 1024
D_SB = 1024
N_GROUPS = 4
POOL_GROUP = 256
HEAD_DIM = 64
LANES = 128
D_IN_EVEN = 6144
D_IN_ODD = 8192
EPS = 1e-6
ADAM_LR = 0.001
ADAM_B1 = 0.9
ADAM_B2 = 0.999
ADAM_EPS = 1e-08
ADAM_WD = 0.01
ADAM_STEP = 10

ROW_TILE = 256
ATT_TILE = 256
HALO = 16
VMEM_LIMIT = 48 * 1024 * 1024
SLAB_ROWS = 32


def _params(n_axes):
    return pltpu.CompilerParams(dimension_semantics=("arbitrary",) * n_axes, vmem_limit_bytes=VMEM_LIMIT)


def _sigmoid(x):
    return 1.0 / (1.0 + jnp.exp(-x))


def _softplus(z):
    return jnp.maximum(z, 0.0) + jnp.log1p(jnp.exp(-jnp.abs(z)))


def _split_bf16(x):
    hi = x.astype(BF16)
    lo = (x - hi.astype(F32)).astype(BF16)
    return hi, lo


def _dot(a, b):
    return jnp.dot(a, b, preferred_element_type=F32)


def _dot_nt(a, b):
    return lax.dot_general(a, b, (((1,), (1,)), ((), ())), preferred_element_type=F32)


def _dot_tn(a, b):
    return lax.dot_general(a, b, (((0,), (0,)), ((), ())), preferred_element_type=F32)


def _mm(name, a, b, *, grid, a_spec, b_spec, o_spec, o_shape, o_dtype, dot, acc_axis=None, acc_shape=None):
    n_acc = grid[acc_axis] if acc_axis is not None else 1

    def body(a_ref, b_ref, o_ref, *scratch):
        prod = dot(a_ref[...], b_ref[...])
        if acc_axis is None:
            o_ref[...] = prod.astype(o_dtype)
        else:
            acc = scratch[0]
            k = pl.program_id(acc_axis)

            @pl.when(k == 0)
            def _():
                acc[...] = prod

            @pl.when(k > 0)
            def _():
                acc[...] += prod

            @pl.when(k == n_acc - 1)
            def _():
                o_ref[...] = acc[...].astype(o_dtype)

    scratch = [] if acc_axis is None else [pltpu.VMEM(acc_shape, F32)]
    return pl.pallas_call(
        body, name=name, grid=grid, in_specs=[a_spec, b_spec], out_specs=o_spec,
        out_shape=SDS(o_shape, o_dtype), scratch_shapes=scratch, compiler_params=_params(len(grid)),
    )(a, b)


def _proj_in(name, h, wg):
    s = h.shape[0]
    cn = wg.shape[2]
    tm = min(s, 512)
    return _mm(name, h, wg, grid=(s // tm, N_DEV),
               a_spec=pl.BlockSpec((tm, D_MODEL), lambda i, d: (i, 0)),
               b_spec=pl.BlockSpec((None, D_MODEL, cn), lambda i, d: (d, 0, 0)),
               o_spec=pl.BlockSpec((tm, cn), lambda i, d: (i, d)),
               o_shape=(s, N_DEV * cn), o_dtype=F32, dot=_dot)


def _proj_in_dual(name, h, wg):
    s = h.shape[0]
    cn = wg.shape[2]
    tm = min(s, 512)

    def body(a_ref, b_ref, o_ref, ob_ref):
        prod = _dot(a_ref[...], b_ref[...])
        o_ref[...] = prod
        ob_ref[...] = prod.astype(BF16)

    out = pl.BlockSpec((tm, cn), lambda i, d: (i, d))
    return pl.pallas_call(
        body, name=name, grid=(s // tm, N_DEV),
        in_specs=[pl.BlockSpec((tm, D_MODEL), lambda i, d: (i, 0)),
                  pl.BlockSpec((None, D_MODEL, cn), lambda i, d: (d, 0, 0))],
        out_specs=[out, out], out_shape=[SDS((s, N_DEV * cn), F32), SDS((s, N_DEV * cn), BF16)],
        compiler_params=_params(2),
    )(h, wg)


def _proj_out(name, y, w):
    s = y.shape[0]
    tm = min(s, 512)
    return _mm(name, y, w, grid=(s // tm,),
               a_spec=pl.BlockSpec((tm, D_INNER), lambda i: (i, 0)),
               b_spec=pl.BlockSpec((D_INNER, D_MODEL), lambda i: (0, 0)),
               o_spec=pl.BlockSpec((tm, D_MODEL), lambda i: (i, 0)),
               o_shape=(s, D_MODEL), o_dtype=F32, dot=_dot)


def _proj_out_bwd(name, dyo, w):
    s = dyo.shape[0]
    tm = min(s, 512)
    return _mm(name, dyo, w, grid=(s // tm,),
               a_spec=pl.BlockSpec((tm, D_MODEL), lambda i: (i, 0)),
               b_spec=pl.BlockSpec((D_INNER, D_MODEL), lambda i: (0, 0)),
               o_spec=pl.BlockSpec((tm, D_INNER), lambda i: (i, 0)),
               o_shape=(s, D_INNER), o_dtype=F32, dot=_dot_nt)


def _wgrad_out(name, y, dyo):
    s = y.shape[0]
    ts = min(s, 512)
    return _mm(name, y, dyo, grid=(s // ts,),
               a_spec=pl.BlockSpec((ts, D_INNER), lambda k: (k, 0)),
               b_spec=pl.BlockSpec((ts, D_MODEL), lambda k: (k, 0)),
               o_spec=pl.BlockSpec((D_INNER, D_MODEL), lambda k: (0, 0)),
               o_shape=(D_INNER, D_MODEL), o_dtype=BF16, dot=_dot_tn, acc_axis=0, acc_shape=(D_INNER, D_MODEL))


def _proj_in_bwd(name, dproj, wg):
    s = dproj.shape[0]
    cn = wg.shape[2]
    tm = min(s, 512)
    return _mm(name, dproj, wg, grid=(s // tm, N_DEV),
               a_spec=pl.BlockSpec((tm, cn), lambda i, d: (i, d)),
               b_spec=pl.BlockSpec((None, D_MODEL, cn), lambda i, d: (d, 0, 0)),
               o_spec=pl.BlockSpec((tm, D_MODEL), lambda i, d: (i, 0)),
               o_shape=(s, D_MODEL), o_dtype=F32, dot=_dot_nt, acc_axis=1, acc_shape=(tm, D_MODEL))


def _wgrad_in(name, h, dproj):
    s = h.shape[0]
    cn = dproj.shape[1] // N_DEV
    ts = min(s, 512)
    return _mm(name, h, dproj, grid=(N_DEV, s // ts),
               a_spec=pl.BlockSpec((ts, D_MODEL), lambda d, k: (k, 0)),
               b_spec=pl.BlockSpec((ts, cn), lambda d, k: (k, d)),
               o_spec=pl.BlockSpec((None, D_MODEL, cn), lambda d, k: (d, 0, 0)),
               o_shape=(N_DEV, D_MODEL, cn), o_dtype=BF16, dot=_dot_tn, acc_axis=1, acc_shape=(D_MODEL, cn))


def _pool_mm(name, p, wp, dot):
    s = p.shape[0]
    tm = min(s, 512)
    return _mm(name, p, wp, grid=(s // tm, N_GROUPS),
               a_spec=pl.BlockSpec((tm, POOL_GROUP), lambda i, g: (i, g)),
               b_spec=pl.BlockSpec((None, POOL_GROUP, POOL_GROUP), lambda i, g: (g, 0, 0)),
               o_spec=pl.BlockSpec((tm, POOL_GROUP), lambda i, g: (i, g)),
               o_shape=(s, D_POOL), o_dtype=F32, dot=dot)


def _pool_wgrad(name, p, dyp):
    s = p.shape[0]
    ts = min(s, 512)
    return _mm(name, p, dyp, grid=(N_GROUPS, s // ts),
               a_spec=pl.BlockSpec((ts, POOL_GROUP), lambda g, k: (k, g)),
               b_spec=pl.BlockSpec((ts, POOL_GROUP), lambda g, k: (k, g)),
               o_spec=pl.BlockSpec((None, POOL_GROUP, POOL_GROUP), lambda g, k: (g, 0, 0)),
               o_shape=(N_GROUPS, POOL_GROUP, POOL_GROUP), o_dtype=F32, dot=_dot_tn, acc_axis=1,
               acc_shape=(POOL_GROUP, POOL_GROUP))


def _vec_spec():
    return pl.BlockSpec((1, D_MODEL), lambda i: (0, 0))


def _row_spec(width=D_MODEL, col=0):
    return pl.BlockSpec((ROW_TILE, width), lambda i: (i, col))


def _ln_mod(name, x, g, scale, shift):
    s = x.shape[0]

    def body(x_ref, g_ref, sc_ref, sh_ref, h_ref):
        xv = x_ref[...]
        r = lax.rsqrt(jnp.mean(xv * xv, axis=-1, keepdims=True) + EPS)
        n = (xv * r) * g_ref[...]
        h_ref[...] = (n * (1.0 + sc_ref[...]) + sh_ref[...]).astype(BF16)

    return pl.pallas_call(
        body, name=name, grid=(s // ROW_TILE,),
        in_specs=[_row_spec(), _vec_spec(), _vec_spec(), _vec_spec()], out_specs=_row_spec(),
        out_shape=SDS((s, D_MODEL), BF16), compiler_params=_params(1),
    )(x, g, scale, shift)


def _resid_ln_mod(name, x, yo, gate, g, scale, shift):
    s = x.shape[0]

    def body(x_ref, yo_ref, gt_ref, g_ref, sc_ref, sh_ref, xn_ref, h_ref):
        xv = x_ref[...] + (1.0 + gt_ref[...]) * yo_ref[...]
        xn_ref[...] = xv
        r = lax.rsqrt(jnp.mean(xv * xv, axis=-1, keepdims=True) + EPS)
        n = (xv * r) * g_ref[...]
        h_ref[...] = (n * (1.0 + sc_ref[...]) + sh_ref[...]).astype(BF16)

    return pl.pallas_call(
        body, name=name, grid=(s // ROW_TILE,),
        in_specs=[_row_spec(), _row_spec(), _vec_spec(), _vec_spec(), _vec_spec(), _vec_spec()],
        out_specs=[_row_spec(), _row_spec()],
        out_shape=[SDS((s, D_MODEL), F32), SDS((s, D_MODEL), BF16)], compiler_params=_params(1),
    )(x, yo, gate, g, scale, shift)


def _final_loss(name, x1, yo1, gate1, gf, target):
    s = x1.shape[0]

    def body(x_ref, yo_ref, gt_ref, gf_ref, t_ref, dx_ref, dyo_ref, loss_ref, dgf_ref, dgt_ref):
        i = pl.program_id(0)

        @pl.when(i == 0)
        def _():
            loss_ref[...] = jnp.zeros_like(loss_ref)
            dgf_ref[...] = jnp.zeros_like(dgf_ref)
            dgt_ref[...] = jnp.zeros_like(dgt_ref)

        yo = yo_ref[...]
        one_gate = 1.0 + gt_ref[...]
        x2 = x_ref[...] + one_gate * yo
        r = lax.rsqrt(jnp.mean(x2 * x2, axis=-1, keepdims=True) + EPS)
        xn = x2 * r
        gf_v = gf_ref[...]
        err = xn * gf_v - t_ref[...]
        loss_ref[...] += 0.5 * jnp.sum(jnp.mean(err * err, axis=-1, keepdims=True))
        dout = err * (1.0 / D_MODEL)
        dgf_ref[...] += jnp.sum(dout * xn, axis=0, keepdims=True)
        dxn = dout * gf_v
        dx2 = r * (dxn - xn * jnp.mean(dxn * xn, axis=-1, keepdims=True))
        dx_ref[...] = dx2
        dyo_ref[...] = (dx2 * one_gate).astype(BF16)
        dgt_ref[...] += jnp.sum(dx2 * yo, axis=0, keepdims=True)

    return pl.pallas_call(
        body, name=name, grid=(s // ROW_TILE,),
        in_specs=[_row_spec(), _row_spec(), _vec_spec(), _vec_spec(), _row_spec()],
        out_specs=[_row_spec(), _row_spec(), pl.BlockSpec((1, LANES), lambda i: (0, 0)), _vec_spec(), _vec_spec()],
        out_shape=[SDS((s, D_MODEL), F32), SDS((s, D_MODEL), BF16), SDS((1, LANES), F32),
                   SDS((1, D_MODEL), F32), SDS((1, D_MODEL), F32)],
        compiler_params=_params(1),
    )(x1, yo1, gate1, gf, target)


def _ln_mod_bwd(name, dh, x, dx_next, g, scale):
    s = x.shape[0]

    def body(dh_ref, x_ref, dxn_ref, g_ref, sc_ref, dx_ref, dsh_ref, dsc_ref, dg_ref):
        i = pl.program_id(0)

        @pl.when(i == 0)
        def _():
            dsh_ref[...] = jnp.zeros_like(dsh_ref)
            dsc_ref[...] = jnp.zeros_like(dsc_ref)
            dg_ref[...] = jnp.zeros_like(dg_ref)

        dh_v = dh_ref[...]
        xv = x_ref[...]
        g_v = g_ref[...]
        r = lax.rsqrt(jnp.mean(xv * xv, axis=-1, keepdims=True) + EPS)
        xn = xv * r
        dsh_ref[...] += jnp.sum(dh_v, axis=0, keepdims=True)
        dsc_ref[...] += jnp.sum(dh_v * (xn * g_v), axis=0, keepdims=True)
        dn = dh_v * (1.0 + sc_ref[...])
        dg_ref[...] += jnp.sum(dn * xn, axis=0, keepdims=True)
        dxh = dn * g_v
        dx_ref[...] = dxn_ref[...] + r * (dxh - xn * jnp.mean(dxh * xn, axis=-1, keepdims=True))

    return pl.pallas_call(
        body, name=name, grid=(s // ROW_TILE,),
        in_specs=[_row_spec(), _row_spec(), _row_spec(), _vec_spec(), _vec_spec()],
        out_specs=[_row_spec(), _vec_spec(), _vec_spec(), _vec_spec()],
        out_shape=[SDS((s, D_MODEL), F32)] + [SDS((1, D_MODEL), F32)] * 3, compiler_params=_params(1),
    )(dh, x, dx_next, g, scale)


def _resid_bwd(name, dx, yo, gate):
    s = dx.shape[0]

    def body(dx_ref, yo_ref, gt_ref, dyo_ref, dgt_ref):
        i = pl.program_id(0)

        @pl.when(i == 0)
        def _():
            dgt_ref[...] = jnp.zeros_like(dgt_ref)

        dx_v = dx_ref[...]
        dyo_ref[...] = (dx_v * (1.0 + gt_ref[...])).astype(BF16)
        dgt_ref[...] += jnp.sum(dx_v * yo_ref[...], axis=0, keepdims=True)

    return pl.pallas_call(
        body, name=name, grid=(s // ROW_TILE,),
        in_specs=[_row_spec(), _row_spec(), _vec_spec()], out_specs=[_row_spec(), _vec_spec()],
        out_shape=[SDS((s, D_MODEL), BF16), SDS((1, D_MODEL), F32)], compiler_params=_params(1),
    )(dx, yo, gate)


def _window_of(g):
    return jnp.left_shift(2, g)


def _pool_fwd(name, proj0):
    s = proj0.shape[0]
    hb = ROW_TILE // HALO
    ext_rows = ROW_TILE + HALO

    def body(u_ref, halo_ref, p_ref):
        i = pl.program_id(0)
        g = pl.program_id(1)
        u = u_ref[...]
        halo = jnp.where(i == 0, 0.0, halo_ref[...])
        ext = jnp.concatenate([halo, u], axis=0)
        s2 = ext + pltpu.roll(ext, 1, axis=0)
        s4 = s2 + pltpu.roll(s2, 2, axis=0)
        s8 = s4 + pltpu.roll(s4, 4, axis=0)
        s16 = s8 + pltpu.roll(s8, 8, axis=0)
        win = jnp.where(g == 0, s2, jnp.where(g == 1, s4, jnp.where(g == 2, s8, s16)))[HALO:, :]
        t = i * ROW_TILE + lax.broadcasted_iota(jnp.int32, (ROW_TILE, 1), 0)
        cnt = jnp.minimum(t + 1, _window_of(g)).astype(F32)
        p_ref[...] = (win / cnt - u).astype(BF16)

    return pl.pallas_call(
        body, name=name, grid=(s // ROW_TILE, N_GROUPS),
        in_specs=[pl.BlockSpec((ROW_TILE, POOL_GROUP), lambda i, g: (i, g)),
                  pl.BlockSpec((HALO, POOL_GROUP), lambda i, g: (jnp.maximum(i * hb - 1, 0), g))],
        out_specs=pl.BlockSpec((ROW_TILE, POOL_GROUP), lambda i, g: (i, g)),
        out_shape=SDS((s, D_POOL), BF16), compiler_params=_params(2),
    )(proj0, proj0)


def _pool_bwd(name, dp):
    s = dp.shape[0]
    hb = ROW_TILE // HALO
    n_hb = s // HALO
    n_tiles = s // ROW_TILE
    ext_rows = ROW_TILE + HALO

    def body(dp_ref, halo_ref, du_ref):
        i = pl.program_id(0)
        g = pl.program_id(1)
        w = _window_of(g)
        dp_v = dp_ref[...]
        t = i * ROW_TILE + lax.broadcasted_iota(jnp.int32, (ext_rows, 1), 0)
        cnt = jnp.minimum(t + 1, w).astype(F32)
        halo = jnp.where(i == n_tiles - 1, 0.0, halo_ref[...])
        ext = jnp.concatenate([dp_v, halo], axis=0) / cnt
        s2 = ext + pltpu.roll(ext, ext_rows - 1, axis=0)
        s4 = s2 + pltpu.roll(s2, ext_rows - 2, axis=0)
        s8 = s4 + pltpu.roll(s4, ext_rows - 4, axis=0)
        s16 = s8 + pltpu.roll(s8, ext_rows - 8, axis=0)
        win = jnp.where(g == 0, s2, jnp.where(g == 1, s4, jnp.where(g == 2, s8, s16)))[:ROW_TILE, :]
        du_ref[...] = (win - dp_v).astype(BF16)

    return pl.pallas_call(
        body, name=name, grid=(n_tiles, N_GROUPS),
        in_specs=[pl.BlockSpec((ROW_TILE, POOL_GROUP), lambda i, g: (i, g)),
                  pl.BlockSpec((HALO, POOL_GROUP), lambda i, g: (jnp.minimum((i + 1) * hb, n_hb - 1), g))],
        out_specs=pl.BlockSpec((ROW_TILE, POOL_GROUP), lambda i, g: (i, g)),
        out_shape=SDS((s, D_POOL), BF16), compiler_params=_params(2),
    )(dp, dp)


FWD_HEADS_PER_STEP = 8
BWD_HEADS_PER_STEP = 4
ATT_SCALE = 0.125


def _att_groups(nh):
    lanes = nh * HEAD_DIM
    return lanes, D_SB // lanes, D_POOL // lanes, (D_POOL + D_SB) // lanes, (D_POOL + 2 * D_SB) // lanes


def _att_consts():
    r = lax.broadcasted_iota(jnp.int32, (ATT_TILE, ATT_TILE), 0)
    c = lax.broadcasted_iota(jnp.int32, (ATT_TILE, ATT_TILE), 1)
    first = lax.broadcasted_iota(jnp.int32, (1, LANES), 1) < HEAD_DIM
    return r, c, first


def _pair(x, p):
    return x[:, p * LANES:(p + 1) * LANES]


def _one_head(x, first, hh):
    zero = jnp.zeros_like(x)
    return jnp.where(first, x, zero) if hh == 0 else jnp.where(first, zero, x)


def _neg_softplus(z):
    return -(jnp.maximum(z, 0.0) + jnp.log(1.0 + jnp.exp(-jnp.abs(z))))


def _side_exchange(side_refs, n_side, by_chunk, is_first, is_last):
    ins, outs = side_refs[:n_side], side_refs[n_side:2 * n_side]
    sems = side_refs[2 * n_side:]

    @pl.when(is_first)
    def _():
        for cp in _peer_copies(ins, outs, *sems, by_chunk=by_chunk):
            cp.start()

    @pl.when(is_last)
    def _():
        for cp in _peer_copies(ins, outs, *sems, by_chunk=by_chunk):
            cp.wait()


def _attn_fwd(name, proj0, shards):
    s = proj0.shape[0]
    nq = s // ATT_TILE
    nh = FWD_HEADS_PER_STEP
    ATT_GROUP, N_ATT_GROUPS, Q_GRP, K_GRP, V_GRP = _att_groups(nh)
    n_side = len(shards)

    def body(q_ref, k_ref, v_ref, *rest):
        o_ref = rest[n_side]
        side = rest[:n_side] + rest[n_side + 1:]
        j = pl.program_id(0)
        i = pl.program_id(1)
        _side_exchange(side, n_side, False, (j == 0) & (i == 0), (j == N_ATT_GROUPS - 1) & (i == nq - 1))
        r, c, first = _att_consts()
        tri = (r >= c).astype(BF16)
        below = c < r
        q = q_ref[...] * ATT_SCALE
        qh = [_one_head(_pair(q, h // 2), first, h % 2) for h in range(nh)]

        def tile(kb, carry, diagonal):
            k0 = pl.multiple_of(kb * ATT_TILE, ATT_TILE)
            kt = k_ref[pl.ds(k0, ATT_TILE), :]
            vt = v_ref[pl.ds(k0, ATT_TILE), :]
            z = [_dot_nt(qh[h], _pair(kt, h // 2)) for h in range(nh)]
            lf = [_neg_softplus(z[h]) for h in range(nh)]
            if diagonal:
                lf = [jnp.where(below, x, 0.0) for x in lf]
            parts = [_split_bf16(x) for x in lf]
            run = [_dot(parts[h][0], tri) + _dot(parts[h][1], tri) for h in range(nh)]
            a = [jnp.exp(z[h] + run[h] + carry[h]) for h in range(nh)]
            if diagonal:
                a = [jnp.where(below, x, 0.0) for x in a]
            out_acc = [carry[nh + h] + _dot(a[h].astype(BF16), _pair(vt, h // 2)) for h in range(nh)]
            out_c = [carry[h] + jnp.sum(lf[h], axis=1, keepdims=True) for h in range(nh)]
            return tuple(out_c + out_acc)

        init = tuple([jnp.zeros((ATT_TILE, 1), F32)] * nh + [jnp.zeros((ATT_TILE, LANES), F32)] * nh)
        carry = tile(i, init, True)
        carry = lax.fori_loop(1, i + 1, lambda n, cr: tile(i - n, cr, False), carry)
        for p in range(nh // 2):
            o_ref[:, p * LANES:(p + 1) * LANES] = jnp.where(first, carry[nh + 2 * p], carry[nh + 2 * p + 1])

    out = pl.pallas_call(
        body, name=name, grid=(N_ATT_GROUPS, nq),
        in_specs=[pl.BlockSpec((ATT_TILE, ATT_GROUP), lambda j, i: (i, Q_GRP + j)),
                  pl.BlockSpec((s, ATT_GROUP), lambda j, i: (0, K_GRP + j)),
                  pl.BlockSpec((s, ATT_GROUP), lambda j, i: (0, V_GRP + j))] + [HBM_SPEC] * n_side,
        out_specs=[pl.BlockSpec((ATT_TILE, ATT_GROUP), lambda j, i: (i, j))] + [HBM_SPEC] * n_side,
        out_shape=[SDS((s, D_SB), F32)] + [SDS((N_DEV,) + sh.shape, sh.dtype) for sh in shards],
        scratch_shapes=_peer_sems(n_side), compiler_params=_params(2),
    )(proj0, proj0, proj0, *shards)
    return out[0], out[1:]


def _attn_bwd(name, proj0, o, do, dws):
    s = proj0.shape[0]
    nq = s // ATT_TILE
    nh = BWD_HEADS_PER_STEP
    ATT_GROUP, N_ATT_GROUPS, Q_GRP, K_GRP, V_GRP = _att_groups(nh)
    n_side = len(dws)

    def body(q_ref, k_ref, v_ref, o_ref, do_ref, *rest):
        dq_ref, dk_ref, dv_ref = rest[n_side:n_side + 3]
        dk_acc, dv_acc = rest[2 * n_side + 3:2 * n_side + 5]
        side = rest[:n_side] + rest[n_side + 3:2 * n_side + 3] + rest[2 * n_side + 5:]
        j = pl.program_id(0)
        i = pl.program_id(1)
        _side_exchange(side, n_side, True, (j == 0) & (i == 0), (j == N_ATT_GROUPS - 1) & (i == nq - 1))

        @pl.when(i == 0)
        def _():
            dk_acc[...] = jnp.zeros_like(dk_acc)
            dv_acc[...] = jnp.zeros_like(dv_acc)

        r, c, first = _att_consts()
        tri = (r >= c).astype(BF16)
        tri_x = (r > c).astype(BF16)
        below = c < r
        q = q_ref[...] * ATT_SCALE
        do_b = do_ref[...].astype(BF16)
        do_o = do_b.astype(F32) * o_ref[...]
        qh = [_one_head(_pair(q, h // 2), first, h % 2) for h in range(nh)]
        doh = [_one_head(_pair(do_b, h // 2), first, h % 2) for h in range(nh)]
        dsum = [jnp.sum(_one_head(_pair(do_o, h // 2), first, h % 2), axis=1, keepdims=True) for h in range(nh)]

        def tile(kb, carry, diagonal):
            k0 = pl.multiple_of(kb * ATT_TILE, ATT_TILE)
            kt = k_ref[pl.ds(k0, ATT_TILE), :]
            vt = v_ref[pl.ds(k0, ATT_TILE), :]
            hs = range(nh)
            z = [_dot_nt(qh[h], _pair(kt, h // 2)) for h in hs]
            d_a = [_dot_nt(doh[h], _pair(vt, h // 2)) for h in hs]
            lf = [_neg_softplus(z[h]) for h in hs]
            sig = [jnp.exp(z[h] + lf[h]) for h in hs]
            if diagonal:
                lf = [jnp.where(below, x, 0.0) for x in lf]
            parts = [_split_bf16(x) for x in lf]
            run = [_dot(parts[h][0], tri) + _dot(parts[h][1], tri) for h in hs]
            a = [jnp.exp(z[h] + run[h] + carry[h]) for h in hs]
            if diagonal:
                a = [jnp.where(below, x, 0.0) for x in a]
            a_b = [x.astype(BF16) for x in a]
            g = [a_b[h].astype(F32) * d_a[h] for h in hs]
            gparts = [_split_bf16(x) for x in g]
            later = [_dot(gparts[h][0], tri_x) + _dot(gparts[h][1], tri_x) for h in hs]
            dv_t = [_dot_tn(a_b[2 * p], doh[2 * p]) + _dot_tn(a_b[2 * p + 1], doh[2 * p + 1]) for p in range(nh // 2)]
            dz = [g[h] - sig[h] * (dsum[h] - later[h] - carry[nh + h]) for h in hs]
            if diagonal:
                dz = [jnp.where(below, x, 0.0) for x in dz]
            dz = [x.astype(BF16) for x in dz]
            out_dq = [carry[2 * nh + h] + _dot(dz[h], _pair(kt, h // 2)) for h in hs]
            dk_t = [_dot_tn(dz[2 * p], qh[2 * p]) + _dot_tn(dz[2 * p + 1], qh[2 * p + 1]) for p in range(nh // 2)]
            for p in range(nh // 2):
                dk_acc[pl.ds(k0, ATT_TILE), p * LANES:(p + 1) * LANES] += dk_t[p]
                dv_acc[pl.ds(k0, ATT_TILE), p * LANES:(p + 1) * LANES] += dv_t[p]
            out_c1 = [carry[h] + jnp.sum(lf[h], axis=1, keepdims=True) for h in hs]
            out_c2 = [carry[nh + h] + jnp.sum(g[h], axis=1, keepdims=True) for h in hs]
            return tuple(out_c1 + out_c2 + out_dq)

        init = tuple([jnp.zeros((ATT_TILE, 1), F32)] * (2 * nh) + [jnp.zeros((ATT_TILE, LANES), F32)] * nh)
        carry = tile(i, init, True)
        carry = lax.fori_loop(1, i + 1, lambda n, cr: tile(i - n, cr, False), carry)
        for p in range(nh // 2):
            dq_p = jnp.where(first, carry[2 * nh + 2 * p], carry[2 * nh + 2 * p + 1]) * ATT_SCALE
            dq_ref[:, p * LANES:(p + 1) * LANES] = dq_p.astype(BF16)

        @pl.when(i == nq - 1)
        def _():
            dk_ref[...] = dk_acc[...].astype(BF16)
            dv_ref[...] = dv_acc[...].astype(BF16)

    tile_spec = pl.BlockSpec((ATT_TILE, ATT_GROUP), lambda j, i: (i, j))
    full = pl.BlockSpec((s, ATT_GROUP), lambda j, i: (0, j))
    out = pl.pallas_call(
        body, name=name, grid=(N_ATT_GROUPS, nq),
        in_specs=[pl.BlockSpec((ATT_TILE, ATT_GROUP), lambda j, i: (i, Q_GRP + j)),
                  pl.BlockSpec((s, ATT_GROUP), lambda j, i: (0, K_GRP + j)),
                  pl.BlockSpec((s, ATT_GROUP), lambda j, i: (0, V_GRP + j)),
                  tile_spec, tile_spec] + [HBM_SPEC] * n_side,
        out_specs=[tile_spec, full, full] + [HBM_SPEC] * n_side,
        out_shape=[SDS((s, D_SB), BF16)] * 3 + [SDS(dw.shape, dw.dtype) for dw in dws],
        scratch_shapes=[pltpu.VMEM((s, ATT_GROUP), F32), pltpu.VMEM((s, ATT_GROUP), F32)] + _peer_sems(n_side),
        compiler_params=_params(2),
    )(proj0, proj0, proj0, o, do, *dws)
    return out[0], out[1], out[2], out[3:]


GATE0_COL = (D_POOL + 3 * D_SB) // D_INNER


def _gate_fwd0(name, yp_raw, o, proj0, ps):
    s = o.shape[0]

    def body(yp_ref, o_ref, gt_ref, ps_ref, y_ref):
        gt = gt_ref[...]
        sg = gt * _sigmoid(gt)
        y_ref[:, :D_POOL] = (yp_ref[...] * ps_ref[...] * sg[:, :D_POOL]).astype(BF16)
        y_ref[:, D_POOL:] = (o_ref[...] * sg[:, D_POOL:]).astype(BF16)

    return pl.pallas_call(
        body, name=name, grid=(s // ROW_TILE,),
        in_specs=[_row_spec(), _row_spec(), _row_spec(D_INNER, GATE0_COL), _vec_spec()],
        out_specs=_row_spec(D_INNER),
        out_shape=SDS((s, D_INNER), BF16), compiler_params=_params(1),
    )(yp_raw, o, proj0, ps)


def _dsilu(x):
    sg = _sigmoid(x)
    return sg * (1.0 + x * (1.0 - sg))


def _gate_bwd0(name, dymix, yp_raw, o, proj0, ps):
    s = o.shape[0]

    def body(dy_ref, yp_ref, o_ref, gt_ref, ps_ref, dyp_ref, do_ref, dgt_ref, dps_ref):
        i = pl.program_id(0)

        @pl.when(i == 0)
        def _():
            dps_ref[...] = jnp.zeros_like(dps_ref)

        gt = gt_ref[...]
        dy = dy_ref[...]
        sg = gt * _sigmoid(gt)
        dsg = _dsilu(gt)
        dcat = dy * sg
        yp = yp_ref[...]
        ps_v = ps_ref[...]
        dyp_ref[...] = (dcat[:, :D_POOL] * ps_v).astype(BF16)
        do_ref[...] = dcat[:, D_POOL:]
        dps_ref[...] += jnp.sum(dcat[:, :D_POOL] * yp, axis=0, keepdims=True)
        dgt_ref[:, :D_POOL] = (dy[:, :D_POOL] * (yp * ps_v) * dsg[:, :D_POOL]).astype(BF16)
        dgt_ref[:, D_POOL:] = (dy[:, D_POOL:] * o_ref[...] * dsg[:, D_POOL:]).astype(BF16)

    return pl.pallas_call(
        body, name=name, grid=(s // ROW_TILE,),
        in_specs=[_row_spec(D_INNER), _row_spec(), _row_spec(), _row_spec(D_INNER, GATE0_COL), _vec_spec()],
        out_specs=[_row_spec(), _row_spec(), _row_spec(D_INNER), _vec_spec()],
        out_shape=[SDS((s, D_POOL), BF16), SDS((s, D_SB), F32), SDS((s, D_INNER), BF16), SDS((1, D_POOL), F32)],
        compiler_params=_params(1),
    )(dymix, yp_raw, o, proj0, ps)


CONV_COLS = 512
N_CONV_BLK = D_INNER // CONV_COLS
CONV_HALO = 8


def _conv_fwd(name, proj1, cw, cb):
    s = proj1.shape[0]
    hb = ROW_TILE // CONV_HALO
    ext_rows = ROW_TILE + CONV_HALO

    def body(gb_ref, gc_ref, u_ref, gt_ref, gch_ref, uh_ref, cw_ref, cb_ref, y_ref):
        i = pl.program_id(1)
        uc = gc_ref[...] * u_ref[...]
        halo = jnp.where(i == 0, 0.0, gch_ref[...] * uh_ref[...])
        ext = jnp.concatenate([halo, uc], axis=0)
        uc1 = pltpu.roll(ext, 1, axis=0)[CONV_HALO:, :]
        uc2 = pltpu.roll(ext, 2, axis=0)[CONV_HALO:, :]
        cw_v = cw_ref[...]
        conv = cb_ref[...] + cw_v[0:1, :] * uc2 + cw_v[1:2, :] * uc1 + cw_v[2:3, :] * uc
        gt = gt_ref[...]
        y_ref[...] = (gb_ref[...] * conv * (gt * _sigmoid(gt))).astype(BF16)

    def tile(part):
        return pl.BlockSpec((ROW_TILE, CONV_COLS), lambda b, i: (i, part * N_CONV_BLK + b))

    def halo(part):
        return pl.BlockSpec((CONV_HALO, CONV_COLS), lambda b, i: (jnp.maximum(i * hb - 1, 0), part * N_CONV_BLK + b))

    return pl.pallas_call(
        body, name=name, grid=(N_CONV_BLK, s // ROW_TILE),
        in_specs=[tile(0), tile(1), tile(2), tile(3), halo(1), halo(2),
                  pl.BlockSpec((3, CONV_COLS), lambda b, i: (0, b)), pl.BlockSpec((1, CONV_COLS), lambda b, i: (0, b))],
        out_specs=pl.BlockSpec((ROW_TILE, CONV_COLS), lambda b, i: (i, b)),
        out_shape=SDS((s, D_INNER), BF16), compiler_params=_params(2),
    )(proj1, proj1, proj1, proj1, proj1, proj1, cw, cb)


def _conv_bwd(name, dymix, proj1, cw, cb):
    s = proj1.shape[0]
    hb = ROW_TILE // CONV_HALO
    n_hb = s // CONV_HALO
    n_tiles = s // ROW_TILE
    ext_rows = ROW_TILE + CONV_HALO

    def body(dy_ref, gb_ref, gc_ref, u_ref, gt_ref, gch_ref, uh_ref, dyn_ref, gbn_ref, gtn_ref, cw_ref, cb_ref,
             dgb_ref, dgc_ref, du_ref, dgt_ref, dcw_ref, dcb_ref):
        i = pl.program_id(1)

        @pl.when(i == 0)
        def _():
            dcw_ref[...] = jnp.zeros_like(dcw_ref)
            dcb_ref[...] = jnp.zeros_like(dcb_ref)

        gc = gc_ref[...]
        u = u_ref[...]
        gb = gb_ref[...]
        gt = gt_ref[...]
        dy = dy_ref[...]
        uc = gc * u
        halo = jnp.where(i == 0, 0.0, gch_ref[...] * uh_ref[...])
        ext = jnp.concatenate([halo, uc], axis=0)
        uc1 = pltpu.roll(ext, 1, axis=0)[CONV_HALO:, :]
        uc2 = pltpu.roll(ext, 2, axis=0)[CONV_HALO:, :]
        cw_v = cw_ref[...]
        w0, w1, w2 = cw_v[0:1, :], cw_v[1:2, :], cw_v[2:3, :]
        conv = cb_ref[...] + w0 * uc2 + w1 * uc1 + w2 * uc
        sg = gt * _sigmoid(gt)
        dconv = dy * gb * sg
        gtn = gtn_ref[...]
        dconv_next = jnp.where(i == n_tiles - 1, 0.0, dyn_ref[...] * gbn_ref[...] * (gtn * _sigmoid(gtn)))
        dext = jnp.concatenate([dconv, dconv_next], axis=0)
        dconv_p1 = pltpu.roll(dext, ext_rows - 1, axis=0)[:ROW_TILE, :]
        dconv_p2 = pltpu.roll(dext, ext_rows - 2, axis=0)[:ROW_TILE, :]
        duc = w2 * dconv + w1 * dconv_p1 + w0 * dconv_p2
        dgb_ref[...] = (dy * conv * sg).astype(BF16)
        dgc_ref[...] = (duc * u).astype(BF16)
        du_ref[...] = (duc * gc).astype(BF16)
        dgt_ref[...] = (dy * gb * conv * _dsilu(gt)).astype(BF16)
        dcw_ref[0:1, :] += jnp.sum(dconv * uc2, axis=0, keepdims=True)
        dcw_ref[1:2, :] += jnp.sum(dconv * uc1, axis=0, keepdims=True)
        dcw_ref[2:3, :] += jnp.sum(dconv * uc, axis=0, keepdims=True)
        dcb_ref[...] += jnp.sum(dconv, axis=0, keepdims=True)

    def tile(part):
        return pl.BlockSpec((ROW_TILE, CONV_COLS), lambda b, i: (i, part * N_CONV_BLK + b))

    def prev(part):
        return pl.BlockSpec((CONV_HALO, CONV_COLS), lambda b, i: (jnp.maximum(i * hb - 1, 0), part * N_CONV_BLK + b))

    def nxt(part):
        return pl.BlockSpec((CONV_HALO, CONV_COLS), lambda b, i: (jnp.minimum((i + 1) * hb, n_hb - 1), part * N_CONV_BLK + b))

    out_tile = pl.BlockSpec((ROW_TILE, CONV_COLS), lambda b, i: (i, b))
    return pl.pallas_call(
        body, name=name, grid=(N_CONV_BLK, n_tiles),
        in_specs=[tile(0), tile(0), tile(1), tile(2), tile(3), prev(1), prev(2), nxt(0), nxt(0), nxt(3),
                  pl.BlockSpec((3, CONV_COLS), lambda b, i: (0, b)), pl.BlockSpec((1, CONV_COLS), lambda b, i: (0, b))],
        out_specs=[out_tile, out_tile, out_tile, out_tile,
                   pl.BlockSpec((3, CONV_COLS), lambda b, i: (0, b)), pl.BlockSpec((1, CONV_COLS), lambda b, i: (0, b))],
        out_shape=[SDS((s, D_INNER), BF16)] * 4 + [SDS((3, D_INNER), F32), SDS((1, D_INNER), F32)],
        compiler_params=_params(2),
    )(dymix, proj1, proj1, proj1, proj1, proj1, proj1, dymix, proj1, proj1, cw, cb)


def _place():
    x, y, c = lax.axis_index("x"), lax.axis_index("y"), lax.axis_index("c")
    return x, y, c


def _flip(x, y, c, k):
    fx, fy, fc = (k >> 2) & 1, (k >> 1) & 1, k & 1
    return (1 - x if fx else x, 1 - y if fy else y, 1 - c if fc else c)


def _dev_index(p):
    return 4 * p[0] + 2 * p[1] + p[2]


HBM_SPEC = pl.BlockSpec(memory_space=pltpu.HBM)
VMEM_SPEC = pl.BlockSpec(memory_space=pltpu.VMEM)


def _allgather_weights(shards):
    n_w = len(shards)

    def body(*refs):
        ins, outs = refs[:n_w], refs[n_w:2 * n_w]
        send_sems, recv_sems, local_sems = refs[2 * n_w:]
        x, y, c = _place()
        me, sibling = (x, y, c), (x, y, 1 - c)
        chips = [(1 - x, y), (x, 1 - y), (1 - x, 1 - y)]

        def copy(w, k, block, to, src=None):
            rows = outs[w].at[_dev_index(block)]
            return pltpu.make_async_remote_copy(
                src_ref=rows if src is None else src, dst_ref=rows,
                send_sem=send_sems.at[7 * w + k], recv_sem=recv_sems.at[7 * w + k],
                device_id=to, device_id_type=MESH)

        mine, first, passed = [], [], []
        for w in range(n_w):
            cp = pltpu.make_async_copy(ins[w], outs[w].at[_dev_index(me)], local_sems.at[w])
            cp.start()
            mine.append(cp)
            fw = [copy(w, 0, me, sibling, src=ins[w])]
            fw += [copy(w, 1 + j, me, (*chip, c), src=ins[w]) for j, chip in enumerate(chips)]
            for cp in fw:
                cp.start()
            first += fw
        for w in range(n_w):
            for j, chip in enumerate(chips):
                copy(w, 1 + j, (*chip, c), me).wait_recv()
                cp = copy(w, 4 + j, (*chip, c), sibling)
                cp.start()
                passed.append(cp)
        for w in range(n_w):
            copy(w, 0, sibling, me).wait_recv()
            for j, chip in enumerate(chips):
                copy(w, 4 + j, (*chip, 1 - c), me).wait_recv()
        for cp in first + passed:
            cp.wait_send()
        for cp in mine:
            cp.wait()

    return pl.pallas_call(
        body, name="allgather_weights",
        out_shape=[SDS((N_DEV,) + sh.shape, sh.dtype) for sh in shards],
        in_specs=[HBM_SPEC] * n_w, out_specs=[HBM_SPEC] * n_w,
        scratch_shapes=[pltpu.SemaphoreType.DMA((7 * n_w,)), pltpu.SemaphoreType.DMA((7 * n_w,)),
                        pltpu.SemaphoreType.DMA((n_w,))],
    )(*shards)


def _peer_copies(ins, outs, send_sems, recv_sems, local_sems, by_chunk):
    x, y, c = _place()
    my = _dev_index((x, y, c))
    copies = []
    for w in range(len(ins)):
        copies.append(pltpu.make_async_copy(ins[w].at[my] if by_chunk else ins[w], outs[w].at[my], local_sems.at[w]))
        for k in range(1, N_DEV):
            peer = _flip(x, y, c, k)
            copies.append(pltpu.make_async_remote_copy(
                src_ref=ins[w].at[_dev_index(peer)] if by_chunk else ins[w], dst_ref=outs[w].at[my],
                send_sem=send_sems.at[7 * w + k - 1], recv_sem=recv_sems.at[7 * w + k - 1],
                device_id=peer, device_id_type=MESH))
    return copies


def _peer_sems(n_w):
    return [pltpu.SemaphoreType.DMA((7 * n_w,)), pltpu.SemaphoreType.DMA((7 * n_w,)), pltpu.SemaphoreType.DMA((n_w,))]


def _exchange_wgrads(dws):
    n_w = len(dws)

    def body(*refs):
        ins, outs = refs[:n_w], refs[n_w:2 * n_w]
        copies = _peer_copies(ins, outs, *refs[2 * n_w:], by_chunk=True)
        for cp in copies:
            cp.start()
        for cp in copies:
            cp.wait()

    return pl.pallas_call(
        body, name="exchange_wgrads",
        out_shape=[SDS(dw.shape, dw.dtype) for dw in dws],
        in_specs=[HBM_SPEC] * n_w, out_specs=[HBM_SPEC] * n_w,
        scratch_shapes=[pltpu.SemaphoreType.DMA((7 * n_w,)), pltpu.SemaphoreType.DMA((7 * n_w,)),
                        pltpu.SemaphoreType.DMA((n_w,))],
    )(*dws)


ADA_COLS = 3 * D_MODEL // N_DEV


def _ada_forward(c_row, conv_w, conv_b, ada_w, ada_b):
    cw_cols = conv_w.shape[1]

    def body(c_ref, cw_ref, cb_ref, aw_ref, ab_ref, m_ref, cs_ref, cwf_ref, cbf_ref,
             slab, gath, part, land, send_sems, recv_sems):
        x, y, c = _place()
        my = _dev_index((x, y, c))
        slab[...] = jnp.zeros_like(slab)
        slab[0:1, :] = c_ref[...]
        slab[1:4, 0:cw_cols] = cw_ref[...]
        slab[4:5, 0:cw_cols] = cb_ref[...]
        gath[my] = slab[...]
        sends = []
        for k in range(1, N_DEV):
            peer = _flip(x, y, c, k)
            cp = pltpu.make_async_remote_copy(
                src_ref=slab, dst_ref=gath.at[my], send_sem=send_sems.at[k - 1], recv_sem=recv_sems.at[k - 1],
                device_id=peer, device_id_type=MESH)
            cp.start()
            sends.append(cp)
        for cp in sends:
            cp.wait()
        for d in range(N_DEV):
            c_d = gath[d, 0:1, :]
            cs_ref[d:d + 1, :] = c_d * _sigmoid(c_d)
            cwf_ref[:, d * cw_cols:(d + 1) * cw_cols] = gath[d, 1:4, 0:cw_cols]
            cbf_ref[:, d * cw_cols:(d + 1) * cw_cols] = gath[d, 4:5, 0:cw_cols]
        cs = cs_ref[...]
        part[...] = jnp.zeros_like(part)
        for layer in range(2):
            m_part = jnp.dot(cs, aw_ref[layer], preferred_element_type=F32, precision=lax.Precision.HIGHEST)
            for d in range(N_DEV):
                part[d, layer:layer + 1, :] = m_part[d:d + 1, :]
        land[my] = part[my]
        sends = []
        for k in range(1, N_DEV):
            peer = _flip(x, y, c, k)
            cp = pltpu.make_async_remote_copy(
                src_ref=part.at[_dev_index(peer)], dst_ref=land.at[my],
                send_sem=send_sems.at[6 + k], recv_sem=recv_sems.at[6 + k],
                device_id=peer, device_id_type=MESH)
            cp.start()
            sends.append(cp)
        for cp in sends:
            cp.wait()
        for d in range(N_DEV):
            cols = slice(d * ADA_COLS, (d + 1) * ADA_COLS)
            m_ref[:, cols] = land[d, 0:2, :] + ab_ref[:, cols]

    return pl.pallas_call(
        body, name="ada_forward",
        out_shape=[SDS((2, 3 * D_MODEL), F32), SDS((N_DEV, D_MODEL), F32), SDS((3, N_DEV * cw_cols), F32),
                   SDS((1, N_DEV * cw_cols), F32)],
        in_specs=[VMEM_SPEC] * 5, out_specs=[VMEM_SPEC] * 4,
        scratch_shapes=[pltpu.VMEM((8, D_MODEL), F32), pltpu.VMEM((N_DEV, 8, D_MODEL), F32),
                        pltpu.VMEM((N_DEV, 8, ADA_COLS), F32), pltpu.VMEM((N_DEV, 8, ADA_COLS), F32),
                        pltpu.SemaphoreType.DMA((14,)), pltpu.SemaphoreType.DMA((14,))],
        compiler_params=pltpu.CompilerParams(vmem_limit_bytes=VMEM_LIMIT),
    )(c_row, conv_w, conv_b, ada_w, ada_b)


def _small_grads(slab):
    def body(slab_ref, gath_ref, tot_ref, send_sems, recv_sems):
        x, y, c = _place()
        my = _dev_index((x, y, c))
        gath_ref[my] = slab_ref[...]
        sends = []
        for k in range(1, N_DEV):
            peer = _flip(x, y, c, k)
            cp = pltpu.make_async_remote_copy(
                src_ref=slab_ref, dst_ref=gath_ref.at[my], send_sem=send_sems.at[k - 1], recv_sem=recv_sems.at[k - 1],
                device_id=peer, device_id_type=MESH)
            cp.start()
            sends.append(cp)
        for cp in sends:
            cp.wait()
        tot = gath_ref[0]
        for d in range(1, N_DEV):
            tot = tot + gath_ref[d]
        tot_ref[...] = tot

    return pl.pallas_call(
        body, name="small_grads",
        out_shape=[SDS((N_DEV, SLAB_ROWS, D_MODEL), F32), SDS((SLAB_ROWS, D_MODEL), F32)],
        in_specs=[VMEM_SPEC], out_specs=[VMEM_SPEC] * 2,
        scratch_shapes=[pltpu.SemaphoreType.DMA((7,)), pltpu.SemaphoreType.DMA((7,))],
    )(slab)


def _adamw_math(w, g, m, v):
    m = ADAM_B1 * m + (1.0 - ADAM_B1) * g
    v = ADAM_B2 * v + (1.0 - ADAM_B2) * jnp.square(g)
    m_hat = m / (1.0 - ADAM_B1 ** ADAM_STEP)
    v_hat = v / (1.0 - ADAM_B2 ** ADAM_STEP)
    delta = -ADAM_LR * (m_hat / (jnp.sqrt(v_hat) + ADAM_EPS) + ADAM_WD * w)
    return delta, m, v


def _sum_adamw(name, recv, w, m, v):
    rows, cols = w.shape
    tr = min(rows, 256)

    def body(r_ref, w_ref, m_ref, v_ref, g_ref, d_ref, nm_ref, nv_ref):
        g = r_ref[0].astype(F32)
        for d in range(1, N_DEV):
            g = g + r_ref[d].astype(F32)
        g_ref[...] = g
        d_ref[...], nm_ref[...], nv_ref[...] = _adamw_math(w_ref[...], g, m_ref[...], v_ref[...])

    blk = pl.BlockSpec((tr, cols), lambda i: (i, 0))
    return pl.pallas_call(
        body, name=name, grid=(rows // tr,),
        in_specs=[pl.BlockSpec((N_DEV, tr, cols), lambda i: (0, i, 0)), blk, blk, blk],
        out_specs=[blk] * 4, out_shape=[SDS((rows, cols), F32)] * 4, compiler_params=_params(1),
    )(recv, w, m, v)


def _ada_w_adamw(name, cs_t, dm_cols, w, m, v):
    def body(cs_ref, dm_ref, w_ref, m_ref, v_ref, g_ref, d_ref, nm_ref, nv_ref):
        cs = cs_ref[...]
        dm = dm_ref[...]
        g = cs[:, 0:1] * dm[0:1, :]
        for b in range(1, N_DEV):
            g = g + cs[:, b:b + 1] * dm[b:b + 1, :]
        g_ref[...] = g
        d_ref[...], nm_ref[...], nv_ref[...] = _adamw_math(w_ref[...], g, m_ref[...], v_ref[...])

    blk = pl.BlockSpec((None, D_MODEL, ADA_COLS), lambda l: (l, 0, 0))
    return pl.pallas_call(
        body, name=name, grid=(2,),
        in_specs=[pl.BlockSpec((D_MODEL, N_DEV), lambda l: (0, 0)),
                  pl.BlockSpec((None, N_DEV, ADA_COLS), lambda l: (l, 0, 0)), blk, blk, blk],
        out_specs=[blk] * 4, out_shape=[SDS((2, D_MODEL, ADA_COLS), F32)] * 4, compiler_params=_params(1),
    )(cs_t, dm_cols, w, m, v)


def _small_adamw(name, triples):
    n = len(triples)

    def body(*refs):
        ins, outs = refs[:4 * n], refs[4 * n:]
        for j in range(n):
            w_ref, g_ref, m_ref, v_ref = ins[4 * j:4 * j + 4]
            d, nm, nv = _adamw_math(w_ref[...], g_ref[...], m_ref[...], v_ref[...])
            outs[3 * j][...] = d
            outs[3 * j + 1][...] = nm
            outs[3 * j + 2][...] = nv

    flat = [a for t in triples for a in t]
    return pl.pallas_call(
        body, name=name,
        out_shape=[SDS(t[0].shape, F32) for t in triples for _ in range(3)],
        in_specs=[VMEM_SPEC] * (4 * n), out_specs=[VMEM_SPEC] * (3 * n),
    )(*flat)


def kernel(x, c, norm_g, ada_w, ada_b, even_w_in, pool_w, pool_scale, even_w_out, odd_w_in, conv_w, conv_b, odd_w_out, final_g, loss_target, m_norm_g, m_ada_w, m_ada_b, m_even_w_in, m_pool_w, m_pool_scale, m_even_w_out, m_odd_w_in, m_conv_w, m_conv_b, m_odd_w_out, m_final_g, v_norm_g, v_ada_w, v_ada_b, v_even_w_in, v_pool_w, v_pool_scale, v_even_w_out, v_odd_w_in, v_conv_w, v_conv_b, v_odd_w_out, v_final_g):
    seq = x.shape[1]
    x0 = x[0]
    target = loss_target[0]
    final_g2 = final_g.reshape(1, D_MODEL)

    w_in_e = even_w_in[0]
    w_out_e = even_w_out[0]
    w_in_o = odd_w_in[0]
    w_out_o = odd_w_out[0]
    w_pool = pool_w[0].reshape(N_GROUPS * 32, POOL_GROUP)
    shards = [w.astype(BF16) for w in (w_in_e, w_out_e, w_in_o, w_out_o, w_pool)]
    (wg_in_e,) = _allgather_weights(shards[:1])

    m_vec, cs_all, conv_w_full, conv_b_full = _ada_forward(c, conv_w[0], conv_b, ada_w, ada_b)
    shift = [m_vec[l:l + 1, 0:D_MODEL] for l in range(2)]
    scale = [m_vec[l:l + 1, D_MODEL:2 * D_MODEL] for l in range(2)]
    gate = [m_vec[l:l + 1, 2 * D_MODEL:] for l in range(2)]
    ng = [norm_g[l:l + 1] for l in range(2)]

    h0 = _ln_mod("ln_mod0", x0, ng[0], scale[0], shift[0])
    proj0, proj0_b = _proj_in_dual("proj_in0", h0, wg_in_e)
    o, (wg_out_e, wg_in_o, wg_out_o, wg_pool) = _attn_fwd("attn_fwd", proj0_b, shards[1:])
    wf_out_e = wg_out_e.reshape(D_INNER, D_MODEL)
    wf_out_o = wg_out_o.reshape(D_INNER, D_MODEL)
    wf_pool = wg_pool.reshape(N_DEV, N_GROUPS, 32, POOL_GROUP).transpose(1, 0, 2, 3).reshape(N_GROUPS, POOL_GROUP, POOL_GROUP)
    p = _pool_fwd("pool_fwd", proj0)
    yp_raw = _pool_mm("pool_mix", p, wf_pool, _dot)
    ymix0 = _gate_fwd0("gate_fwd0", yp_raw, o, proj0, pool_scale)
    yo0 = _proj_out("proj_out0", ymix0, wf_out_e)

    x1, h1 = _resid_ln_mod("resid_ln_mod1", x0, yo0, gate[0], ng[1], scale[1], shift[1])
    proj1 = _proj_in("proj_in1", h1, wg_in_o)
    ymix1 = _conv_fwd("conv_fwd", proj1, conv_w_full, conv_b_full)
    yo1 = _proj_out("proj_out1", ymix1, wf_out_o)

    dx2, dyo1, loss_acc, d_final_g, d_gate1 = _final_loss("final_loss", x1, yo1, gate[1], final_g2, target)
    loss = lax.psum(loss_acc[0, 0], ("x", "y", "c"))

    dymix1 = _proj_out_bwd("proj_out1_bwd", dyo1, wf_out_o)
    dw_out_o = _wgrad_out("wgrad_out1", ymix1, dyo1)
    dgb, dgc, du, dgt1, d_conv_w, d_conv_b = _conv_bwd("conv_bwd", dymix1, proj1, conv_w_full, conv_b_full)
    dproj1 = jnp.concatenate([dgb, dgc, du, dgt1], axis=1)
    dh1 = _proj_in_bwd("proj_in1_bwd", dproj1, wg_in_o)
    dw_in_o = _wgrad_in("wgrad_in1", h1, dproj1)
    dx1, d_shift1, d_scale1, d_ng1 = _ln_mod_bwd("ln_mod1_bwd", dh1, x1, dx2, ng[1], scale[1])

    dyo0, d_gate0 = _resid_bwd("resid0_bwd", dx1, yo0, gate[0])
    dymix0 = _proj_out_bwd("proj_out0_bwd", dyo0, wf_out_e)
    dw_out_e = _wgrad_out("wgrad_out0", ymix0, dyo0)
    dyp, do, dgt0, d_pool_scale = _gate_bwd0("gate_bwd0", dymix0, yp_raw, o, proj0, pool_scale)
    dp = _pool_mm("pool_mix_bwd", dyp, wf_pool, _dot_nt)
    dw_pool = _pool_wgrad("pool_wgrad", p, dyp)
    du_pool = _pool_bwd("pool_bwd", dp)
    dw_pool_c = dw_pool.reshape(N_GROUPS, N_DEV, 32, POOL_GROUP).transpose(1, 0, 2, 3).reshape(N_DEV, N_GROUPS * 32, POOL_GROUP).astype(BF16)
    ready = [dw_out_e.reshape(N_DEV, D_INNER // N_DEV, D_MODEL), dw_in_o,
             dw_out_o.reshape(N_DEV, D_INNER // N_DEV, D_MODEL), dw_pool_c]
    dq, dk, dv, (r_out_e, r_in_o, r_out_o, r_pool) = _attn_bwd("attn_bwd", proj0_b, o, do, ready)
    dproj0 = jnp.concatenate([du_pool, dq, dk, dv, dgt0], axis=1)
    dh0 = _proj_in_bwd("proj_in0_bwd", dproj0, wg_in_e)
    dw_in_e = _wgrad_in("wgrad_in0", h0, dproj0)
    dx0, d_shift0, d_scale0, d_ng0 = _ln_mod_bwd("ln_mod0_bwd", dh0, x0, dx1, ng[0], scale[0])
    grad_x = dx0[None]

    (r_in_e,) = _exchange_wgrads([dw_in_e])
    big = {}
    big["even_w_in"] = _sum_adamw("adamw_even_w_in", r_in_e, w_in_e, m_even_w_in[0], v_even_w_in[0])
    big["even_w_out"] = _sum_adamw("adamw_even_w_out", r_out_e, w_out_e, m_even_w_out[0], v_even_w_out[0])
    big["odd_w_in"] = _sum_adamw("adamw_odd_w_in", r_in_o, w_in_o, m_odd_w_in[0], v_odd_w_in[0])
    big["odd_w_out"] = _sum_adamw("adamw_odd_w_out", r_out_o, w_out_o, m_odd_w_out[0], v_odd_w_out[0])
    big["pool_w"] = _sum_adamw("adamw_pool_w", r_pool, w_pool, m_pool_w[0].reshape(N_GROUPS * 32, POOL_GROUP),
                               v_pool_w[0].reshape(N_GROUPS * 32, POOL_GROUP))
    big = {k: [a.reshape(shape) for a in v] for (k, v), shape in zip(
        big.items(), [even_w_in.shape, even_w_out.shape, odd_w_in.shape, odd_w_out.shape, pool_w.shape])}

    dm = jnp.concatenate([jnp.concatenate([d_shift0, d_scale0, d_gate0], axis=1),
                          jnp.concatenate([d_shift1, d_scale1, d_gate1], axis=1)], axis=0)
    slab = jnp.zeros((SLAB_ROWS, D_MODEL), F32)
    slab = slab.at[0:6].set(dm.reshape(6, D_MODEL))
    slab = slab.at[8:9].set(d_ng0).at[9:10].set(d_ng1).at[10:11].set(d_pool_scale).at[11:12].set(d_final_g)
    slab = slab.at[16:22].set(d_conv_w.reshape(6, D_MODEL)).at[24:26].set(d_conv_b.reshape(2, D_MODEL))
    gathered, total = _small_grads(slab)
    my = 4 * lax.axis_index("x") + 2 * lax.axis_index("y") + lax.axis_index("c")
    g_ada_b = total[0:6].reshape(2, 3 * D_MODEL)
    g_norm_g = total[8:10]
    g_pool_scale = total[10:11]
    g_final_g = total[11:12]
    cw_cols = conv_w.shape[2]
    g_conv_w = lax.dynamic_slice_in_dim(total[16:22].reshape(3, D_INNER), my * cw_cols, cw_cols, axis=1)
    g_conv_b = lax.dynamic_slice_in_dim(total[24:26].reshape(1, D_INNER), my * cw_cols, cw_cols, axis=1)
    dm_all = gathered[:, 0:6, :].reshape(N_DEV, 2, 3 * D_MODEL)
    dm_cols = lax.dynamic_slice_in_dim(dm_all, my * ADA_COLS, ADA_COLS, axis=2).transpose(1, 0, 2)
    ada = _ada_w_adamw("adamw_ada_w", cs_all.T, dm_cols, ada_w, m_ada_w, v_ada_w)

    small = _small_adamw("adamw_small", [
        (norm_g, g_norm_g, m_norm_g, v_norm_g),
        (ada_b, g_ada_b, m_ada_b, v_ada_b),
        (pool_scale, g_pool_scale, m_pool_scale, v_pool_scale),
        (conv_w[0], g_conv_w, m_conv_w[0], v_conv_w[0]),
        (conv_b, g_conv_b, m_conv_b, v_conv_b),
        (final_g2, g_final_g, m_final_g.reshape(1, D_MODEL), v_final_g.reshape(1, D_MODEL)),
    ])
    small = [small[3 * j:3 * j + 3] for j in range(6)]

    grads = {
        "norm_g": g_norm_g, "ada_w": ada[0], "ada_b": g_ada_b, "even_w_in": big["even_w_in"][0],
        "pool_w": big["pool_w"][0], "pool_scale": g_pool_scale, "even_w_out": big["even_w_out"][0],
        "odd_w_in": big["odd_w_in"][0], "conv_w": g_conv_w.reshape(conv_w.shape), "conv_b": g_conv_b,
        "odd_w_out": big["odd_w_out"][0], "final_g": g_final_g.reshape(D_MODEL),
    }
    rest = []
    for idx in range(3):
        rest += [
            small[0][idx], ada[1 + idx], small[1][idx], big["even_w_in"][1 + idx], big["pool_w"][1 + idx],
            small[2][idx], big["even_w_out"][1 + idx], big["odd_w_in"][1 + idx],
            small[3][idx].reshape(conv_w.shape), small[4][idx], big["odd_w_out"][1 + idx],
            small[5][idx].reshape(D_MODEL),
        ]
    order = ["norm_g", "ada_w", "ada_b", "even_w_in", "pool_w", "pool_scale", "even_w_out", "odd_w_in",
             "conv_w", "conv_b", "odd_w_out", "final_g"]
    return (loss, grad_x, *[grads[n] for n in order], *rest)
```

```python
import jax
import jax.numpy as jnp
from jax import lax
from jax.experimental import pallas as pl
from jax.experimental.pallas import tpu as pltpu

F32 = jnp.float32
BF16 = jnp.bfloat16
SDS = jax.ShapeDtypeStruct
MESH = pl.DeviceIdType.MESH

N_DEV = 8
D_MODEL = 1024
D_INNER = 2048
D_POOL = 1024
D_SB = 1024
N_GROUPS = 4
POOL_GROUP = 256
HEAD_DIM = 64
LANES = 128
D_IN_EVEN = 6144
D_IN_ODD = 8192
EPS = 1e-6
ADAM_LR = 0.001
ADAM_B1 = 0.9
ADAM_B2 = 0.999
ADAM_EPS = 1e-08
ADAM_WD = 0.01
ADAM_STEP = 10

ROW_TILE = 256
ATT_TILE = 256
HALO = 16
VMEM_LIMIT = 48 * 1024 * 1024
SLAB_ROWS = 32


def _params(n_axes):
    return pltpu.CompilerParams(dimension_semantics=("arbitrary",) * n_axes, vmem_limit_bytes=VMEM_LIMIT)


def _sigmoid(x):
    return 1.0 / (1.0 + jnp.exp(-x))


def _softplus(z):
    return jnp.maximum(z, 0.0) + jnp.log1p(jnp.exp(-jnp.abs(z)))


def _split_bf16(x):
    hi = x.astype(BF16)
    lo = (x - hi.astype(F32)).astype(BF16)
    return hi, lo


def _dot(a, b):
    return jnp.dot(a, b, preferred_element_type=F32)


def _dot_nt(a, b):
    return lax.dot_general(a, b, (((1,), (1,)), ((), ())), preferred_element_type=F32)


def _dot_tn(a, b):
    return lax.dot_general(a, b, (((0,), (0,)), ((), ())), preferred_element_type=F32)


def _mm(name, a, b, *, grid, a_spec, b_spec, o_spec, o_shape, o_dtype, dot, acc_axis=None, acc_shape=None):
    n_acc = grid[acc_axis] if acc_axis is not None else 1

    def body(a_ref, b_ref, o_ref, *scratch):
        prod = dot(a_ref[...], b_ref[...])
        if acc_axis is None:
            o_ref[...] = prod.astype(o_dtype)
        else:
            acc = scratch[0]
            k = pl.program_id(acc_axis)

            @pl.when(k == 0)
            def _():
                acc[...] = prod

            @pl.when(k > 0)
            def _():
                acc[...] += prod

            @pl.when(k == n_acc - 1)
            def _():
                o_ref[...] = acc[...].astype(o_dtype)

    scratch = [] if acc_axis is None else [pltpu.VMEM(acc_shape, F32)]
    return pl.pallas_call(
        body, name=name, grid=grid, in_specs=[a_spec, b_spec], out_specs=o_spec,
        out_shape=SDS(o_shape, o_dtype), scratch_shapes=scratch, compiler_params=_params(len(grid)),
    )(a, b)


def _proj_in(name, h, wg):
    s = h.shape[0]
    cn = wg.shape[2]
    tm = min(s, 512)
    return _mm(name, h, wg, grid=(N_DEV, s // tm),
               a_spec=pl.BlockSpec((tm, D_MODEL), lambda d, i: (i, 0)),
               b_spec=pl.BlockSpec((None, D_MODEL, cn), lambda d, i: (d, 0, 0)),
               o_spec=pl.BlockSpec((tm, cn), lambda d, i: (i, d)),
               o_shape=(s, N_DEV * cn), o_dtype=F32, dot=_dot)


def _proj_in_dual(name, h, wg):
    s = h.shape[0]
    cn = wg.shape[2]
    tm = min(s, 512)

    def body(a_ref, b_ref, o_ref, ob_ref):
        prod = _dot(a_ref[...], b_ref[...])
        o_ref[...] = prod
        ob_ref[...] = prod.astype(BF16)

    out = pl.BlockSpec((tm, cn), lambda d, i: (i, d))
    return pl.pallas_call(
        body, name=name, grid=(N_DEV, s // tm),
        in_specs=[pl.BlockSpec((tm, D_MODEL), lambda d, i: (i, 0)),
                  pl.BlockSpec((None, D_MODEL, cn), lambda d, i: (d, 0, 0))],
        out_specs=[out, out], out_shape=[SDS((s, N_DEV * cn), F32), SDS((s, N_DEV * cn), BF16)],
        compiler_params=_params(2),
    )(h, wg)


def _proj_out(name, y, w):
    s = y.shape[0]
    tm = min(s, 512)
    return _mm(name, y, w, grid=(s // tm,),
               a_spec=pl.BlockSpec((tm, D_INNER), lambda i: (i, 0)),
               b_spec=pl.BlockSpec((D_INNER, D_MODEL), lambda i: (0, 0)),
               o_spec=pl.BlockSpec((tm, D_MODEL), lambda i: (i, 0)),
               o_shape=(s, D_MODEL), o_dtype=F32, dot=_dot)


def _proj_out_bwd(name, dyo, w):
    s = dyo.shape[0]
    tm = min(s, 512)
    return _mm(name, dyo, w, grid=(s // tm,),
               a_spec=pl.BlockSpec((tm, D_MODEL), lambda i: (i, 0)),
               b_spec=pl.BlockSpec((D_INNER, D_MODEL), lambda i: (0, 0)),
               o_spec=pl.BlockSpec((tm, D_INNER), lambda i: (i, 0)),
               o_shape=(s, D_INNER), o_dtype=F32, dot=_dot_nt)


def _wgrad_out(name, y, dyo):
    s = y.shape[0]
    ts = min(s, 512)
    return _mm(name, y, dyo, grid=(s // ts,),
               a_spec=pl.BlockSpec((ts, D_INNER), lambda k: (k, 0)),
               b_spec=pl.BlockSpec((ts, D_MODEL), lambda k: (k, 0)),
               o_spec=pl.BlockSpec((D_INNER, D_MODEL), lambda k: (0, 0)),
               o_shape=(D_INNER, D_MODEL), o_dtype=BF16, dot=_dot_tn, acc_axis=0, acc_shape=(D_INNER, D_MODEL))


def _proj_in_bwd(name, dproj, wt, dws=()):
    s, k_all = dproj.shape
    tm = min(s, ROW_TILE)
    n_i = s // tm
    n_side = len(dws)
    w_all = wt.reshape(k_all, D_MODEL)

    def body(a_ref, b_ref, *rest):
        o_ref = rest[n_side]
        side = rest[:n_side] + rest[n_side + 1:]
        i = pl.program_id(0)
        if n_side:
            _side_exchange(side, n_side, True, i == 0, i == n_i - 1)
        o_ref[...] = _dot(a_ref[...], b_ref[...])

    out = pl.pallas_call(
        body, name=name, grid=(n_i,),
        in_specs=[pl.BlockSpec((tm, k_all), lambda i: (i, 0)),
                  pl.BlockSpec((k_all, D_MODEL), lambda i: (0, 0), pipeline_mode=pl.Buffered(1))] + [HBM_SPEC] * n_side,
        out_specs=[pl.BlockSpec((tm, D_MODEL), lambda i: (i, 0))] + [HBM_SPEC] * n_side,
        out_shape=[SDS((s, D_MODEL), F32)] + [SDS(dw.shape, dw.dtype) for dw in dws],
        scratch_shapes=_peer_sems(n_side) if n_side else [],
        compiler_params=_params(1),
    )(dproj, w_all, *dws)
    return out[0], out[1:]


def _wgrad_in(name, h_t, dproj):
    s = h_t.shape[1]
    cn = dproj.shape[1] // N_DEV

    def body(a_ref, b_ref, o_ref):
        o_ref[...] = _dot(a_ref[...], b_ref[...]).astype(BF16)

    return pl.pallas_call(
        body, name=name, grid=(N_DEV,),
        in_specs=[pl.BlockSpec((D_MODEL, s), lambda d: (0, 0), pipeline_mode=pl.Buffered(1)),
                  pl.BlockSpec((s, cn), lambda d: (0, d))],
        out_specs=pl.BlockSpec((None, D_MODEL, cn), lambda d: (d, 0, 0)),
        out_shape=SDS((N_DEV, D_MODEL, cn), BF16), compiler_params=_params(1),
    )(h_t, dproj)


def _pool_mm(name, p, wp, dot):
    s = p.shape[0]
    tm = min(s, 512)
    return _mm(name, p, wp, grid=(s // tm, N_GROUPS),
               a_spec=pl.BlockSpec((tm, POOL_GROUP), lambda i, g: (i, g)),
               b_spec=pl.BlockSpec((None, POOL_GROUP, POOL_GROUP), lambda i, g: (g, 0, 0)),
               o_spec=pl.BlockSpec((tm, POOL_GROUP), lambda i, g: (i, g)),
               o_shape=(s, D_POOL), o_dtype=F32, dot=dot)


def _pool_wgrad(name, p, dyp):
    s = p.shape[0]
    ts = min(s, 512)
    return _mm(name, p, dyp, grid=(N_GROUPS, s // ts),
               a_spec=pl.BlockSpec((ts, POOL_GROUP), lambda g, k: (k, g)),
               b_spec=pl.BlockSpec((ts, POOL_GROUP), lambda g, k: (k, g)),
               o_spec=pl.BlockSpec((None, POOL_GROUP, POOL_GROUP), lambda g, k: (g, 0, 0)),
               o_shape=(N_GROUPS, POOL_GROUP, POOL_GROUP), o_dtype=F32, dot=_dot_tn, acc_axis=1,
               acc_shape=(POOL_GROUP, POOL_GROUP))


def _vec_spec():
    return pl.BlockSpec((1, D_MODEL), lambda i: (0, 0))


def _row_spec(width=D_MODEL, col=0):
    return pl.BlockSpec((ROW_TILE, width), lambda i: (i, col))


def _col_spec():
    return pl.BlockSpec((D_MODEL, ROW_TILE), lambda i: (0, i))


def _ln_mod(name, x, g, scale, shift):
    s = x.shape[0]

    def body(x_ref, g_ref, sc_ref, sh_ref, h_ref, ht_ref):
        xv = x_ref[...]
        r = lax.rsqrt(jnp.mean(xv * xv, axis=-1, keepdims=True) + EPS)
        n = (xv * r) * g_ref[...]
        h = (n * (1.0 + sc_ref[...]) + sh_ref[...]).astype(BF16)
        h_ref[...] = h
        ht_ref[...] = h.T

    return pl.pallas_call(
        body, name=name, grid=(s // ROW_TILE,),
        in_specs=[_row_spec(), _vec_spec(), _vec_spec(), _vec_spec()], out_specs=[_row_spec(), _col_spec()],
        out_shape=[SDS((s, D_MODEL), BF16), SDS((D_MODEL, s), BF16)], compiler_params=_params(1),
    )(x, g, scale, shift)


def _resid_ln_mod(name, x, yo, gate, g, scale, shift):
    s = x.shape[0]

    def body(x_ref, yo_ref, gt_ref, g_ref, sc_ref, sh_ref, xn_ref, h_ref, ht_ref):
        xv = x_ref[...] + (1.0 + gt_ref[...]) * yo_ref[...]
        xn_ref[...] = xv
        r = lax.rsqrt(jnp.mean(xv * xv, axis=-1, keepdims=True) + EPS)
        n = (xv * r) * g_ref[...]
        h = (n * (1.0 + sc_ref[...]) + sh_ref[...]).astype(BF16)
        h_ref[...] = h
        ht_ref[...] = h.T

    return pl.pallas_call(
        body, name=name, grid=(s // ROW_TILE,),
        in_specs=[_row_spec(), _row_spec(), _vec_spec(), _vec_spec(), _vec_spec(), _vec_spec()],
        out_specs=[_row_spec(), _row_spec(), _col_spec()],
        out_shape=[SDS((s, D_MODEL), F32), SDS((s, D_MODEL), BF16), SDS((D_MODEL, s), BF16)],
        compiler_params=_params(1),
    )(x, yo, gate, g, scale, shift)


def _final_loss(name, x1, yo1, gate1, gf, target):
    s = x1.shape[0]

    def body(x_ref, yo_ref, gt_ref, gf_ref, t_ref, dx_ref, dyo_ref, loss_ref, dgf_ref, dgt_ref):
        i = pl.program_id(0)

        @pl.when(i == 0)
        def _():
            loss_ref[...] = jnp.zeros_like(loss_ref)
            dgf_ref[...] = jnp.zeros_like(dgf_ref)
            dgt_ref[...] = jnp.zeros_like(dgt_ref)

        yo = yo_ref[...]
        one_gate = 1.0 + gt_ref[...]
        x2 = x_ref[...] + one_gate * yo
        r = lax.rsqrt(jnp.mean(x2 * x2, axis=-1, keepdims=True) + EPS)
        xn = x2 * r
        gf_v = gf_ref[...]
        err = xn * gf_v - t_ref[...]
        loss_ref[...] += 0.5 * jnp.sum(jnp.mean(err * err, axis=-1, keepdims=True))
        dout = err * (1.0 / D_MODEL)
        dgf_ref[...] += jnp.sum(dout * xn, axis=0, keepdims=True)
        dxn = dout * gf_v
        dx2 = r * (dxn - xn * jnp.mean(dxn * xn, axis=-1, keepdims=True))
        dx_ref[...] = dx2
        dyo_ref[...] = (dx2 * one_gate).astype(BF16)
        dgt_ref[...] += jnp.sum(dx2 * yo, axis=0, keepdims=True)

    return pl.pallas_call(
        body, name=name, grid=(s // ROW_TILE,),
        in_specs=[_row_spec(), _row_spec(), _vec_spec(), _vec_spec(), _row_spec()],
        out_specs=[_row_spec(), _row_spec(), pl.BlockSpec((1, LANES), lambda i: (0, 0)), _vec_spec(), _vec_spec()],
        out_shape=[SDS((s, D_MODEL), F32), SDS((s, D_MODEL), BF16), SDS((1, LANES), F32),
                   SDS((1, D_MODEL), F32), SDS((1, D_MODEL), F32)],
        compiler_params=_params(1),
    )(x1, yo1, gate1, gf, target)


def _ln_mod_bwd(name, dh, x, dx_next, g, scale):
    s = x.shape[0]

    def body(dh_ref, x_ref, dxn_ref, g_ref, sc_ref, dx_ref, dsh_ref, dsc_ref, dg_ref):
        i = pl.program_id(0)

        @pl.when(i == 0)
        def _():
            dsh_ref[...] = jnp.zeros_like(dsh_ref)
            dsc_ref[...] = jnp.zeros_like(dsc_ref)
            dg_ref[...] = jnp.zeros_like(dg_ref)

        dh_v = dh_ref[...]
        xv = x_ref[...]
        g_v = g_ref[...]
        r = lax.rsqrt(jnp.mean(xv * xv, axis=-1, keepdims=True) + EPS)
        xn = xv * r
        dsh_ref[...] += jnp.sum(dh_v, axis=0, keepdims=True)
        dsc_ref[...] += jnp.sum(dh_v * (xn * g_v), axis=0, keepdims=True)
        dn = dh_v * (1.0 + sc_ref[...])
        dg_ref[...] += jnp.sum(dn * xn, axis=0, keepdims=True)
        dxh = dn * g_v
        dx_ref[...] = dxn_ref[...] + r * (dxh - xn * jnp.mean(dxh * xn, axis=-1, keepdims=True))

    return pl.pallas_call(
        body, name=name, grid=(s // ROW_TILE,),
        in_specs=[_row_spec(), _row_spec(), _row_spec(), _vec_spec(), _vec_spec()],
        out_specs=[_row_spec(), _vec_spec(), _vec_spec(), _vec_spec()],
        out_shape=[SDS((s, D_MODEL), F32)] + [SDS((1, D_MODEL), F32)] * 3, compiler_params=_params(1),
    )(dh, x, dx_next, g, scale)


def _resid_bwd(name, dx, yo, gate):
    s = dx.shape[0]

    def body(dx_ref, yo_ref, gt_ref, dyo_ref, dgt_ref):
        i = pl.program_id(0)

        @pl.when(i == 0)
        def _():
            dgt_ref[...] = jnp.zeros_like(dgt_ref)

        dx_v = dx_ref[...]
        dyo_ref[...] = (dx_v * (1.0 + gt_ref[...])).astype(BF16)
        dgt_ref[...] += jnp.sum(dx_v * yo_ref[...], axis=0, keepdims=True)

    return pl.pallas_call(
        body, name=name, grid=(s // ROW_TILE,),
        in_specs=[_row_spec(), _row_spec(), _vec_spec()], out_specs=[_row_spec(), _vec_spec()],
        out_shape=[SDS((s, D_MODEL), BF16), SDS((1, D_MODEL), F32)], compiler_params=_params(1),
    )(dx, yo, gate)


def _window_of(g):
    return jnp.left_shift(2, g)


def _pool_fwd(name, proj0):
    s = proj0.shape[0]
    hb = ROW_TILE // HALO
    ext_rows = ROW_TILE + HALO

    def body(u_ref, halo_ref, p_ref):
        i = pl.program_id(0)
        g = pl.program_id(1)
        u = u_ref[...]
        halo = jnp.where(i == 0, 0.0, halo_ref[...])
        ext = jnp.concatenate([halo, u], axis=0)
        s2 = ext + pltpu.roll(ext, 1, axis=0)
        s4 = s2 + pltpu.roll(s2, 2, axis=0)
        s8 = s4 + pltpu.roll(s4, 4, axis=0)
        s16 = s8 + pltpu.roll(s8, 8, axis=0)
        win = jnp.where(g == 0, s2, jnp.where(g == 1, s4, jnp.where(g == 2, s8, s16)))[HALO:, :]
        t = i * ROW_TILE + lax.broadcasted_iota(jnp.int32, (ROW_TILE, 1), 0)
        cnt = jnp.minimum(t + 1, _window_of(g)).astype(F32)
        p_ref[...] = (win / cnt - u).astype(BF16)

    return pl.pallas_call(
        body, name=name, grid=(s // ROW_TILE, N_GROUPS),
        in_specs=[pl.BlockSpec((ROW_TILE, POOL_GROUP), lambda i, g: (i, g)),
                  pl.BlockSpec((HALO, POOL_GROUP), lambda i, g: (jnp.maximum(i * hb - 1, 0), g))],
        out_specs=pl.BlockSpec((ROW_TILE, POOL_GROUP), lambda i, g: (i, g)),
        out_shape=SDS((s, D_POOL), BF16), compiler_params=_params(2),
    )(proj0, proj0)


def _pool_bwd(name, dp):
    s = dp.shape[0]
    hb = ROW_TILE // HALO
    n_hb = s // HALO
    n_tiles = s // ROW_TILE
    ext_rows = ROW_TILE + HALO

    def body(dp_ref, halo_ref, du_ref):
        i = pl.program_id(0)
        g = pl.program_id(1)
        w = _window_of(g)
        dp_v = dp_ref[...]
        t = i * ROW_TILE + lax.broadcasted_iota(jnp.int32, (ext_rows, 1), 0)
        cnt = jnp.minimum(t + 1, w).astype(F32)
        halo = jnp.where(i == n_tiles - 1, 0.0, halo_ref[...])
        ext = jnp.concatenate([dp_v, halo], axis=0) / cnt
        s2 = ext + pltpu.roll(ext, ext_rows - 1, axis=0)
        s4 = s2 + pltpu.roll(s2, ext_rows - 2, axis=0)
        s8 = s4 + pltpu.roll(s4, ext_rows - 4, axis=0)
        s16 = s8 + pltpu.roll(s8, ext_rows - 8, axis=0)
        win = jnp.where(g == 0, s2, jnp.where(g == 1, s4, jnp.where(g == 2, s8, s16)))[:ROW_TILE, :]
        du_ref[...] = (win - dp_v).astype(BF16)

    return pl.pallas_call(
        body, name=name, grid=(n_tiles, N_GROUPS),
        in_specs=[pl.BlockSpec((ROW_TILE, POOL_GROUP), lambda i, g: (i, g)),
                  pl.BlockSpec((HALO, POOL_GROUP), lambda i, g: (jnp.minimum((i + 1) * hb, n_hb - 1), g))],
        out_specs=pl.BlockSpec((ROW_TILE, POOL_GROUP), lambda i, g: (i, g)),
        out_shape=SDS((s, D_POOL), BF16), compiler_params=_params(2),
    )(dp, dp)


FWD_HEADS_PER_STEP = 8
BWD_HEADS_PER_STEP = 4
ATT_SCALE = 0.125


def _att_groups(nh):
    lanes = nh * HEAD_DIM
    return lanes, D_SB // lanes, D_POOL // lanes, (D_POOL + D_SB) // lanes, (D_POOL + 2 * D_SB) // lanes


def _att_consts():
    r = lax.broadcasted_iota(jnp.int32, (ATT_TILE, ATT_TILE), 0)
    c = lax.broadcasted_iota(jnp.int32, (ATT_TILE, ATT_TILE), 1)
    first = lax.broadcasted_iota(jnp.int32, (1, LANES), 1) < HEAD_DIM
    return r, c, first


def _pair(x, p):
    return x[:, p * LANES:(p + 1) * LANES]


def _one_head(x, first, hh):
    zero = jnp.zeros_like(x)
    return jnp.where(first, x, zero) if hh == 0 else jnp.where(first, zero, x)


def _neg_softplus(z):
    return -(jnp.maximum(z, 0.0) + jnp.log(1.0 + jnp.exp(-jnp.abs(z))))


def _side_exchange(side_refs, n_side, by_chunk, is_first, is_last):
    ins, outs = side_refs[:n_side], side_refs[n_side:2 * n_side]
    sems = side_refs[2 * n_side:2 * n_side + 3]

    @pl.when(is_first)
    def _():
        for cp in _peer_copies(ins, outs, *sems, by_chunk=by_chunk):
            cp.start()

    @pl.when(is_last)
    def _():
        for cp in _peer_copies(ins, outs, *sems, by_chunk=by_chunk):
            cp.wait()


def _attn_fwd(name, proj0, shards):
    s = proj0.shape[0]
    nq = s // ATT_TILE
    nh = FWD_HEADS_PER_STEP
    ATT_GROUP, N_ATT_GROUPS, Q_GRP, K_GRP, V_GRP = _att_groups(nh)
    n_side = len(shards)

    def body(q_ref, k_ref, v_ref, *rest):
        o_ref = rest[n_side]
        side = rest[:n_side] + rest[n_side + 1:]
        j = pl.program_id(0)
        i = pl.program_id(1)
        _side_exchange(side, n_side, False, (j == 0) & (i == 0), (j == N_ATT_GROUPS - 1) & (i == nq - 1))
        r, c, first = _att_consts()
        tri = (r >= c).astype(BF16)
        below = c < r
        q = q_ref[...] * ATT_SCALE
        qh = [_one_head(_pair(q, h // 2), first, h % 2) for h in range(nh)]

        def tile(kb, carry, diagonal):
            k0 = pl.multiple_of(kb * ATT_TILE, ATT_TILE)
            kt = k_ref[pl.ds(k0, ATT_TILE), :]
            vt = v_ref[pl.ds(k0, ATT_TILE), :]
            z = [_dot_nt(qh[h], _pair(kt, h // 2)) for h in range(nh)]
            lf = [_neg_softplus(z[h]) for h in range(nh)]
            if diagonal:
                lf = [jnp.where(below, x, 0.0) for x in lf]
            parts = [_split_bf16(x) for x in lf]
            run = [_dot(parts[h][0], tri) + _dot(parts[h][1], tri) for h in range(nh)]
            a = [jnp.exp(z[h] + run[h] + carry[h]) for h in range(nh)]
            if diagonal:
                a = [jnp.where(below, x, 0.0) for x in a]
            out_acc = [carry[nh + h] + _dot(a[h].astype(BF16), _pair(vt, h // 2)) for h in range(nh)]
            out_c = [carry[h] + jnp.sum(lf[h], axis=1, keepdims=True) for h in range(nh)]
            return tuple(out_c + out_acc)

        init = tuple([jnp.zeros((ATT_TILE, 1), F32)] * nh + [jnp.zeros((ATT_TILE, LANES), F32)] * nh)
        carry = tile(i, init, True)
        carry = lax.fori_loop(1, i + 1, lambda n, cr: tile(i - n, cr, False), carry)
        for p in range(nh // 2):
            o_ref[:, p * LANES:(p + 1) * LANES] = jnp.where(first, carry[nh + 2 * p], carry[nh + 2 * p + 1])

    out = pl.pallas_call(
        body, name=name, grid=(N_ATT_GROUPS, nq),
        in_specs=[pl.BlockSpec((ATT_TILE, ATT_GROUP), lambda j, i: (i, Q_GRP + j)),
                  pl.BlockSpec((s, ATT_GROUP), lambda j, i: (0, K_GRP + j)),
                  pl.BlockSpec((s, ATT_GROUP), lambda j, i: (0, V_GRP + j))] + [HBM_SPEC] * n_side,
        out_specs=[pl.BlockSpec((ATT_TILE, ATT_GROUP), lambda j, i: (i, j))] + [HBM_SPEC] * n_side,
        out_shape=[SDS((s, D_SB), F32)] + [SDS((N_DEV,) + sh.shape, sh.dtype) for sh in shards],
        scratch_shapes=_peer_sems(n_side), compiler_params=_params(2),
    )(proj0, proj0, proj0, *shards)
    return out[0], out[1:]


def _attn_bwd(name, proj0, o, do, dws, shards):
    s = proj0.shape[0]
    nq = s // ATT_TILE
    nh = BWD_HEADS_PER_STEP
    ATT_GROUP, N_ATT_GROUPS, Q_GRP, K_GRP, V_GRP = _att_groups(nh)
    n1, n2 = len(dws), len(shards)
    n_side = n1 + n2

    def body(q_ref, k_ref, v_ref, o_ref, do_ref, *rest):
        dq_ref, dk_ref, dv_ref = rest[n_side:n_side + 3]
        dk_acc, dv_acc = rest[2 * n_side + 3:2 * n_side + 5]
        srcs, dsts, sems = rest[:n_side], rest[n_side + 3:2 * n_side + 3], rest[2 * n_side + 5:]
        j = pl.program_id(0)
        i = pl.program_id(1)
        is_first, is_last = (j == 0) & (i == 0), (j == N_ATT_GROUPS - 1) & (i == nq - 1)
        _side_exchange(srcs[:n1] + dsts[:n1] + sems[:3], n1, True, is_first, is_last)
        _side_exchange(srcs[n1:] + dsts[n1:] + sems[3:], n2, False, is_first, is_last)

        @pl.when(i == 0)
        def _():
            dk_acc[...] = jnp.zeros_like(dk_acc)
            dv_acc[...] = jnp.zeros_like(dv_acc)

        r, c, first = _att_consts()
        tri = (r >= c).astype(BF16)
        tri_x = (r > c).astype(BF16)
        below = c < r
        q = q_ref[...] * ATT_SCALE
        do_b = do_ref[...].astype(BF16)
        do_o = do_b.astype(F32) * o_ref[...]
        qh = [_one_head(_pair(q, h // 2), first, h % 2) for h in range(nh)]
        doh = [_one_head(_pair(do_b, h // 2), first, h % 2) for h in range(nh)]
        dsum = [jnp.sum(_one_head(_pair(do_o, h // 2), first, h % 2), axis=1, keepdims=True) for h in range(nh)]

        def tile(kb, carry, diagonal):
            k0 = pl.multiple_of(kb * ATT_TILE, ATT_TILE)
            kt = k_ref[pl.ds(k0, ATT_TILE), :]
            vt = v_ref[pl.ds(k0, ATT_TILE), :]
            hs = range(nh)
            z = [_dot_nt(qh[h], _pair(kt, h // 2)) for h in hs]
            d_a = [_dot_nt(doh[h], _pair(vt, h // 2)) for h in hs]
            lf = [_neg_softplus(z[h]) for h in hs]
            sig = [jnp.exp(z[h] + lf[h]) for h in hs]
            if diagonal:
                lf = [jnp.where(below, x, 0.0) for x in lf]
            parts = [_split_bf16(x) for x in lf]
            run = [_dot(parts[h][0], tri) + _dot(parts[h][1], tri) for h in hs]
            a = [jnp.exp(z[h] + run[h] + carry[h]) for h in hs]
            if diagonal:
                a = [jnp.where(below, x, 0.0) for x in a]
            a_b = [x.astype(BF16) for x in a]
            g = [a_b[h].astype(F32) * d_a[h] for h in hs]
            gparts = [_split_bf16(x) for x in g]
            later = [_dot(gparts[h][0], tri_x) + _dot(gparts[h][1], tri_x) for h in hs]
            dv_t = [_dot_tn(a_b[2 * p], doh[2 * p]) + _dot_tn(a_b[2 * p + 1], doh[2 * p + 1]) for p in range(nh // 2)]
            dz = [g[h] - sig[h] * (dsum[h] - later[h] - carry[nh + h]) for h in hs]
            if diagonal:
                dz = [jnp.where(below, x, 0.0) for x in dz]
            dz = [x.astype(BF16) for x in dz]
            out_dq = [carry[2 * nh + h] + _dot(dz[h], _pair(kt, h // 2)) for h in hs]
            dk_t = [_dot_tn(dz[2 * p], qh[2 * p]) + _dot_tn(dz[2 * p + 1], qh[2 * p + 1]) for p in range(nh // 2)]
            for p in range(nh // 2):
                dk_acc[pl.ds(k0, ATT_TILE), p * LANES:(p + 1) * LANES] += dk_t[p]
                dv_acc[pl.ds(k0, ATT_TILE), p * LANES:(p + 1) * LANES] += dv_t[p]
            out_c1 = [carry[h] + jnp.sum(lf[h], axis=1, keepdims=True) for h in hs]
            out_c2 = [carry[nh + h] + jnp.sum(g[h], axis=1, keepdims=True) for h in hs]
            return tuple(out_c1 + out_c2 + out_dq)

        init = tuple([jnp.zeros((ATT_TILE, 1), F32)] * (2 * nh) + [jnp.zeros((ATT_TILE, LANES), F32)] * nh)
        carry = tile(i, init, True)
        carry = lax.fori_loop(1, i + 1, lambda n, cr: tile(i - n, cr, False), carry)
        for p in range(nh // 2):
            dq_p = jnp.where(first, carry[2 * nh + 2 * p], carry[2 * nh + 2 * p + 1]) * ATT_SCALE
            dq_ref[:, p * LANES:(p + 1) * LANES] = dq_p.astype(BF16)

        @pl.when(i == nq - 1)
        def _():
            dk_ref[...] = dk_acc[...].astype(BF16)
            dv_ref[...] = dv_acc[...].astype(BF16)

    tile_spec = pl.BlockSpec((ATT_TILE, ATT_GROUP), lambda j, i: (i, j))
    full = pl.BlockSpec((s, ATT_GROUP), lambda j, i: (0, j))
    out = pl.pallas_call(
        body, name=name, grid=(N_ATT_GROUPS, nq),
        in_specs=[pl.BlockSpec((ATT_TILE, ATT_GROUP), lambda j, i: (i, Q_GRP + j)),
                  pl.BlockSpec((s, ATT_GROUP), lambda j, i: (0, K_GRP + j)),
                  pl.BlockSpec((s, ATT_GROUP), lambda j, i: (0, V_GRP + j)),
                  tile_spec, tile_spec] + [HBM_SPEC] * n_side,
        out_specs=[tile_spec, full, full] + [HBM_SPEC] * n_side,
        out_shape=[SDS((s, D_SB), BF16)] * 3 + [SDS(dw.shape, dw.dtype) for dw in dws]
        + [SDS((N_DEV,) + sh.shape, sh.dtype) for sh in shards],
        scratch_shapes=[pltpu.VMEM((s, ATT_GROUP), F32), pltpu.VMEM((s, ATT_GROUP), F32)] + _peer_sems(n1) + _peer_sems(n2),
        compiler_params=_params(2),
    )(proj0, proj0, proj0, o, do, *dws, *shards)
    return out[0], out[1], out[2], out[3:3 + n1], out[3 + n1:]


GATE0_COL = (D_POOL + 3 * D_SB) // D_INNER


def _gate_fwd0(name, yp_raw, o, proj0, ps):
    s = o.shape[0]

    def body(yp_ref, o_ref, gt_ref, ps_ref, y_ref):
        gt = gt_ref[...]
        sg = gt * _sigmoid(gt)
        y_ref[:, :D_POOL] = (yp_ref[...] * ps_ref[...] * sg[:, :D_POOL]).astype(BF16)
        y_ref[:, D_POOL:] = (o_ref[...] * sg[:, D_POOL:]).astype(BF16)

    return pl.pallas_call(
        body, name=name, grid=(s // ROW_TILE,),
        in_specs=[_row_spec(), _row_spec(), _row_spec(D_INNER, GATE0_COL), _vec_spec()],
        out_specs=_row_spec(D_INNER),
        out_shape=SDS((s, D_INNER), BF16), compiler_params=_params(1),
    )(yp_raw, o, proj0, ps)


def _dsilu(x):
    sg = _sigmoid(x)
    return sg * (1.0 + x * (1.0 - sg))


def _gate_bwd0(name, dymix, yp_raw, o, proj0, ps):
    s = o.shape[0]

    def body(dy_ref, yp_ref, o_ref, gt_ref, ps_ref, dyp_ref, do_ref, dgt_ref, dps_ref):
        i = pl.program_id(0)

        @pl.when(i == 0)
        def _():
            dps_ref[...] = jnp.zeros_like(dps_ref)

        gt = gt_ref[...]
        dy = dy_ref[...]
        sg = gt * _sigmoid(gt)
        dsg = _dsilu(gt)
        dcat = dy * sg
        yp = yp_ref[...]
        ps_v = ps_ref[...]
        dyp_ref[...] = (dcat[:, :D_POOL] * ps_v).astype(BF16)
        do_ref[...] = dcat[:, D_POOL:]
        dps_ref[...] += jnp.sum(dcat[:, :D_POOL] * yp, axis=0, keepdims=True)
        dgt_ref[:, :D_POOL] = (dy[:, :D_POOL] * (yp * ps_v) * dsg[:, :D_POOL]).astype(BF16)
        dgt_ref[:, D_POOL:] = (dy[:, D_POOL:] * o_ref[...] * dsg[:, D_POOL:]).astype(BF16)

    return pl.pallas_call(
        body, name=name, grid=(s // ROW_TILE,),
        in_specs=[_row_spec(D_INNER), _row_spec(), _row_spec(), _row_spec(D_INNER, GATE0_COL), _vec_spec()],
        out_specs=[_row_spec(), _row_spec(), _row_spec(D_INNER), _vec_spec()],
        out_shape=[SDS((s, D_POOL), BF16), SDS((s, D_SB), F32), SDS((s, D_INNER), BF16), SDS((1, D_POOL), F32)],
        compiler_params=_params(1),
    )(dymix, yp_raw, o, proj0, ps)


CONV_HALO = 8


def _conv_fwd(name, proj1, cw, cb):
    s = proj1.shape[0]
    hb = ROW_TILE // CONV_HALO
    ext_rows = ROW_TILE + CONV_HALO

    def body(gb_ref, gc_ref, u_ref, gt_ref, gch_ref, uh_ref, cw_ref, cb_ref, y_ref):
        i = pl.program_id(0)
        uc = gc_ref[...] * u_ref[...]
        halo = jnp.where(i == 0, 0.0, gch_ref[...] * uh_ref[...])
        ext = jnp.concatenate([halo, uc], axis=0)
        uc1 = pltpu.roll(ext, 1, axis=0)[CONV_HALO:, :]
        uc2 = pltpu.roll(ext, 2, axis=0)[CONV_HALO:, :]
        cw_v = cw_ref[...]
        conv = cb_ref[...] + cw_v[0:1, :] * uc2 + cw_v[1:2, :] * uc1 + cw_v[2:3, :] * uc
        gt = gt_ref[...]
        y_ref[...] = (gb_ref[...] * conv * (gt * _sigmoid(gt))).astype(BF16)

    def tile(part):
        return pl.BlockSpec((ROW_TILE, D_INNER), lambda i: (i, part))

    def halo(part):
        return pl.BlockSpec((CONV_HALO, D_INNER), lambda i: (jnp.maximum(i * hb - 1, 0), part))

    return pl.pallas_call(
        body, name=name, grid=(s // ROW_TILE,),
        in_specs=[tile(0), tile(1), tile(2), tile(3), halo(1), halo(2),
                  pl.BlockSpec((3, D_INNER), lambda i: (0, 0)), pl.BlockSpec((1, D_INNER), lambda i: (0, 0))],
        out_specs=pl.BlockSpec((ROW_TILE, D_INNER), lambda i: (i, 0)),
        out_shape=SDS((s, D_INNER), BF16), compiler_params=_params(1),
    )(proj1, proj1, proj1, proj1, proj1, proj1, cw, cb)


def _conv_bwd(name, dymix, proj1, cw, cb):
    s = proj1.shape[0]
    hb = ROW_TILE // CONV_HALO
    n_hb = s // CONV_HALO
    n_tiles = s // ROW_TILE
    ext_rows = ROW_TILE + CONV_HALO

    def body(dy_ref, gb_ref, gc_ref, u_ref, gt_ref, gch_ref, uh_ref, dyn_ref, gbn_ref, gtn_ref, cw_ref, cb_ref,
             dproj_ref, dcw_ref, dcb_ref):
        i = pl.program_id(0)

        @pl.when(i == 0)
        def _():
            dcw_ref[...] = jnp.zeros_like(dcw_ref)
            dcb_ref[...] = jnp.zeros_like(dcb_ref)

        gc = gc_ref[...]
        u = u_ref[...]
        gb = gb_ref[...]
        gt = gt_ref[...]
        dy = dy_ref[...]
        uc = gc * u
        halo = jnp.where(i == 0, 0.0, gch_ref[...] * uh_ref[...])
        ext = jnp.concatenate([halo, uc], axis=0)
        uc1 = pltpu.roll(ext, 1, axis=0)[CONV_HALO:, :]
        uc2 = pltpu.roll(ext, 2, axis=0)[CONV_HALO:, :]
        cw_v = cw_ref[...]
        w0, w1, w2 = cw_v[0:1, :], cw_v[1:2, :], cw_v[2:3, :]
        conv = cb_ref[...] + w0 * uc2 + w1 * uc1 + w2 * uc
        sg = gt * _sigmoid(gt)
        dconv = dy * gb * sg
        gtn = gtn_ref[...]
        dconv_next = jnp.where(i == n_tiles - 1, 0.0, dyn_ref[...] * gbn_ref[...] * (gtn * _sigmoid(gtn)))
        dext = jnp.concatenate([dconv, dconv_next], axis=0)
        dconv_p1 = pltpu.roll(dext, ext_rows - 1, axis=0)[:ROW_TILE, :]
        dconv_p2 = pltpu.roll(dext, ext_rows - 2, axis=0)[:ROW_TILE, :]
        duc = w2 * dconv + w1 * dconv_p1 + w0 * dconv_p2
        dproj_ref[:, 0:D_INNER] = (dy * conv * sg).astype(BF16)
        dproj_ref[:, D_INNER:2 * D_INNER] = (duc * u).astype(BF16)
        dproj_ref[:, 2 * D_INNER:3 * D_INNER] = (duc * gc).astype(BF16)
        dproj_ref[:, 3 * D_INNER:] = (dy * gb * conv * _dsilu(gt)).astype(BF16)
        dcw_ref[0:1, :] += jnp.sum(dconv * uc2, axis=0, keepdims=True)
        dcw_ref[1:2, :] += jnp.sum(dconv * uc1, axis=0, keepdims=True)
        dcw_ref[2:3, :] += jnp.sum(dconv * uc, axis=0, keepdims=True)
        dcb_ref[...] += jnp.sum(dconv, axis=0, keepdims=True)

    def tile(part):
        return pl.BlockSpec((ROW_TILE, D_INNER), lambda i: (i, part))

    def prev(part):
        return pl.BlockSpec((CONV_HALO, D_INNER), lambda i: (jnp.maximum(i * hb - 1, 0), part))

    def nxt(part):
        return pl.BlockSpec((CONV_HALO, D_INNER), lambda i: (jnp.minimum((i + 1) * hb, n_hb - 1), part))

    whole = lambda rows: pl.BlockSpec((rows, D_INNER), lambda i: (0, 0))
    return pl.pallas_call(
        body, name=name, grid=(n_tiles,),
        in_specs=[tile(0), tile(0), tile(1), tile(2), tile(3), prev(1), prev(2), nxt(0), nxt(0), nxt(3),
                  whole(3), whole(1)],
        out_specs=[pl.BlockSpec((ROW_TILE, 4 * D_INNER), lambda i: (i, 0)), whole(3), whole(1)],
        out_shape=[SDS((s, 4 * D_INNER), BF16), SDS((3, D_INNER), F32), SDS((1, D_INNER), F32)],
        compiler_params=_params(1),
    )(dymix, proj1, proj1, proj1, proj1, proj1, proj1, dymix, proj1, proj1, cw, cb)


def _place():
    x, y, c = lax.axis_index("x"), lax.axis_index("y"), lax.axis_index("c")
    return x, y, c


def _flip(x, y, c, k):
    fx, fy, fc = (k >> 2) & 1, (k >> 1) & 1, k & 1
    return (1 - x if fx else x, 1 - y if fy else y, 1 - c if fc else c)


def _dev_index(p):
    return 4 * p[0] + 2 * p[1] + p[2]


HBM_SPEC = pl.BlockSpec(memory_space=pltpu.HBM)
VMEM_SPEC = pl.BlockSpec(memory_space=pltpu.VMEM)


def _allgather_weights(shards):
    n_w = len(shards)

    def body(*refs):
        ins, outs = refs[:n_w], refs[n_w:2 * n_w]
        send_sems, recv_sems, local_sems = refs[2 * n_w:]
        x, y, c = _place()
        me, sibling = (x, y, c), (x, y, 1 - c)
        chips = [(1 - x, y), (x, 1 - y), (1 - x, 1 - y)]

        def copy(w, k, block, to, src=None):
            rows = outs[w].at[_dev_index(block)]
            return pltpu.make_async_remote_copy(
                src_ref=rows if src is None else src, dst_ref=rows,
                send_sem=send_sems.at[7 * w + k], recv_sem=recv_sems.at[7 * w + k],
                device_id=to, device_id_type=MESH)

        mine, first, passed = [], [], []
        for w in range(n_w):
            cp = pltpu.make_async_copy(ins[w], outs[w].at[_dev_index(me)], local_sems.at[w])
            cp.start()
            mine.append(cp)
            fw = [copy(w, 0, me, sibling, src=ins[w])]
            fw += [copy(w, 1 + j, me, (*chip, c), src=ins[w]) for j, chip in enumerate(chips)]
            for cp in fw:
                cp.start()
            first += fw
        for w in range(n_w):
            for j, chip in enumerate(chips):
                copy(w, 1 + j, (*chip, c), me).wait_recv()
                cp = copy(w, 4 + j, (*chip, c), sibling)
                cp.start()
                passed.append(cp)
        for w in range(n_w):
            copy(w, 0, sibling, me).wait_recv()
            for j, chip in enumerate(chips):
                copy(w, 4 + j, (*chip, 1 - c), me).wait_recv()
        for cp in first + passed:
            cp.wait_send()
        for cp in mine:
            cp.wait()

    return pl.pallas_call(
        body, name="allgather_weights",
        out_shape=[SDS((N_DEV,) + sh.shape, sh.dtype) for sh in shards],
        in_specs=[HBM_SPEC] * n_w, out_specs=[HBM_SPEC] * n_w,
        scratch_shapes=[pltpu.SemaphoreType.DMA((7 * n_w,)), pltpu.SemaphoreType.DMA((7 * n_w,)),
                        pltpu.SemaphoreType.DMA((n_w,))],
    )(*shards)


def _peer_copies(ins, outs, send_sems, recv_sems, local_sems, by_chunk):
    x, y, c = _place()
    my = _dev_index((x, y, c))
    copies = []
    for w in range(len(ins)):
        copies.append(pltpu.make_async_copy(ins[w].at[my] if by_chunk else ins[w], outs[w].at[my], local_sems.at[w]))
        for k in range(1, N_DEV):
            peer = _flip(x, y, c, k)
            copies.append(pltpu.make_async_remote_copy(
                src_ref=ins[w].at[_dev_index(peer)] if by_chunk else ins[w], dst_ref=outs[w].at[my],
                send_sem=send_sems.at[7 * w + k - 1], recv_sem=recv_sems.at[7 * w + k - 1],
                device_id=peer, device_id_type=MESH))
    return copies


def _peer_sems(n_w):
    return [pltpu.SemaphoreType.DMA((7 * n_w,)), pltpu.SemaphoreType.DMA((7 * n_w,)), pltpu.SemaphoreType.DMA((n_w,))]


ADA_COLS = 3 * D_MODEL // N_DEV


def _ada_forward(c_row, conv_w, conv_b, ada_w, ada_b):
    cw_cols = conv_w.shape[1]

    def body(c_ref, cw_ref, cb_ref, aw_ref, ab_ref, m_ref, cs_ref, cwf_ref, cbf_ref,
             slab, gath, part, land, send_sems, recv_sems):
        x, y, c = _place()
        my = _dev_index((x, y, c))
        slab[...] = jnp.zeros_like(slab)
        slab[0:1, :] = c_ref[...]
        slab[1:4, 0:cw_cols] = cw_ref[...]
        slab[4:5, 0:cw_cols] = cb_ref[...]
        gath[my] = slab[...]
        sends = []
        for k in range(1, N_DEV):
            peer = _flip(x, y, c, k)
            cp = pltpu.make_async_remote_copy(
                src_ref=slab, dst_ref=gath.at[my], send_sem=send_sems.at[k - 1], recv_sem=recv_sems.at[k - 1],
                device_id=peer, device_id_type=MESH)
            cp.start()
            sends.append(cp)
        for cp in sends:
            cp.wait()
        for d in range(N_DEV):
            c_d = gath[d, 0:1, :]
            cs_ref[d:d + 1, :] = c_d * _sigmoid(c_d)
            cwf_ref[:, d * cw_cols:(d + 1) * cw_cols] = gath[d, 1:4, 0:cw_cols]
            cbf_ref[:, d * cw_cols:(d + 1) * cw_cols] = gath[d, 4:5, 0:cw_cols]
        cs = cs_ref[...]
        part[...] = jnp.zeros_like(part)
        for layer in range(2):
            m_part = jnp.dot(cs, aw_ref[layer], preferred_element_type=F32, precision=lax.Precision.HIGHEST)
            for d in range(N_DEV):
                part[d, layer:layer + 1, :] = m_part[d:d + 1, :]
        land[my] = part[my]
        sends = []
        for k in range(1, N_DEV):
            peer = _flip(x, y, c, k)
            cp = pltpu.make_async_remote_copy(
                src_ref=part.at[_dev_index(peer)], dst_ref=land.at[my],
                send_sem=send_sems.at[6 + k], recv_sem=recv_sems.at[6 + k],
                device_id=peer, device_id_type=MESH)
            cp.start()
            sends.append(cp)
        for cp in sends:
            cp.wait()
        for d in range(N_DEV):
            cols = slice(d * ADA_COLS, (d + 1) * ADA_COLS)
            m_ref[:, cols] = land[d, 0:2, :] + ab_ref[:, cols]

    return pl.pallas_call(
        body, name="ada_forward",
        out_shape=[SDS((2, 3 * D_MODEL), F32), SDS((N_DEV, D_MODEL), F32), SDS((3, N_DEV * cw_cols), F32),
                   SDS((1, N_DEV * cw_cols), F32)],
        in_specs=[VMEM_SPEC] * 5, out_specs=[VMEM_SPEC] * 4,
        scratch_shapes=[pltpu.VMEM((8, D_MODEL), F32), pltpu.VMEM((N_DEV, 8, D_MODEL), F32),
                        pltpu.VMEM((N_DEV, 8, ADA_COLS), F32), pltpu.VMEM((N_DEV, 8, ADA_COLS), F32),
                        pltpu.SemaphoreType.DMA((14,)), pltpu.SemaphoreType.DMA((14,))],
        compiler_params=pltpu.CompilerParams(vmem_limit_bytes=VMEM_LIMIT),
    )(c_row, conv_w, conv_b, ada_w, ada_b)


def _small_grads(slab):
    def body(slab_ref, gath_ref, tot_ref, send_sems, recv_sems):
        x, y, c = _place()
        my = _dev_index((x, y, c))
        gath_ref[my] = slab_ref[...]
        sends = []
        for k in range(1, N_DEV):
            peer = _flip(x, y, c, k)
            cp = pltpu.make_async_remote_copy(
                src_ref=slab_ref, dst_ref=gath_ref.at[my], send_sem=send_sems.at[k - 1], recv_sem=recv_sems.at[k - 1],
                device_id=peer, device_id_type=MESH)
            cp.start()
            sends.append(cp)
        for cp in sends:
            cp.wait()
        tot = gath_ref[0]
        for d in range(1, N_DEV):
            tot = tot + gath_ref[d]
        tot_ref[...] = tot

    return pl.pallas_call(
        body, name="small_grads",
        out_shape=[SDS((N_DEV, SLAB_ROWS, D_MODEL), F32), SDS((SLAB_ROWS, D_MODEL), F32)],
        in_specs=[VMEM_SPEC], out_specs=[VMEM_SPEC] * 2,
        scratch_shapes=[pltpu.SemaphoreType.DMA((7,)), pltpu.SemaphoreType.DMA((7,))],
    )(slab)


def _adamw_math(w, g, m, v):
    m = ADAM_B1 * m + (1.0 - ADAM_B1) * g
    v = ADAM_B2 * v + (1.0 - ADAM_B2) * jnp.square(g)
    m_hat = m / (1.0 - ADAM_B1 ** ADAM_STEP)
    v_hat = v / (1.0 - ADAM_B2 ** ADAM_STEP)
    delta = -ADAM_LR * (m_hat / (jnp.sqrt(v_hat) + ADAM_EPS) + ADAM_WD * w)
    return delta, m, v


def _sum_adamw(name, recv, w, m, v):
    rows, cols = w.shape
    tr = min(rows, 256)

    def body(r_ref, w_ref, m_ref, v_ref, g_ref, d_ref, nm_ref, nv_ref):
        g = r_ref[0].astype(F32)
        for d in range(1, N_DEV):
            g = g + r_ref[d].astype(F32)
        g_ref[...] = g
        d_ref[...], nm_ref[...], nv_ref[...] = _adamw_math(w_ref[...], g, m_ref[...], v_ref[...])

    blk = pl.BlockSpec((tr, cols), lambda i: (i, 0))
    return pl.pallas_call(
        body, name=name, grid=(rows // tr,),
        in_specs=[pl.BlockSpec((N_DEV, tr, cols), lambda i: (0, i, 0)), blk, blk, blk],
        out_specs=[blk] * 4, out_shape=[SDS((rows, cols), F32)] * 4, compiler_params=_params(1),
    )(recv, w, m, v)


def _ada_w_adamw(name, cs_t, dm_cols, w, m, v):
    def body(cs_ref, dm_ref, w_ref, m_ref, v_ref, g_ref, d_ref, nm_ref, nv_ref):
        cs = cs_ref[...]
        dm = dm_ref[...]
        g = cs[:, 0:1] * dm[0:1, :]
        for b in range(1, N_DEV):
            g = g + cs[:, b:b + 1] * dm[b:b + 1, :]
        g_ref[...] = g
        d_ref[...], nm_ref[...], nv_ref[...] = _adamw_math(w_ref[...], g, m_ref[...], v_ref[...])

    blk = pl.BlockSpec((None, D_MODEL, ADA_COLS), lambda l: (l, 0, 0))
    return pl.pallas_call(
        body, name=name, grid=(2,),
        in_specs=[pl.BlockSpec((D_MODEL, N_DEV), lambda l: (0, 0)),
                  pl.BlockSpec((None, N_DEV, ADA_COLS), lambda l: (l, 0, 0)), blk, blk, blk],
        out_specs=[blk] * 4, out_shape=[SDS((2, D_MODEL, ADA_COLS), F32)] * 4, compiler_params=_params(1),
    )(cs_t, dm_cols, w, m, v)


def _small_adamw(name, triples):
    n = len(triples)

    def body(*refs):
        ins, outs = refs[:4 * n], refs[4 * n:]
        for j in range(n):
            w_ref, g_ref, m_ref, v_ref = ins[4 * j:4 * j + 4]
            d, nm, nv = _adamw_math(w_ref[...], g_ref[...], m_ref[...], v_ref[...])
            outs[3 * j][...] = d
            outs[3 * j + 1][...] = nm
            outs[3 * j + 2][...] = nv

    flat = [a for t in triples for a in t]
    return pl.pallas_call(
        body, name=name,
        out_shape=[SDS(t[0].shape, F32) for t in triples for _ in range(3)],
        in_specs=[VMEM_SPEC] * (4 * n), out_specs=[VMEM_SPEC] * (3 * n),
    )(*flat)


def kernel(x, c, norm_g, ada_w, ada_b, even_w_in, pool_w, pool_scale, even_w_out, odd_w_in, conv_w, conv_b, odd_w_out, final_g, loss_target, m_norm_g, m_ada_w, m_ada_b, m_even_w_in, m_pool_w, m_pool_scale, m_even_w_out, m_odd_w_in, m_conv_w, m_conv_b, m_odd_w_out, m_final_g, v_norm_g, v_ada_w, v_ada_b, v_even_w_in, v_pool_w, v_pool_scale, v_even_w_out, v_odd_w_in, v_conv_w, v_conv_b, v_odd_w_out, v_final_g):
    seq = x.shape[1]
    x0 = x[0]
    target = loss_target[0]
    final_g2 = final_g.reshape(1, D_MODEL)

    w_in_e = even_w_in[0]
    w_out_e = even_w_out[0]
    w_in_o = odd_w_in[0]
    w_out_o = odd_w_out[0]
    w_pool = pool_w[0].reshape(N_GROUPS * 32, POOL_GROUP)
    shards = [w.astype(BF16) for w in (w_in_e, w_out_e, w_in_o, w_out_o, w_pool)]
    (wg_in_e,) = _allgather_weights(shards[:1])

    m_vec, cs_all, conv_w_full, conv_b_full = _ada_forward(c, conv_w[0], conv_b, ada_w, ada_b)
    shift = [m_vec[l:l + 1, 0:D_MODEL] for l in range(2)]
    scale = [m_vec[l:l + 1, D_MODEL:2 * D_MODEL] for l in range(2)]
    gate = [m_vec[l:l + 1, 2 * D_MODEL:] for l in range(2)]
    ng = [norm_g[l:l + 1] for l in range(2)]

    h0, h0_t = _ln_mod("ln_mod0", x0, ng[0], scale[0], shift[0])
    proj0, proj0_b = _proj_in_dual("proj_in0", h0, wg_in_e)
    o, (wg_out_e, wg_in_o, wg_out_o, wg_pool, wt_in_o) = _attn_fwd("attn_fwd", proj0_b, shards[1:] + [shards[2].T])
    wf_out_e = wg_out_e.reshape(D_INNER, D_MODEL)
    wf_out_o = wg_out_o.reshape(D_INNER, D_MODEL)
    wf_pool = wg_pool.reshape(N_DEV, N_GROUPS, 32, POOL_GROUP).transpose(1, 0, 2, 3).reshape(N_GROUPS, POOL_GROUP, POOL_GROUP)
    p = _pool_fwd("pool_fwd", proj0)
    yp_raw = _pool_mm("pool_mix", p, wf_pool, _dot)
    ymix0 = _gate_fwd0("gate_fwd0", yp_raw, o, proj0, pool_scale)
    yo0 = _proj_out("proj_out0", ymix0, wf_out_e)

    x1, h1, h1_t = _resid_ln_mod("resid_ln_mod1", x0, yo0, gate[0], ng[1], scale[1], shift[1])
    proj1 = _proj_in("proj_in1", h1, wg_in_o)
    ymix1 = _conv_fwd("conv_fwd", proj1, conv_w_full, conv_b_full)
    yo1 = _proj_out("proj_out1", ymix1, wf_out_o)

    dx2, dyo1, loss_acc, d_final_g, d_gate1 = _final_loss("final_loss", x1, yo1, gate[1], final_g2, target)
    loss = lax.psum(loss_acc[0, 0], ("x", "y", "c"))

    dymix1 = _proj_out_bwd("proj_out1_bwd", dyo1, wf_out_o)
    dw_out_o = _wgrad_out("wgrad_out1", ymix1, dyo1)
    dproj1, d_conv_w, d_conv_b = _conv_bwd("conv_bwd", dymix1, proj1, conv_w_full, conv_b_full)
    dh1, _ = _proj_in_bwd("proj_in1_bwd", dproj1, wt_in_o)
    dw_in_o = _wgrad_in("wgrad_in1", h1_t, dproj1)
    dx1, d_shift1, d_scale1, d_ng1 = _ln_mod_bwd("ln_mod1_bwd", dh1, x1, dx2, ng[1], scale[1])

    dyo0, d_gate0 = _resid_bwd("resid0_bwd", dx1, yo0, gate[0])
    dymix0 = _proj_out_bwd("proj_out0_bwd", dyo0, wf_out_e)
    dw_out_e = _wgrad_out("wgrad_out0", ymix0, dyo0)
    dyp, do, dgt0, d_pool_scale = _gate_bwd0("gate_bwd0", dymix0, yp_raw, o, proj0, pool_scale)
    dp = _pool_mm("pool_mix_bwd", dyp, wf_pool, _dot_nt)
    dw_pool = _pool_wgrad("pool_wgrad", p, dyp)
    du_pool = _pool_bwd("pool_bwd", dp)
    dw_pool_c = dw_pool.reshape(N_GROUPS, N_DEV, 32, POOL_GROUP).transpose(1, 0, 2, 3).reshape(N_DEV, N_GROUPS * 32, POOL_GROUP).astype(BF16)
    ready = [dw_out_e.reshape(N_DEV, D_INNER // N_DEV, D_MODEL), dw_in_o,
             dw_out_o.reshape(N_DEV, D_INNER // N_DEV, D_MODEL), dw_pool_c]
    dq, dk, dv, (r_out_e, r_in_o, r_out_o, r_pool), (wt_in_e,) = _attn_bwd(
        "attn_bwd", proj0_b, o, do, ready, [shards[0].T])
    dproj0 = jnp.concatenate([du_pool, dq, dk, dv, dgt0], axis=1)
    dw_in_e = _wgrad_in("wgrad_in0", h0_t, dproj0)
    dh0, (r_in_e,) = _proj_in_bwd("proj_in0_bwd", dproj0, wt_in_e, [dw_in_e])
    dx0, d_shift0, d_scale0, d_ng0 = _ln_mod_bwd("ln_mod0_bwd", dh0, x0, dx1, ng[0], scale[0])
    grad_x = dx0[None]

    big = {}
    big["even_w_in"] = _sum_adamw("adamw_even_w_in", r_in_e, w_in_e, m_even_w_in[0], v_even_w_in[0])
    big["even_w_out"] = _sum_adamw("adamw_even_w_out", r_out_e, w_out_e, m_even_w_out[0], v_even_w_out[0])
    big["odd_w_in"] = _sum_adamw("adamw_odd_w_in", r_in_o, w_in_o, m_odd_w_in[0], v_odd_w_in[0])
    big["odd_w_out"] = _sum_adamw("adamw_odd_w_out", r_out_o, w_out_o, m_odd_w_out[0], v_odd_w_out[0])
    big["pool_w"] = _sum_adamw("adamw_pool_w", r_pool, w_pool, m_pool_w[0].reshape(N_GROUPS * 32, POOL_GROUP),
                               v_pool_w[0].reshape(N_GROUPS * 32, POOL_GROUP))
    big = {k: [a.reshape(shape) for a in v] for (k, v), shape in zip(
        big.items(), [even_w_in.shape, even_w_out.shape, odd_w_in.shape, odd_w_out.shape, pool_w.shape])}

    dm = jnp.concatenate([jnp.concatenate([d_shift0, d_scale0, d_gate0], axis=1),
                          jnp.concatenate([d_shift1, d_scale1, d_gate1], axis=1)], axis=0)
    slab = jnp.zeros((SLAB_ROWS, D_MODEL), F32)
    slab = slab.at[0:6].set(dm.reshape(6, D_MODEL))
    slab = slab.at[8:9].set(d_ng0).at[9:10].set(d_ng1).at[10:11].set(d_pool_scale).at[11:12].set(d_final_g)
    slab = slab.at[16:22].set(d_conv_w.reshape(6, D_MODEL)).at[24:26].set(d_conv_b.reshape(2, D_MODEL))
    gathered, total = _small_grads(slab)
    my = 4 * lax.axis_index("x") + 2 * lax.axis_index("y") + lax.axis_index("c")
    g_ada_b = total[0:6].reshape(2, 3 * D_MODEL)
    g_norm_g = total[8:10]
    g_pool_scale = total[10:11]
    g_final_g = total[11:12]
    cw_cols = conv_w.shape[2]
    g_conv_w = lax.dynamic_slice_in_dim(total[16:22].reshape(3, D_INNER), my * cw_cols, cw_cols, axis=1)
    g_conv_b = lax.dynamic_slice_in_dim(total[24:26].reshape(1, D_INNER), my * cw_cols, cw_cols, axis=1)
    dm_all = gathered[:, 0:6, :].reshape(N_DEV, 2, 3 * D_MODEL)
    dm_cols = lax.dynamic_slice_in_dim(dm_all, my * ADA_COLS, ADA_COLS, axis=2).transpose(1, 0, 2)
    ada = _ada_w_adamw("adamw_ada_w", cs_all.T, dm_cols, ada_w, m_ada_w, v_ada_w)

    small = _small_adamw("adamw_small", [
        (norm_g, g_norm_g, m_norm_g, v_norm_g),
        (ada_b, g_ada_b, m_ada_b, v_ada_b),
        (pool_scale, g_pool_scale, m_pool_scale, v_pool_scale),
        (conv_w[0], g_conv_w, m_conv_w[0], v_conv_w[0]),
        (conv_b, g_conv_b, m_conv_b, v_conv_b),
        (final_g2, g_final_g, m_final_g.reshape(1, D_MODEL), v_final_g.reshape(1, D_MODEL)),
    ])
    small = [small[3 * j:3 * j + 3] for j in range(6)]

    grads = {
        "norm_g": g_norm_g, "ada_w": ada[0], "ada_b": g_ada_b, "even_w_in": big["even_w_in"][0],
        "pool_w": big["pool_w"][0], "pool_scale": g_pool_scale, "even_w_out": big["even_w_out"][0],
        "odd_w_in": big["odd_w_in"][0], "conv_w": g_conv_w.reshape(conv_w.shape), "conv_b": g_conv_b,
        "odd_w_out": big["odd_w_out"][0], "final_g": g_final_g.reshape(D_MODEL),
    }
    rest = []
    for idx in range(3):
        rest += [
            small[0][idx], ada[1 + idx], small[1][idx], big["even_w_in"][1 + idx], big["pool_w"][1 + idx],
            small[2][idx], big["even_w_out"][1 + idx], big["odd_w_in"][1 + idx],
            small[3][idx].reshape(conv_w.shape), small[4][idx], big["odd_w_out"][1 + idx],
            small[5][idx].reshape(D_MODEL),
        ]
    order = ["norm_g", "ada_w", "ada_b", "even_w_in", "pool_w", "pool_scale", "even_w_out", "odd_w_in",
             "conv_w", "conv_b", "odd_w_out", "final_g"]
    return (loss, grad_x, *[grads[n] for n in order], *rest)
```

```python
import jax
import jax.numpy as jnp
from jax import lax
from jax.experimental import pallas as pl
from jax.experimental.pallas import tpu as pltpu

F32 = jnp.float32
BF16 = jnp.bfloat16
SDS = jax.ShapeDtypeStruct
MESH = pl.DeviceIdType.MESH

N_DEV = 8
D_MODEL = 1024
D_INNER = 2048
D_POOL = 1024
D_SB = 1024
N_GROUPS = 4
POOL_GROUP = 256
HEAD_DIM = 64
LANES = 128
D_IN_EVEN = 6144
D_IN_ODD = 8192
EPS = 1e-6
ADAM_LR = 0.001
ADAM_B1 = 0.9
ADAM_B2 = 0.999
ADAM_EPS = 1e-08
ADAM_WD = 0.01
ADAM_STEP = 10

ROW_TILE = 256
ATT_TILE = 256
HALO = 16
VMEM_LIMIT = 48 * 1024 * 1024
SLAB_ROWS = 32


def _params(n_axes):
    return pltpu.CompilerParams(dimension_semantics=("arbitrary",) * n_axes, vmem_limit_bytes=VMEM_LIMIT)


def _sigmoid(x):
    return 1.0 / (1.0 + jnp.exp(-x))


def _dot(a, b):
    return jnp.dot(a, b, preferred_element_type=F32)


def _dot_nt(a, b):
    return lax.dot_general(a, b, (((1,), (1,)), ((), ())), preferred_element_type=F32)


def _dot_tn(a, b):
    return lax.dot_general(a, b, (((0,), (0,)), ((), ())), preferred_element_type=F32)


def _mm(name, a, b, *, grid, a_spec, b_spec, o_spec, o_shape, o_dtype, dot, acc_axis=None, acc_shape=None):
    n_acc = grid[acc_axis] if acc_axis is not None else 1

    def body(a_ref, b_ref, o_ref, *scratch):
        prod = dot(a_ref[...], b_ref[...])
        if acc_axis is None:
            o_ref[...] = prod.astype(o_dtype)
        else:
            acc = scratch[0]
            k = pl.program_id(acc_axis)

            @pl.when(k == 0)
            def _():
                acc[...] = prod

            @pl.when(k > 0)
            def _():
                acc[...] += prod

            @pl.when(k == n_acc - 1)
            def _():
                o_ref[...] = acc[...].astype(o_dtype)

    scratch = [] if acc_axis is None else [pltpu.VMEM(acc_shape, F32)]
    return pl.pallas_call(
        body, name=name, grid=grid, in_specs=[a_spec, b_spec], out_specs=o_spec,
        out_shape=SDS(o_shape, o_dtype), scratch_shapes=scratch, compiler_params=_params(len(grid)),
    )(a, b)


def _proj_in(name, h, wg):
    s = h.shape[0]
    cn = wg.shape[2]
    tm = min(s, 512)
    return _mm(name, h, wg, grid=(N_DEV, s // tm),
               a_spec=pl.BlockSpec((tm, D_MODEL), lambda d, i: (i, 0)),
               b_spec=pl.BlockSpec((None, D_MODEL, cn), lambda d, i: (d, 0, 0)),
               o_spec=pl.BlockSpec((tm, cn), lambda d, i: (i, d)),
               o_shape=(s, N_DEV * cn), o_dtype=BF16, dot=_dot)


def _proj_out(name, y, w):
    s = y.shape[0]
    tm = min(s, 512)
    return _mm(name, y, w, grid=(s // tm,),
               a_spec=pl.BlockSpec((tm, D_INNER), lambda i: (i, 0)),
               b_spec=pl.BlockSpec((D_INNER, D_MODEL), lambda i: (0, 0)),
               o_spec=pl.BlockSpec((tm, D_MODEL), lambda i: (i, 0)),
               o_shape=(s, D_MODEL), o_dtype=F32, dot=_dot)


def _proj_out_bwd(name, dyo, w):
    s = dyo.shape[0]
    tm = min(s, 512)
    return _mm(name, dyo, w, grid=(s // tm,),
               a_spec=pl.BlockSpec((tm, D_MODEL), lambda i: (i, 0)),
               b_spec=pl.BlockSpec((D_INNER, D_MODEL), lambda i: (0, 0)),
               o_spec=pl.BlockSpec((tm, D_INNER), lambda i: (i, 0)),
               o_shape=(s, D_INNER), o_dtype=F32, dot=_dot_nt)


def _wgrad_out(name, y, dyo):
    s = y.shape[0]
    ts = min(s, 512)
    return _mm(name, y, dyo, grid=(s // ts,),
               a_spec=pl.BlockSpec((ts, D_INNER), lambda k: (k, 0)),
               b_spec=pl.BlockSpec((ts, D_MODEL), lambda k: (k, 0)),
               o_spec=pl.BlockSpec((D_INNER, D_MODEL), lambda k: (0, 0)),
               o_shape=(D_INNER, D_MODEL), o_dtype=BF16, dot=_dot_tn, acc_axis=0, acc_shape=(D_INNER, D_MODEL))


def _proj_in_bwd(name, dproj, wt, dws=()):
    s, k_all = dproj.shape
    tm = min(s, ROW_TILE)
    n_i = s // tm
    n_side = len(dws)
    w_all = wt.reshape(k_all, D_MODEL)

    def body(a_ref, b_ref, *rest):
        o_ref = rest[n_side]
        side = rest[:n_side] + rest[n_side + 1:]
        i = pl.program_id(0)
        if n_side:
            _side_exchange(side, n_side, True, i == 0, i == n_i - 1)
        o_ref[...] = _dot(a_ref[...], b_ref[...])

    out = pl.pallas_call(
        body, name=name, grid=(n_i,),
        in_specs=[pl.BlockSpec((tm, k_all), lambda i: (i, 0)),
                  pl.BlockSpec((k_all, D_MODEL), lambda i: (0, 0), pipeline_mode=pl.Buffered(1))] + [HBM_SPEC] * n_side,
        out_specs=[pl.BlockSpec((tm, D_MODEL), lambda i: (i, 0))] + [HBM_SPEC] * n_side,
        out_shape=[SDS((s, D_MODEL), F32)] + [SDS(dw.shape, dw.dtype) for dw in dws],
        scratch_shapes=_peer_sems(n_side) if n_side else [],
        compiler_params=_params(1),
    )(dproj, w_all, *dws)
    return out[0], out[1:]


def _wgrad_in(name, h_t, dproj):
    s = h_t.shape[1]
    cn = dproj.shape[1] // N_DEV

    def body(a_ref, b_ref, o_ref):
        o_ref[...] = _dot(a_ref[...], b_ref[...]).astype(BF16)

    return pl.pallas_call(
        body, name=name, grid=(N_DEV,),
        in_specs=[pl.BlockSpec((D_MODEL, s), lambda d: (0, 0), pipeline_mode=pl.Buffered(1)),
                  pl.BlockSpec((s, cn), lambda d: (0, d))],
        out_specs=pl.BlockSpec((None, D_MODEL, cn), lambda d: (d, 0, 0)),
        out_shape=SDS((N_DEV, D_MODEL, cn), BF16), compiler_params=_params(1),
    )(h_t, dproj)


def _pool_mm(name, p, wp, dot):
    s = p.shape[0]
    tm = min(s, 512)
    return _mm(name, p, wp, grid=(s // tm, N_GROUPS),
               a_spec=pl.BlockSpec((tm, POOL_GROUP), lambda i, g: (i, g)),
               b_spec=pl.BlockSpec((None, POOL_GROUP, POOL_GROUP), lambda i, g: (g, 0, 0)),
               o_spec=pl.BlockSpec((tm, POOL_GROUP), lambda i, g: (i, g)),
               o_shape=(s, D_POOL), o_dtype=F32, dot=dot)


def _pool_wgrad(name, p, dyp):
    s = p.shape[0]
    ts = min(s, 512)
    return _mm(name, p, dyp, grid=(N_GROUPS, s // ts),
               a_spec=pl.BlockSpec((ts, POOL_GROUP), lambda g, k: (k, g)),
               b_spec=pl.BlockSpec((ts, POOL_GROUP), lambda g, k: (k, g)),
               o_spec=pl.BlockSpec((None, POOL_GROUP, POOL_GROUP), lambda g, k: (g, 0, 0)),
               o_shape=(N_GROUPS, POOL_GROUP, POOL_GROUP), o_dtype=F32, dot=_dot_tn, acc_axis=1,
               acc_shape=(POOL_GROUP, POOL_GROUP))


def _vec_spec():
    return pl.BlockSpec((1, D_MODEL), lambda i: (0, 0))


def _row_spec(width=D_MODEL, col=0):
    return pl.BlockSpec((ROW_TILE, width), lambda i: (i, col))


def _col_spec():
    return pl.BlockSpec((D_MODEL, ROW_TILE), lambda i: (0, i))


def _ln_mod(name, x, g, scale, shift):
    s = x.shape[0]

    def body(x_ref, g_ref, sc_ref, sh_ref, h_ref, ht_ref):
        xv = x_ref[...]
        r = lax.rsqrt(jnp.mean(xv * xv, axis=-1, keepdims=True) + EPS)
        n = (xv * r) * g_ref[...]
        h = (n * (1.0 + sc_ref[...]) + sh_ref[...]).astype(BF16)
        h_ref[...] = h
        ht_ref[...] = h.T

    return pl.pallas_call(
        body, name=name, grid=(s // ROW_TILE,),
        in_specs=[_row_spec(), _vec_spec(), _vec_spec(), _vec_spec()], out_specs=[_row_spec(), _col_spec()],
        out_shape=[SDS((s, D_MODEL), BF16), SDS((D_MODEL, s), BF16)], compiler_params=_params(1),
    )(x, g, scale, shift)


def _resid_ln_mod(name, x, yo, gate, g, scale, shift):
    s = x.shape[0]

    def body(x_ref, yo_ref, gt_ref, g_ref, sc_ref, sh_ref, xn_ref, h_ref, ht_ref):
        xv = x_ref[...] + (1.0 + gt_ref[...]) * yo_ref[...]
        xn_ref[...] = xv
        r = lax.rsqrt(jnp.mean(xv * xv, axis=-1, keepdims=True) + EPS)
        n = (xv * r) * g_ref[...]
        h = (n * (1.0 + sc_ref[...]) + sh_ref[...]).astype(BF16)
        h_ref[...] = h
        ht_ref[...] = h.T

    return pl.pallas_call(
        body, name=name, grid=(s // ROW_TILE,),
        in_specs=[_row_spec(), _row_spec(), _vec_spec(), _vec_spec(), _vec_spec(), _vec_spec()],
        out_specs=[_row_spec(), _row_spec(), _col_spec()],
        out_shape=[SDS((s, D_MODEL), F32), SDS((s, D_MODEL), BF16), SDS((D_MODEL, s), BF16)],
        compiler_params=_params(1),
    )(x, yo, gate, g, scale, shift)


def _final_loss(name, x1, yo1, gate1, gf, target):
    s = x1.shape[0]

    def body(x_ref, yo_ref, gt_ref, gf_ref, t_ref, dx_ref, dyo_ref, loss_ref, dgf_ref, dgt_ref):
        i = pl.program_id(0)

        @pl.when(i == 0)
        def _():
            loss_ref[...] = jnp.zeros_like(loss_ref)
            dgf_ref[...] = jnp.zeros_like(dgf_ref)
            dgt_ref[...] = jnp.zeros_like(dgt_ref)

        yo = yo_ref[...]
        one_gate = 1.0 + gt_ref[...]
        x2 = x_ref[...] + one_gate * yo
        r = lax.rsqrt(jnp.mean(x2 * x2, axis=-1, keepdims=True) + EPS)
        xn = x2 * r
        gf_v = gf_ref[...]
        err = xn * gf_v - t_ref[...]
        loss_ref[...] += 0.5 * jnp.sum(jnp.mean(err * err, axis=-1, keepdims=True))
        dout = err * (1.0 / D_MODEL)
        dgf_ref[...] += jnp.sum(dout * xn, axis=0, keepdims=True)
        dxn = dout * gf_v
        dx2 = r * (dxn - xn * jnp.mean(dxn * xn, axis=-1, keepdims=True))
        dx_ref[...] = dx2
        dyo_ref[...] = (dx2 * one_gate).astype(BF16)
        dgt_ref[...] += jnp.sum(dx2 * yo, axis=0, keepdims=True)

    return pl.pallas_call(
        body, name=name, grid=(s // ROW_TILE,),
        in_specs=[_row_spec(), _row_spec(), _vec_spec(), _vec_spec(), _row_spec()],
        out_specs=[_row_spec(), _row_spec(), pl.BlockSpec((1, LANES), lambda i: (0, 0)), _vec_spec(), _vec_spec()],
        out_shape=[SDS((s, D_MODEL), F32), SDS((s, D_MODEL), BF16), SDS((1, LANES), F32),
                   SDS((1, D_MODEL), F32), SDS((1, D_MODEL), F32)],
        compiler_params=_params(1),
    )(x1, yo1, gate1, gf, target)


def _ln_mod_bwd(name, dh, x, dx_next, g, scale):
    s = x.shape[0]

    def body(dh_ref, x_ref, dxn_ref, g_ref, sc_ref, dx_ref, dsh_ref, dsc_ref, dg_ref):
        i = pl.program_id(0)

        @pl.when(i == 0)
        def _():
            dsh_ref[...] = jnp.zeros_like(dsh_ref)
            dsc_ref[...] = jnp.zeros_like(dsc_ref)
            dg_ref[...] = jnp.zeros_like(dg_ref)

        dh_v = dh_ref[...]
        xv = x_ref[...]
        g_v = g_ref[...]
        r = lax.rsqrt(jnp.mean(xv * xv, axis=-1, keepdims=True) + EPS)
        xn = xv * r
        dsh_ref[...] += jnp.sum(dh_v, axis=0, keepdims=True)
        dsc_ref[...] += jnp.sum(dh_v * (xn * g_v), axis=0, keepdims=True)
        dn = dh_v * (1.0 + sc_ref[...])
        dg_ref[...] += jnp.sum(dn * xn, axis=0, keepdims=True)
        dxh = dn * g_v
        dx_ref[...] = dxn_ref[...] + r * (dxh - xn * jnp.mean(dxh * xn, axis=-1, keepdims=True))

    return pl.pallas_call(
        body, name=name, grid=(s // ROW_TILE,),
        in_specs=[_row_spec(), _row_spec(), _row_spec(), _vec_spec(), _vec_spec()],
        out_specs=[_row_spec(), _vec_spec(), _vec_spec(), _vec_spec()],
        out_shape=[SDS((s, D_MODEL), F32)] + [SDS((1, D_MODEL), F32)] * 3, compiler_params=_params(1),
    )(dh, x, dx_next, g, scale)


def _resid_bwd(name, dx, yo, gate):
    s = dx.shape[0]

    def body(dx_ref, yo_ref, gt_ref, dyo_ref, dgt_ref):
        i = pl.program_id(0)

        @pl.when(i == 0)
        def _():
            dgt_ref[...] = jnp.zeros_like(dgt_ref)

        dx_v = dx_ref[...]
        dyo_ref[...] = (dx_v * (1.0 + gt_ref[...])).astype(BF16)
        dgt_ref[...] += jnp.sum(dx_v * yo_ref[...], axis=0, keepdims=True)

    return pl.pallas_call(
        body, name=name, grid=(s // ROW_TILE,),
        in_specs=[_row_spec(), _row_spec(), _vec_spec()], out_specs=[_row_spec(), _vec_spec()],
        out_shape=[SDS((s, D_MODEL), BF16), SDS((1, D_MODEL), F32)], compiler_params=_params(1),
    )(dx, yo, gate)


def _window_of(g):
    return jnp.left_shift(2, g)


def _pool_fwd(name, proj0):
    s = proj0.shape[0]
    hb = ROW_TILE // HALO
    ext_rows = ROW_TILE + HALO

    def body(u_ref, halo_ref, p_ref):
        i = pl.program_id(0)
        g = pl.program_id(1)
        u = u_ref[...].astype(F32)
        halo = jnp.where(i == 0, 0.0, halo_ref[...].astype(F32))
        ext = jnp.concatenate([halo, u], axis=0)
        s2 = ext + pltpu.roll(ext, 1, axis=0)
        s4 = s2 + pltpu.roll(s2, 2, axis=0)
        s8 = s4 + pltpu.roll(s4, 4, axis=0)
        s16 = s8 + pltpu.roll(s8, 8, axis=0)
        win = jnp.where(g == 0, s2, jnp.where(g == 1, s4, jnp.where(g == 2, s8, s16)))[HALO:, :]
        t = i * ROW_TILE + lax.broadcasted_iota(jnp.int32, (ROW_TILE, 1), 0)
        cnt = jnp.minimum(t + 1, _window_of(g)).astype(F32)
        p_ref[...] = (win / cnt - u).astype(BF16)

    return pl.pallas_call(
        body, name=name, grid=(s // ROW_TILE, N_GROUPS),
        in_specs=[pl.BlockSpec((ROW_TILE, POOL_GROUP), lambda i, g: (i, g)),
                  pl.BlockSpec((HALO, POOL_GROUP), lambda i, g: (jnp.maximum(i * hb - 1, 0), g))],
        out_specs=pl.BlockSpec((ROW_TILE, POOL_GROUP), lambda i, g: (i, g)),
        out_shape=SDS((s, D_POOL), BF16), compiler_params=_params(2),
    )(proj0, proj0)


def _pool_bwd(name, dp):
    s = dp.shape[0]
    hb = ROW_TILE // HALO
    n_hb = s // HALO
    n_tiles = s // ROW_TILE
    ext_rows = ROW_TILE + HALO

    def body(dp_ref, halo_ref, du_ref):
        i = pl.program_id(0)
        g = pl.program_id(1)
        w = _window_of(g)
        dp_v = dp_ref[...]
        t = i * ROW_TILE + lax.broadcasted_iota(jnp.int32, (ext_rows, 1), 0)
        cnt = jnp.minimum(t + 1, w).astype(F32)
        halo = jnp.where(i == n_tiles - 1, 0.0, halo_ref[...])
        ext = jnp.concatenate([dp_v, halo], axis=0) / cnt
        s2 = ext + pltpu.roll(ext, ext_rows - 1, axis=0)
        s4 = s2 + pltpu.roll(s2, ext_rows - 2, axis=0)
        s8 = s4 + pltpu.roll(s4, ext_rows - 4, axis=0)
        s16 = s8 + pltpu.roll(s8, ext_rows - 8, axis=0)
        win = jnp.where(g == 0, s2, jnp.where(g == 1, s4, jnp.where(g == 2, s8, s16)))[:ROW_TILE, :]
        du_ref[...] = (win - dp_v).astype(BF16)

    return pl.pallas_call(
        body, name=name, grid=(n_tiles, N_GROUPS),
        in_specs=[pl.BlockSpec((ROW_TILE, POOL_GROUP), lambda i, g: (i, g)),
                  pl.BlockSpec((HALO, POOL_GROUP), lambda i, g: (jnp.minimum((i + 1) * hb, n_hb - 1), g))],
        out_specs=pl.BlockSpec((ROW_TILE, POOL_GROUP), lambda i, g: (i, g)),
        out_shape=SDS((s, D_POOL), BF16), compiler_params=_params(2),
    )(dp, dp)


FWD_HEADS_PER_STEP = 8
BWD_HEADS_PER_STEP = 4
ATT_SCALE = 0.125


def _att_groups(nh):
    lanes = nh * HEAD_DIM
    return lanes, D_SB // lanes, D_POOL // lanes, (D_POOL + D_SB) // lanes, (D_POOL + 2 * D_SB) // lanes


def _att_consts():
    r = lax.broadcasted_iota(jnp.int32, (ATT_TILE, ATT_TILE), 0)
    c = lax.broadcasted_iota(jnp.int32, (ATT_TILE, ATT_TILE), 1)
    first = lax.broadcasted_iota(jnp.int32, (1, LANES), 1) < HEAD_DIM
    return r, c, first


def _pair(x, p):
    return x[:, p * LANES:(p + 1) * LANES]


def _one_head(x, first, hh):
    zero = jnp.zeros_like(x)
    return jnp.where(first, x, zero) if hh == 0 else jnp.where(first, zero, x)


def _neg_softplus(z):
    return -(jnp.maximum(z, 0.0) + jnp.log(1.0 + jnp.exp(-jnp.abs(z))))


def _side_exchange(side_refs, n_side, by_chunk, is_first, is_last):
    ins, outs = side_refs[:n_side], side_refs[n_side:2 * n_side]
    sems = side_refs[2 * n_side:2 * n_side + 3]

    @pl.when(is_first)
    def _():
        for cp in _peer_copies(ins, outs, *sems, by_chunk=by_chunk):
            cp.start()

    @pl.when(is_last)
    def _():
        for cp in _peer_copies(ins, outs, *sems, by_chunk=by_chunk):
            cp.wait()


def _attn_fwd(name, proj0, shards):
    s = proj0.shape[0]
    nq = s // ATT_TILE
    nh = FWD_HEADS_PER_STEP
    ATT_GROUP, N_ATT_GROUPS, Q_GRP, K_GRP, V_GRP = _att_groups(nh)
    n_side = len(shards)

    def body(q_ref, k_ref, v_ref, *rest):
        o_ref = rest[n_side]
        side = rest[:n_side] + rest[n_side + 1:]
        j = pl.program_id(0)
        i = pl.program_id(1)
        _side_exchange(side, n_side, False, (j == 0) & (i == 0), (j == N_ATT_GROUPS - 1) & (i == nq - 1))
        r, c, first = _att_consts()
        tri = (r >= c).astype(BF16)
        below = c < r
        q = q_ref[...] * ATT_SCALE
        qh = [_one_head(_pair(q, h // 2), first, h % 2) for h in range(nh)]

        def tile(kb, carry, diagonal):
            k0 = pl.multiple_of(kb * ATT_TILE, ATT_TILE)
            kt = k_ref[pl.ds(k0, ATT_TILE), :]
            vt = v_ref[pl.ds(k0, ATT_TILE), :]
            z = [_dot_nt(qh[h], _pair(kt, h // 2)) for h in range(nh)]
            lf = [_neg_softplus(z[h]) for h in range(nh)]
            if diagonal:
                lf = [jnp.where(below, x, 0.0) for x in lf]
            run = [_dot(lf[h].astype(BF16), tri) for h in range(nh)]
            a = [jnp.exp(z[h] + run[h] + carry[h]) for h in range(nh)]
            if diagonal:
                a = [jnp.where(below, x, 0.0) for x in a]
            out_acc = [carry[nh + h] + _dot(a[h].astype(BF16), _pair(vt, h // 2)) for h in range(nh)]
            out_c = [carry[h] + jnp.sum(lf[h], axis=1, keepdims=True) for h in range(nh)]
            return tuple(out_c + out_acc)

        init = tuple([jnp.zeros((ATT_TILE, 1), F32)] * nh + [jnp.zeros((ATT_TILE, LANES), F32)] * nh)
        carry = tile(i, init, True)
        carry = lax.fori_loop(1, i + 1, lambda n, cr: tile(i - n, cr, False), carry)
        for p in range(nh // 2):
            o_ref[:, p * LANES:(p + 1) * LANES] = jnp.where(first, carry[nh + 2 * p], carry[nh + 2 * p + 1])

    out = pl.pallas_call(
        body, name=name, grid=(N_ATT_GROUPS, nq),
        in_specs=[pl.BlockSpec((ATT_TILE, ATT_GROUP), lambda j, i: (i, Q_GRP + j)),
                  pl.BlockSpec((s, ATT_GROUP), lambda j, i: (0, K_GRP + j)),
                  pl.BlockSpec((s, ATT_GROUP), lambda j, i: (0, V_GRP + j))] + [HBM_SPEC] * n_side,
        out_specs=[pl.BlockSpec((ATT_TILE, ATT_GROUP), lambda j, i: (i, j))] + [HBM_SPEC] * n_side,
        out_shape=[SDS((s, D_SB), F32)] + [SDS((N_DEV,) + sh.shape, sh.dtype) for sh in shards],
        scratch_shapes=_peer_sems(n_side), compiler_params=_params(2),
    )(proj0, proj0, proj0, *shards)
    return out[0], out[1:]


def _attn_bwd(name, proj0, o, do, dws, shards):
    s = proj0.shape[0]
    nq = s // ATT_TILE
    nh = BWD_HEADS_PER_STEP
    ATT_GROUP, N_ATT_GROUPS, Q_GRP, K_GRP, V_GRP = _att_groups(nh)
    n1, n2 = len(dws), len(shards)
    n_side = n1 + n2

    def body(q_ref, k_ref, v_ref, o_ref, do_ref, *rest):
        dq_ref, dk_ref, dv_ref = rest[n_side:n_side + 3]
        dk_acc, dv_acc = rest[2 * n_side + 3:2 * n_side + 5]
        srcs, dsts, sems = rest[:n_side], rest[n_side + 3:2 * n_side + 3], rest[2 * n_side + 5:]
        j = pl.program_id(0)
        i = pl.program_id(1)
        is_first, is_last = (j == 0) & (i == 0), (j == N_ATT_GROUPS - 1) & (i == nq - 1)
        _side_exchange(srcs[:n1] + dsts[:n1] + sems[:3], n1, True, is_first, is_last)
        _side_exchange(srcs[n1:] + dsts[n1:] + sems[3:], n2, False, is_first, is_last)

        @pl.when(i == 0)
        def _():
            dk_acc[...] = jnp.zeros_like(dk_acc)
            dv_acc[...] = jnp.zeros_like(dv_acc)

        r, c, first = _att_consts()
        tri = (r >= c).astype(BF16)
        tri_p = (r <= c).astype(BF16)
        below = c < r
        q = q_ref[...] * ATT_SCALE
        do_b = do_ref[...].astype(BF16)
        do_o = do_b.astype(F32) * o_ref[...]
        qh = [_one_head(_pair(q, h // 2), first, h % 2) for h in range(nh)]
        doh = [_one_head(_pair(do_b, h // 2), first, h % 2) for h in range(nh)]
        dsum = [jnp.sum(_one_head(_pair(do_o, h // 2), first, h % 2), axis=1, keepdims=True) for h in range(nh)]

        def tile(kb, carry, diagonal):
            k0 = pl.multiple_of(kb * ATT_TILE, ATT_TILE)
            kt = k_ref[pl.ds(k0, ATT_TILE), :]
            vt = v_ref[pl.ds(k0, ATT_TILE), :]
            hs = range(nh)
            z = [_dot_nt(qh[h], _pair(kt, h // 2)) for h in hs]
            d_a = [_dot_nt(doh[h], _pair(vt, h // 2)) for h in hs]
            lf = [_neg_softplus(z[h]) for h in hs]
            sig = [jnp.exp(z[h] + lf[h]) for h in hs]
            if diagonal:
                lf = [jnp.where(below, x, 0.0) for x in lf]
            run = [_dot(lf[h].astype(BF16), tri) for h in hs]
            a = [jnp.exp(z[h] + run[h] + carry[h]) for h in hs]
            if diagonal:
                a = [jnp.where(below, x, 0.0) for x in a]
            a_b = [x.astype(BF16) for x in a]
            g = [a_b[h].astype(F32) * d_a[h] for h in hs]
            g_sum = [jnp.sum(g[h], axis=1, keepdims=True) for h in hs]
            early = [dsum[h] - carry[nh + h] - g_sum[h] for h in hs]
            upto = [_dot(g[h].astype(BF16), tri_p) for h in hs]
            dv_t = [_dot_tn(a_b[2 * p], doh[2 * p]) + _dot_tn(a_b[2 * p + 1], doh[2 * p + 1]) for p in range(nh // 2)]
            dz = [g[h] - sig[h] * (early[h] + upto[h]) for h in hs]
            if diagonal:
                dz = [jnp.where(below, x, 0.0) for x in dz]
            dz = [x.astype(BF16) for x in dz]
            out_dq = [carry[2 * nh + h] + _dot(dz[h], _pair(kt, h // 2)) for h in hs]
            dk_t = [_dot_tn(dz[2 * p], qh[2 * p]) + _dot_tn(dz[2 * p + 1], qh[2 * p + 1]) for p in range(nh // 2)]
            for p in range(nh // 2):
                dk_acc[pl.ds(k0, ATT_TILE), p * LANES:(p + 1) * LANES] += dk_t[p]
                dv_acc[pl.ds(k0, ATT_TILE), p * LANES:(p + 1) * LANES] += dv_t[p]
            out_c1 = [carry[h] + jnp.sum(lf[h], axis=1, keepdims=True) for h in hs]
            out_c2 = [carry[nh + h] + g_sum[h] for h in hs]
            return tuple(out_c1 + out_c2 + out_dq)

        init = tuple([jnp.zeros((ATT_TILE, 1), F32)] * (2 * nh) + [jnp.zeros((ATT_TILE, LANES), F32)] * nh)
        carry = tile(i, init, True)
        carry = lax.fori_loop(1, i + 1, lambda n, cr: tile(i - n, cr, False), carry)
        for p in range(nh // 2):
            dq_p = jnp.where(first, carry[2 * nh + 2 * p], carry[2 * nh + 2 * p + 1]) * ATT_SCALE
            dq_ref[:, p * LANES:(p + 1) * LANES] = dq_p.astype(BF16)

        @pl.when(i == nq - 1)
        def _():
            dk_ref[...] = dk_acc[...].astype(BF16)
            dv_ref[...] = dv_acc[...].astype(BF16)

    tile_spec = pl.BlockSpec((ATT_TILE, ATT_GROUP), lambda j, i: (i, j))
    full = pl.BlockSpec((s, ATT_GROUP), lambda j, i: (0, j))
    out = pl.pallas_call(
        body, name=name, grid=(N_ATT_GROUPS, nq),
        in_specs=[pl.BlockSpec((ATT_TILE, ATT_GROUP), lambda j, i: (i, Q_GRP + j)),
                  pl.BlockSpec((s, ATT_GROUP), lambda j, i: (0, K_GRP + j)),
                  pl.BlockSpec((s, ATT_GROUP), lambda j, i: (0, V_GRP + j)),
                  tile_spec, tile_spec] + [HBM_SPEC] * n_side,
        out_specs=[tile_spec, full, full] + [HBM_SPEC] * n_side,
        out_shape=[SDS((s, D_SB), BF16)] * 3 + [SDS(dw.shape, dw.dtype) for dw in dws]
        + [SDS((N_DEV,) + sh.shape, sh.dtype) for sh in shards],
        scratch_shapes=[pltpu.VMEM((s, ATT_GROUP), F32), pltpu.VMEM((s, ATT_GROUP), F32)] + _peer_sems(n1) + _peer_sems(n2),
        compiler_params=_params(2),
    )(proj0, proj0, proj0, o, do, *dws, *shards)
    return out[0], out[1], out[2], out[3:3 + n1], out[3 + n1:]


GATE0_COL = (D_POOL + 3 * D_SB) // D_INNER


def _gate_fwd0(name, yp_raw, o, proj0, ps):
    s = o.shape[0]

    def body(yp_ref, o_ref, gt_ref, ps_ref, y_ref):
        gt = gt_ref[...].astype(F32)
        sg = gt * _sigmoid(gt)
        y_ref[:, :D_POOL] = (yp_ref[...] * ps_ref[...] * sg[:, :D_POOL]).astype(BF16)
        y_ref[:, D_POOL:] = (o_ref[...] * sg[:, D_POOL:]).astype(BF16)

    return pl.pallas_call(
        body, name=name, grid=(s // ROW_TILE,),
        in_specs=[_row_spec(), _row_spec(), _row_spec(D_INNER, GATE0_COL), _vec_spec()],
        out_specs=_row_spec(D_INNER),
        out_shape=SDS((s, D_INNER), BF16), compiler_params=_params(1),
    )(yp_raw, o, proj0, ps)


def _dsilu(x):
    sg = _sigmoid(x)
    return sg * (1.0 + x * (1.0 - sg))


def _gate_bwd0(name, dymix, yp_raw, o, proj0, ps):
    s = o.shape[0]

    def body(dy_ref, yp_ref, o_ref, gt_ref, ps_ref, dyp_ref, do_ref, dgt_ref, dps_ref):
        i = pl.program_id(0)

        @pl.when(i == 0)
        def _():
            dps_ref[...] = jnp.zeros_like(dps_ref)

        gt = gt_ref[...].astype(F32)
        dy = dy_ref[...]
        sg = gt * _sigmoid(gt)
        dsg = _dsilu(gt)
        dcat = dy * sg
        yp = yp_ref[...]
        ps_v = ps_ref[...]
        dyp_ref[...] = (dcat[:, :D_POOL] * ps_v).astype(BF16)
        do_ref[...] = dcat[:, D_POOL:]
        dps_ref[...] += jnp.sum(dcat[:, :D_POOL] * yp, axis=0, keepdims=True)
        dgt_ref[:, :D_POOL] = (dy[:, :D_POOL] * (yp * ps_v) * dsg[:, :D_POOL]).astype(BF16)
        dgt_ref[:, D_POOL:] = (dy[:, D_POOL:] * o_ref[...] * dsg[:, D_POOL:]).astype(BF16)

    return pl.pallas_call(
        body, name=name, grid=(s // ROW_TILE,),
        in_specs=[_row_spec(D_INNER), _row_spec(), _row_spec(), _row_spec(D_INNER, GATE0_COL), _vec_spec()],
        out_specs=[_row_spec(), _row_spec(), _row_spec(D_INNER), _vec_spec()],
        out_shape=[SDS((s, D_POOL), BF16), SDS((s, D_SB), F32), SDS((s, D_INNER), BF16), SDS((1, D_POOL), F32)],
        compiler_params=_params(1),
    )(dymix, yp_raw, o, proj0, ps)


CONV_HALO = 16


def _conv_fwd(name, proj1, cw, cb):
    s = proj1.shape[0]
    hb = ROW_TILE // CONV_HALO
    ext_rows = ROW_TILE + CONV_HALO

    def body(gb_ref, gc_ref, u_ref, gt_ref, gch_ref, uh_ref, cw_ref, cb_ref, y_ref):
        i = pl.program_id(0)
        uc = gc_ref[...].astype(F32) * u_ref[...].astype(F32)
        halo = jnp.where(i == 0, 0.0, gch_ref[...].astype(F32) * uh_ref[...].astype(F32))
        ext = jnp.concatenate([halo, uc], axis=0)
        uc1 = pltpu.roll(ext, 1, axis=0)[CONV_HALO:, :]
        uc2 = pltpu.roll(ext, 2, axis=0)[CONV_HALO:, :]
        cw_v = cw_ref[...]
        conv = cb_ref[...] + cw_v[0:1, :] * uc2 + cw_v[1:2, :] * uc1 + cw_v[2:3, :] * uc
        gt = gt_ref[...].astype(F32)
        y_ref[...] = (gb_ref[...].astype(F32) * conv * (gt * _sigmoid(gt))).astype(BF16)

    def tile(part):
        return pl.BlockSpec((ROW_TILE, D_INNER), lambda i: (i, part))

    def halo(part):
        return pl.BlockSpec((CONV_HALO, D_INNER), lambda i: (jnp.maximum(i * hb - 1, 0), part))

    return pl.pallas_call(
        body, name=name, grid=(s // ROW_TILE,),
        in_specs=[tile(0), tile(1), tile(2), tile(3), halo(1), halo(2),
                  pl.BlockSpec((3, D_INNER), lambda i: (0, 0)), pl.BlockSpec((1, D_INNER), lambda i: (0, 0))],
        out_specs=pl.BlockSpec((ROW_TILE, D_INNER), lambda i: (i, 0)),
        out_shape=SDS((s, D_INNER), BF16), compiler_params=_params(1),
    )(proj1, proj1, proj1, proj1, proj1, proj1, cw, cb)


def _conv_bwd(name, dymix, proj1, cw, cb):
    s = proj1.shape[0]
    hb = ROW_TILE // CONV_HALO
    n_hb = s // CONV_HALO
    n_tiles = s // ROW_TILE
    ext_rows = ROW_TILE + CONV_HALO

    def body(dy_ref, gb_ref, gc_ref, u_ref, gt_ref, gch_ref, uh_ref, dyn_ref, gbn_ref, gtn_ref, cw_ref, cb_ref,
             dproj_ref, dcw_ref, dcb_ref):
        i = pl.program_id(0)

        @pl.when(i == 0)
        def _():
            dcw_ref[...] = jnp.zeros_like(dcw_ref)
            dcb_ref[...] = jnp.zeros_like(dcb_ref)

        gc = gc_ref[...].astype(F32)
        u = u_ref[...].astype(F32)
        gb = gb_ref[...].astype(F32)
        gt = gt_ref[...].astype(F32)
        dy = dy_ref[...]
        uc = gc * u
        halo = jnp.where(i == 0, 0.0, gch_ref[...].astype(F32) * uh_ref[...].astype(F32))
        ext = jnp.concatenate([halo, uc], axis=0)
        uc1 = pltpu.roll(ext, 1, axis=0)[CONV_HALO:, :]
        uc2 = pltpu.roll(ext, 2, axis=0)[CONV_HALO:, :]
        cw_v = cw_ref[...]
        w0, w1, w2 = cw_v[0:1, :], cw_v[1:2, :], cw_v[2:3, :]
        conv = cb_ref[...] + w0 * uc2 + w1 * uc1 + w2 * uc
        sig = _sigmoid(gt)
        sg = gt * sig
        dconv = dy * gb * sg
        gtn = gtn_ref[...].astype(F32)
        dconv_next = jnp.where(i == n_tiles - 1, 0.0, dyn_ref[...] * gbn_ref[...].astype(F32) * (gtn * _sigmoid(gtn)))
        dext = jnp.concatenate([dconv, dconv_next], axis=0)
        dconv_p1 = pltpu.roll(dext, ext_rows - 1, axis=0)[:ROW_TILE, :]
        dconv_p2 = pltpu.roll(dext, ext_rows - 2, axis=0)[:ROW_TILE, :]
        duc = w2 * dconv + w1 * dconv_p1 + w0 * dconv_p2
        dproj_ref[:, 0:D_INNER] = (dy * conv * sg).astype(BF16)
        dproj_ref[:, D_INNER:2 * D_INNER] = (duc * u).astype(BF16)
        dproj_ref[:, 2 * D_INNER:3 * D_INNER] = (duc * gc).astype(BF16)
        dproj_ref[:, 3 * D_INNER:] = (dy * gb * conv * (sig + sg * (1.0 - sig))).astype(BF16)
        dcw_ref[0:1, :] += jnp.sum(dconv * uc2, axis=0, keepdims=True)
        dcw_ref[1:2, :] += jnp.sum(dconv * uc1, axis=0, keepdims=True)
        dcw_ref[2:3, :] += jnp.sum(dconv * uc, axis=0, keepdims=True)
        dcb_ref[...] += jnp.sum(dconv, axis=0, keepdims=True)

    def tile(part):
        return pl.BlockSpec((ROW_TILE, D_INNER), lambda i: (i, part))

    def prev(part):
        return pl.BlockSpec((CONV_HALO, D_INNER), lambda i: (jnp.maximum(i * hb - 1, 0), part))

    def nxt(part):
        return pl.BlockSpec((CONV_HALO, D_INNER), lambda i: (jnp.minimum((i + 1) * hb, n_hb - 1), part))

    whole = lambda rows: pl.BlockSpec((rows, D_INNER), lambda i: (0, 0))
    return pl.pallas_call(
        body, name=name, grid=(n_tiles,),
        in_specs=[tile(0), tile(0), tile(1), tile(2), tile(3), prev(1), prev(2), nxt(0), nxt(0), nxt(3),
                  whole(3), whole(1)],
        out_specs=[pl.BlockSpec((ROW_TILE, 4 * D_INNER), lambda i: (i, 0)), whole(3), whole(1)],
        out_shape=[SDS((s, 4 * D_INNER), BF16), SDS((3, D_INNER), F32), SDS((1, D_INNER), F32)],
        compiler_params=_params(1),
    )(dymix, proj1, proj1, proj1, proj1, proj1, proj1, dymix, proj1, proj1, cw, cb)


def _place():
    x, y, c = lax.axis_index("x"), lax.axis_index("y"), lax.axis_index("c")
    return x, y, c


def _flip(x, y, c, k):
    fx, fy, fc = (k >> 2) & 1, (k >> 1) & 1, k & 1
    return (1 - x if fx else x, 1 - y if fy else y, 1 - c if fc else c)


def _dev_index(p):
    return 4 * p[0] + 2 * p[1] + p[2]


HBM_SPEC = pl.BlockSpec(memory_space=pltpu.HBM)
VMEM_SPEC = pl.BlockSpec(memory_space=pltpu.VMEM)


def _allgather_weights(shards):
    n_w = len(shards)

    def body(*refs):
        ins, outs = refs[:n_w], refs[n_w:2 * n_w]
        send_sems, recv_sems, local_sems = refs[2 * n_w:]
        x, y, c = _place()
        me, sibling = (x, y, c), (x, y, 1 - c)
        chips = [(1 - x, y), (x, 1 - y), (1 - x, 1 - y)]

        def copy(w, k, block, to, src=None):
            rows = outs[w].at[_dev_index(block)]
            return pltpu.make_async_remote_copy(
                src_ref=rows if src is None else src, dst_ref=rows,
                send_sem=send_sems.at[7 * w + k], recv_sem=recv_sems.at[7 * w + k],
                device_id=to, device_id_type=MESH)

        mine, first, passed = [], [], []
        for w in range(n_w):
            cp = pltpu.make_async_copy(ins[w], outs[w].at[_dev_index(me)], local_sems.at[w])
            cp.start()
            mine.append(cp)
            fw = [copy(w, 0, me, sibling, src=ins[w])]
            fw += [copy(w, 1 + j, me, (*chip, c), src=ins[w]) for j, chip in enumerate(chips)]
            for cp in fw:
                cp.start()
            first += fw
        for w in range(n_w):
            for j, chip in enumerate(chips):
                copy(w, 1 + j, (*chip, c), me).wait_recv()
                cp = copy(w, 4 + j, (*chip, c), sibling)
                cp.start()
                passed.append(cp)
        for w in range(n_w):
            copy(w, 0, sibling, me).wait_recv()
            for j, chip in enumerate(chips):
                copy(w, 4 + j, (*chip, 1 - c), me).wait_recv()
        for cp in first + passed:
            cp.wait_send()
        for cp in mine:
            cp.wait()

    return pl.pallas_call(
        body, name="allgather_weights",
        out_shape=[SDS((N_DEV,) + sh.shape, sh.dtype) for sh in shards],
        in_specs=[HBM_SPEC] * n_w, out_specs=[HBM_SPEC] * n_w,
        scratch_shapes=[pltpu.SemaphoreType.DMA((7 * n_w,)), pltpu.SemaphoreType.DMA((7 * n_w,)),
                        pltpu.SemaphoreType.DMA((n_w,))],
    )(*shards)


def _peer_copies(ins, outs, send_sems, recv_sems, local_sems, by_chunk):
    x, y, c = _place()
    my = _dev_index((x, y, c))
    copies = []
    for w in range(len(ins)):
        copies.append(pltpu.make_async_copy(ins[w].at[my] if by_chunk else ins[w], outs[w].at[my], local_sems.at[w]))
        for k in range(1, N_DEV):
            peer = _flip(x, y, c, k)
            copies.append(pltpu.make_async_remote_copy(
                src_ref=ins[w].at[_dev_index(peer)] if by_chunk else ins[w], dst_ref=outs[w].at[my],
                send_sem=send_sems.at[7 * w + k - 1], recv_sem=recv_sems.at[7 * w + k - 1],
                device_id=peer, device_id_type=MESH))
    return copies


def _peer_sems(n_w):
    return [pltpu.SemaphoreType.DMA((7 * n_w,)), pltpu.SemaphoreType.DMA((7 * n_w,)), pltpu.SemaphoreType.DMA((n_w,))]


ADA_COLS = 3 * D_MODEL // N_DEV


def _ada_forward(c_row, conv_w, conv_b, ada_w, ada_b):
    cw_cols = conv_w.shape[1]

    def body(c_ref, cw_ref, cb_ref, aw_ref, ab_ref, m_ref, cs_ref, cwf_ref, cbf_ref,
             slab, gath, part, land, send_sems, recv_sems):
        x, y, c = _place()
        my = _dev_index((x, y, c))
        slab[...] = jnp.zeros_like(slab)
        slab[0:1, :] = c_ref[...]
        slab[1:4, 0:cw_cols] = cw_ref[...]
        slab[4:5, 0:cw_cols] = cb_ref[...]
        gath[my] = slab[...]
        sends = []
        for k in range(1, N_DEV):
            peer = _flip(x, y, c, k)
            cp = pltpu.make_async_remote_copy(
                src_ref=slab, dst_ref=gath.at[my], send_sem=send_sems.at[k - 1], recv_sem=recv_sems.at[k - 1],
                device_id=peer, device_id_type=MESH)
            cp.start()
            sends.append(cp)
        for cp in sends:
            cp.wait()
        for d in range(N_DEV):
            c_d = gath[d, 0:1, :]
            cs_ref[d:d + 1, :] = c_d * _sigmoid(c_d)
            cwf_ref[:, d * cw_cols:(d + 1) * cw_cols] = gath[d, 1:4, 0:cw_cols]
            cbf_ref[:, d * cw_cols:(d + 1) * cw_cols] = gath[d, 4:5, 0:cw_cols]
        cs = cs_ref[...]
        part[...] = jnp.zeros_like(part)
        for layer in range(2):
            m_part = jnp.dot(cs, aw_ref[layer], preferred_element_type=F32, precision=lax.Precision.HIGHEST)
            for d in range(N_DEV):
                part[d, layer:layer + 1, :] = m_part[d:d + 1, :]
        land[my] = part[my]
        sends = []
        for k in range(1, N_DEV):
            peer = _flip(x, y, c, k)
            cp = pltpu.make_async_remote_copy(
                src_ref=part.at[_dev_index(peer)], dst_ref=land.at[my],
                send_sem=send_sems.at[6 + k], recv_sem=recv_sems.at[6 + k],
                device_id=peer, device_id_type=MESH)
            cp.start()
            sends.append(cp)
        for cp in sends:
            cp.wait()
        for d in range(N_DEV):
            cols = slice(d * ADA_COLS, (d + 1) * ADA_COLS)
            m_ref[:, cols] = land[d, 0:2, :] + ab_ref[:, cols]

    return pl.pallas_call(
        body, name="ada_forward",
        out_shape=[SDS((2, 3 * D_MODEL), F32), SDS((N_DEV, D_MODEL), F32), SDS((3, N_DEV * cw_cols), F32),
                   SDS((1, N_DEV * cw_cols), F32)],
        in_specs=[VMEM_SPEC] * 5, out_specs=[VMEM_SPEC] * 4,
        scratch_shapes=[pltpu.VMEM((8, D_MODEL), F32), pltpu.VMEM((N_DEV, 8, D_MODEL), F32),
                        pltpu.VMEM((N_DEV, 8, ADA_COLS), F32), pltpu.VMEM((N_DEV, 8, ADA_COLS), F32),
                        pltpu.SemaphoreType.DMA((14,)), pltpu.SemaphoreType.DMA((14,))],
        compiler_params=pltpu.CompilerParams(vmem_limit_bytes=VMEM_LIMIT),
    )(c_row, conv_w, conv_b, ada_w, ada_b)


def _small_grads(slab):
    def body(slab_ref, gath_ref, tot_ref, send_sems, recv_sems):
        x, y, c = _place()
        my = _dev_index((x, y, c))
        gath_ref[my] = slab_ref[...]
        sends = []
        for k in range(1, N_DEV):
            peer = _flip(x, y, c, k)
            cp = pltpu.make_async_remote_copy(
                src_ref=slab_ref, dst_ref=gath_ref.at[my], send_sem=send_sems.at[k - 1], recv_sem=recv_sems.at[k - 1],
                device_id=peer, device_id_type=MESH)
            cp.start()
            sends.append(cp)
        for cp in sends:
            cp.wait()
        tot = gath_ref[0]
        for d in range(1, N_DEV):
            tot = tot + gath_ref[d]
        tot_ref[...] = tot

    return pl.pallas_call(
        body, name="small_grads",
        out_shape=[SDS((N_DEV, SLAB_ROWS, D_MODEL), F32), SDS((SLAB_ROWS, D_MODEL), F32)],
        in_specs=[VMEM_SPEC], out_specs=[VMEM_SPEC] * 2,
        scratch_shapes=[pltpu.SemaphoreType.DMA((7,)), pltpu.SemaphoreType.DMA((7,))],
    )(slab)


def _adamw_math(w, g, m, v):
    m = ADAM_B1 * m + (1.0 - ADAM_B1) * g
    v = ADAM_B2 * v + (1.0 - ADAM_B2) * jnp.square(g)
    m_hat = m / (1.0 - ADAM_B1 ** ADAM_STEP)
    v_hat = v / (1.0 - ADAM_B2 ** ADAM_STEP)
    delta = -ADAM_LR * (m_hat / (jnp.sqrt(v_hat) + ADAM_EPS) + ADAM_WD * w)
    return delta, m, v


def _sum_adamw(name, recv, w, m, v):
    rows, cols = w.shape
    tr = min(rows, 256)

    def body(r_ref, w_ref, m_ref, v_ref, g_ref, d_ref, nm_ref, nv_ref):
        g = r_ref[0].astype(F32)
        for d in range(1, N_DEV):
            g = g + r_ref[d].astype(F32)
        g_ref[...] = g
        d_ref[...], nm_ref[...], nv_ref[...] = _adamw_math(w_ref[...], g, m_ref[...], v_ref[...])

    blk = pl.BlockSpec((tr, cols), lambda i: (i, 0))
    return pl.pallas_call(
        body, name=name, grid=(rows // tr,),
        in_specs=[pl.BlockSpec((N_DEV, tr, cols), lambda i: (0, i, 0)), blk, blk, blk],
        out_specs=[blk] * 4, out_shape=[SDS((rows, cols), F32)] * 4, compiler_params=_params(1),
    )(recv, w, m, v)


def _ada_w_adamw(name, cs_t, dm_cols, w, m, v):
    def body(cs_ref, dm_ref, w_ref, m_ref, v_ref, g_ref, d_ref, nm_ref, nv_ref):
        cs = cs_ref[...]
        dm = dm_ref[...]
        g = cs[:, 0:1] * dm[0:1, :]
        for b in range(1, N_DEV):
            g = g + cs[:, b:b + 1] * dm[b:b + 1, :]
        g_ref[...] = g
        d_ref[...], nm_ref[...], nv_ref[...] = _adamw_math(w_ref[...], g, m_ref[...], v_ref[...])

    blk = pl.BlockSpec((None, D_MODEL, ADA_COLS), lambda l: (l, 0, 0))
    return pl.pallas_call(
        body, name=name, grid=(2,),
        in_specs=[pl.BlockSpec((D_MODEL, N_DEV), lambda l: (0, 0)),
                  pl.BlockSpec((None, N_DEV, ADA_COLS), lambda l: (l, 0, 0)), blk, blk, blk],
        out_specs=[blk] * 4, out_shape=[SDS((2, D_MODEL, ADA_COLS), F32)] * 4, compiler_params=_params(1),
    )(cs_t, dm_cols, w, m, v)


def _small_adamw(name, triples):
    n = len(triples)

    def body(*refs):
        ins, outs = refs[:4 * n], refs[4 * n:]
        for j in range(n):
            w_ref, g_ref, m_ref, v_ref = ins[4 * j:4 * j + 4]
            d, nm, nv = _adamw_math(w_ref[...], g_ref[...], m_ref[...], v_ref[...])
            outs[3 * j][...] = d
            outs[3 * j + 1][...] = nm
            outs[3 * j + 2][...] = nv

    flat = [a for t in triples for a in t]
    return pl.pallas_call(
        body, name=name,
        out_shape=[SDS(t[0].shape, F32) for t in triples for _ in range(3)],
        in_specs=[VMEM_SPEC] * (4 * n), out_specs=[VMEM_SPEC] * (3 * n),
    )(*flat)


def kernel(x, c, norm_g, ada_w, ada_b, even_w_in, pool_w, pool_scale, even_w_out, odd_w_in, conv_w, conv_b, odd_w_out, final_g, loss_target, m_norm_g, m_ada_w, m_ada_b, m_even_w_in, m_pool_w, m_pool_scale, m_even_w_out, m_odd_w_in, m_conv_w, m_conv_b, m_odd_w_out, m_final_g, v_norm_g, v_ada_w, v_ada_b, v_even_w_in, v_pool_w, v_pool_scale, v_even_w_out, v_odd_w_in, v_conv_w, v_conv_b, v_odd_w_out, v_final_g):
    seq = x.shape[1]
    x0 = x[0]
    target = loss_target[0]
    final_g2 = final_g.reshape(1, D_MODEL)

    w_in_e = even_w_in[0]
    w_out_e = even_w_out[0]
    w_in_o = odd_w_in[0]
    w_out_o = odd_w_out[0]
    w_pool = pool_w[0].reshape(N_GROUPS * 32, POOL_GROUP)
    shards = [w.astype(BF16) for w in (w_in_e, w_out_e, w_in_o, w_out_o, w_pool)]
    (wg_in_e,) = _allgather_weights(shards[:1])

    m_vec, cs_all, conv_w_full, conv_b_full = _ada_forward(c, conv_w[0], conv_b, ada_w, ada_b)
    shift = [m_vec[l:l + 1, 0:D_MODEL] for l in range(2)]
    scale = [m_vec[l:l + 1, D_MODEL:2 * D_MODEL] for l in range(2)]
    gate = [m_vec[l:l + 1, 2 * D_MODEL:] for l in range(2)]
    ng = [norm_g[l:l + 1] for l in range(2)]

    h0, h0_t = _ln_mod("ln_mod0", x0, ng[0], scale[0], shift[0])
    proj0 = _proj_in("proj_in0", h0, wg_in_e)
    o, (wg_out_e, wg_in_o, wg_out_o, wg_pool, wt_in_o) = _attn_fwd("attn_fwd", proj0, shards[1:] + [shards[2].T])
    wf_out_e = wg_out_e.reshape(D_INNER, D_MODEL)
    wf_out_o = wg_out_o.reshape(D_INNER, D_MODEL)
    wf_pool = wg_pool.reshape(N_DEV, N_GROUPS, 32, POOL_GROUP).transpose(1, 0, 2, 3).reshape(N_GROUPS, POOL_GROUP, POOL_GROUP)
    p = _pool_fwd("pool_fwd", proj0)
    yp_raw = _pool_mm("pool_mix", p, wf_pool, _dot)
    ymix0 = _gate_fwd0("gate_fwd0", yp_raw, o, proj0, pool_scale)
    yo0 = _proj_out("proj_out0", ymix0, wf_out_e)

    x1, h1, h1_t = _resid_ln_mod("resid_ln_mod1", x0, yo0, gate[0], ng[1], scale[1], shift[1])
    proj1 = _proj_in("proj_in1", h1, wg_in_o)
    ymix1 = _conv_fwd("conv_fwd", proj1, conv_w_full, conv_b_full)
    yo1 = _proj_out("proj_out1", ymix1, wf_out_o)

    dx2, dyo1, loss_acc, d_final_g, d_gate1 = _final_loss("final_loss", x1, yo1, gate[1], final_g2, target)
    loss = lax.psum(loss_acc[0, 0], ("x", "y", "c"))

    dymix1 = _proj_out_bwd("proj_out1_bwd", dyo1, wf_out_o)
    dw_out_o = _wgrad_out("wgrad_out1", ymix1, dyo1)
    dproj1, d_conv_w, d_conv_b = _conv_bwd("conv_bwd", dymix1, proj1, conv_w_full, conv_b_full)
    dh1, _ = _proj_in_bwd("proj_in1_bwd", dproj1, wt_in_o)
    dw_in_o = _wgrad_in("wgrad_in1", h1_t, dproj1)
    dx1, d_shift1, d_scale1, d_ng1 = _ln_mod_bwd("ln_mod1_bwd", dh1, x1, dx2, ng[1], scale[1])

    dyo0, d_gate0 = _resid_bwd("resid0_bwd", dx1, yo0, gate[0])
    dymix0 = _proj_out_bwd("proj_out0_bwd", dyo0, wf_out_e)
    dw_out_e = _wgrad_out("wgrad_out0", ymix0, dyo0)
    dyp, do, dgt0, d_pool_scale = _gate_bwd0("gate_bwd0", dymix0, yp_raw, o, proj0, pool_scale)
    dp = _pool_mm("pool_mix_bwd", dyp, wf_pool, _dot_nt)
    dw_pool = _pool_wgrad("pool_wgrad", p, dyp)
    du_pool = _pool_bwd("pool_bwd", dp)
    dw_pool_c = dw_pool.reshape(N_GROUPS, N_DEV, 32, POOL_GROUP).transpose(1, 0, 2, 3).reshape(N_DEV, N_GROUPS * 32, POOL_GROUP).astype(BF16)
    ready = [dw_out_e.reshape(N_DEV, D_INNER // N_DEV, D_MODEL), dw_in_o,
             dw_out_o.reshape(N_DEV, D_INNER // N_DEV, D_MODEL), dw_pool_c]
    dq, dk, dv, (r_out_e, r_in_o, r_out_o, r_pool), (wt_in_e,) = _attn_bwd(
        "attn_bwd", proj0, o, do, ready, [shards[0].T])
    dproj0 = jnp.concatenate([du_pool, dq, dk, dv, dgt0], axis=1)
    dw_in_e = _wgrad_in("wgrad_in0", h0_t, dproj0)
    dh0, (r_in_e,) = _proj_in_bwd("proj_in0_bwd", dproj0, wt_in_e, [dw_in_e])
    dx0, d_shift0, d_scale0, d_ng0 = _ln_mod_bwd("ln_mod0_bwd", dh0, x0, dx1, ng[0], scale[0])
    grad_x = dx0[None]

    big = {}
    big["even_w_in"] = _sum_adamw("adamw_even_w_in", r_in_e, w_in_e, m_even_w_in[0], v_even_w_in[0])
    big["even_w_out"] = _sum_adamw("adamw_even_w_out", r_out_e, w_out_e, m_even_w_out[0], v_even_w_out[0])
    big["odd_w_in"] = _sum_adamw("adamw_odd_w_in", r_in_o, w_in_o, m_odd_w_in[0], v_odd_w_in[0])
    big["odd_w_out"] = _sum_adamw("adamw_odd_w_out", r_out_o, w_out_o, m_odd_w_out[0], v_odd_w_out[0])
    big["pool_w"] = _sum_adamw("adamw_pool_w", r_pool, w_pool, m_pool_w[0].reshape(N_GROUPS * 32, POOL_GROUP),
                               v_pool_w[0].reshape(N_GROUPS * 32, POOL_GROUP))
    big = {k: [a.reshape(shape) for a in v] for (k, v), shape in zip(
        big.items(), [even_w_in.shape, even_w_out.shape, odd_w_in.shape, odd_w_out.shape, pool_w.shape])}

    dm = jnp.concatenate([jnp.concatenate([d_shift0, d_scale0, d_gate0], axis=1),
                          jnp.concatenate([d_shift1, d_scale1, d_gate1], axis=1)], axis=0)
    slab = jnp.zeros((SLAB_ROWS, D_MODEL), F32)
    slab = slab.at[0:6].set(dm.reshape(6, D_MODEL))
    slab = slab.at[8:9].set(d_ng0).at[9:10].set(d_ng1).at[10:11].set(d_pool_scale).at[11:12].set(d_final_g)
    slab = slab.at[16:22].set(d_conv_w.reshape(6, D_MODEL)).at[24:26].set(d_conv_b.reshape(2, D_MODEL))
    gathered, total = _small_grads(slab)
    my = 4 * lax.axis_index("x") + 2 * lax.axis_index("y") + lax.axis_index("c")
    g_ada_b = total[0:6].reshape(2, 3 * D_MODEL)
    g_norm_g = total[8:10]
    g_pool_scale = total[10:11]
    g_final_g = total[11:12]
    cw_cols = conv_w.shape[2]
    g_conv_w = lax.dynamic_slice_in_dim(total[16:22].reshape(3, D_INNER), my * cw_cols, cw_cols, axis=1)
    g_conv_b = lax.dynamic_slice_in_dim(total[24:26].reshape(1, D_INNER), my * cw_cols, cw_cols, axis=1)
    dm_all = gathered[:, 0:6, :].reshape(N_DEV, 2, 3 * D_MODEL)
    dm_cols = lax.dynamic_slice_in_dim(dm_all, my * ADA_COLS, ADA_COLS, axis=2).transpose(1, 0, 2)
    ada = _ada_w_adamw("adamw_ada_w", cs_all.T, dm_cols, ada_w, m_ada_w, v_ada_w)

    small = _small_adamw("adamw_small", [
        (norm_g, g_norm_g, m_norm_g, v_norm_g),
        (ada_b, g_ada_b, m_ada_b, v_ada_b),
        (pool_scale, g_pool_scale, m_pool_scale, v_pool_scale),
        (conv_w[0], g_conv_w, m_conv_w[0], v_conv_w[0]),
        (conv_b, g_conv_b, m_conv_b, v_conv_b),
        (final_g2, g_final_g, m_final_g.reshape(1, D_MODEL), v_final_g.reshape(1, D_MODEL)),
    ])
    small = [small[3 * j:3 * j + 3] for j in range(6)]

    grads = {
        "norm_g": g_norm_g, "ada_w": ada[0], "ada_b": g_ada_b, "even_w_in": big["even_w_in"][0],
        "pool_w": big["pool_w"][0], "pool_scale": g_pool_scale, "even_w_out": big["even_w_out"][0],
        "odd_w_in": big["odd_w_in"][0], "conv_w": g_conv_w.reshape(conv_w.shape), "conv_b": g_conv_b,
        "odd_w_out": big["odd_w_out"][0], "final_g": g_final_g.reshape(D_MODEL),
    }
    rest = []
    for idx in range(3):
        rest += [
            small[0][idx], ada[1 + idx], small[1][idx], big["even_w_in"][1 + idx], big["pool_w"][1 + idx],
            small[2][idx], big["even_w_out"][1 + idx], big["odd_w_in"][1 + idx],
            small[3][idx].reshape(conv_w.shape), small[4][idx], big["odd_w_out"][1 + idx],
            small[5][idx].reshape(D_MODEL),
        ]
    order = ["norm_g", "ada_w", "ada_b", "even_w_in", "pool_w", "pool_scale", "even_w_out", "odd_w_in",
             "conv_w", "conv_b", "odd_w_out", "final_g"]
    return (loss, grad_x, *[grads[n] for n in order], *rest)
```

```python
import jax
import jax.numpy as jnp
from jax import lax
from jax.experimental import pallas as pl
from jax.experimental.pallas import tpu as pltpu

F32 = jnp.float32
BF16 = jnp.bfloat16
SDS = jax.ShapeDtypeStruct
MESH = pl.DeviceIdType.MESH

N_DEV = 8
D_MODEL = 1024
D_INNER = 2048
D_POOL = 1024
D_SB = 1024
N_GROUPS = 4
POOL_GROUP = 256
HEAD_DIM = 64
LANES = 128
D_IN_EVEN = 6144
D_IN_ODD = 8192
EPS = 1e-6
ADAM_LR = 0.001
ADAM_B1 = 0.9
ADAM_B2 = 0.999
ADAM_EPS = 1e-08
ADAM_WD = 0.01
ADAM_STEP = 10

ROW_TILE = 256
ATT_TILE = 256
HALO = 16
VMEM_LIMIT = 48 * 1024 * 1024
SLAB_ROWS = 32


def _params(n_axes):
    return pltpu.CompilerParams(dimension_semantics=("arbitrary",) * n_axes, vmem_limit_bytes=VMEM_LIMIT)


def _sigmoid(x):
    return 1.0 / (1.0 + jnp.exp(-x))


def _dot(a, b):
    return jnp.dot(a, b, preferred_element_type=F32)


def _dot_nt(a, b):
    return lax.dot_general(a, b, (((1,), (1,)), ((), ())), preferred_element_type=F32)


def _dot_tn(a, b):
    return lax.dot_general(a, b, (((0,), (0,)), ((), ())), preferred_element_type=F32)


def _mm(name, a, b, *, grid, a_spec, b_spec, o_spec, o_shape, o_dtype, dot, acc_axis=None, acc_shape=None):
    n_acc = grid[acc_axis] if acc_axis is not None else 1

    def body(a_ref, b_ref, o_ref, *scratch):
        prod = dot(a_ref[...], b_ref[...])
        if acc_axis is None:
            o_ref[...] = prod.astype(o_dtype)
        else:
            acc = scratch[0]
            k = pl.program_id(acc_axis)

            @pl.when(k == 0)
            def _():
                acc[...] = prod

            @pl.when(k > 0)
            def _():
                acc[...] += prod

            @pl.when(k == n_acc - 1)
            def _():
                o_ref[...] = acc[...].astype(o_dtype)

    scratch = [] if acc_axis is None else [pltpu.VMEM(acc_shape, F32)]
    return pl.pallas_call(
        body, name=name, grid=grid, in_specs=[a_spec, b_spec], out_specs=o_spec,
        out_shape=SDS(o_shape, o_dtype), scratch_shapes=scratch, compiler_params=_params(len(grid)),
    )(a, b)


def _proj_in(name, h, wg):
    s = h.shape[0]
    cn = wg.shape[2]
    tm = min(s, ROW_TILE)

    def body(a_ref, w_ref, o_ref):
        a = a_ref[...]
        for d in range(N_DEV):
            o_ref[:, d * cn:(d + 1) * cn] = _dot(a, w_ref[d]).astype(BF16)

    return pl.pallas_call(
        body, name=name, grid=(s // tm,),
        in_specs=[pl.BlockSpec((tm, D_MODEL), lambda i: (i, 0)),
                  pl.BlockSpec((N_DEV, D_MODEL, cn), lambda i: (0, 0, 0), pipeline_mode=pl.Buffered(1))],
        out_specs=pl.BlockSpec((tm, N_DEV * cn), lambda i: (i, 0)),
        out_shape=SDS((s, N_DEV * cn), BF16), compiler_params=_params(1),
    )(h, wg)


def _proj_out(name, y, w):
    s = y.shape[0]
    tm = min(s, 512)
    return _mm(name, y, w, grid=(s // tm,),
               a_spec=pl.BlockSpec((tm, D_INNER), lambda i: (i, 0)),
               b_spec=pl.BlockSpec((D_INNER, D_MODEL), lambda i: (0, 0)),
               o_spec=pl.BlockSpec((tm, D_MODEL), lambda i: (i, 0)),
               o_shape=(s, D_MODEL), o_dtype=F32, dot=_dot)


def _proj_out_bwd(name, dyo, w):
    s = dyo.shape[0]
    tm = min(s, 512)
    return _mm(name, dyo, w, grid=(s // tm,),
               a_spec=pl.BlockSpec((tm, D_MODEL), lambda i: (i, 0)),
               b_spec=pl.BlockSpec((D_INNER, D_MODEL), lambda i: (0, 0)),
               o_spec=pl.BlockSpec((tm, D_INNER), lambda i: (i, 0)),
               o_shape=(s, D_INNER), o_dtype=F32, dot=_dot_nt)


def _wgrad_out(name, y, dyo):
    s = y.shape[0]
    ts = min(s, 512)
    return _mm(name, y, dyo, grid=(s // ts,),
               a_spec=pl.BlockSpec((ts, D_INNER), lambda k: (k, 0)),
               b_spec=pl.BlockSpec((ts, D_MODEL), lambda k: (k, 0)),
               o_spec=pl.BlockSpec((D_INNER, D_MODEL), lambda k: (0, 0)),
               o_shape=(D_INNER, D_MODEL), o_dtype=BF16, dot=_dot_tn, acc_axis=0, acc_shape=(D_INNER, D_MODEL))


def _proj_in_bwd(name, dproj, wt, dws=()):
    s, k_all = dproj.shape
    tm = min(s, ROW_TILE)
    n_i = s // tm
    n_side = len(dws)
    w_all = wt.reshape(k_all, D_MODEL)

    def body(a_ref, b_ref, *rest):
        o_ref = rest[n_side]
        side = rest[:n_side] + rest[n_side + 1:]
        i = pl.program_id(0)
        if n_side:
            _side_exchange(side, n_side, True, i == 0, i == n_i - 1)
        o_ref[...] = _dot(a_ref[...], b_ref[...])

    out = pl.pallas_call(
        body, name=name, grid=(n_i,),
        in_specs=[pl.BlockSpec((tm, k_all), lambda i: (i, 0)),
                  pl.BlockSpec((k_all, D_MODEL), lambda i: (0, 0), pipeline_mode=pl.Buffered(1))] + [HBM_SPEC] * n_side,
        out_specs=[pl.BlockSpec((tm, D_MODEL), lambda i: (i, 0))] + [HBM_SPEC] * n_side,
        out_shape=[SDS((s, D_MODEL), F32)] + [SDS(dw.shape, dw.dtype) for dw in dws],
        scratch_shapes=_peer_sems(n_side) if n_side else [],
        compiler_params=_params(1),
    )(dproj, w_all, *dws)
    return out[0], out[1:]


def _wgrad_in(name, h_t, dproj):
    s = h_t.shape[1]
    cn = dproj.shape[1] // N_DEV

    def body(a_ref, b_ref, o_ref):
        o_ref[...] = _dot(a_ref[...], b_ref[...]).astype(BF16)

    return pl.pallas_call(
        body, name=name, grid=(N_DEV,),
        in_specs=[pl.BlockSpec((D_MODEL, s), lambda d: (0, 0), pipeline_mode=pl.Buffered(1)),
                  pl.BlockSpec((s, cn), lambda d: (0, d))],
        out_specs=pl.BlockSpec((None, D_MODEL, cn), lambda d: (d, 0, 0)),
        out_shape=SDS((N_DEV, D_MODEL, cn), BF16), compiler_params=_params(1),
    )(h_t, dproj)


def _vec_spec():
    return pl.BlockSpec((1, D_MODEL), lambda i: (0, 0))


def _row_spec(width=D_MODEL, col=0):
    return pl.BlockSpec((ROW_TILE, width), lambda i: (i, col))


def _col_spec():
    return pl.BlockSpec((D_MODEL, ROW_TILE), lambda i: (0, i))


def _ln_mod(name, x, g, scale, shift):
    s = x.shape[0]

    def body(x_ref, g_ref, sc_ref, sh_ref, h_ref, ht_ref):
        xv = x_ref[...]
        r = lax.rsqrt(jnp.mean(xv * xv, axis=-1, keepdims=True) + EPS)
        n = (xv * r) * g_ref[...]
        h = (n * (1.0 + sc_ref[...]) + sh_ref[...]).astype(BF16)
        h_ref[...] = h
        ht_ref[...] = h.T

    return pl.pallas_call(
        body, name=name, grid=(s // ROW_TILE,),
        in_specs=[_row_spec(), _vec_spec(), _vec_spec(), _vec_spec()], out_specs=[_row_spec(), _col_spec()],
        out_shape=[SDS((s, D_MODEL), BF16), SDS((D_MODEL, s), BF16)], compiler_params=_params(1),
    )(x, g, scale, shift)


def _resid_ln_mod(name, x, yo, gate, g, scale, shift):
    s = x.shape[0]

    def body(x_ref, yo_ref, gt_ref, g_ref, sc_ref, sh_ref, xn_ref, h_ref, ht_ref):
        xv = x_ref[...] + (1.0 + gt_ref[...]) * yo_ref[...]
        xn_ref[...] = xv
        r = lax.rsqrt(jnp.mean(xv * xv, axis=-1, keepdims=True) + EPS)
        n = (xv * r) * g_ref[...]
        h = (n * (1.0 + sc_ref[...]) + sh_ref[...]).astype(BF16)
        h_ref[...] = h
        ht_ref[...] = h.T

    return pl.pallas_call(
        body, name=name, grid=(s // ROW_TILE,),
        in_specs=[_row_spec(), _row_spec(), _vec_spec(), _vec_spec(), _vec_spec(), _vec_spec()],
        out_specs=[_row_spec(), _row_spec(), _col_spec()],
        out_shape=[SDS((s, D_MODEL), F32), SDS((s, D_MODEL), BF16), SDS((D_MODEL, s), BF16)],
        compiler_params=_params(1),
    )(x, yo, gate, g, scale, shift)


def _final_loss(name, x1, yo1, gate1, gf, target):
    s = x1.shape[0]

    def body(x_ref, yo_ref, gt_ref, gf_ref, t_ref, dx_ref, dyo_ref, loss_ref, dgf_ref, dgt_ref):
        i = pl.program_id(0)

        @pl.when(i == 0)
        def _():
            loss_ref[...] = jnp.zeros_like(loss_ref)
            dgf_ref[...] = jnp.zeros_like(dgf_ref)
            dgt_ref[...] = jnp.zeros_like(dgt_ref)

        yo = yo_ref[...]
        one_gate = 1.0 + gt_ref[...]
        x2 = x_ref[...] + one_gate * yo
        r = lax.rsqrt(jnp.mean(x2 * x2, axis=-1, keepdims=True) + EPS)
        xn = x2 * r
        gf_v = gf_ref[...]
        err = xn * gf_v - t_ref[...]
        loss_ref[...] += 0.5 * jnp.sum(jnp.mean(err * err, axis=-1, keepdims=True))
        dout = err * (1.0 / D_MODEL)
        dgf_ref[...] += jnp.sum(dout * xn, axis=0, keepdims=True)
        dxn = dout * gf_v
        dx2 = r * (dxn - xn * jnp.mean(dxn * xn, axis=-1, keepdims=True))
        dx_ref[...] = dx2
        dyo_ref[...] = (dx2 * one_gate).astype(BF16)
        dgt_ref[...] += jnp.sum(dx2 * yo, axis=0, keepdims=True)

    return pl.pallas_call(
        body, name=name, grid=(s // ROW_TILE,),
        in_specs=[_row_spec(), _row_spec(), _vec_spec(), _vec_spec(), _row_spec()],
        out_specs=[_row_spec(), _row_spec(), pl.BlockSpec((1, LANES), lambda i: (0, 0)), _vec_spec(), _vec_spec()],
        out_shape=[SDS((s, D_MODEL), F32), SDS((s, D_MODEL), BF16), SDS((1, LANES), F32),
                   SDS((1, D_MODEL), F32), SDS((1, D_MODEL), F32)],
        compiler_params=_params(1),
    )(x1, yo1, gate1, gf, target)


def _ln_mod_bwd(name, dh, x, dx_next, g, scale):
    s = x.shape[0]

    def body(dh_ref, x_ref, dxn_ref, g_ref, sc_ref, dx_ref, dsh_ref, dsc_ref, dg_ref):
        i = pl.program_id(0)

        @pl.when(i == 0)
        def _():
            dsh_ref[...] = jnp.zeros_like(dsh_ref)
            dsc_ref[...] = jnp.zeros_like(dsc_ref)
            dg_ref[...] = jnp.zeros_like(dg_ref)

        dh_v = dh_ref[...]
        xv = x_ref[...]
        g_v = g_ref[...]
        r = lax.rsqrt(jnp.mean(xv * xv, axis=-1, keepdims=True) + EPS)
        xn = xv * r
        dsh_ref[...] += jnp.sum(dh_v, axis=0, keepdims=True)
        dsc_ref[...] += jnp.sum(dh_v * (xn * g_v), axis=0, keepdims=True)
        dn = dh_v * (1.0 + sc_ref[...])
        dg_ref[...] += jnp.sum(dn * xn, axis=0, keepdims=True)
        dxh = dn * g_v
        dx_ref[...] = dxn_ref[...] + r * (dxh - xn * jnp.mean(dxh * xn, axis=-1, keepdims=True))

    return pl.pallas_call(
        body, name=name, grid=(s // ROW_TILE,),
        in_specs=[_row_spec(), _row_spec(), _row_spec(), _vec_spec(), _vec_spec()],
        out_specs=[_row_spec(), _vec_spec(), _vec_spec(), _vec_spec()],
        out_shape=[SDS((s, D_MODEL), F32)] + [SDS((1, D_MODEL), F32)] * 3, compiler_params=_params(1),
    )(dh, x, dx_next, g, scale)


def _resid_bwd(name, dx, yo, gate):
    s = dx.shape[0]

    def body(dx_ref, yo_ref, gt_ref, dyo_ref, dgt_ref):
        i = pl.program_id(0)

        @pl.when(i == 0)
        def _():
            dgt_ref[...] = jnp.zeros_like(dgt_ref)

        dx_v = dx_ref[...]
        dyo_ref[...] = (dx_v * (1.0 + gt_ref[...])).astype(BF16)
        dgt_ref[...] += jnp.sum(dx_v * yo_ref[...], axis=0, keepdims=True)

    return pl.pallas_call(
        body, name=name, grid=(s // ROW_TILE,),
        in_specs=[_row_spec(), _row_spec(), _vec_spec()], out_specs=[_row_spec(), _vec_spec()],
        out_shape=[SDS((s, D_MODEL), BF16), SDS((1, D_MODEL), F32)], compiler_params=_params(1),
    )(dx, yo, gate)


POOL_WINDOWS = (2, 4, 8, 16)


def _window_sum(x, window, rows, backward):
    acc, step = x, 1
    while step < window:
        acc = acc + pltpu.roll(acc, step if backward else rows - step, axis=0)
        step *= 2
    return acc


def _pool_fwd(name, proj0, wp):
    s = proj0.shape[0]
    hb = ROW_TILE // HALO
    ext_rows = ROW_TILE + HALO

    def body(u_ref, halo_ref, w_ref, p_ref, y_ref):
        i = pl.program_id(0)
        t = i * ROW_TILE + lax.broadcasted_iota(jnp.int32, (ROW_TILE, 1), 0)
        for g, window in enumerate(POOL_WINDOWS):
            cols = slice(g * POOL_GROUP, (g + 1) * POOL_GROUP)
            u = u_ref[:, cols].astype(F32)
            halo = jnp.where(i == 0, 0.0, halo_ref[:, cols].astype(F32))
            ext = jnp.concatenate([halo, u], axis=0)
            win = _window_sum(ext, window, ext_rows, True)[HALO:, :]
            cnt = jnp.minimum(t + 1, window).astype(F32)
            p = (win / cnt - u).astype(BF16)
            p_ref[:, cols] = p
            y_ref[:, cols] = _dot(p, w_ref[g])

    return pl.pallas_call(
        body, name=name, grid=(s // ROW_TILE,),
        in_specs=[pl.BlockSpec((ROW_TILE, D_POOL), lambda i: (i, 0)),
                  pl.BlockSpec((HALO, D_POOL), lambda i: (jnp.maximum(i * hb - 1, 0), 0)),
                  pl.BlockSpec((N_GROUPS, POOL_GROUP, POOL_GROUP), lambda i: (0, 0, 0))],
        out_specs=[_row_spec(D_POOL), _row_spec(D_POOL)],
        out_shape=[SDS((s, D_POOL), BF16), SDS((s, D_POOL), F32)], compiler_params=_params(1),
    )(proj0, proj0, wp)


def _pool_bwd(name, dyp, p, wp):
    s = dyp.shape[0]
    hb = ROW_TILE // HALO
    n_hb = s // HALO
    n_tiles = s // ROW_TILE
    ext_rows = ROW_TILE + HALO

    def body(dy_ref, nxt_ref, p_ref, w_ref, du_ref, dw_ref):
        i = pl.program_id(0)

        @pl.when(i == 0)
        def _():
            dw_ref[...] = jnp.zeros_like(dw_ref)

        t = i * ROW_TILE + lax.broadcasted_iota(jnp.int32, (ext_rows, 1), 0)
        for g, window in enumerate(POOL_WINDOWS):
            cols = slice(g * POOL_GROUP, (g + 1) * POOL_GROUP)
            dy = dy_ref[:, cols]
            nxt = nxt_ref[:, cols]
            nxt = jnp.where(i == n_tiles - 1, jnp.zeros_like(nxt), nxt)
            dp = _dot_nt(jnp.concatenate([dy, nxt], axis=0), w_ref[g])
            cnt = jnp.minimum(t + 1, window).astype(F32)
            win = _window_sum(dp / cnt, window, ext_rows, False)[:ROW_TILE, :]
            du_ref[:, cols] = (win - dp[:ROW_TILE, :]).astype(BF16)
            dw_ref[g] += _dot_tn(p_ref[:, cols], dy)

    return pl.pallas_call(
        body, name=name, grid=(n_tiles,),
        in_specs=[_row_spec(D_POOL),
                  pl.BlockSpec((HALO, D_POOL), lambda i: (jnp.minimum((i + 1) * hb, n_hb - 1), 0)),
                  _row_spec(D_POOL),
                  pl.BlockSpec((N_GROUPS, POOL_GROUP, POOL_GROUP), lambda i: (0, 0, 0))],
        out_specs=[_row_spec(D_POOL), pl.BlockSpec((N_GROUPS, POOL_GROUP, POOL_GROUP), lambda i: (0, 0, 0))],
        out_shape=[SDS((s, D_POOL), BF16), SDS((N_GROUPS, POOL_GROUP, POOL_GROUP), F32)],
        compiler_params=_params(1),
    )(dyp, dyp, p, wp)


FWD_HEADS_PER_STEP = 8
BWD_HEADS_PER_STEP = 4
ATT_SCALE = 0.125


def _att_groups(nh):
    lanes = nh * HEAD_DIM
    return lanes, D_SB // lanes, D_POOL // lanes, (D_POOL + D_SB) // lanes, (D_POOL + 2 * D_SB) // lanes


def _att_consts():
    r = lax.broadcasted_iota(jnp.int32, (ATT_TILE, ATT_TILE), 0)
    c = lax.broadcasted_iota(jnp.int32, (ATT_TILE, ATT_TILE), 1)
    first = lax.broadcasted_iota(jnp.int32, (1, LANES), 1) < HEAD_DIM
    return r, c, first


def _pair(x, p):
    return x[:, p * LANES:(p + 1) * LANES]


def _one_head(x, first, hh):
    zero = jnp.zeros_like(x)
    return jnp.where(first, x, zero) if hh == 0 else jnp.where(first, zero, x)


def _neg_softplus(z):
    return -(jnp.maximum(z, 0.0) + jnp.log(1.0 + jnp.exp(-jnp.abs(z))))


def _side_exchange(side_refs, n_side, by_chunk, is_first, is_last):
    ins, outs = side_refs[:n_side], side_refs[n_side:2 * n_side]
    sems = side_refs[2 * n_side:2 * n_side + 3]

    @pl.when(is_first)
    def _():
        for cp in _peer_copies(ins, outs, *sems, by_chunk=by_chunk):
            cp.start()

    @pl.when(is_last)
    def _():
        for cp in _peer_copies(ins, outs, *sems, by_chunk=by_chunk):
            cp.wait()


def _attn_fwd(name, proj0, shards):
    s = proj0.shape[0]
    nq = s // ATT_TILE
    nh = FWD_HEADS_PER_STEP
    ATT_GROUP, N_ATT_GROUPS, Q_GRP, K_GRP, V_GRP = _att_groups(nh)
    n_side = len(shards)

    def body(q_ref, k_ref, v_ref, *rest):
        o_ref = rest[n_side]
        side = rest[:n_side] + rest[n_side + 1:]
        j = pl.program_id(0)
        i = pl.program_id(1)
        _side_exchange(side, n_side, False, (j == 0) & (i == 0), (j == N_ATT_GROUPS - 1) & (i == nq - 1))
        r, c, first = _att_consts()
        tri = (r >= c).astype(BF16)
        below = c < r
        q = q_ref[...] * ATT_SCALE
        qh = [_one_head(_pair(q, h // 2), first, h % 2) for h in range(nh)]

        def tile(kb, carry, diagonal):
            k0 = pl.multiple_of(kb * ATT_TILE, ATT_TILE)
            kt = k_ref[pl.ds(k0, ATT_TILE), :]
            vt = v_ref[pl.ds(k0, ATT_TILE), :]
            z = [_dot_nt(qh[h], _pair(kt, h // 2)) for h in range(nh)]
            lf = [_neg_softplus(z[h]) for h in range(nh)]
            if diagonal:
                lf = [jnp.where(below, x, 0.0) for x in lf]
            run = [_dot(lf[h].astype(BF16), tri) for h in range(nh)]
            a = [jnp.exp(z[h] + run[h] + carry[h]) for h in range(nh)]
            if diagonal:
                a = [jnp.where(below, x, 0.0) for x in a]
            out_acc = [carry[nh + h] + _dot(a[h].astype(BF16), _pair(vt, h // 2)) for h in range(nh)]
            out_c = [carry[h] + jnp.sum(lf[h], axis=1, keepdims=True) for h in range(nh)]
            return tuple(out_c + out_acc)

        init = tuple([jnp.zeros((ATT_TILE, 1), F32)] * nh + [jnp.zeros((ATT_TILE, LANES), F32)] * nh)
        carry = tile(i, init, True)
        carry = lax.fori_loop(1, i + 1, lambda n, cr: tile(i - n, cr, False), carry)
        for p in range(nh // 2):
            o_ref[:, p * LANES:(p + 1) * LANES] = jnp.where(first, carry[nh + 2 * p], carry[nh + 2 * p + 1])

    out = pl.pallas_call(
        body, name=name, grid=(N_ATT_GROUPS, nq),
        in_specs=[pl.BlockSpec((ATT_TILE, ATT_GROUP), lambda j, i: (i, Q_GRP + j)),
                  pl.BlockSpec((s, ATT_GROUP), lambda j, i: (0, K_GRP + j)),
                  pl.BlockSpec((s, ATT_GROUP), lambda j, i: (0, V_GRP + j))] + [HBM_SPEC] * n_side,
        out_specs=[pl.BlockSpec((ATT_TILE, ATT_GROUP), lambda j, i: (i, j))] + [HBM_SPEC] * n_side,
        out_shape=[SDS((s, D_SB), F32)] + [SDS((N_DEV,) + sh.shape, sh.dtype) for sh in shards],
        scratch_shapes=_peer_sems(n_side), compiler_params=_params(2),
    )(proj0, proj0, proj0, *shards)
    return out[0], out[1:]


def _attn_bwd(name, proj0, o, do, dws, shards):
    s = proj0.shape[0]
    nq = s // ATT_TILE
    nh = BWD_HEADS_PER_STEP
    ATT_GROUP, N_ATT_GROUPS, Q_GRP, K_GRP, V_GRP = _att_groups(nh)
    n1, n2 = len(dws), len(shards)
    n_side = n1 + n2

    def body(q_ref, k_ref, v_ref, o_ref, do_ref, *rest):
        dq_ref, dk_ref, dv_ref = rest[n_side:n_side + 3]
        dk_acc, dv_acc = rest[2 * n_side + 3:2 * n_side + 5]
        srcs, dsts, sems = rest[:n_side], rest[n_side + 3:2 * n_side + 3], rest[2 * n_side + 5:]
        j = pl.program_id(0)
        i = pl.program_id(1)
        is_first, is_last = (j == 0) & (i == 0), (j == N_ATT_GROUPS - 1) & (i == nq - 1)
        _side_exchange(srcs[:n1] + dsts[:n1] + sems[:3], n1, True, is_first, is_last)
        _side_exchange(srcs[n1:] + dsts[n1:] + sems[3:], n2, False, is_first, is_last)

        @pl.when(i == 0)
        def _():
            dk_acc[...] = jnp.zeros_like(dk_acc)
            dv_acc[...] = jnp.zeros_like(dv_acc)

        r, c, first = _att_consts()
        tri = (r >= c).astype(BF16)
        tri_p = (r <= c).astype(BF16)
        below = c < r
        q = q_ref[...] * ATT_SCALE
        do_b = do_ref[...].astype(BF16)
        do_o = do_b.astype(F32) * o_ref[...]
        qh = [_one_head(_pair(q, h // 2), first, h % 2) for h in range(nh)]
        doh = [_one_head(_pair(do_b, h // 2), first, h % 2) for h in range(nh)]
        dsum = [jnp.sum(_one_head(_pair(do_o, h // 2), first, h % 2), axis=1, keepdims=True) for h in range(nh)]

        def tile(kb, carry, diagonal):
            k0 = pl.multiple_of(kb * ATT_TILE, ATT_TILE)
            kt = k_ref[pl.ds(k0, ATT_TILE), :]
            vt = v_ref[pl.ds(k0, ATT_TILE), :]
            hs = range(nh)
            z = [_dot_nt(qh[h], _pair(kt, h // 2)) for h in hs]
            d_a = [_dot_nt(doh[h], _pair(vt, h // 2)) for h in hs]
            lf = [_neg_softplus(z[h]) for h in hs]
            sig = [jnp.exp(z[h] + lf[h]) for h in hs]
            if diagonal:
                lf = [jnp.where(below, x, 0.0) for x in lf]
            run = [_dot(lf[h].astype(BF16), tri) for h in hs]
            a = [jnp.exp(z[h] + run[h] + carry[h]) for h in hs]
            if diagonal:
                a = [jnp.where(below, x, 0.0) for x in a]
            a_b = [x.astype(BF16) for x in a]
            g = [a_b[h].astype(F32) * d_a[h] for h in hs]
            g_sum = [jnp.sum(g[h], axis=1, keepdims=True) for h in hs]
            early = [dsum[h] - carry[nh + h] - g_sum[h] for h in hs]
            upto = [_dot(g[h].astype(BF16), tri_p) for h in hs]
            dv_t = [_dot_tn(a_b[2 * p], doh[2 * p]) + _dot_tn(a_b[2 * p + 1], doh[2 * p + 1]) for p in range(nh // 2)]
            dz = [g[h] - sig[h] * (early[h] + upto[h]) for h in hs]
            if diagonal:
                dz = [jnp.where(below, x, 0.0) for x in dz]
            dz = [x.astype(BF16) for x in dz]
            out_dq = [carry[2 * nh + h] + _dot(dz[h], _pair(kt, h // 2)) for h in hs]
            dk_t = [_dot_tn(dz[2 * p], qh[2 * p]) + _dot_tn(dz[2 * p + 1], qh[2 * p + 1]) for p in range(nh // 2)]
            for p in range(nh // 2):
                dk_acc[pl.ds(k0, ATT_TILE), p * LANES:(p + 1) * LANES] += dk_t[p]
                dv_acc[pl.ds(k0, ATT_TILE), p * LANES:(p + 1) * LANES] += dv_t[p]
            out_c1 = [carry[h] + jnp.sum(lf[h], axis=1, keepdims=True) for h in hs]
            out_c2 = [carry[nh + h] + g_sum[h] for h in hs]
            return tuple(out_c1 + out_c2 + out_dq)

        init = tuple([jnp.zeros((ATT_TILE, 1), F32)] * (2 * nh) + [jnp.zeros((ATT_TILE, LANES), F32)] * nh)
        carry = tile(i, init, True)
        carry = lax.fori_loop(1, i + 1, lambda n, cr: tile(i - n, cr, False), carry)
        for p in range(nh // 2):
            dq_p = jnp.where(first, carry[2 * nh + 2 * p], carry[2 * nh + 2 * p + 1]) * ATT_SCALE
            dq_ref[:, p * LANES:(p + 1) * LANES] = dq_p.astype(BF16)

        @pl.when(i == nq - 1)
        def _():
            dk_ref[...] = dk_acc[...].astype(BF16)
            dv_ref[...] = dv_acc[...].astype(BF16)

    tile_spec = pl.BlockSpec((ATT_TILE, ATT_GROUP), lambda j, i: (i, j))
    full = pl.BlockSpec((s, ATT_GROUP), lambda j, i: (0, j))
    out = pl.pallas_call(
        body, name=name, grid=(N_ATT_GROUPS, nq),
        in_specs=[pl.BlockSpec((ATT_TILE, ATT_GROUP), lambda j, i: (i, Q_GRP + j)),
                  pl.BlockSpec((s, ATT_GROUP), lambda j, i: (0, K_GRP + j)),
                  pl.BlockSpec((s, ATT_GROUP), lambda j, i: (0, V_GRP + j)),
                  tile_spec, tile_spec] + [HBM_SPEC] * n_side,
        out_specs=[tile_spec, full, full] + [HBM_SPEC] * n_side,
        out_shape=[SDS((s, D_SB), BF16)] * 3 + [SDS(dw.shape, dw.dtype) for dw in dws]
        + [SDS((N_DEV,) + sh.shape, sh.dtype) for sh in shards],
        scratch_shapes=[pltpu.VMEM((s, ATT_GROUP), F32), pltpu.VMEM((s, ATT_GROUP), F32)] + _peer_sems(n1) + _peer_sems(n2),
        compiler_params=_params(2),
    )(proj0, proj0, proj0, o, do, *dws, *shards)
    return out[0], out[1], out[2], out[3:3 + n1], out[3 + n1:]


GATE0_COL = (D_POOL + 3 * D_SB) // D_INNER


def _gate_fwd0(name, yp_raw, o, proj0, ps):
    s = o.shape[0]

    def body(yp_ref, o_ref, gt_ref, ps_ref, y_ref):
        gt = gt_ref[...].astype(F32)
        sg = gt * _sigmoid(gt)
        y_ref[:, :D_POOL] = (yp_ref[...] * ps_ref[...] * sg[:, :D_POOL]).astype(BF16)
        y_ref[:, D_POOL:] = (o_ref[...] * sg[:, D_POOL:]).astype(BF16)

    return pl.pallas_call(
        body, name=name, grid=(s // ROW_TILE,),
        in_specs=[_row_spec(), _row_spec(), _row_spec(D_INNER, GATE0_COL), _vec_spec()],
        out_specs=_row_spec(D_INNER),
        out_shape=SDS((s, D_INNER), BF16), compiler_params=_params(1),
    )(yp_raw, o, proj0, ps)


def _dsilu(x):
    sg = _sigmoid(x)
    return sg * (1.0 + x * (1.0 - sg))


def _gate_bwd0(name, dymix, yp_raw, o, proj0, ps):
    s = o.shape[0]

    def body(dy_ref, yp_ref, o_ref, gt_ref, ps_ref, dyp_ref, do_ref, dgt_ref, dps_ref):
        i = pl.program_id(0)

        @pl.when(i == 0)
        def _():
            dps_ref[...] = jnp.zeros_like(dps_ref)

        gt = gt_ref[...].astype(F32)
        dy = dy_ref[...]
        sg = gt * _sigmoid(gt)
        dsg = _dsilu(gt)
        dcat = dy * sg
        yp = yp_ref[...]
        ps_v = ps_ref[...]
        dyp_ref[...] = (dcat[:, :D_POOL] * ps_v).astype(BF16)
        do_ref[...] = dcat[:, D_POOL:]
        dps_ref[...] += jnp.sum(dcat[:, :D_POOL] * yp, axis=0, keepdims=True)
        dgt_ref[:, :D_POOL] = (dy[:, :D_POOL] * (yp * ps_v) * dsg[:, :D_POOL]).astype(BF16)
        dgt_ref[:, D_POOL:] = (dy[:, D_POOL:] * o_ref[...] * dsg[:, D_POOL:]).astype(BF16)

    return pl.pallas_call(
        body, name=name, grid=(s // ROW_TILE,),
        in_specs=[_row_spec(D_INNER), _row_spec(), _row_spec(), _row_spec(D_INNER, GATE0_COL), _vec_spec()],
        out_specs=[_row_spec(), _row_spec(), _row_spec(D_INNER), _vec_spec()],
        out_shape=[SDS((s, D_POOL), BF16), SDS((s, D_SB), F32), SDS((s, D_INNER), BF16), SDS((1, D_POOL), F32)],
        compiler_params=_params(1),
    )(dymix, yp_raw, o, proj0, ps)


CONV_HALO = 16


def _conv_fwd(name, proj1, cw, cb):
    s = proj1.shape[0]
    hb = ROW_TILE // CONV_HALO
    ext_rows = ROW_TILE + CONV_HALO

    def body(gb_ref, gc_ref, u_ref, gt_ref, gch_ref, uh_ref, cw_ref, cb_ref, y_ref):
        i = pl.program_id(0)
        uc = gc_ref[...].astype(F32) * u_ref[...].astype(F32)
        halo = jnp.where(i == 0, 0.0, gch_ref[...].astype(F32) * uh_ref[...].astype(F32))
        ext = jnp.concatenate([halo, uc], axis=0)
        uc1 = pltpu.roll(ext, 1, axis=0)[CONV_HALO:, :]
        uc2 = pltpu.roll(ext, 2, axis=0)[CONV_HALO:, :]
        cw_v = cw_ref[...]
        conv = cb_ref[...] + cw_v[0:1, :] * uc2 + cw_v[1:2, :] * uc1 + cw_v[2:3, :] * uc
        gt = gt_ref[...].astype(F32)
        y_ref[...] = (gb_ref[...].astype(F32) * conv * (gt * _sigmoid(gt))).astype(BF16)

    def tile(part):
        return pl.BlockSpec((ROW_TILE, D_INNER), lambda i: (i, part))

    def halo(part):
        return pl.BlockSpec((CONV_HALO, D_INNER), lambda i: (jnp.maximum(i * hb - 1, 0), part))

    return pl.pallas_call(
        body, name=name, grid=(s // ROW_TILE,),
        in_specs=[tile(0), tile(1), tile(2), tile(3), halo(1), halo(2),
                  pl.BlockSpec((3, D_INNER), lambda i: (0, 0)), pl.BlockSpec((1, D_INNER), lambda i: (0, 0))],
        out_specs=pl.BlockSpec((ROW_TILE, D_INNER), lambda i: (i, 0)),
        out_shape=SDS((s, D_INNER), BF16), compiler_params=_params(1),
    )(proj1, proj1, proj1, proj1, proj1, proj1, cw, cb)


def _conv_bwd(name, dymix, proj1, cw, cb):
    s = proj1.shape[0]
    hb = ROW_TILE // CONV_HALO
    n_hb = s // CONV_HALO
    n_tiles = s // ROW_TILE
    ext_rows = ROW_TILE + CONV_HALO

    def body(dy_ref, gb_ref, gc_ref, u_ref, gt_ref, gch_ref, uh_ref, dyn_ref, gbn_ref, gtn_ref, cw_ref, cb_ref,
             dproj_ref, dcw_ref, dcb_ref):
        i = pl.program_id(0)

        @pl.when(i == 0)
        def _():
            dcw_ref[...] = jnp.zeros_like(dcw_ref)
            dcb_ref[...] = jnp.zeros_like(dcb_ref)

        gc = gc_ref[...].astype(F32)
        u = u_ref[...].astype(F32)
        gb = gb_ref[...].astype(F32)
        gt = gt_ref[...].astype(F32)
        dy = dy_ref[...]
        uc = gc * u
        halo = jnp.where(i == 0, 0.0, gch_ref[...].astype(F32) * uh_ref[...].astype(F32))
        ext = jnp.concatenate([halo, uc], axis=0)
        uc1 = pltpu.roll(ext, 1, axis=0)[CONV_HALO:, :]
        uc2 = pltpu.roll(ext, 2, axis=0)[CONV_HALO:, :]
        cw_v = cw_ref[...]
        w0, w1, w2 = cw_v[0:1, :], cw_v[1:2, :], cw_v[2:3, :]
        conv = cb_ref[...] + w0 * uc2 + w1 * uc1 + w2 * uc
        sig = _sigmoid(gt)
        sg = gt * sig
        dconv = dy * gb * sg
        gtn = gtn_ref[...].astype(F32)
        dconv_next = jnp.where(i == n_tiles - 1, 0.0, dyn_ref[...] * gbn_ref[...].astype(F32) * (gtn * _sigmoid(gtn)))
        dext = jnp.concatenate([dconv, dconv_next], axis=0)
        dconv_p1 = pltpu.roll(dext, ext_rows - 1, axis=0)[:ROW_TILE, :]
        dconv_p2 = pltpu.roll(dext, ext_rows - 2, axis=0)[:ROW_TILE, :]
        duc = w2 * dconv + w1 * dconv_p1 + w0 * dconv_p2
        dproj_ref[:, 0:D_INNER] = (dy * conv * sg).astype(BF16)
        dproj_ref[:, D_INNER:2 * D_INNER] = (duc * u).astype(BF16)
        dproj_ref[:, 2 * D_INNER:3 * D_INNER] = (duc * gc).astype(BF16)
        dproj_ref[:, 3 * D_INNER:] = (dy * gb * conv * (sig + sg * (1.0 - sig))).astype(BF16)
        dcw_ref[0:1, :] += jnp.sum(dconv * uc2, axis=0, keepdims=True)
        dcw_ref[1:2, :] += jnp.sum(dconv * uc1, axis=0, keepdims=True)
        dcw_ref[2:3, :] += jnp.sum(dconv * uc, axis=0, keepdims=True)
        dcb_ref[...] += jnp.sum(dconv, axis=0, keepdims=True)

    def tile(part):
        return pl.BlockSpec((ROW_TILE, D_INNER), lambda i: (i, part))

    def prev(part):
        return pl.BlockSpec((CONV_HALO, D_INNER), lambda i: (jnp.maximum(i * hb - 1, 0), part))

    def nxt(part):
        return pl.BlockSpec((CONV_HALO, D_INNER), lambda i: (jnp.minimum((i + 1) * hb, n_hb - 1), part))

    whole = lambda rows: pl.BlockSpec((rows, D_INNER), lambda i: (0, 0))
    return pl.pallas_call(
        body, name=name, grid=(n_tiles,),
        in_specs=[tile(0), tile(0), tile(1), tile(2), tile(3), prev(1), prev(2), nxt(0), nxt(0), nxt(3),
                  whole(3), whole(1)],
        out_specs=[pl.BlockSpec((ROW_TILE, 4 * D_INNER), lambda i: (i, 0)), whole(3), whole(1)],
        out_shape=[SDS((s, 4 * D_INNER), BF16), SDS((3, D_INNER), F32), SDS((1, D_INNER), F32)],
        compiler_params=_params(1),
    )(dymix, proj1, proj1, proj1, proj1, proj1, proj1, dymix, proj1, proj1, cw, cb)


def _place():
    x, y, c = lax.axis_index("x"), lax.axis_index("y"), lax.axis_index("c")
    return x, y, c


def _flip(x, y, c, k):
    fx, fy, fc = (k >> 2) & 1, (k >> 1) & 1, k & 1
    return (1 - x if fx else x, 1 - y if fy else y, 1 - c if fc else c)


def _dev_index(p):
    return 4 * p[0] + 2 * p[1] + p[2]


HBM_SPEC = pl.BlockSpec(memory_space=pltpu.HBM)
VMEM_SPEC = pl.BlockSpec(memory_space=pltpu.VMEM)


def _two_level_gather(src, dst, send_sems, recv_sems, local_sem):
    x, y, c = _place()
    me, sibling = (x, y, c), (x, y, 1 - c)
    chips = [(1 - x, y), (x, 1 - y), (1 - x, 1 - y)]

    def copy(k, block, to, from_src=False):
        rows = dst.at[_dev_index(block)]
        return pltpu.make_async_remote_copy(
            src_ref=src if from_src else rows, dst_ref=rows, send_sem=send_sems.at[k], recv_sem=recv_sems.at[k],
            device_id=to, device_id_type=MESH)

    mine = pltpu.make_async_copy(src, dst.at[_dev_index(me)], local_sem)
    first = [copy(0, me, sibling, True)] + [copy(1 + j, me, (*chip, c), True) for j, chip in enumerate(chips)]
    passed = [copy(4 + j, (*chip, c), sibling) for j, chip in enumerate(chips)]

    def start():
        mine.start()
        for cp in first:
            cp.start()

    def pass_on():
        for j, chip in enumerate(chips):
            copy(1 + j, (*chip, c), me).wait_recv()
            passed[j].start()

    def finish():
        copy(0, sibling, me).wait_recv()
        for j, chip in enumerate(chips):
            copy(4 + j, (*chip, 1 - c), me).wait_recv()
        for cp in first + passed:
            cp.wait_send()
        mine.wait()

    return start, pass_on, finish


def _peer_copies(ins, outs, send_sems, recv_sems, local_sems, by_chunk):
    x, y, c = _place()
    my = _dev_index((x, y, c))
    copies = []
    for w in range(len(ins)):
        copies.append(pltpu.make_async_copy(ins[w].at[my] if by_chunk else ins[w], outs[w].at[my], local_sems.at[w]))
        for k in range(1, N_DEV):
            peer = _flip(x, y, c, k)
            copies.append(pltpu.make_async_remote_copy(
                src_ref=ins[w].at[_dev_index(peer)] if by_chunk else ins[w], dst_ref=outs[w].at[my],
                send_sem=send_sems.at[7 * w + k - 1], recv_sem=recv_sems.at[7 * w + k - 1],
                device_id=peer, device_id_type=MESH))
    return copies


def _peer_sems(n_w):
    return [pltpu.SemaphoreType.DMA((7 * n_w,)), pltpu.SemaphoreType.DMA((7 * n_w,)), pltpu.SemaphoreType.DMA((n_w,))]


ADA_COLS = 3 * D_MODEL // N_DEV


def _ada_forward(c_row, conv_w, conv_b, ada_w, ada_b, w_shard):
    cw_cols = conv_w.shape[1]

    def body(c_ref, cw_ref, cb_ref, aw_ref, ab_ref, w_ref, m_ref, cs_ref, cwf_ref, cbf_ref, wg_ref,
             slab, gath, part, land, send_sems, recv_sems, w_send, w_recv, w_local):
        x, y, c = _place()
        my = _dev_index((x, y, c))
        w_start, w_pass_on, w_finish = _two_level_gather(w_ref, wg_ref, w_send, w_recv, w_local)
        w_start()
        slab[...] = jnp.zeros_like(slab)
        slab[0:1, :] = c_ref[...]
        slab[1:4, 0:cw_cols] = cw_ref[...]
        slab[4:5, 0:cw_cols] = cb_ref[...]
        gath[my] = slab[...]
        sends = []
        for k in range(1, N_DEV):
            peer = _flip(x, y, c, k)
            cp = pltpu.make_async_remote_copy(
                src_ref=slab, dst_ref=gath.at[my], send_sem=send_sems.at[k - 1], recv_sem=recv_sems.at[k - 1],
                device_id=peer, device_id_type=MESH)
            cp.start()
            sends.append(cp)
        for cp in sends:
            cp.wait()
        for d in range(N_DEV):
            c_d = gath[d, 0:1, :]
            cs_ref[d:d + 1, :] = c_d * _sigmoid(c_d)
            cwf_ref[:, d * cw_cols:(d + 1) * cw_cols] = gath[d, 1:4, 0:cw_cols]
            cbf_ref[:, d * cw_cols:(d + 1) * cw_cols] = gath[d, 4:5, 0:cw_cols]
        cs = cs_ref[...]
        part[...] = jnp.zeros_like(part)
        for layer in range(2):
            m_part = jnp.dot(cs, aw_ref[layer], preferred_element_type=F32, precision=lax.Precision.HIGHEST)
            for d in range(N_DEV):
                part[d, layer:layer + 1, :] = m_part[d:d + 1, :]
        land[my] = part[my]
        sends = []
        for k in range(1, N_DEV):
            peer = _flip(x, y, c, k)
            cp = pltpu.make_async_remote_copy(
                src_ref=part.at[_dev_index(peer)], dst_ref=land.at[my],
                send_sem=send_sems.at[6 + k], recv_sem=recv_sems.at[6 + k],
                device_id=peer, device_id_type=MESH)
            cp.start()
            sends.append(cp)
        for cp in sends:
            cp.wait()
        for d in range(N_DEV):
            cols = slice(d * ADA_COLS, (d + 1) * ADA_COLS)
            m_ref[:, cols] = land[d, 0:2, :] + ab_ref[:, cols]
        w_pass_on()
        w_finish()

    return pl.pallas_call(
        body, name="ada_forward",
        out_shape=[SDS((2, 3 * D_MODEL), F32), SDS((N_DEV, D_MODEL), F32), SDS((3, N_DEV * cw_cols), F32),
                   SDS((1, N_DEV * cw_cols), F32), SDS((N_DEV,) + w_shard.shape, w_shard.dtype)],
        in_specs=[VMEM_SPEC] * 5 + [HBM_SPEC], out_specs=[VMEM_SPEC] * 4 + [HBM_SPEC],
        scratch_shapes=[pltpu.VMEM((8, D_MODEL), F32), pltpu.VMEM((N_DEV, 8, D_MODEL), F32),
                        pltpu.VMEM((N_DEV, 8, ADA_COLS), F32), pltpu.VMEM((N_DEV, 8, ADA_COLS), F32),
                        pltpu.SemaphoreType.DMA((14,)), pltpu.SemaphoreType.DMA((14,)),
                        pltpu.SemaphoreType.DMA((7,)), pltpu.SemaphoreType.DMA((7,)), pltpu.SemaphoreType.DMA],
        compiler_params=pltpu.CompilerParams(vmem_limit_bytes=VMEM_LIMIT),
    )(c_row, conv_w, conv_b, ada_w, ada_b, w_shard)


def _small_grads(slab):
    def body(slab_ref, gath_ref, tot_ref, send_sems, recv_sems):
        x, y, c = _place()
        my = _dev_index((x, y, c))
        gath_ref[my] = slab_ref[...]
        sends = []
        for k in range(1, N_DEV):
            peer = _flip(x, y, c, k)
            cp = pltpu.make_async_remote_copy(
                src_ref=slab_ref, dst_ref=gath_ref.at[my], send_sem=send_sems.at[k - 1], recv_sem=recv_sems.at[k - 1],
                device_id=peer, device_id_type=MESH)
            cp.start()
            sends.append(cp)
        for cp in sends:
            cp.wait()
        tot = gath_ref[0]
        for d in range(1, N_DEV):
            tot = tot + gath_ref[d]
        tot_ref[...] = tot

    return pl.pallas_call(
        body, name="small_grads",
        out_shape=[SDS((N_DEV, SLAB_ROWS, D_MODEL), F32), SDS((SLAB_ROWS, D_MODEL), F32)],
        in_specs=[VMEM_SPEC], out_specs=[VMEM_SPEC] * 2,
        scratch_shapes=[pltpu.SemaphoreType.DMA((7,)), pltpu.SemaphoreType.DMA((7,))],
    )(slab)


def _adamw_math(w, g, m, v):
    m = ADAM_B1 * m + (1.0 - ADAM_B1) * g
    v = ADAM_B2 * v + (1.0 - ADAM_B2) * jnp.square(g)
    m_hat = m / (1.0 - ADAM_B1 ** ADAM_STEP)
    v_hat = v / (1.0 - ADAM_B2 ** ADAM_STEP)
    delta = -ADAM_LR * (m_hat / (jnp.sqrt(v_hat) + ADAM_EPS) + ADAM_WD * w)
    return delta, m, v


def _sum_adamw(name, recv, w, m, v):
    rows, cols = w.shape
    tr = min(rows, 256)

    def body(r_ref, w_ref, m_ref, v_ref, g_ref, d_ref, nm_ref, nv_ref):
        g = r_ref[0].astype(F32)
        for d in range(1, N_DEV):
            g = g + r_ref[d].astype(F32)
        g_ref[...] = g
        d_ref[...], nm_ref[...], nv_ref[...] = _adamw_math(w_ref[...], g, m_ref[...], v_ref[...])

    blk = pl.BlockSpec((tr, cols), lambda i: (i, 0))
    return pl.pallas_call(
        body, name=name, grid=(rows // tr,),
        in_specs=[pl.BlockSpec((N_DEV, tr, cols), lambda i: (0, i, 0)), blk, blk, blk],
        out_specs=[blk] * 4, out_shape=[SDS((rows, cols), F32)] * 4, compiler_params=_params(1),
    )(recv, w, m, v)


def _ada_w_adamw(name, cs_t, dm_cols, w, m, v):
    def body(cs_ref, dm_ref, w_ref, m_ref, v_ref, g_ref, d_ref, nm_ref, nv_ref):
        cs = cs_ref[...]
        dm = dm_ref[...]
        g = cs[:, 0:1] * dm[0:1, :]
        for b in range(1, N_DEV):
            g = g + cs[:, b:b + 1] * dm[b:b + 1, :]
        g_ref[...] = g
        d_ref[...], nm_ref[...], nv_ref[...] = _adamw_math(w_ref[...], g, m_ref[...], v_ref[...])

    blk = pl.BlockSpec((None, D_MODEL, ADA_COLS), lambda l: (l, 0, 0))
    return pl.pallas_call(
        body, name=name, grid=(2,),
        in_specs=[pl.BlockSpec((D_MODEL, N_DEV), lambda l: (0, 0)),
                  pl.BlockSpec((None, N_DEV, ADA_COLS), lambda l: (l, 0, 0)), blk, blk, blk],
        out_specs=[blk] * 4, out_shape=[SDS((2, D_MODEL, ADA_COLS), F32)] * 4, compiler_params=_params(1),
    )(cs_t, dm_cols, w, m, v)


def _small_adamw(name, triples):
    n = len(triples)

    def body(*refs):
        ins, outs = refs[:4 * n], refs[4 * n:]
        for j in range(n):
            w_ref, g_ref, m_ref, v_ref = ins[4 * j:4 * j + 4]
            d, nm, nv = _adamw_math(w_ref[...], g_ref[...], m_ref[...], v_ref[...])
            outs[3 * j][...] = d
            outs[3 * j + 1][...] = nm
            outs[3 * j + 2][...] = nv

    flat = [a for t in triples for a in t]
    return pl.pallas_call(
        body, name=name,
        out_shape=[SDS(t[0].shape, F32) for t in triples for _ in range(3)],
        in_specs=[VMEM_SPEC] * (4 * n), out_specs=[VMEM_SPEC] * (3 * n),
    )(*flat)


def kernel(x, c, norm_g, ada_w, ada_b, even_w_in, pool_w, pool_scale, even_w_out, odd_w_in, conv_w, conv_b, odd_w_out, final_g, loss_target, m_norm_g, m_ada_w, m_ada_b, m_even_w_in, m_pool_w, m_pool_scale, m_even_w_out, m_odd_w_in, m_conv_w, m_conv_b, m_odd_w_out, m_final_g, v_norm_g, v_ada_w, v_ada_b, v_even_w_in, v_pool_w, v_pool_scale, v_even_w_out, v_odd_w_in, v_conv_w, v_conv_b, v_odd_w_out, v_final_g):
    seq = x.shape[1]
    x0 = x[0]
    target = loss_target[0]
    final_g2 = final_g.reshape(1, D_MODEL)

    w_in_e = even_w_in[0]
    w_out_e = even_w_out[0]
    w_in_o = odd_w_in[0]
    w_out_o = odd_w_out[0]
    w_pool = pool_w[0].reshape(N_GROUPS * 32, POOL_GROUP)
    shards = [w.astype(BF16) for w in (w_in_e, w_out_e, w_in_o, w_out_o, w_pool)]

    m_vec, cs_all, conv_w_full, conv_b_full, wg_in_e = _ada_forward(c, conv_w[0], conv_b, ada_w, ada_b, shards[0])
    shift = [m_vec[l:l + 1, 0:D_MODEL] for l in range(2)]
    scale = [m_vec[l:l + 1, D_MODEL:2 * D_MODEL] for l in range(2)]
    gate = [m_vec[l:l + 1, 2 * D_MODEL:] for l in range(2)]
    ng = [norm_g[l:l + 1] for l in range(2)]

    h0, h0_t = _ln_mod("ln_mod0", x0, ng[0], scale[0], shift[0])
    proj0 = _proj_in("proj_in0", h0, wg_in_e)
    o, (wg_out_e, wg_in_o, wg_out_o, wg_pool, wt_in_o) = _attn_fwd("attn_fwd", proj0, shards[1:] + [shards[2].T])
    wf_out_e = wg_out_e.reshape(D_INNER, D_MODEL)
    wf_out_o = wg_out_o.reshape(D_INNER, D_MODEL)
    wf_pool = wg_pool.reshape(N_DEV, N_GROUPS, 32, POOL_GROUP).transpose(1, 0, 2, 3).reshape(N_GROUPS, POOL_GROUP, POOL_GROUP)
    p, yp_raw = _pool_fwd("pool_fwd", proj0, wf_pool)
    ymix0 = _gate_fwd0("gate_fwd0", yp_raw, o, proj0, pool_scale)
    yo0 = _proj_out("proj_out0", ymix0, wf_out_e)

    x1, h1, h1_t = _resid_ln_mod("resid_ln_mod1", x0, yo0, gate[0], ng[1], scale[1], shift[1])
    proj1 = _proj_in("proj_in1", h1, wg_in_o)
    ymix1 = _conv_fwd("conv_fwd", proj1, conv_w_full, conv_b_full)
    yo1 = _proj_out("proj_out1", ymix1, wf_out_o)

    dx2, dyo1, loss_acc, d_final_g, d_gate1 = _final_loss("final_loss", x1, yo1, gate[1], final_g2, target)
    loss = lax.psum(loss_acc[0, 0], ("x", "y", "c"))

    dymix1 = _proj_out_bwd("proj_out1_bwd", dyo1, wf_out_o)
    dw_out_o = _wgrad_out("wgrad_out1", ymix1, dyo1)
    dproj1, d_conv_w, d_conv_b = _conv_bwd("conv_bwd", dymix1, proj1, conv_w_full, conv_b_full)
    dh1, _ = _proj_in_bwd("proj_in1_bwd", dproj1, wt_in_o)
    dw_in_o = _wgrad_in("wgrad_in1", h1_t, dproj1)
    dx1, d_shift1, d_scale1, d_ng1 = _ln_mod_bwd("ln_mod1_bwd", dh1, x1, dx2, ng[1], scale[1])

    dyo0, d_gate0 = _resid_bwd("resid0_bwd", dx1, yo0, gate[0])
    dymix0 = _proj_out_bwd("proj_out0_bwd", dyo0, wf_out_e)
    dw_out_e = _wgrad_out("wgrad_out0", ymix0, dyo0)
    dyp, do, dgt0, d_pool_scale = _gate_bwd0("gate_bwd0", dymix0, yp_raw, o, proj0, pool_scale)
    du_pool, dw_pool = _pool_bwd("pool_bwd", dyp, p, wf_pool)
    dw_pool_c = dw_pool.reshape(N_GROUPS, N_DEV, 32, POOL_GROUP).transpose(1, 0, 2, 3).reshape(N_DEV, N_GROUPS * 32, POOL_GROUP).astype(BF16)
    ready = [dw_out_e.reshape(N_DEV, D_INNER // N_DEV, D_MODEL), dw_in_o,
             dw_out_o.reshape(N_DEV, D_INNER // N_DEV, D_MODEL), dw_pool_c]
    dq, dk, dv, (r_out_e, r_in_o, r_out_o, r_pool), (wt_in_e,) = _attn_bwd(
        "attn_bwd", proj0, o, do, ready, [shards[0].T])
    dproj0 = jnp.concatenate([du_pool, dq, dk, dv, dgt0], axis=1)
    dw_in_e = _wgrad_in("wgrad_in0", h0_t, dproj0)
    dh0, (r_in_e,) = _proj_in_bwd("proj_in0_bwd", dproj0, wt_in_e, [dw_in_e])
    dx0, d_shift0, d_scale0, d_ng0 = _ln_mod_bwd("ln_mod0_bwd", dh0, x0, dx1, ng[0], scale[0])
    grad_x = dx0[None]

    big = {}
    big["even_w_in"] = _sum_adamw("adamw_even_w_in", r_in_e, w_in_e, m_even_w_in[0], v_even_w_in[0])
    big["even_w_out"] = _sum_adamw("adamw_even_w_out", r_out_e, w_out_e, m_even_w_out[0], v_even_w_out[0])
    big["odd_w_in"] = _sum_adamw("adamw_odd_w_in", r_in_o, w_in_o, m_odd_w_in[0], v_odd_w_in[0])
    big["odd_w_out"] = _sum_adamw("adamw_odd_w_out", r_out_o, w_out_o, m_odd_w_out[0], v_odd_w_out[0])
    big["pool_w"] = _sum_adamw("adamw_pool_w", r_pool, w_pool, m_pool_w[0].reshape(N_GROUPS * 32, POOL_GROUP),
                               v_pool_w[0].reshape(N_GROUPS * 32, POOL_GROUP))
    big = {k: [a.reshape(shape) for a in v] for (k, v), shape in zip(
        big.items(), [even_w_in.shape, even_w_out.shape, odd_w_in.shape, odd_w_out.shape, pool_w.shape])}

    dm = jnp.concatenate([jnp.concatenate([d_shift0, d_scale0, d_gate0], axis=1),
                          jnp.concatenate([d_shift1, d_scale1, d_gate1], axis=1)], axis=0)
    slab = jnp.zeros((SLAB_ROWS, D_MODEL), F32)
    slab = slab.at[0:6].set(dm.reshape(6, D_MODEL))
    slab = slab.at[8:9].set(d_ng0).at[9:10].set(d_ng1).at[10:11].set(d_pool_scale).at[11:12].set(d_final_g)
    slab = slab.at[16:22].set(d_conv_w.reshape(6, D_MODEL)).at[24:26].set(d_conv_b.reshape(2, D_MODEL))
    gathered, total = _small_grads(slab)
    my = 4 * lax.axis_index("x") + 2 * lax.axis_index("y") + lax.axis_index("c")
    g_ada_b = total[0:6].reshape(2, 3 * D_MODEL)
    g_norm_g = total[8:10]
    g_pool_scale = total[10:11]
    g_final_g = total[11:12]
    cw_cols = conv_w.shape[2]
    g_conv_w = lax.dynamic_slice_in_dim(total[16:22].reshape(3, D_INNER), my * cw_cols, cw_cols, axis=1)
    g_conv_b = lax.dynamic_slice_in_dim(total[24:26].reshape(1, D_INNER), my * cw_cols, cw_cols, axis=1)
    dm_all = gathered[:, 0:6, :].reshape(N_DEV, 2, 3 * D_MODEL)
    dm_cols = lax.dynamic_slice_in_dim(dm_all, my * ADA_COLS, ADA_COLS, axis=2).transpose(1, 0, 2)
    ada = _ada_w_adamw("adamw_ada_w", cs_all.T, dm_cols, ada_w, m_ada_w, v_ada_w)

    small = _small_adamw("adamw_small", [
        (norm_g, g_norm_g, m_norm_g, v_norm_g),
        (ada_b, g_ada_b, m_ada_b, v_ada_b),
        (pool_scale, g_pool_scale, m_pool_scale, v_pool_scale),
        (conv_w[0], g_conv_w, m_conv_w[0], v_conv_w[0]),
        (conv_b, g_conv_b, m_conv_b, v_conv_b),
        (final_g2, g_final_g, m_final_g.reshape(1, D_MODEL), v_final_g.reshape(1, D_MODEL)),
    ])
    small = [small[3 * j:3 * j + 3] for j in range(6)]

    grads = {
        "norm_g": g_norm_g, "ada_w": ada[0], "ada_b": g_ada_b, "even_w_in": big["even_w_in"][0],
        "pool_w": big["pool_w"][0], "pool_scale": g_pool_scale, "even_w_out": big["even_w_out"][0],
        "odd_w_in": big["odd_w_in"][0], "conv_w": g_conv_w.reshape(conv_w.shape), "conv_b": g_conv_b,
        "odd_w_out": big["odd_w_out"][0], "final_g": g_final_g.reshape(D_MODEL),
    }
    rest = []
    for idx in range(3):
        rest += [
            small[0][idx], ada[1 + idx], small[1][idx], big["even_w_in"][1 + idx], big["pool_w"][1 + idx],
            small[2][idx], big["even_w_out"][1 + idx], big["odd_w_in"][1 + idx],
            small[3][idx].reshape(conv_w.shape), small[4][idx], big["odd_w_out"][1 + idx],
            small[5][idx].reshape(D_MODEL),
        ]
    order = ["norm_g", "ada_w", "ada_b", "even_w_in", "pool_w", "pool_scale", "even_w_out", "odd_w_in",
             "conv_w", "conv_b", "odd_w_out", "final_g"]
    return (loss, grad_x, *[grads[n] for n in order], *rest)
```

```python
import jax
import jax.numpy as jnp
from jax import lax
from jax.experimental import pallas as pl
from jax.experimental.pallas import tpu as pltpu

F32 = jnp.float32
BF16 = jnp.bfloat16
SDS = jax.ShapeDtypeStruct
MESH = pl.DeviceIdType.MESH

N_DEV = 8
D_MODEL = 1024
D_INNER = 2048
D_POOL = 1024
D_SB = 1024
N_GROUPS = 4
POOL_GROUP = 256
HEAD_DIM = 64
LANES = 128
D_IN_EVEN = 6144
D_IN_ODD = 8192
EPS = 1e-6
ADAM_LR = 0.001
ADAM_B1 = 0.9
ADAM_B2 = 0.999
ADAM_EPS = 1e-08
ADAM_WD = 0.01
ADAM_STEP = 10

ROW_TILE = 256
ATT_TILE = 256
HALO = 16
VMEM_LIMIT = 48 * 1024 * 1024
SLAB_ROWS = 32
WGRAD_BLOCK = 256


def _params(n_axes):
    return pltpu.CompilerParams(dimension_semantics=("arbitrary",) * n_axes, vmem_limit_bytes=VMEM_LIMIT)


def _sigmoid(x):
    return 1.0 / (1.0 + jnp.exp(-x))


def _dot(a, b):
    return jnp.dot(a, b, preferred_element_type=F32)


def _dot_nt(a, b):
    return lax.dot_general(a, b, (((1,), (1,)), ((), ())), preferred_element_type=F32)


def _dot_tn(a, b):
    return lax.dot_general(a, b, (((0,), (0,)), ((), ())), preferred_element_type=F32)


def _mm(name, a, b, *, grid, a_spec, b_spec, o_spec, o_shape, o_dtype, dot, acc_axis=None, acc_shape=None):
    n_acc = grid[acc_axis] if acc_axis is not None else 1

    def body(a_ref, b_ref, o_ref, *scratch):
        prod = dot(a_ref[...], b_ref[...])
        if acc_axis is None:
            o_ref[...] = prod.astype(o_dtype)
        else:
            acc = scratch[0]
            k = pl.program_id(acc_axis)

            @pl.when(k == 0)
            def _():
                acc[...] = prod

            @pl.when(k > 0)
            def _():
                acc[...] += prod

            @pl.when(k == n_acc - 1)
            def _():
                o_ref[...] = acc[...].astype(o_dtype)

    scratch = [] if acc_axis is None else [pltpu.VMEM(acc_shape, F32)]
    return pl.pallas_call(
        body, name=name, grid=grid, in_specs=[a_spec, b_spec], out_specs=o_spec,
        out_shape=SDS(o_shape, o_dtype), scratch_shapes=scratch, compiler_params=_params(len(grid)),
    )(a, b)


def _proj_in(name, h, wg):
    s = h.shape[0]
    cn = wg.shape[2]
    tm = min(s, ROW_TILE)

    def body(a_ref, w_ref, o_ref):
        a = a_ref[...]
        for d in range(N_DEV):
            o_ref[:, d * cn:(d + 1) * cn] = _dot(a, w_ref[d]).astype(BF16)

    return pl.pallas_call(
        body, name=name, grid=(s // tm,),
        in_specs=[pl.BlockSpec((tm, D_MODEL), lambda i: (i, 0)),
                  pl.BlockSpec((N_DEV, D_MODEL, cn), lambda i: (0, 0, 0), pipeline_mode=pl.Buffered(1))],
        out_specs=pl.BlockSpec((tm, N_DEV * cn), lambda i: (i, 0)),
        out_shape=SDS((s, N_DEV * cn), BF16), compiler_params=_params(1),
    )(h, wg)


def _proj_out(name, y, w):
    s = y.shape[0]
    tm = min(s, 512)
    return _mm(name, y, w, grid=(s // tm,),
               a_spec=pl.BlockSpec((tm, D_INNER), lambda i: (i, 0)),
               b_spec=pl.BlockSpec((D_INNER, D_MODEL), lambda i: (0, 0)),
               o_spec=pl.BlockSpec((tm, D_MODEL), lambda i: (i, 0)),
               o_shape=(s, D_MODEL), o_dtype=F32, dot=_dot)


def _proj_out_bwd(name, dyo, w):
    s = dyo.shape[0]
    tm = min(s, 512)
    return _mm(name, dyo, w, grid=(s // tm,),
               a_spec=pl.BlockSpec((tm, D_MODEL), lambda i: (i, 0)),
               b_spec=pl.BlockSpec((D_INNER, D_MODEL), lambda i: (0, 0)),
               o_spec=pl.BlockSpec((tm, D_INNER), lambda i: (i, 0)),
               o_shape=(s, D_INNER), o_dtype=F32, dot=_dot_nt)


def _wgrad_out(name, y, dyo):
    s = y.shape[0]
    ts = min(s, 512)
    return _mm(name, y, dyo, grid=(s // ts,),
               a_spec=pl.BlockSpec((ts, D_INNER), lambda k: (k, 0)),
               b_spec=pl.BlockSpec((ts, D_MODEL), lambda k: (k, 0)),
               o_spec=pl.BlockSpec((D_INNER, D_MODEL), lambda k: (0, 0)),
               o_shape=(D_INNER, D_MODEL), o_dtype=BF16, dot=_dot_tn, acc_axis=0, acc_shape=(D_INNER, D_MODEL))


def _proj_in_bwd_ln(name, parts, wt, x, dx_next, g, scale, resid=None, dws=()):
    s = x.shape[0]
    widths = [p.shape[1] for p in parts]
    offs = [sum(widths[:k]) for k in range(len(parts))]
    k_all = sum(widths)
    n_i = s // ROW_TILE
    n_p, n_r, n_side = len(parts), (2 if resid else 0), len(dws)
    w_all = wt.reshape(k_all, D_MODEL)

    def body(*refs):
        part_refs = refs[:n_p]
        w_ref, x_ref, dxn_ref, g_ref, sc_ref = refs[n_p:n_p + 5]
        resid_refs = refs[n_p + 5:n_p + 5 + n_r]
        srcs = refs[n_p + 5 + n_r:n_p + 5 + n_r + n_side]
        outs = refs[n_p + 5 + n_r + n_side:]
        dx_ref, dsh_ref, dsc_ref, dg_ref = outs[:4]
        resid_outs = outs[4:4 + n_r]
        dsts = outs[4 + n_r:4 + n_r + n_side]
        sems = outs[4 + n_r + n_side:]
        i = pl.program_id(0)
        if n_side:
            _side_exchange(srcs + dsts + sems, n_side, True, i == 0, i == n_i - 1)

        @pl.when(i == 0)
        def _():
            dsh_ref[...] = jnp.zeros_like(dsh_ref)
            dsc_ref[...] = jnp.zeros_like(dsc_ref)
            dg_ref[...] = jnp.zeros_like(dg_ref)
            if resid:
                resid_outs[1][...] = jnp.zeros_like(resid_outs[1])

        dh_v = _dot(part_refs[0][...], w_ref[offs[0]:offs[0] + widths[0], :])
        for k in range(1, n_p):
            dh_v = dh_v + _dot(part_refs[k][...], w_ref[offs[k]:offs[k] + widths[k], :])
        xv = x_ref[...]
        g_v = g_ref[...]
        r = lax.rsqrt(jnp.mean(xv * xv, axis=-1, keepdims=True) + EPS)
        xn = xv * r
        dsh_ref[...] += jnp.sum(dh_v, axis=0, keepdims=True)
        dsc_ref[...] += jnp.sum(dh_v * (xn * g_v), axis=0, keepdims=True)
        dn = dh_v * (1.0 + sc_ref[...])
        dg_ref[...] += jnp.sum(dn * xn, axis=0, keepdims=True)
        dxh = dn * g_v
        dx = dxn_ref[...] + r * (dxh - xn * jnp.mean(dxh * xn, axis=-1, keepdims=True))
        dx_ref[...] = dx
        if resid:
            yo_ref, gt_ref = resid_refs
            resid_outs[0][...] = (dx * (1.0 + gt_ref[...])).astype(BF16)
            resid_outs[1][...] += jnp.sum(dx * yo_ref[...], axis=0, keepdims=True)

    row, vec = _row_spec(), _vec_spec()
    out = pl.pallas_call(
        body, name=name, grid=(n_i,),
        in_specs=[_row_spec(w) for w in widths]
        + [pl.BlockSpec((k_all, D_MODEL), lambda i: (0, 0), pipeline_mode=pl.Buffered(1)), row, row, vec, vec]
        + ([row, vec] if resid else []) + [HBM_SPEC] * n_side,
        out_specs=[row, vec, vec, vec] + ([row, vec] if resid else []) + [HBM_SPEC] * n_side,
        out_shape=[SDS((s, D_MODEL), F32)] + [SDS((1, D_MODEL), F32)] * 3
        + ([SDS((s, D_MODEL), BF16), SDS((1, D_MODEL), F32)] if resid else [])
        + [SDS(dw.shape, dw.dtype) for dw in dws],
        scratch_shapes=_peer_sems(n_side) if n_side else [],
        compiler_params=_params(1),
    )(*parts, w_all, x, dx_next, g, scale, *(resid or ()), *dws)
    return out[:4 + n_r], out[4 + n_r:]


def _wgrad_in(name, h_t, parts, blk):
    s = h_t.shape[1]
    widths = [p.shape[1] for p in parts]
    cn = sum(widths) // N_DEV
    per_dev = cn // blk
    starts = [sum(widths[:k]) // blk for k in range(len(parts))]
    counts = [w // blk for w in widths]
    n_blk = sum(counts)

    def body(a_ref, *rest):
        o_ref = rest[len(parts)]
        b = pl.program_id(0)
        for k in range(len(parts)):
            @pl.when((b >= starts[k]) & (b < starts[k] + counts[k]))
            def _(k=k):
                o_ref[...] = _dot(a_ref[...], rest[k][...]).astype(BF16)

    def part_spec(k):
        return pl.BlockSpec((s, blk), lambda b: (0, jnp.clip(b - starts[k], 0, counts[k] - 1)))

    return pl.pallas_call(
        body, name=name, grid=(n_blk,),
        in_specs=[pl.BlockSpec((D_MODEL, s), lambda b: (0, 0), pipeline_mode=pl.Buffered(1))]
        + [part_spec(k) for k in range(len(parts))],
        out_specs=pl.BlockSpec((None, D_MODEL, blk), lambda b: (b // per_dev, 0, b % per_dev)),
        out_shape=SDS((N_DEV, D_MODEL, cn), BF16), compiler_params=_params(1),
    )(h_t, *parts)


def _vec_spec():
    return pl.BlockSpec((1, D_MODEL), lambda i: (0, 0))


def _row_spec(width=D_MODEL, col=0):
    return pl.BlockSpec((ROW_TILE, width), lambda i: (i, col))


def _col_spec():
    return pl.BlockSpec((D_MODEL, ROW_TILE), lambda i: (0, i))


def _ln_mod(name, x, g, scale, shift):
    s = x.shape[0]

    def body(x_ref, g_ref, sc_ref, sh_ref, h_ref, ht_ref):
        xv = x_ref[...]
        r = lax.rsqrt(jnp.mean(xv * xv, axis=-1, keepdims=True) + EPS)
        n = (xv * r) * g_ref[...]
        h = (n * (1.0 + sc_ref[...]) + sh_ref[...]).astype(BF16)
        h_ref[...] = h
        ht_ref[...] = h.T

    return pl.pallas_call(
        body, name=name, grid=(s // ROW_TILE,),
        in_specs=[_row_spec(), _vec_spec(), _vec_spec(), _vec_spec()], out_specs=[_row_spec(), _col_spec()],
        out_shape=[SDS((s, D_MODEL), BF16), SDS((D_MODEL, s), BF16)], compiler_params=_params(1),
    )(x, g, scale, shift)


def _resid_ln_mod(name, x, yo, gate, g, scale, shift):
    s = x.shape[0]

    def body(x_ref, yo_ref, gt_ref, g_ref, sc_ref, sh_ref, xn_ref, h_ref, ht_ref):
        xv = x_ref[...] + (1.0 + gt_ref[...]) * yo_ref[...]
        xn_ref[...] = xv
        r = lax.rsqrt(jnp.mean(xv * xv, axis=-1, keepdims=True) + EPS)
        n = (xv * r) * g_ref[...]
        h = (n * (1.0 + sc_ref[...]) + sh_ref[...]).astype(BF16)
        h_ref[...] = h
        ht_ref[...] = h.T

    return pl.pallas_call(
        body, name=name, grid=(s // ROW_TILE,),
        in_specs=[_row_spec(), _row_spec(), _vec_spec(), _vec_spec(), _vec_spec(), _vec_spec()],
        out_specs=[_row_spec(), _row_spec(), _col_spec()],
        out_shape=[SDS((s, D_MODEL), F32), SDS((s, D_MODEL), BF16), SDS((D_MODEL, s), BF16)],
        compiler_params=_params(1),
    )(x, yo, gate, g, scale, shift)


def _final_loss(name, x1, yo1, gate1, gf, target):
    s = x1.shape[0]

    def body(x_ref, yo_ref, gt_ref, gf_ref, t_ref, dx_ref, dyo_ref, loss_ref, dgf_ref, dgt_ref):
        i = pl.program_id(0)

        @pl.when(i == 0)
        def _():
            loss_ref[...] = jnp.zeros_like(loss_ref)
            dgf_ref[...] = jnp.zeros_like(dgf_ref)
            dgt_ref[...] = jnp.zeros_like(dgt_ref)

        yo = yo_ref[...]
        one_gate = 1.0 + gt_ref[...]
        x2 = x_ref[...] + one_gate * yo
        r = lax.rsqrt(jnp.mean(x2 * x2, axis=-1, keepdims=True) + EPS)
        xn = x2 * r
        gf_v = gf_ref[...]
        err = xn * gf_v - t_ref[...]
        loss_ref[...] += 0.5 * jnp.sum(jnp.mean(err * err, axis=-1, keepdims=True))
        dout = err * (1.0 / D_MODEL)
        dgf_ref[...] += jnp.sum(dout * xn, axis=0, keepdims=True)
        dxn = dout * gf_v
        dx2 = r * (dxn - xn * jnp.mean(dxn * xn, axis=-1, keepdims=True))
        dx_ref[...] = dx2
        dyo_ref[...] = (dx2 * one_gate).astype(BF16)
        dgt_ref[...] += jnp.sum(dx2 * yo, axis=0, keepdims=True)

    return pl.pallas_call(
        body, name=name, grid=(s // ROW_TILE,),
        in_specs=[_row_spec(), _row_spec(), _vec_spec(), _vec_spec(), _row_spec()],
        out_specs=[_row_spec(), _row_spec(), pl.BlockSpec((1, LANES), lambda i: (0, 0)), _vec_spec(), _vec_spec()],
        out_shape=[SDS((s, D_MODEL), F32), SDS((s, D_MODEL), BF16), SDS((1, LANES), F32),
                   SDS((1, D_MODEL), F32), SDS((1, D_MODEL), F32)],
        compiler_params=_params(1),
    )(x1, yo1, gate1, gf, target)


POOL_WINDOWS = (2, 4, 8, 16)


def _window_sum(x, window, rows, backward):
    acc, step = x, 1
    while step < window:
        acc = acc + pltpu.roll(acc, step if backward else rows - step, axis=0)
        step *= 2
    return acc


def _pool_fwd(name, proj0, wp):
    s = proj0.shape[0]
    hb = ROW_TILE // HALO
    ext_rows = ROW_TILE + HALO

    def body(u_ref, halo_ref, w_ref, p_ref, y_ref):
        i = pl.program_id(0)
        t = i * ROW_TILE + lax.broadcasted_iota(jnp.int32, (ROW_TILE, 1), 0)
        for g, window in enumerate(POOL_WINDOWS):
            cols = slice(g * POOL_GROUP, (g + 1) * POOL_GROUP)
            u = u_ref[:, cols].astype(F32)
            halo = jnp.where(i == 0, 0.0, halo_ref[:, cols].astype(F32))
            ext = jnp.concatenate([halo, u], axis=0)
            win = _window_sum(ext, window, ext_rows, True)[HALO:, :]
            cnt = jnp.minimum(t + 1, window).astype(F32)
            p = (win / cnt - u).astype(BF16)
            p_ref[:, cols] = p
            y_ref[:, cols] = _dot(p, w_ref[g])

    return pl.pallas_call(
        body, name=name, grid=(s // ROW_TILE,),
        in_specs=[pl.BlockSpec((ROW_TILE, D_POOL), lambda i: (i, 0)),
                  pl.BlockSpec((HALO, D_POOL), lambda i: (jnp.maximum(i * hb - 1, 0), 0)),
                  pl.BlockSpec((N_GROUPS, POOL_GROUP, POOL_GROUP), lambda i: (0, 0, 0))],
        out_specs=[_row_spec(D_POOL), _row_spec(D_POOL)],
        out_shape=[SDS((s, D_POOL), BF16), SDS((s, D_POOL), F32)], compiler_params=_params(1),
    )(proj0, proj0, wp)


def _pool_bwd(name, dyp, p, wp):
    s = dyp.shape[0]
    hb = ROW_TILE // HALO
    n_hb = s // HALO
    n_tiles = s // ROW_TILE
    ext_rows = ROW_TILE + HALO

    def body(dy_ref, nxt_ref, p_ref, w_ref, du_ref, dw_ref):
        i = pl.program_id(0)

        @pl.when(i == 0)
        def _():
            dw_ref[...] = jnp.zeros_like(dw_ref)

        t = i * ROW_TILE + lax.broadcasted_iota(jnp.int32, (ext_rows, 1), 0)
        for g, window in enumerate(POOL_WINDOWS):
            cols = slice(g * POOL_GROUP, (g + 1) * POOL_GROUP)
            dy = dy_ref[:, cols]
            nxt = nxt_ref[:, cols]
            nxt = jnp.where(i == n_tiles - 1, jnp.zeros_like(nxt), nxt)
            dp = _dot_nt(jnp.concatenate([dy, nxt], axis=0), w_ref[g])
            cnt = jnp.minimum(t + 1, window).astype(F32)
            win = _window_sum(dp / cnt, window, ext_rows, False)[:ROW_TILE, :]
            du_ref[:, cols] = (win - dp[:ROW_TILE, :]).astype(BF16)
            dw_ref[g] += _dot_tn(p_ref[:, cols], dy)

    return pl.pallas_call(
        body, name=name, grid=(n_tiles,),
        in_specs=[_row_spec(D_POOL),
                  pl.BlockSpec((HALO, D_POOL), lambda i: (jnp.minimum((i + 1) * hb, n_hb - 1), 0)),
                  _row_spec(D_POOL),
                  pl.BlockSpec((N_GROUPS, POOL_GROUP, POOL_GROUP), lambda i: (0, 0, 0))],
        out_specs=[_row_spec(D_POOL), pl.BlockSpec((N_GROUPS, POOL_GROUP, POOL_GROUP), lambda i: (0, 0, 0))],
        out_shape=[SDS((s, D_POOL), BF16), SDS((N_GROUPS, POOL_GROUP, POOL_GROUP), F32)],
        compiler_params=_params(1),
    )(dyp, dyp, p, wp)


FWD_HEADS_PER_STEP = 8
BWD_HEADS_PER_STEP = 4
ATT_SCALE = 0.125


def _att_groups(nh):
    lanes = nh * HEAD_DIM
    return lanes, D_SB // lanes, D_POOL // lanes, (D_POOL + D_SB) // lanes, (D_POOL + 2 * D_SB) // lanes


def _att_consts():
    r = lax.broadcasted_iota(jnp.int32, (ATT_TILE, ATT_TILE), 0)
    c = lax.broadcasted_iota(jnp.int32, (ATT_TILE, ATT_TILE), 1)
    first = lax.broadcasted_iota(jnp.int32, (1, LANES), 1) < HEAD_DIM
    return r, c, first


def _pair(x, p):
    return x[:, p * LANES:(p + 1) * LANES]


def _one_head(x, first, hh):
    zero = jnp.zeros_like(x)
    return jnp.where(first, x, zero) if hh == 0 else jnp.where(first, zero, x)


def _neg_softplus(z):
    return -(jnp.maximum(z, 0.0) + jnp.log(1.0 + jnp.exp(-jnp.abs(z))))


def _side_exchange(side_refs, n_side, by_chunk, is_first, is_last):
    ins, outs = side_refs[:n_side], side_refs[n_side:2 * n_side]
    sems = side_refs[2 * n_side:2 * n_side + 3]

    @pl.when(is_first)
    def _():
        for cp in _peer_copies(ins, outs, *sems, by_chunk=by_chunk):
            cp.start()

    @pl.when(is_last)
    def _():
        for cp in _peer_copies(ins, outs, *sems, by_chunk=by_chunk):
            cp.wait()


def _attn_fwd(name, proj0, shards):
    s = proj0.shape[0]
    nq = s // ATT_TILE
    nh = FWD_HEADS_PER_STEP
    ATT_GROUP, N_ATT_GROUPS, Q_GRP, K_GRP, V_GRP = _att_groups(nh)
    n_side = len(shards)

    def body(q_ref, k_ref, v_ref, *rest):
        o_ref = rest[n_side]
        side = rest[:n_side] + rest[n_side + 1:]
        j = pl.program_id(0)
        i = pl.program_id(1)
        _side_exchange(side, n_side, False, (j == 0) & (i == 0), (j == N_ATT_GROUPS - 1) & (i == nq - 1))
        r, c, first = _att_consts()
        tri = (r >= c).astype(BF16)
        below = c < r
        q = q_ref[...] * ATT_SCALE
        qh = [_one_head(_pair(q, h // 2), first, h % 2) for h in range(nh)]

        def tile(kb, carry, diagonal):
            k0 = pl.multiple_of(kb * ATT_TILE, ATT_TILE)
            kt = k_ref[pl.ds(k0, ATT_TILE), :]
            vt = v_ref[pl.ds(k0, ATT_TILE), :]
            z = [_dot_nt(qh[h], _pair(kt, h // 2)) for h in range(nh)]
            lf = [_neg_softplus(z[h]) for h in range(nh)]
            if diagonal:
                lf = [jnp.where(below, x, 0.0) for x in lf]
            run = [_dot(lf[h].astype(BF16), tri) for h in range(nh)]
            a = [jnp.exp(z[h] + run[h] + carry[h]) for h in range(nh)]
            if diagonal:
                a = [jnp.where(below, x, 0.0) for x in a]
            out_acc = [carry[nh + h] + _dot(a[h].astype(BF16), _pair(vt, h // 2)) for h in range(nh)]
            out_c = [carry[h] + jnp.sum(lf[h], axis=1, keepdims=True) for h in range(nh)]
            return tuple(out_c + out_acc)

        init = tuple([jnp.zeros((ATT_TILE, 1), F32)] * nh + [jnp.zeros((ATT_TILE, LANES), F32)] * nh)
        carry = tile(i, init, True)
        carry = lax.fori_loop(1, i + 1, lambda n, cr: tile(i - n, cr, False), carry)
        for p in range(nh // 2):
            o_ref[:, p * LANES:(p + 1) * LANES] = jnp.where(first, carry[nh + 2 * p], carry[nh + 2 * p + 1])

    out = pl.pallas_call(
        body, name=name, grid=(N_ATT_GROUPS, nq),
        in_specs=[pl.BlockSpec((ATT_TILE, ATT_GROUP), lambda j, i: (i, Q_GRP + j)),
                  pl.BlockSpec((s, ATT_GROUP), lambda j, i: (0, K_GRP + j)),
                  pl.BlockSpec((s, ATT_GROUP), lambda j, i: (0, V_GRP + j))] + [HBM_SPEC] * n_side,
        out_specs=[pl.BlockSpec((ATT_TILE, ATT_GROUP), lambda j, i: (i, j))] + [HBM_SPEC] * n_side,
        out_shape=[SDS((s, D_SB), F32)] + [SDS((N_DEV,) + sh.shape, sh.dtype) for sh in shards],
        scratch_shapes=_peer_sems(n_side), compiler_params=_params(2),
    )(proj0, proj0, proj0, *shards)
    return out[0], out[1:]


def _attn_bwd(name, proj0, o, do, dws, shards):
    s = proj0.shape[0]
    nq = s // ATT_TILE
    nh = BWD_HEADS_PER_STEP
    ATT_GROUP, N_ATT_GROUPS, Q_GRP, K_GRP, V_GRP = _att_groups(nh)
    n1, n2 = len(dws), len(shards)
    n_side = n1 + n2

    def body(q_ref, k_ref, v_ref, o_ref, do_ref, *rest):
        dq_ref, dk_ref, dv_ref = rest[n_side:n_side + 3]
        dk_acc, dv_acc = rest[2 * n_side + 3:2 * n_side + 5]
        srcs, dsts, sems = rest[:n_side], rest[n_side + 3:2 * n_side + 3], rest[2 * n_side + 5:]
        j = pl.program_id(0)
        i = pl.program_id(1)
        is_first, is_last = (j == 0) & (i == 0), (j == N_ATT_GROUPS - 1) & (i == nq - 1)
        _side_exchange(srcs[:n1] + dsts[:n1] + sems[:3], n1, True, is_first, is_last)
        _side_exchange(srcs[n1:] + dsts[n1:] + sems[3:], n2, False, is_first, is_last)

        @pl.when(i == 0)
        def _():
            dk_acc[...] = jnp.zeros_like(dk_acc)
            dv_acc[...] = jnp.zeros_like(dv_acc)

        r, c, first = _att_consts()
        tri = (r >= c).astype(BF16)
        tri_p = (r <= c).astype(BF16)
        below = c < r
        q = q_ref[...] * ATT_SCALE
        do_b = do_ref[...].astype(BF16)
        do_o = do_b.astype(F32) * o_ref[...]
        qh = [_one_head(_pair(q, h // 2), first, h % 2) for h in range(nh)]
        doh = [_one_head(_pair(do_b, h // 2), first, h % 2) for h in range(nh)]
        dsum = [jnp.sum(_one_head(_pair(do_o, h // 2), first, h % 2), axis=1, keepdims=True) for h in range(nh)]

        def tile(kb, carry, diagonal):
            k0 = pl.multiple_of(kb * ATT_TILE, ATT_TILE)
            kt = k_ref[pl.ds(k0, ATT_TILE), :]
            vt = v_ref[pl.ds(k0, ATT_TILE), :]
            hs = range(nh)
            z = [_dot_nt(qh[h], _pair(kt, h // 2)) for h in hs]
            d_a = [_dot_nt(doh[h], _pair(vt, h // 2)) for h in hs]
            lf = [_neg_softplus(z[h]) for h in hs]
            sig = [jnp.exp(z[h] + lf[h]) for h in hs]
            if diagonal:
                lf = [jnp.where(below, x, 0.0) for x in lf]
            run = [_dot(lf[h].astype(BF16), tri) for h in hs]
            a = [jnp.exp(z[h] + run[h] + carry[h]) for h in hs]
            if diagonal:
                a = [jnp.where(below, x, 0.0) for x in a]
            a_b = [x.astype(BF16) for x in a]
            g = [a_b[h].astype(F32) * d_a[h] for h in hs]
            g_sum = [jnp.sum(g[h], axis=1, keepdims=True) for h in hs]
            early = [dsum[h] - carry[nh + h] - g_sum[h] for h in hs]
            upto = [_dot(g[h].astype(BF16), tri_p) for h in hs]
            dv_t = [_dot_tn(a_b[2 * p], doh[2 * p]) + _dot_tn(a_b[2 * p + 1], doh[2 * p + 1]) for p in range(nh // 2)]
            dz = [g[h] - sig[h] * (early[h] + upto[h]) for h in hs]
            if diagonal:
                dz = [jnp.where(below, x, 0.0) for x in dz]
            dz = [x.astype(BF16) for x in dz]
            out_dq = [carry[2 * nh + h] + _dot(dz[h], _pair(kt, h // 2)) for h in hs]
            dk_t = [_dot_tn(dz[2 * p], qh[2 * p]) + _dot_tn(dz[2 * p + 1], qh[2 * p + 1]) for p in range(nh // 2)]
            for p in range(nh // 2):
                dk_acc[pl.ds(k0, ATT_TILE), p * LANES:(p + 1) * LANES] += dk_t[p]
                dv_acc[pl.ds(k0, ATT_TILE), p * LANES:(p + 1) * LANES] += dv_t[p]
            out_c1 = [carry[h] + jnp.sum(lf[h], axis=1, keepdims=True) for h in hs]
            out_c2 = [carry[nh + h] + g_sum[h] for h in hs]
            return tuple(out_c1 + out_c2 + out_dq)

        init = tuple([jnp.zeros((ATT_TILE, 1), F32)] * (2 * nh) + [jnp.zeros((ATT_TILE, LANES), F32)] * nh)
        carry = tile(i, init, True)
        carry = lax.fori_loop(1, i + 1, lambda n, cr: tile(i - n, cr, False), carry)
        for p in range(nh // 2):
            dq_p = jnp.where(first, carry[2 * nh + 2 * p], carry[2 * nh + 2 * p + 1]) * ATT_SCALE
            dq_ref[:, p * LANES:(p + 1) * LANES] = dq_p.astype(BF16)

        @pl.when(i == nq - 1)
        def _():
            dk_ref[...] = dk_acc[...].astype(BF16)
            dv_ref[...] = dv_acc[...].astype(BF16)

    tile_spec = pl.BlockSpec((ATT_TILE, ATT_GROUP), lambda j, i: (i, j))
    full = pl.BlockSpec((s, ATT_GROUP), lambda j, i: (0, j))
    out = pl.pallas_call(
        body, name=name, grid=(N_ATT_GROUPS, nq),
        in_specs=[pl.BlockSpec((ATT_TILE, ATT_GROUP), lambda j, i: (i, Q_GRP + j)),
                  pl.BlockSpec((s, ATT_GROUP), lambda j, i: (0, K_GRP + j)),
                  pl.BlockSpec((s, ATT_GROUP), lambda j, i: (0, V_GRP + j)),
                  tile_spec, tile_spec] + [HBM_SPEC] * n_side,
        out_specs=[tile_spec, full, full] + [HBM_SPEC] * n_side,
        out_shape=[SDS((s, D_SB), BF16)] * 3 + [SDS(dw.shape, dw.dtype) for dw in dws]
        + [SDS((N_DEV,) + sh.shape, sh.dtype) for sh in shards],
        scratch_shapes=[pltpu.VMEM((s, ATT_GROUP), F32), pltpu.VMEM((s, ATT_GROUP), F32)] + _peer_sems(n1) + _peer_sems(n2),
        compiler_params=_params(2),
    )(proj0, proj0, proj0, o, do, *dws, *shards)
    return out[0], out[1], out[2], out[3:3 + n1], out[3 + n1:]


GATE0_COL = (D_POOL + 3 * D_SB) // D_INNER


def _gate_fwd0(name, yp_raw, o, proj0, ps):
    s = o.shape[0]

    def body(yp_ref, o_ref, gt_ref, ps_ref, y_ref):
        gt = gt_ref[...].astype(F32)
        sg = gt * _sigmoid(gt)
        y_ref[:, :D_POOL] = (yp_ref[...] * ps_ref[...] * sg[:, :D_POOL]).astype(BF16)
        y_ref[:, D_POOL:] = (o_ref[...] * sg[:, D_POOL:]).astype(BF16)

    return pl.pallas_call(
        body, name=name, grid=(s // ROW_TILE,),
        in_specs=[_row_spec(), _row_spec(), _row_spec(D_INNER, GATE0_COL), _vec_spec()],
        out_specs=_row_spec(D_INNER),
        out_shape=SDS((s, D_INNER), BF16), compiler_params=_params(1),
    )(yp_raw, o, proj0, ps)


def _dsilu(x):
    sg = _sigmoid(x)
    return sg * (1.0 + x * (1.0 - sg))


def _gate_bwd0(name, dymix, yp_raw, o, proj0, ps):
    s = o.shape[0]

    def body(dy_ref, yp_ref, o_ref, gt_ref, ps_ref, dyp_ref, do_ref, dgt_ref, dps_ref):
        i = pl.program_id(0)

        @pl.when(i == 0)
        def _():
            dps_ref[...] = jnp.zeros_like(dps_ref)

        gt = gt_ref[...].astype(F32)
        dy = dy_ref[...]
        sg = gt * _sigmoid(gt)
        dsg = _dsilu(gt)
        dcat = dy * sg
        yp = yp_ref[...]
        ps_v = ps_ref[...]
        dyp_ref[...] = (dcat[:, :D_POOL] * ps_v).astype(BF16)
        do_ref[...] = dcat[:, D_POOL:]
        dps_ref[...] += jnp.sum(dcat[:, :D_POOL] * yp, axis=0, keepdims=True)
        dgt_ref[:, :D_POOL] = (dy[:, :D_POOL] * (yp * ps_v) * dsg[:, :D_POOL]).astype(BF16)
        dgt_ref[:, D_POOL:] = (dy[:, D_POOL:] * o_ref[...] * dsg[:, D_POOL:]).astype(BF16)

    return pl.pallas_call(
        body, name=name, grid=(s // ROW_TILE,),
        in_specs=[_row_spec(D_INNER), _row_spec(), _row_spec(), _row_spec(D_INNER, GATE0_COL), _vec_spec()],
        out_specs=[_row_spec(), _row_spec(), _row_spec(D_INNER), _vec_spec()],
        out_shape=[SDS((s, D_POOL), BF16), SDS((s, D_SB), F32), SDS((s, D_INNER), BF16), SDS((1, D_POOL), F32)],
        compiler_params=_params(1),
    )(dymix, yp_raw, o, proj0, ps)


CONV_HALO = 16


def _conv_fwd(name, proj1, cw, cb):
    s = proj1.shape[0]
    hb = ROW_TILE // CONV_HALO
    ext_rows = ROW_TILE + CONV_HALO

    def body(gb_ref, gc_ref, u_ref, gt_ref, gch_ref, uh_ref, cw_ref, cb_ref, y_ref):
        i = pl.program_id(0)
        uc = gc_ref[...].astype(F32) * u_ref[...].astype(F32)
        halo = jnp.where(i == 0, 0.0, gch_ref[...].astype(F32) * uh_ref[...].astype(F32))
        ext = jnp.concatenate([halo, uc], axis=0)
        uc1 = pltpu.roll(ext, 1, axis=0)[CONV_HALO:, :]
        uc2 = pltpu.roll(ext, 2, axis=0)[CONV_HALO:, :]
        cw_v = cw_ref[...]
        conv = cb_ref[...] + cw_v[0:1, :] * uc2 + cw_v[1:2, :] * uc1 + cw_v[2:3, :] * uc
        gt = gt_ref[...].astype(F32)
        y_ref[...] = (gb_ref[...].astype(F32) * conv * (gt * _sigmoid(gt))).astype(BF16)

    def tile(part):
        return pl.BlockSpec((ROW_TILE, D_INNER), lambda i: (i, part))

    def halo(part):
        return pl.BlockSpec((CONV_HALO, D_INNER), lambda i: (jnp.maximum(i * hb - 1, 0), part))

    return pl.pallas_call(
        body, name=name, grid=(s // ROW_TILE,),
        in_specs=[tile(0), tile(1), tile(2), tile(3), halo(1), halo(2),
                  pl.BlockSpec((3, D_INNER), lambda i: (0, 0)), pl.BlockSpec((1, D_INNER), lambda i: (0, 0))],
        out_specs=pl.BlockSpec((ROW_TILE, D_INNER), lambda i: (i, 0)),
        out_shape=SDS((s, D_INNER), BF16), compiler_params=_params(1),
    )(proj1, proj1, proj1, proj1, proj1, proj1, cw, cb)


def _conv_bwd(name, dymix, proj1, cw, cb):
    s = proj1.shape[0]
    hb = ROW_TILE // CONV_HALO
    n_hb = s // CONV_HALO
    n_tiles = s // ROW_TILE
    ext_rows = ROW_TILE + CONV_HALO

    def body(dy_ref, gb_ref, gc_ref, u_ref, gt_ref, gch_ref, uh_ref, dyn_ref, gbn_ref, gtn_ref, cw_ref, cb_ref,
             dproj_ref, dcw_ref, dcb_ref):
        i = pl.program_id(0)

        @pl.when(i == 0)
        def _():
            dcw_ref[...] = jnp.zeros_like(dcw_ref)
            dcb_ref[...] = jnp.zeros_like(dcb_ref)

        gc = gc_ref[...].astype(F32)
        u = u_ref[...].astype(F32)
        gb = gb_ref[...].astype(F32)
        gt = gt_ref[...].astype(F32)
        dy = dy_ref[...]
        uc = gc * u
        halo = jnp.where(i == 0, 0.0, gch_ref[...].astype(F32) * uh_ref[...].astype(F32))
        ext = jnp.concatenate([halo, uc], axis=0)
        uc1 = pltpu.roll(ext, 1, axis=0)[CONV_HALO:, :]
        uc2 = pltpu.roll(ext, 2, axis=0)[CONV_HALO:, :]
        cw_v = cw_ref[...]
        w0, w1, w2 = cw_v[0:1, :], cw_v[1:2, :], cw_v[2:3, :]
        conv = cb_ref[...] + w0 * uc2 + w1 * uc1 + w2 * uc
        sig = _sigmoid(gt)
        sg = gt * sig
        dconv = dy * gb * sg
        gtn = gtn_ref[...].astype(F32)
        dconv_next = jnp.where(i == n_tiles - 1, 0.0, dyn_ref[...] * gbn_ref[...].astype(F32) * (gtn * _sigmoid(gtn)))
        dext = jnp.concatenate([dconv, dconv_next], axis=0)
        dconv_p1 = pltpu.roll(dext, ext_rows - 1, axis=0)[:ROW_TILE, :]
        dconv_p2 = pltpu.roll(dext, ext_rows - 2, axis=0)[:ROW_TILE, :]
        duc = w2 * dconv + w1 * dconv_p1 + w0 * dconv_p2
        dproj_ref[:, 0:D_INNER] = (dy * conv * sg).astype(BF16)
        dproj_ref[:, D_INNER:2 * D_INNER] = (duc * u).astype(BF16)
        dproj_ref[:, 2 * D_INNER:3 * D_INNER] = (duc * gc).astype(BF16)
        dproj_ref[:, 3 * D_INNER:] = (dy * gb * conv * (sig + sg * (1.0 - sig))).astype(BF16)
        dcw_ref[0:1, :] += jnp.sum(dconv * uc2, axis=0, keepdims=True)
        dcw_ref[1:2, :] += jnp.sum(dconv * uc1, axis=0, keepdims=True)
        dcw_ref[2:3, :] += jnp.sum(dconv * uc, axis=0, keepdims=True)
        dcb_ref[...] += jnp.sum(dconv, axis=0, keepdims=True)

    def tile(part):
        return pl.BlockSpec((ROW_TILE, D_INNER), lambda i: (i, part))

    def prev(part):
        return pl.BlockSpec((CONV_HALO, D_INNER), lambda i: (jnp.maximum(i * hb - 1, 0), part))

    def nxt(part):
        return pl.BlockSpec((CONV_HALO, D_INNER), lambda i: (jnp.minimum((i + 1) * hb, n_hb - 1), part))

    whole = lambda rows: pl.BlockSpec((rows, D_INNER), lambda i: (0, 0))
    return pl.pallas_call(
        body, name=name, grid=(n_tiles,),
        in_specs=[tile(0), tile(0), tile(1), tile(2), tile(3), prev(1), prev(2), nxt(0), nxt(0), nxt(3),
                  whole(3), whole(1)],
        out_specs=[pl.BlockSpec((ROW_TILE, 4 * D_INNER), lambda i: (i, 0)), whole(3), whole(1)],
        out_shape=[SDS((s, 4 * D_INNER), BF16), SDS((3, D_INNER), F32), SDS((1, D_INNER), F32)],
        compiler_params=_params(1),
    )(dymix, proj1, proj1, proj1, proj1, proj1, proj1, dymix, proj1, proj1, cw, cb)


def _place():
    x, y, c = lax.axis_index("x"), lax.axis_index("y"), lax.axis_index("c")
    return x, y, c


def _flip(x, y, c, k):
    fx, fy, fc = (k >> 2) & 1, (k >> 1) & 1, k & 1
    return (1 - x if fx else x, 1 - y if fy else y, 1 - c if fc else c)


def _dev_index(p):
    return 4 * p[0] + 2 * p[1] + p[2]


HBM_SPEC = pl.BlockSpec(memory_space=pltpu.HBM)
VMEM_SPEC = pl.BlockSpec(memory_space=pltpu.VMEM)


def _two_level_gather(src, dst, send_sems, recv_sems, local_sem):
    x, y, c = _place()
    me, sibling = (x, y, c), (x, y, 1 - c)
    chips = [(1 - x, y), (x, 1 - y), (1 - x, 1 - y)]

    def copy(k, block, to, from_src=False):
        rows = dst.at[_dev_index(block)]
        return pltpu.make_async_remote_copy(
            src_ref=src if from_src else rows, dst_ref=rows, send_sem=send_sems.at[k], recv_sem=recv_sems.at[k],
            device_id=to, device_id_type=MESH)

    mine = pltpu.make_async_copy(src, dst.at[_dev_index(me)], local_sem)
    first = [copy(0, me, sibling, True)] + [copy(1 + j, me, (*chip, c), True) for j, chip in enumerate(chips)]
    passed = [copy(4 + j, (*chip, c), sibling) for j, chip in enumerate(chips)]

    def start():
        mine.start()
        for cp in first:
            cp.start()

    def pass_on():
        for j, chip in enumerate(chips):
            copy(1 + j, (*chip, c), me).wait_recv()
            passed[j].start()

    def finish():
        copy(0, sibling, me).wait_recv()
        for j, chip in enumerate(chips):
            copy(4 + j, (*chip, 1 - c), me).wait_recv()
        for cp in first + passed:
            cp.wait_send()
        mine.wait()

    return start, pass_on, finish


def _peer_copies(ins, outs, send_sems, recv_sems, local_sems, by_chunk):
    x, y, c = _place()
    my = _dev_index((x, y, c))
    copies = []
    for w in range(len(ins)):
        copies.append(pltpu.make_async_copy(ins[w].at[my] if by_chunk else ins[w], outs[w].at[my], local_sems.at[w]))
        for k in range(1, N_DEV):
            peer = _flip(x, y, c, k)
            copies.append(pltpu.make_async_remote_copy(
                src_ref=ins[w].at[_dev_index(peer)] if by_chunk else ins[w], dst_ref=outs[w].at[my],
                send_sem=send_sems.at[7 * w + k - 1], recv_sem=recv_sems.at[7 * w + k - 1],
                device_id=peer, device_id_type=MESH))
    return copies


def _peer_sems(n_w):
    return [pltpu.SemaphoreType.DMA((7 * n_w,)), pltpu.SemaphoreType.DMA((7 * n_w,)), pltpu.SemaphoreType.DMA((n_w,))]


ADA_COLS = 3 * D_MODEL // N_DEV


def _ada_forward(c_row, conv_w, conv_b, ada_w, ada_b, w_shard):
    cw_cols = conv_w.shape[1]

    def body(c_ref, cw_ref, cb_ref, aw_ref, ab_ref, w_ref, m_ref, cs_ref, cwf_ref, cbf_ref, wg_ref,
             slab, gath, part, land, send_sems, recv_sems, w_send, w_recv, w_local):
        x, y, c = _place()
        my = _dev_index((x, y, c))
        w_start, w_pass_on, w_finish = _two_level_gather(w_ref, wg_ref, w_send, w_recv, w_local)
        w_start()
        slab[...] = jnp.zeros_like(slab)
        slab[0:1, :] = c_ref[...]
        slab[1:4, 0:cw_cols] = cw_ref[...]
        slab[4:5, 0:cw_cols] = cb_ref[...]
        gath[my] = slab[...]
        sends = []
        for k in range(1, N_DEV):
            peer = _flip(x, y, c, k)
            cp = pltpu.make_async_remote_copy(
                src_ref=slab, dst_ref=gath.at[my], send_sem=send_sems.at[k - 1], recv_sem=recv_sems.at[k - 1],
                device_id=peer, device_id_type=MESH)
            cp.start()
            sends.append(cp)
        for cp in sends:
            cp.wait()
        for d in range(N_DEV):
            c_d = gath[d, 0:1, :]
            cs_ref[d:d + 1, :] = c_d * _sigmoid(c_d)
            cwf_ref[:, d * cw_cols:(d + 1) * cw_cols] = gath[d, 1:4, 0:cw_cols]
            cbf_ref[:, d * cw_cols:(d + 1) * cw_cols] = gath[d, 4:5, 0:cw_cols]
        cs = cs_ref[...]
        part[...] = jnp.zeros_like(part)
        for layer in range(2):
            m_part = jnp.dot(cs, aw_ref[layer], preferred_element_type=F32, precision=lax.Precision.HIGHEST)
            for d in range(N_DEV):
                part[d, layer:layer + 1, :] = m_part[d:d + 1, :]
        land[my] = part[my]
        sends = []
        for k in range(1, N_DEV):
            peer = _flip(x, y, c, k)
            cp = pltpu.make_async_remote_copy(
                src_ref=part.at[_dev_index(peer)], dst_ref=land.at[my],
                send_sem=send_sems.at[6 + k], recv_sem=recv_sems.at[6 + k],
                device_id=peer, device_id_type=MESH)
            cp.start()
            sends.append(cp)
        for cp in sends:
            cp.wait()
        for d in range(N_DEV):
            cols = slice(d * ADA_COLS, (d + 1) * ADA_COLS)
            m_ref[:, cols] = land[d, 0:2, :] + ab_ref[:, cols]
        w_pass_on()
        w_finish()

    return pl.pallas_call(
        body, name="ada_forward",
        out_shape=[SDS((2, 3 * D_MODEL), F32), SDS((N_DEV, D_MODEL), F32), SDS((3, N_DEV * cw_cols), F32),
                   SDS((1, N_DEV * cw_cols), F32), SDS((N_DEV,) + w_shard.shape, w_shard.dtype)],
        in_specs=[VMEM_SPEC] * 5 + [HBM_SPEC], out_specs=[VMEM_SPEC] * 4 + [HBM_SPEC],
        scratch_shapes=[pltpu.VMEM((8, D_MODEL), F32), pltpu.VMEM((N_DEV, 8, D_MODEL), F32),
                        pltpu.VMEM((N_DEV, 8, ADA_COLS), F32), pltpu.VMEM((N_DEV, 8, ADA_COLS), F32),
                        pltpu.SemaphoreType.DMA((14,)), pltpu.SemaphoreType.DMA((14,)),
                        pltpu.SemaphoreType.DMA((7,)), pltpu.SemaphoreType.DMA((7,)), pltpu.SemaphoreType.DMA],
        compiler_params=pltpu.CompilerParams(vmem_limit_bytes=VMEM_LIMIT),
    )(c_row, conv_w, conv_b, ada_w, ada_b, w_shard)


def _small_grads(slab):
    def body(slab_ref, gath_ref, tot_ref, send_sems, recv_sems):
        x, y, c = _place()
        my = _dev_index((x, y, c))
        gath_ref[my] = slab_ref[...]
        sends = []
        for k in range(1, N_DEV):
            peer = _flip(x, y, c, k)
            cp = pltpu.make_async_remote_copy(
                src_ref=slab_ref, dst_ref=gath_ref.at[my], send_sem=send_sems.at[k - 1], recv_sem=recv_sems.at[k - 1],
                device_id=peer, device_id_type=MESH)
            cp.start()
            sends.append(cp)
        for cp in sends:
            cp.wait()
        tot = gath_ref[0]
        for d in range(1, N_DEV):
            tot = tot + gath_ref[d]
        tot_ref[...] = tot

    return pl.pallas_call(
        body, name="small_grads",
        out_shape=[SDS((N_DEV, SLAB_ROWS, D_MODEL), F32), SDS((SLAB_ROWS, D_MODEL), F32)],
        in_specs=[VMEM_SPEC], out_specs=[VMEM_SPEC] * 2,
        scratch_shapes=[pltpu.SemaphoreType.DMA((7,)), pltpu.SemaphoreType.DMA((7,))],
    )(slab)


def _adamw_math(w, g, m, v):
    m = ADAM_B1 * m + (1.0 - ADAM_B1) * g
    v = ADAM_B2 * v + (1.0 - ADAM_B2) * jnp.square(g)
    m_hat = m / (1.0 - ADAM_B1 ** ADAM_STEP)
    v_hat = v / (1.0 - ADAM_B2 ** ADAM_STEP)
    delta = -ADAM_LR * (m_hat / (jnp.sqrt(v_hat) + ADAM_EPS) + ADAM_WD * w)
    return delta, m, v


def _sum_adamw(name, recv, w, m, v):
    rows, cols = w.shape
    tr = min(rows, 256)

    def body(r_ref, w_ref, m_ref, v_ref, g_ref, d_ref, nm_ref, nv_ref):
        g = r_ref[0].astype(F32)
        for d in range(1, N_DEV):
            g = g + r_ref[d].astype(F32)
        g_ref[...] = g
        d_ref[...], nm_ref[...], nv_ref[...] = _adamw_math(w_ref[...], g, m_ref[...], v_ref[...])

    blk = pl.BlockSpec((tr, cols), lambda i: (i, 0))
    return pl.pallas_call(
        body, name=name, grid=(rows // tr,),
        in_specs=[pl.BlockSpec((N_DEV, tr, cols), lambda i: (0, i, 0)), blk, blk, blk],
        out_specs=[blk] * 4, out_shape=[SDS((rows, cols), F32)] * 4, compiler_params=_params(1),
    )(recv, w, m, v)


def _ada_w_adamw(name, cs_t, dm_cols, w, m, v):
    def body(cs_ref, dm_ref, w_ref, m_ref, v_ref, g_ref, d_ref, nm_ref, nv_ref):
        cs = cs_ref[...]
        dm = dm_ref[...]
        g = cs[:, 0:1] * dm[0:1, :]
        for b in range(1, N_DEV):
            g = g + cs[:, b:b + 1] * dm[b:b + 1, :]
        g_ref[...] = g
        d_ref[...], nm_ref[...], nv_ref[...] = _adamw_math(w_ref[...], g, m_ref[...], v_ref[...])

    blk = pl.BlockSpec((None, D_MODEL, ADA_COLS), lambda l: (l, 0, 0))
    return pl.pallas_call(
        body, name=name, grid=(2,),
        in_specs=[pl.BlockSpec((D_MODEL, N_DEV), lambda l: (0, 0)),
                  pl.BlockSpec((None, N_DEV, ADA_COLS), lambda l: (l, 0, 0)), blk, blk, blk],
        out_specs=[blk] * 4, out_shape=[SDS((2, D_MODEL, ADA_COLS), F32)] * 4, compiler_params=_params(1),
    )(cs_t, dm_cols, w, m, v)


def _small_adamw(name, triples):
    n = len(triples)

    def body(*refs):
        ins, outs = refs[:4 * n], refs[4 * n:]
        for j in range(n):
            w_ref, g_ref, m_ref, v_ref = ins[4 * j:4 * j + 4]
            d, nm, nv = _adamw_math(w_ref[...], g_ref[...], m_ref[...], v_ref[...])
            outs[3 * j][...] = d
            outs[3 * j + 1][...] = nm
            outs[3 * j + 2][...] = nv

    flat = [a for t in triples for a in t]
    return pl.pallas_call(
        body, name=name,
        out_shape=[SDS(t[0].shape, F32) for t in triples for _ in range(3)],
        in_specs=[VMEM_SPEC] * (4 * n), out_specs=[VMEM_SPEC] * (3 * n),
    )(*flat)


def kernel(x, c, norm_g, ada_w, ada_b, even_w_in, pool_w, pool_scale, even_w_out, odd_w_in, conv_w, conv_b, odd_w_out, final_g, loss_target, m_norm_g, m_ada_w, m_ada_b, m_even_w_in, m_pool_w, m_pool_scale, m_even_w_out, m_odd_w_in, m_conv_w, m_conv_b, m_odd_w_out, m_final_g, v_norm_g, v_ada_w, v_ada_b, v_even_w_in, v_pool_w, v_pool_scale, v_even_w_out, v_odd_w_in, v_conv_w, v_conv_b, v_odd_w_out, v_final_g):
    seq = x.shape[1]
    x0 = x[0]
    target = loss_target[0]
    final_g2 = final_g.reshape(1, D_MODEL)

    w_in_e = even_w_in[0]
    w_out_e = even_w_out[0]
    w_in_o = odd_w_in[0]
    w_out_o = odd_w_out[0]
    w_pool = pool_w[0].reshape(N_GROUPS * 32, POOL_GROUP)
    shards = [w.astype(BF16) for w in (w_in_e, w_out_e, w_in_o, w_out_o, w_pool)]

    m_vec, cs_all, conv_w_full, conv_b_full, wg_in_e = _ada_forward(c, conv_w[0], conv_b, ada_w, ada_b, shards[0])
    shift = [m_vec[l:l + 1, 0:D_MODEL] for l in range(2)]
    scale = [m_vec[l:l + 1, D_MODEL:2 * D_MODEL] for l in range(2)]
    gate = [m_vec[l:l + 1, 2 * D_MODEL:] for l in range(2)]
    ng = [norm_g[l:l + 1] for l in range(2)]

    h0, h0_t = _ln_mod("ln_mod0", x0, ng[0], scale[0], shift[0])
    proj0 = _proj_in("proj_in0", h0, wg_in_e)
    o, (wg_out_e, wg_in_o, wg_out_o, wg_pool, wt_in_o) = _attn_fwd("attn_fwd", proj0, shards[1:] + [shards[2].T])
    wf_out_e = wg_out_e.reshape(D_INNER, D_MODEL)
    wf_out_o = wg_out_o.reshape(D_INNER, D_MODEL)
    wf_pool = wg_pool.reshape(N_DEV, N_GROUPS, 32, POOL_GROUP).transpose(1, 0, 2, 3).reshape(N_GROUPS, POOL_GROUP, POOL_GROUP)
    p, yp_raw = _pool_fwd("pool_fwd", proj0, wf_pool)
    ymix0 = _gate_fwd0("gate_fwd0", yp_raw, o, proj0, pool_scale)
    yo0 = _proj_out("proj_out0", ymix0, wf_out_e)

    x1, h1, h1_t = _resid_ln_mod("resid_ln_mod1", x0, yo0, gate[0], ng[1], scale[1], shift[1])
    proj1 = _proj_in("proj_in1", h1, wg_in_o)
    ymix1 = _conv_fwd("conv_fwd", proj1, conv_w_full, conv_b_full)
    yo1 = _proj_out("proj_out1", ymix1, wf_out_o)

    dx2, dyo1, loss_acc, d_final_g, d_gate1 = _final_loss("final_loss", x1, yo1, gate[1], final_g2, target)
    loss = lax.psum(loss_acc[0, 0], ("x", "y", "c"))

    dymix1 = _proj_out_bwd("proj_out1_bwd", dyo1, wf_out_o)
    dw_out_o = _wgrad_out("wgrad_out1", ymix1, dyo1)
    dproj1, d_conv_w, d_conv_b = _conv_bwd("conv_bwd", dymix1, proj1, conv_w_full, conv_b_full)
    dw_in_o = _wgrad_in("wgrad_in1", h1_t, [dproj1], D_IN_ODD // N_DEV)
    (dx1, d_shift1, d_scale1, d_ng1, dyo0, d_gate0), _ = _proj_in_bwd_ln(
        "proj_in1_bwd", [dproj1], wt_in_o, x1, dx2, ng[1], scale[1], resid=(yo0, gate[0]))

    dymix0 = _proj_out_bwd("proj_out0_bwd", dyo0, wf_out_e)
    dw_out_e = _wgrad_out("wgrad_out0", ymix0, dyo0)
    dyp, do, dgt0, d_pool_scale = _gate_bwd0("gate_bwd0", dymix0, yp_raw, o, proj0, pool_scale)
    du_pool, dw_pool = _pool_bwd("pool_bwd", dyp, p, wf_pool)
    dw_pool_c = dw_pool.reshape(N_GROUPS, N_DEV, 32, POOL_GROUP).transpose(1, 0, 2, 3).reshape(N_DEV, N_GROUPS * 32, POOL_GROUP).astype(BF16)
    ready = [dw_out_e.reshape(N_DEV, D_INNER // N_DEV, D_MODEL), dw_in_o,
             dw_out_o.reshape(N_DEV, D_INNER // N_DEV, D_MODEL), dw_pool_c]
    dq, dk, dv, (r_out_e, r_in_o, r_out_o, r_pool), (wt_in_e,) = _attn_bwd(
        "attn_bwd", proj0, o, do, ready, [shards[0].T])
    dparts0 = [du_pool, dq, dk, dv, dgt0]
    dw_in_e = _wgrad_in("wgrad_in0", h0_t, dparts0, WGRAD_BLOCK)
    (dx0, d_shift0, d_scale0, d_ng0), (r_in_e,) = _proj_in_bwd_ln(
        "proj_in0_bwd", dparts0, wt_in_e, x0, dx1, ng[0], scale[0], dws=[dw_in_e])
    grad_x = dx0[None]

    big = {}
    big["even_w_in"] = _sum_adamw("adamw_even_w_in", r_in_e, w_in_e, m_even_w_in[0], v_even_w_in[0])
    big["even_w_out"] = _sum_adamw("adamw_even_w_out", r_out_e, w_out_e, m_even_w_out[0], v_even_w_out[0])
    big["odd_w_in"] = _sum_adamw("adamw_odd_w_in", r_in_o, w_in_o, m_odd_w_in[0], v_odd_w_in[0])
    big["odd_w_out"] = _sum_adamw("adamw_odd_w_out", r_out_o, w_out_o, m_odd_w_out[0], v_odd_w_out[0])
    big["pool_w"] = _sum_adamw("adamw_pool_w", r_pool, w_pool, m_pool_w[0].reshape(N_GROUPS * 32, POOL_GROUP),
                               v_pool_w[0].reshape(N_GROUPS * 32, POOL_GROUP))
    big = {k: [a.reshape(shape) for a in v] for (k, v), shape in zip(
        big.items(), [even_w_in.shape, even_w_out.shape, odd_w_in.shape, odd_w_out.shape, pool_w.shape])}

    dm = jnp.concatenate([jnp.concatenate([d_shift0, d_scale0, d_gate0], axis=1),
                          jnp.concatenate([d_shift1, d_scale1, d_gate1], axis=1)], axis=0)
    slab = jnp.zeros((SLAB_ROWS, D_MODEL), F32)
    slab = slab.at[0:6].set(dm.reshape(6, D_MODEL))
    slab = slab.at[8:9].set(d_ng0).at[9:10].set(d_ng1).at[10:11].set(d_pool_scale).at[11:12].set(d_final_g)
    slab = slab.at[16:22].set(d_conv_w.reshape(6, D_MODEL)).at[24:26].set(d_conv_b.reshape(2, D_MODEL))
    gathered, total = _small_grads(slab)
    my = 4 * lax.axis_index("x") + 2 * lax.axis_index("y") + lax.axis_index("c")
    g_ada_b = total[0:6].reshape(2, 3 * D_MODEL)
    g_norm_g = total[8:10]
    g_pool_scale = total[10:11]
    g_final_g = total[11:12]
    cw_cols = conv_w.shape[2]
    g_conv_w = lax.dynamic_slice_in_dim(total[16:22].reshape(3, D_INNER), my * cw_cols, cw_cols, axis=1)
    g_conv_b = lax.dynamic_slice_in_dim(total[24:26].reshape(1, D_INNER), my * cw_cols, cw_cols, axis=1)
    dm_all = gathered[:, 0:6, :].reshape(N_DEV, 2, 3 * D_MODEL)
    dm_cols = lax.dynamic_slice_in_dim(dm_all, my * ADA_COLS, ADA_COLS, axis=2).transpose(1, 0, 2)
    ada = _ada_w_adamw("adamw_ada_w", cs_all.T, dm_cols, ada_w, m_ada_w, v_ada_w)

    small = _small_adamw("adamw_small", [
        (norm_g, g_norm_g, m_norm_g, v_norm_g),
        (ada_b, g_ada_b, m_ada_b, v_ada_b),
        (pool_scale, g_pool_scale, m_pool_scale, v_pool_scale),
        (conv_w[0], g_conv_w, m_conv_w[0], v_conv_w[0]),
        (conv_b, g_conv_b, m_conv_b, v_conv_b),
        (final_g2, g_final_g, m_final_g.reshape(1, D_MODEL), v_final_g.reshape(1, D_MODEL)),
    ])
    small = [small[3 * j:3 * j + 3] for j in range(6)]

    grads = {
        "norm_g": g_norm_g, "ada_w": ada[0], "ada_b": g_ada_b, "even_w_in": big["even_w_in"][0],
        "pool_w": big["pool_w"][0], "pool_scale": g_pool_scale, "even_w_out": big["even_w_out"][0],
        "odd_w_in": big["odd_w_in"][0], "conv_w": g_conv_w.reshape(conv_w.shape), "conv_b": g_conv_b,
        "odd_w_out": big["odd_w_out"][0], "final_g": g_final_g.reshape(D_MODEL),
    }
    rest = []
    for idx in range(3):
        rest += [
            small[0][idx], ada[1 + idx], small[1][idx], big["even_w_in"][1 + idx], big["pool_w"][1 + idx],
            small[2][idx], big["even_w_out"][1 + idx], big["odd_w_in"][1 + idx],
            small[3][idx].reshape(conv_w.shape), small[4][idx], big["odd_w_out"][1 + idx],
            small[5][idx].reshape(D_MODEL),
        ]
    order = ["norm_g", "ada_w", "ada_b", "even_w_in", "pool_w", "pool_scale", "even_w_out", "odd_w_in",
             "conv_w", "conv_b", "odd_w_out", "final_g"]
    return (loss, grad_x, *[grads[n] for n in order], *rest)
```

```python
import jax
import jax.numpy as jnp
from jax import lax
from jax.experimental import pallas as pl
from jax.experimental.pallas import tpu as pltpu

F32 = jnp.float32
BF16 = jnp.bfloat16
SDS = jax.ShapeDtypeStruct
MESH = pl.DeviceIdType.MESH

N_DEV = 8
D_MODEL = 1024
D_INNER = 2048
D_POOL = 1024
D_SB = 1024
N_GROUPS = 4
POOL_GROUP = 256
HEAD_DIM = 64
LANES = 128
D_IN_EVEN = 6144
D_IN_ODD = 8192
EPS = 1e-6
ADAM_LR = 0.001
ADAM_B1 = 0.9
ADAM_B2 = 0.999
ADAM_EPS = 1e-08
ADAM_WD = 0.01
ADAM_STEP = 10

ROW_TILE = 256
ATT_TILE = 256
HALO = 16
VMEM_LIMIT = 48 * 1024 * 1024
SLAB_ROWS = 32
WGRAD_BLOCK = 256


def _params(n_axes):
    return pltpu.CompilerParams(dimension_semantics=("arbitrary",) * n_axes, vmem_limit_bytes=VMEM_LIMIT)


def _sigmoid(x):
    return 1.0 / (1.0 + jnp.exp(-x))


def _dot(a, b):
    return jnp.dot(a, b, preferred_element_type=F32)


def _dot_nt(a, b):
    return lax.dot_general(a, b, (((1,), (1,)), ((), ())), preferred_element_type=F32)


def _dot_tn(a, b):
    return lax.dot_general(a, b, (((0,), (0,)), ((), ())), preferred_element_type=F32)


def _mm(name, a, b, *, grid, a_spec, b_spec, o_spec, o_shape, o_dtype, dot, acc_axis=None, acc_shape=None):
    n_acc = grid[acc_axis] if acc_axis is not None else 1

    def body(a_ref, b_ref, o_ref, *scratch):
        prod = dot(a_ref[...], b_ref[...])
        if acc_axis is None:
            o_ref[...] = prod.astype(o_dtype)
        else:
            acc = scratch[0]
            k = pl.program_id(acc_axis)

            @pl.when(k == 0)
            def _():
                acc[...] = prod

            @pl.when(k > 0)
            def _():
                acc[...] += prod

            @pl.when(k == n_acc - 1)
            def _():
                o_ref[...] = acc[...].astype(o_dtype)

    scratch = [] if acc_axis is None else [pltpu.VMEM(acc_shape, F32)]
    return pl.pallas_call(
        body, name=name, grid=grid, in_specs=[a_spec, b_spec], out_specs=o_spec,
        out_shape=SDS(o_shape, o_dtype), scratch_shapes=scratch, compiler_params=_params(len(grid)),
    )(a, b)


def _proj_in(name, h, wg):
    s = h.shape[0]
    cn = wg.shape[2]
    tm = min(s, ROW_TILE)

    def body(a_ref, w_ref, o_ref):
        a = a_ref[...]
        for d in range(N_DEV):
            o_ref[:, d * cn:(d + 1) * cn] = _dot(a, w_ref[d]).astype(BF16)

    return pl.pallas_call(
        body, name=name, grid=(s // tm,),
        in_specs=[pl.BlockSpec((tm, D_MODEL), lambda i: (i, 0)),
                  pl.BlockSpec((N_DEV, D_MODEL, cn), lambda i: (0, 0, 0), pipeline_mode=pl.Buffered(1))],
        out_specs=pl.BlockSpec((tm, N_DEV * cn), lambda i: (i, 0)),
        out_shape=SDS((s, N_DEV * cn), BF16), compiler_params=_params(1),
    )(h, wg)


def _proj_out(name, y, w):
    s = y.shape[0]
    tm = min(s, 512)
    return _mm(name, y, w, grid=(s // tm,),
               a_spec=pl.BlockSpec((tm, D_INNER), lambda i: (i, 0)),
               b_spec=pl.BlockSpec((D_INNER, D_MODEL), lambda i: (0, 0)),
               o_spec=pl.BlockSpec((tm, D_MODEL), lambda i: (i, 0)),
               o_shape=(s, D_MODEL), o_dtype=F32, dot=_dot)


def _proj_out_bwd(name, dyo, w):
    s = dyo.shape[0]
    tm = min(s, 512)
    return _mm(name, dyo, w, grid=(s // tm,),
               a_spec=pl.BlockSpec((tm, D_MODEL), lambda i: (i, 0)),
               b_spec=pl.BlockSpec((D_INNER, D_MODEL), lambda i: (0, 0)),
               o_spec=pl.BlockSpec((tm, D_INNER), lambda i: (i, 0)),
               o_shape=(s, D_INNER), o_dtype=F32, dot=_dot_nt)


def _wgrad_out(name, y, dyo):
    s = y.shape[0]
    ts = min(s, 512)
    return _mm(name, y, dyo, grid=(s // ts,),
               a_spec=pl.BlockSpec((ts, D_INNER), lambda k: (k, 0)),
               b_spec=pl.BlockSpec((ts, D_MODEL), lambda k: (k, 0)),
               o_spec=pl.BlockSpec((D_INNER, D_MODEL), lambda k: (0, 0)),
               o_shape=(D_INNER, D_MODEL), o_dtype=BF16, dot=_dot_tn, acc_axis=0, acc_shape=(D_INNER, D_MODEL))


def _proj_in_bwd_ln(name, parts, wt, x, dx_next, g, scale, resid=None, dws=()):
    s = x.shape[0]
    widths = [p.shape[1] for p in parts]
    offs = [sum(widths[:k]) for k in range(len(parts))]
    k_all = sum(widths)
    n_i = s // ROW_TILE
    n_p, n_r, n_side = len(parts), (2 if resid else 0), len(dws)
    w_all = wt.reshape(k_all, D_MODEL)

    def body(*refs):
        part_refs = refs[:n_p]
        w_ref, x_ref, dxn_ref, g_ref, sc_ref = refs[n_p:n_p + 5]
        resid_refs = refs[n_p + 5:n_p + 5 + n_r]
        srcs = refs[n_p + 5 + n_r:n_p + 5 + n_r + n_side]
        outs = refs[n_p + 5 + n_r + n_side:]
        dx_ref, dsh_ref, dsc_ref, dg_ref = outs[:4]
        resid_outs = outs[4:4 + n_r]
        dsts = outs[4 + n_r:4 + n_r + n_side]
        sems = outs[4 + n_r + n_side:]
        i = pl.program_id(0)
        if n_side:
            _side_exchange(srcs + dsts + sems, n_side, True, i == 0, i == n_i - 1, chips=True)

        @pl.when(i == 0)
        def _():
            dsh_ref[...] = jnp.zeros_like(dsh_ref)
            dsc_ref[...] = jnp.zeros_like(dsc_ref)
            dg_ref[...] = jnp.zeros_like(dg_ref)
            if resid:
                resid_outs[1][...] = jnp.zeros_like(resid_outs[1])

        dh_v = _dot(part_refs[0][...], w_ref[offs[0]:offs[0] + widths[0], :])
        for k in range(1, n_p):
            dh_v = dh_v + _dot(part_refs[k][...], w_ref[offs[k]:offs[k] + widths[k], :])
        xv = x_ref[...]
        g_v = g_ref[...]
        r = lax.rsqrt(jnp.mean(xv * xv, axis=-1, keepdims=True) + EPS)
        xn = xv * r
        dsh_ref[...] += jnp.sum(dh_v, axis=0, keepdims=True)
        dsc_ref[...] += jnp.sum(dh_v * (xn * g_v), axis=0, keepdims=True)
        dn = dh_v * (1.0 + sc_ref[...])
        dg_ref[...] += jnp.sum(dn * xn, axis=0, keepdims=True)
        dxh = dn * g_v
        dx = dxn_ref[...] + r * (dxh - xn * jnp.mean(dxh * xn, axis=-1, keepdims=True))
        dx_ref[...] = dx
        if resid:
            yo_ref, gt_ref = resid_refs
            resid_outs[0][...] = (dx * (1.0 + gt_ref[...])).astype(BF16)
            resid_outs[1][...] += jnp.sum(dx * yo_ref[...], axis=0, keepdims=True)

    row, vec = _row_spec(), _vec_spec()
    out = pl.pallas_call(
        body, name=name, grid=(n_i,),
        in_specs=[_row_spec(w) for w in widths]
        + [pl.BlockSpec((k_all, D_MODEL), lambda i: (0, 0), pipeline_mode=pl.Buffered(1)), row, row, vec, vec]
        + ([row, vec] if resid else []) + [HBM_SPEC] * n_side,
        out_specs=[row, vec, vec, vec] + ([row, vec] if resid else []) + [HBM_SPEC] * n_side,
        out_shape=[SDS((s, D_MODEL), F32)] + [SDS((1, D_MODEL), F32)] * 3
        + ([SDS((s, D_MODEL), BF16), SDS((1, D_MODEL), F32)] if resid else [])
        + [SDS(dw.shape, dw.dtype) for dw in dws],
        scratch_shapes=_peer_sems(n_side) if n_side else [],
        compiler_params=_params(1),
    )(*parts, w_all, x, dx_next, g, scale, *(resid or ()), *dws)
    return out[:4 + n_r], out[4 + n_r:]


def _wgrad_in(name, h_t, parts, blk):
    s = h_t.shape[1]
    widths = [p.shape[1] for p in parts]
    cn = sum(widths) // N_DEV
    per_dev = cn // blk
    starts = [sum(widths[:k]) // blk for k in range(len(parts))]
    counts = [w // blk for w in widths]
    n_blk = sum(counts)

    def body(a_ref, *rest):
        o_ref = rest[len(parts)]
        b = pl.program_id(0)
        for k in range(len(parts)):
            @pl.when((b >= starts[k]) & (b < starts[k] + counts[k]))
            def _(k=k):
                o_ref[...] = _dot(a_ref[...], rest[k][...]).astype(BF16)

    def part_spec(k):
        return pl.BlockSpec((s, blk), lambda b: (0, jnp.clip(b - starts[k], 0, counts[k] - 1)))

    return pl.pallas_call(
        body, name=name, grid=(n_blk,),
        in_specs=[pl.BlockSpec((D_MODEL, s), lambda b: (0, 0), pipeline_mode=pl.Buffered(1))]
        + [part_spec(k) for k in range(len(parts))],
        out_specs=pl.BlockSpec((None, D_MODEL, blk), lambda b: (b // per_dev, 0, b % per_dev)),
        out_shape=SDS((N_DEV, D_MODEL, cn), BF16), compiler_params=_params(1),
    )(h_t, *parts)


def _vec_spec():
    return pl.BlockSpec((1, D_MODEL), lambda i: (0, 0))


def _row_spec(width=D_MODEL, col=0):
    return pl.BlockSpec((ROW_TILE, width), lambda i: (i, col))


def _col_spec():
    return pl.BlockSpec((D_MODEL, ROW_TILE), lambda i: (0, i))


def _ln_mod(name, x, g, scale, shift):
    s = x.shape[0]

    def body(x_ref, g_ref, sc_ref, sh_ref, h_ref, ht_ref):
        xv = x_ref[...]
        r = lax.rsqrt(jnp.mean(xv * xv, axis=-1, keepdims=True) + EPS)
        n = (xv * r) * g_ref[...]
        h = (n * (1.0 + sc_ref[...]) + sh_ref[...]).astype(BF16)
        h_ref[...] = h
        ht_ref[...] = h.T

    return pl.pallas_call(
        body, name=name, grid=(s // ROW_TILE,),
        in_specs=[_row_spec(), _vec_spec(), _vec_spec(), _vec_spec()], out_specs=[_row_spec(), _col_spec()],
        out_shape=[SDS((s, D_MODEL), BF16), SDS((D_MODEL, s), BF16)], compiler_params=_params(1),
    )(x, g, scale, shift)


def _resid_ln_mod(name, x, yo, gate, g, scale, shift):
    s = x.shape[0]

    def body(x_ref, yo_ref, gt_ref, g_ref, sc_ref, sh_ref, xn_ref, h_ref, ht_ref):
        xv = x_ref[...] + (1.0 + gt_ref[...]) * yo_ref[...]
        xn_ref[...] = xv
        r = lax.rsqrt(jnp.mean(xv * xv, axis=-1, keepdims=True) + EPS)
        n = (xv * r) * g_ref[...]
        h = (n * (1.0 + sc_ref[...]) + sh_ref[...]).astype(BF16)
        h_ref[...] = h
        ht_ref[...] = h.T

    return pl.pallas_call(
        body, name=name, grid=(s // ROW_TILE,),
        in_specs=[_row_spec(), _row_spec(), _vec_spec(), _vec_spec(), _vec_spec(), _vec_spec()],
        out_specs=[_row_spec(), _row_spec(), _col_spec()],
        out_shape=[SDS((s, D_MODEL), F32), SDS((s, D_MODEL), BF16), SDS((D_MODEL, s), BF16)],
        compiler_params=_params(1),
    )(x, yo, gate, g, scale, shift)


def _final_loss(name, x1, yo1, gate1, gf, target):
    s = x1.shape[0]

    def body(x_ref, yo_ref, gt_ref, gf_ref, t_ref, dx_ref, dyo_ref, loss_ref, dgf_ref, dgt_ref):
        i = pl.program_id(0)

        @pl.when(i == 0)
        def _():
            loss_ref[...] = jnp.zeros_like(loss_ref)
            dgf_ref[...] = jnp.zeros_like(dgf_ref)
            dgt_ref[...] = jnp.zeros_like(dgt_ref)

        yo = yo_ref[...]
        one_gate = 1.0 + gt_ref[...]
        x2 = x_ref[...] + one_gate * yo
        r = lax.rsqrt(jnp.mean(x2 * x2, axis=-1, keepdims=True) + EPS)
        xn = x2 * r
        gf_v = gf_ref[...]
        err = xn * gf_v - t_ref[...]
        loss_ref[...] += 0.5 * jnp.sum(jnp.mean(err * err, axis=-1, keepdims=True))
        dout = err * (1.0 / D_MODEL)
        dgf_ref[...] += jnp.sum(dout * xn, axis=0, keepdims=True)
        dxn = dout * gf_v
        dx2 = r * (dxn - xn * jnp.mean(dxn * xn, axis=-1, keepdims=True))
        dx_ref[...] = dx2
        dyo_ref[...] = (dx2 * one_gate).astype(BF16)
        dgt_ref[...] += jnp.sum(dx2 * yo, axis=0, keepdims=True)

    return pl.pallas_call(
        body, name=name, grid=(s // ROW_TILE,),
        in_specs=[_row_spec(), _row_spec(), _vec_spec(), _vec_spec(), _row_spec()],
        out_specs=[_row_spec(), _row_spec(), pl.BlockSpec((1, LANES), lambda i: (0, 0)), _vec_spec(), _vec_spec()],
        out_shape=[SDS((s, D_MODEL), F32), SDS((s, D_MODEL), BF16), SDS((1, LANES), F32),
                   SDS((1, D_MODEL), F32), SDS((1, D_MODEL), F32)],
        compiler_params=_params(1),
    )(x1, yo1, gate1, gf, target)


POOL_WINDOWS = (2, 4, 8, 16)


def _window_sum(x, window, rows, backward):
    acc, step = x, 1
    while step < window:
        acc = acc + pltpu.roll(acc, step if backward else rows - step, axis=0)
        step *= 2
    return acc


def _pool_fwd(name, proj0, wp):
    s = proj0.shape[0]
    hb = ROW_TILE // HALO
    ext_rows = ROW_TILE + HALO

    def body(u_ref, halo_ref, w_ref, p_ref, y_ref):
        i = pl.program_id(0)
        t = i * ROW_TILE + lax.broadcasted_iota(jnp.int32, (ROW_TILE, 1), 0)
        for g, window in enumerate(POOL_WINDOWS):
            cols = slice(g * POOL_GROUP, (g + 1) * POOL_GROUP)
            u = u_ref[:, cols].astype(F32)
            halo = jnp.where(i == 0, 0.0, halo_ref[:, cols].astype(F32))
            ext = jnp.concatenate([halo, u], axis=0)
            win = _window_sum(ext, window, ext_rows, True)[HALO:, :]
            cnt = jnp.minimum(t + 1, window).astype(F32)
            p = (win / cnt - u).astype(BF16)
            p_ref[:, cols] = p
            y_ref[:, cols] = _dot(p, w_ref[g])

    return pl.pallas_call(
        body, name=name, grid=(s // ROW_TILE,),
        in_specs=[pl.BlockSpec((ROW_TILE, D_POOL), lambda i: (i, 0)),
                  pl.BlockSpec((HALO, D_POOL), lambda i: (jnp.maximum(i * hb - 1, 0), 0)),
                  pl.BlockSpec((N_GROUPS, POOL_GROUP, POOL_GROUP), lambda i: (0, 0, 0))],
        out_specs=[_row_spec(D_POOL), _row_spec(D_POOL)],
        out_shape=[SDS((s, D_POOL), BF16), SDS((s, D_POOL), F32)], compiler_params=_params(1),
    )(proj0, proj0, wp)


def _pool_bwd(name, dyp, p, wp):
    s = dyp.shape[0]
    hb = ROW_TILE // HALO
    n_hb = s // HALO
    n_tiles = s // ROW_TILE
    ext_rows = ROW_TILE + HALO

    def body(dy_ref, nxt_ref, p_ref, w_ref, du_ref, dw_ref):
        i = pl.program_id(0)

        @pl.when(i == 0)
        def _():
            dw_ref[...] = jnp.zeros_like(dw_ref)

        t = i * ROW_TILE + lax.broadcasted_iota(jnp.int32, (ext_rows, 1), 0)
        for g, window in enumerate(POOL_WINDOWS):
            cols = slice(g * POOL_GROUP, (g + 1) * POOL_GROUP)
            dy = dy_ref[:, cols]
            nxt = nxt_ref[:, cols]
            nxt = jnp.where(i == n_tiles - 1, jnp.zeros_like(nxt), nxt)
            dp = _dot_nt(jnp.concatenate([dy, nxt], axis=0), w_ref[g])
            cnt = jnp.minimum(t + 1, window).astype(F32)
            win = _window_sum(dp / cnt, window, ext_rows, False)[:ROW_TILE, :]
            du_ref[:, cols] = (win - dp[:ROW_TILE, :]).astype(BF16)
            dw_ref[g] += _dot_tn(p_ref[:, cols], dy)

    return pl.pallas_call(
        body, name=name, grid=(n_tiles,),
        in_specs=[_row_spec(D_POOL),
                  pl.BlockSpec((HALO, D_POOL), lambda i: (jnp.minimum((i + 1) * hb, n_hb - 1), 0)),
                  _row_spec(D_POOL),
                  pl.BlockSpec((N_GROUPS, POOL_GROUP, POOL_GROUP), lambda i: (0, 0, 0))],
        out_specs=[_row_spec(D_POOL), pl.BlockSpec((N_GROUPS, POOL_GROUP, POOL_GROUP), lambda i: (0, 0, 0))],
        out_shape=[SDS((s, D_POOL), BF16), SDS((N_GROUPS, POOL_GROUP, POOL_GROUP), F32)],
        compiler_params=_params(1),
    )(dyp, dyp, p, wp)


FWD_HEADS_PER_STEP = 8
BWD_HEADS_PER_STEP = 4
ATT_SCALE = 0.125
FWD_SKEW = 1


def _att_groups(nh):
    lanes = nh * HEAD_DIM
    return lanes, D_SB // lanes, D_POOL // lanes, (D_POOL + D_SB) // lanes, (D_POOL + 2 * D_SB) // lanes


def _att_consts():
    r = lax.broadcasted_iota(jnp.int32, (ATT_TILE, ATT_TILE), 0)
    c = lax.broadcasted_iota(jnp.int32, (ATT_TILE, ATT_TILE), 1)
    first = lax.broadcasted_iota(jnp.int32, (1, LANES), 1) < HEAD_DIM
    return r, c, first


def _pair(x, p):
    return x[:, p * LANES:(p + 1) * LANES]


def _one_head(x, first, hh):
    zero = jnp.zeros_like(x)
    return jnp.where(first, x, zero) if hh == 0 else jnp.where(first, zero, x)


def _neg_softplus(z):
    return -(jnp.maximum(z, 0.0) + jnp.log(1.0 + jnp.exp(-jnp.abs(z))))


def _side_exchange(side_refs, n_side, by_chunk, is_first, is_last, chips=False):
    ins, outs = side_refs[:n_side], side_refs[n_side:2 * n_side]
    sems = side_refs[2 * n_side:2 * n_side + 3]

    def copies():
        return _chip_copies(ins, outs, *sems) if chips else _peer_copies(ins, outs, *sems, by_chunk=by_chunk)

    @pl.when(is_first)
    def _():
        for cp in copies():
            cp.start()

    @pl.when(is_last)
    def _():
        for cp in copies():
            cp.wait()


def _attn_fwd(name, proj0, shards):
    s = proj0.shape[0]
    nq = s // ATT_TILE
    nh = FWD_HEADS_PER_STEP
    ATT_GROUP, N_ATT_GROUPS, Q_GRP, K_GRP, V_GRP = _att_groups(nh)
    n_side = len(shards)

    def body(q_ref, k_ref, v_ref, *rest):
        o_ref = rest[n_side]
        side = rest[:n_side] + rest[n_side + 1:]
        j = pl.program_id(0)
        i = pl.program_id(1)
        _side_exchange(side, n_side, False, (j == 0) & (i == 0), (j == N_ATT_GROUPS - 1) & (i == nq - 1))
        r, c, first = _att_consts()
        tri = (r >= c).astype(BF16)
        below = c < r
        q = q_ref[...] * ATT_SCALE
        qh = [_one_head(_pair(q, h // 2), first, h % 2) for h in range(nh)]

        def tile(kb, carry, diagonal):
            k0 = pl.multiple_of(kb * ATT_TILE, ATT_TILE)
            kt = k_ref[pl.ds(k0, ATT_TILE), :]
            vt = v_ref[pl.ds(k0, ATT_TILE), :]
            z, lf_b, a_b = [None] * nh, [None] * nh, [None] * nh
            out_c, out_acc = [None] * nh, [None] * nh
            for t in range(nh + 2 * FWD_SKEW):
                if t < nh:
                    z[t] = _dot_nt(qh[t], _pair(kt, t // 2))
                    lf = _neg_softplus(z[t])
                    if diagonal:
                        lf = jnp.where(below, lf, 0.0)
                    lf_b[t] = lf.astype(BF16)
                    out_c[t] = carry[t] + jnp.sum(lf, axis=1, keepdims=True)
                u = t - FWD_SKEW
                if 0 <= u < nh:
                    a = jnp.exp(z[u] + _dot(lf_b[u], tri) + carry[u])
                    if diagonal:
                        a = jnp.where(below, a, 0.0)
                    a_b[u] = a.astype(BF16)
                w = t - 2 * FWD_SKEW
                if 0 <= w < nh:
                    out_acc[w] = carry[nh + w] + _dot(a_b[w], _pair(vt, w // 2))
            return tuple(out_c + out_acc)

        init = tuple([jnp.zeros((ATT_TILE, 1), F32)] * nh + [jnp.zeros((ATT_TILE, LANES), F32)] * nh)
        carry = tile(i, init, True)
        carry = lax.fori_loop(1, i + 1, lambda n, cr: tile(i - n, cr, False), carry)
        for p in range(nh // 2):
            o_ref[:, p * LANES:(p + 1) * LANES] = jnp.where(first, carry[nh + 2 * p], carry[nh + 2 * p + 1])

    out = pl.pallas_call(
        body, name=name, grid=(N_ATT_GROUPS, nq),
        in_specs=[pl.BlockSpec((ATT_TILE, ATT_GROUP), lambda j, i: (i, Q_GRP + j)),
                  pl.BlockSpec((s, ATT_GROUP), lambda j, i: (0, K_GRP + j)),
                  pl.BlockSpec((s, ATT_GROUP), lambda j, i: (0, V_GRP + j))] + [HBM_SPEC] * n_side,
        out_specs=[pl.BlockSpec((ATT_TILE, ATT_GROUP), lambda j, i: (i, j))] + [HBM_SPEC] * n_side,
        out_shape=[SDS((s, D_SB), F32)] + [SDS((N_DEV,) + sh.shape, sh.dtype) for sh in shards],
        scratch_shapes=_peer_sems(n_side), compiler_params=_params(2),
    )(proj0, proj0, proj0, *shards)
    return out[0], out[1:]


def _attn_bwd(name, proj0, o, do, dws, shards):
    s = proj0.shape[0]
    nq = s // ATT_TILE
    nh = BWD_HEADS_PER_STEP
    ATT_GROUP, N_ATT_GROUPS, Q_GRP, K_GRP, V_GRP = _att_groups(nh)
    n1, n2 = len(dws), len(shards)
    n_side = n1 + n2

    def body(q_ref, k_ref, v_ref, o_ref, do_ref, *rest):
        dq_ref, dk_ref, dv_ref = rest[n_side:n_side + 3]
        dk_acc, dv_acc = rest[2 * n_side + 3:2 * n_side + 5]
        srcs, dsts, sems = rest[:n_side], rest[n_side + 3:2 * n_side + 3], rest[2 * n_side + 5:]
        j = pl.program_id(0)
        i = pl.program_id(1)
        is_first, is_last = (j == 0) & (i == 0), (j == N_ATT_GROUPS - 1) & (i == nq - 1)
        _side_exchange(srcs[:n1] + dsts[:n1] + sems[:3], n1, True, is_first, is_last)
        _side_exchange(srcs[n1:] + dsts[n1:] + sems[3:], n2, False, is_first, is_last)

        @pl.when(i == 0)
        def _():
            dk_acc[...] = jnp.zeros_like(dk_acc)
            dv_acc[...] = jnp.zeros_like(dv_acc)

        r, c, first = _att_consts()
        tri = (r >= c).astype(BF16)
        tri_p = (r <= c).astype(BF16)
        below = c < r
        q = q_ref[...] * ATT_SCALE
        do_b = do_ref[...].astype(BF16)
        do_o = do_b.astype(F32) * o_ref[...]
        qh = [_one_head(_pair(q, h // 2), first, h % 2) for h in range(nh)]
        doh = [_one_head(_pair(do_b, h // 2), first, h % 2) for h in range(nh)]
        dsum = [jnp.sum(_one_head(_pair(do_o, h // 2), first, h % 2), axis=1, keepdims=True) for h in range(nh)]

        def tile(kb, carry, diagonal):
            k0 = pl.multiple_of(kb * ATT_TILE, ATT_TILE)
            kt = k_ref[pl.ds(k0, ATT_TILE), :]
            vt = v_ref[pl.ds(k0, ATT_TILE), :]
            none = lambda: [None] * nh
            z, d_a, sig, lf_b, a_b, g, early, dz = none(), none(), none(), none(), none(), none(), none(), none()
            out_c1, out_c2, out_dq, dk_t, dv_t = none(), none(), none(), none(), none()
            for t in range(nh + 3):
                if t < nh:
                    z[t] = _dot_nt(qh[t], _pair(kt, t // 2))
                    d_a[t] = _dot_nt(doh[t], _pair(vt, t // 2))
                    lf = _neg_softplus(z[t])
                    sig[t] = jnp.exp(z[t] + lf)
                    if diagonal:
                        lf = jnp.where(below, lf, 0.0)
                    lf_b[t] = lf.astype(BF16)
                    out_c1[t] = carry[t] + jnp.sum(lf, axis=1, keepdims=True)
                u = t - 1
                if 0 <= u < nh:
                    a = jnp.exp(z[u] + _dot(lf_b[u], tri) + carry[u])
                    if diagonal:
                        a = jnp.where(below, a, 0.0)
                    a_b[u] = a.astype(BF16)
                    g[u] = a_b[u].astype(F32) * d_a[u]
                    g_sum = jnp.sum(g[u], axis=1, keepdims=True)
                    early[u] = dsum[u] - carry[nh + u] - g_sum
                    out_c2[u] = carry[nh + u] + g_sum
                w = t - 2
                if 0 <= w < nh:
                    upto = _dot(g[w].astype(BF16), tri_p)
                    dv_t[w] = _dot_tn(a_b[w], doh[w])
                    d = g[w] - sig[w] * (early[w] + upto)
                    if diagonal:
                        d = jnp.where(below, d, 0.0)
                    dz[w] = d.astype(BF16)
                y = t - 3
                if 0 <= y < nh:
                    out_dq[y] = carry[2 * nh + y] + _dot(dz[y], _pair(kt, y // 2))
                    dk_t[y] = _dot_tn(dz[y], qh[y])
            for p in range(nh // 2):
                dk_acc[pl.ds(k0, ATT_TILE), p * LANES:(p + 1) * LANES] += dk_t[2 * p] + dk_t[2 * p + 1]
                dv_acc[pl.ds(k0, ATT_TILE), p * LANES:(p + 1) * LANES] += dv_t[2 * p] + dv_t[2 * p + 1]
            return tuple(out_c1 + out_c2 + out_dq)

        init = tuple([jnp.zeros((ATT_TILE, 1), F32)] * (2 * nh) + [jnp.zeros((ATT_TILE, LANES), F32)] * nh)
        carry = tile(i, init, True)
        carry = lax.fori_loop(1, i + 1, lambda n, cr: tile(i - n, cr, False), carry)
        for p in range(nh // 2):
            dq_p = jnp.where(first, carry[2 * nh + 2 * p], carry[2 * nh + 2 * p + 1]) * ATT_SCALE
            dq_ref[:, p * LANES:(p + 1) * LANES] = dq_p.astype(BF16)

        @pl.when(i == nq - 1)
        def _():
            dk_ref[...] = dk_acc[...].astype(BF16)
            dv_ref[...] = dv_acc[...].astype(BF16)

    tile_spec = pl.BlockSpec((ATT_TILE, ATT_GROUP), lambda j, i: (i, j))
    full = pl.BlockSpec((s, ATT_GROUP), lambda j, i: (0, j))
    out = pl.pallas_call(
        body, name=name, grid=(N_ATT_GROUPS, nq),
        in_specs=[pl.BlockSpec((ATT_TILE, ATT_GROUP), lambda j, i: (i, Q_GRP + j)),
                  pl.BlockSpec((s, ATT_GROUP), lambda j, i: (0, K_GRP + j)),
                  pl.BlockSpec((s, ATT_GROUP), lambda j, i: (0, V_GRP + j)),
                  tile_spec, tile_spec] + [HBM_SPEC] * n_side,
        out_specs=[tile_spec, full, full] + [HBM_SPEC] * n_side,
        out_shape=[SDS((s, D_SB), BF16)] * 3 + [SDS(dw.shape, dw.dtype) for dw in dws]
        + [SDS((N_DEV,) + sh.shape, sh.dtype) for sh in shards],
        scratch_shapes=[pltpu.VMEM((s, ATT_GROUP), F32), pltpu.VMEM((s, ATT_GROUP), F32)] + _peer_sems(n1) + _peer_sems(n2),
        compiler_params=_params(2),
    )(proj0, proj0, proj0, o, do, *dws, *shards)
    return out[0], out[1], out[2], out[3:3 + n1], out[3 + n1:]


GATE0_COL = (D_POOL + 3 * D_SB) // D_INNER


def _gate_fwd0(name, yp_raw, o, proj0, ps):
    s = o.shape[0]

    def body(yp_ref, o_ref, gt_ref, ps_ref, y_ref):
        gt = gt_ref[...].astype(F32)
        sg = gt * _sigmoid(gt)
        y_ref[:, :D_POOL] = (yp_ref[...] * ps_ref[...] * sg[:, :D_POOL]).astype(BF16)
        y_ref[:, D_POOL:] = (o_ref[...] * sg[:, D_POOL:]).astype(BF16)

    return pl.pallas_call(
        body, name=name, grid=(s // ROW_TILE,),
        in_specs=[_row_spec(), _row_spec(), _row_spec(D_INNER, GATE0_COL), _vec_spec()],
        out_specs=_row_spec(D_INNER),
        out_shape=SDS((s, D_INNER), BF16), compiler_params=_params(1),
    )(yp_raw, o, proj0, ps)


def _dsilu(x):
    sg = _sigmoid(x)
    return sg * (1.0 + x * (1.0 - sg))


def _gate_bwd0(name, dymix, yp_raw, o, proj0, ps):
    s = o.shape[0]

    def body(dy_ref, yp_ref, o_ref, gt_ref, ps_ref, dyp_ref, do_ref, dgt_ref, dps_ref):
        i = pl.program_id(0)

        @pl.when(i == 0)
        def _():
            dps_ref[...] = jnp.zeros_like(dps_ref)

        gt = gt_ref[...].astype(F32)
        dy = dy_ref[...]
        sg = gt * _sigmoid(gt)
        dsg = _dsilu(gt)
        dcat = dy * sg
        yp = yp_ref[...]
        ps_v = ps_ref[...]
        dyp_ref[...] = (dcat[:, :D_POOL] * ps_v).astype(BF16)
        do_ref[...] = dcat[:, D_POOL:]
        dps_ref[...] += jnp.sum(dcat[:, :D_POOL] * yp, axis=0, keepdims=True)
        dgt_ref[:, :D_POOL] = (dy[:, :D_POOL] * (yp * ps_v) * dsg[:, :D_POOL]).astype(BF16)
        dgt_ref[:, D_POOL:] = (dy[:, D_POOL:] * o_ref[...] * dsg[:, D_POOL:]).astype(BF16)

    return pl.pallas_call(
        body, name=name, grid=(s // ROW_TILE,),
        in_specs=[_row_spec(D_INNER), _row_spec(), _row_spec(), _row_spec(D_INNER, GATE0_COL), _vec_spec()],
        out_specs=[_row_spec(), _row_spec(), _row_spec(D_INNER), _vec_spec()],
        out_shape=[SDS((s, D_POOL), BF16), SDS((s, D_SB), F32), SDS((s, D_INNER), BF16), SDS((1, D_POOL), F32)],
        compiler_params=_params(1),
    )(dymix, yp_raw, o, proj0, ps)


CONV_HALO = 16


def _conv_fwd(name, proj1, cw, cb):
    s = proj1.shape[0]
    hb = ROW_TILE // CONV_HALO
    ext_rows = ROW_TILE + CONV_HALO

    def body(gb_ref, gc_ref, u_ref, gt_ref, gch_ref, uh_ref, cw_ref, cb_ref, y_ref):
        i = pl.program_id(0)
        uc = gc_ref[...].astype(F32) * u_ref[...].astype(F32)
        halo = jnp.where(i == 0, 0.0, gch_ref[...].astype(F32) * uh_ref[...].astype(F32))
        ext = jnp.concatenate([halo, uc], axis=0)
        uc1 = pltpu.roll(ext, 1, axis=0)[CONV_HALO:, :]
        uc2 = pltpu.roll(ext, 2, axis=0)[CONV_HALO:, :]
        cw_v = cw_ref[...]
        conv = cb_ref[...] + cw_v[0:1, :] * uc2 + cw_v[1:2, :] * uc1 + cw_v[2:3, :] * uc
        gt = gt_ref[...].astype(F32)
        y_ref[...] = (gb_ref[...].astype(F32) * conv * (gt * _sigmoid(gt))).astype(BF16)

    def tile(part):
        return pl.BlockSpec((ROW_TILE, D_INNER), lambda i: (i, part))

    def halo(part):
        return pl.BlockSpec((CONV_HALO, D_INNER), lambda i: (jnp.maximum(i * hb - 1, 0), part))

    return pl.pallas_call(
        body, name=name, grid=(s // ROW_TILE,),
        in_specs=[tile(0), tile(1), tile(2), tile(3), halo(1), halo(2),
                  pl.BlockSpec((3, D_INNER), lambda i: (0, 0)), pl.BlockSpec((1, D_INNER), lambda i: (0, 0))],
        out_specs=pl.BlockSpec((ROW_TILE, D_INNER), lambda i: (i, 0)),
        out_shape=SDS((s, D_INNER), BF16), compiler_params=_params(1),
    )(proj1, proj1, proj1, proj1, proj1, proj1, cw, cb)


def _conv_bwd(name, dymix, proj1, cw, cb):
    s = proj1.shape[0]
    hb = ROW_TILE // CONV_HALO
    n_hb = s // CONV_HALO
    n_tiles = s // ROW_TILE
    ext_rows = ROW_TILE + CONV_HALO

    def body(dy_ref, gb_ref, gc_ref, u_ref, gt_ref, gch_ref, uh_ref, dyn_ref, gbn_ref, gtn_ref, cw_ref, cb_ref,
             dproj_ref, dcw_ref, dcb_ref):
        i = pl.program_id(0)

        @pl.when(i == 0)
        def _():
            dcw_ref[...] = jnp.zeros_like(dcw_ref)
            dcb_ref[...] = jnp.zeros_like(dcb_ref)

        gc = gc_ref[...].astype(F32)
        u = u_ref[...].astype(F32)
        gb = gb_ref[...].astype(F32)
        gt = gt_ref[...].astype(F32)
        dy = dy_ref[...]
        uc = gc * u
        halo = jnp.where(i == 0, 0.0, gch_ref[...].astype(F32) * uh_ref[...].astype(F32))
        ext = jnp.concatenate([halo, uc], axis=0)
        uc1 = pltpu.roll(ext, 1, axis=0)[CONV_HALO:, :]
        uc2 = pltpu.roll(ext, 2, axis=0)[CONV_HALO:, :]
        cw_v = cw_ref[...]
        w0, w1, w2 = cw_v[0:1, :], cw_v[1:2, :], cw_v[2:3, :]
        conv = cb_ref[...] + w0 * uc2 + w1 * uc1 + w2 * uc
        sig = _sigmoid(gt)
        sg = gt * sig
        dconv = dy * gb * sg
        gtn = gtn_ref[...].astype(F32)
        dconv_next = jnp.where(i == n_tiles - 1, 0.0, dyn_ref[...] * gbn_ref[...].astype(F32) * (gtn * _sigmoid(gtn)))
        dext = jnp.concatenate([dconv, dconv_next], axis=0)
        dconv_p1 = pltpu.roll(dext, ext_rows - 1, axis=0)[:ROW_TILE, :]
        dconv_p2 = pltpu.roll(dext, ext_rows - 2, axis=0)[:ROW_TILE, :]
        duc = w2 * dconv + w1 * dconv_p1 + w0 * dconv_p2
        dproj_ref[:, 0:D_INNER] = (dy * conv * sg).astype(BF16)
        dproj_ref[:, D_INNER:2 * D_INNER] = (duc * u).astype(BF16)
        dproj_ref[:, 2 * D_INNER:3 * D_INNER] = (duc * gc).astype(BF16)
        dproj_ref[:, 3 * D_INNER:] = (dy * gb * conv * (sig + sg * (1.0 - sig))).astype(BF16)
        dcw_ref[0:1, :] += jnp.sum(dconv * uc2, axis=0, keepdims=True)
        dcw_ref[1:2, :] += jnp.sum(dconv * uc1, axis=0, keepdims=True)
        dcw_ref[2:3, :] += jnp.sum(dconv * uc, axis=0, keepdims=True)
        dcb_ref[...] += jnp.sum(dconv, axis=0, keepdims=True)

    def tile(part):
        return pl.BlockSpec((ROW_TILE, D_INNER), lambda i: (i, part))

    def prev(part):
        return pl.BlockSpec((CONV_HALO, D_INNER), lambda i: (jnp.maximum(i * hb - 1, 0), part))

    def nxt(part):
        return pl.BlockSpec((CONV_HALO, D_INNER), lambda i: (jnp.minimum((i + 1) * hb, n_hb - 1), part))

    whole = lambda rows: pl.BlockSpec((rows, D_INNER), lambda i: (0, 0))
    return pl.pallas_call(
        body, name=name, grid=(n_tiles,),
        in_specs=[tile(0), tile(0), tile(1), tile(2), tile(3), prev(1), prev(2), nxt(0), nxt(0), nxt(3),
                  whole(3), whole(1)],
        out_specs=[pl.BlockSpec((ROW_TILE, 4 * D_INNER), lambda i: (i, 0)), whole(3), whole(1)],
        out_shape=[SDS((s, 4 * D_INNER), BF16), SDS((3, D_INNER), F32), SDS((1, D_INNER), F32)],
        compiler_params=_params(1),
    )(dymix, proj1, proj1, proj1, proj1, proj1, proj1, dymix, proj1, proj1, cw, cb)


def _place():
    x, y, c = lax.axis_index("x"), lax.axis_index("y"), lax.axis_index("c")
    return x, y, c


def _flip(x, y, c, k):
    fx, fy, fc = (k >> 2) & 1, (k >> 1) & 1, k & 1
    return (1 - x if fx else x, 1 - y if fy else y, 1 - c if fc else c)


def _dev_index(p):
    return 4 * p[0] + 2 * p[1] + p[2]


HBM_SPEC = pl.BlockSpec(memory_space=pltpu.HBM)
VMEM_SPEC = pl.BlockSpec(memory_space=pltpu.VMEM)


def _two_level_gather(src, dst, send_sems, recv_sems, local_sem):
    x, y, c = _place()
    me, sibling = (x, y, c), (x, y, 1 - c)
    chips = [(1 - x, y), (x, 1 - y), (1 - x, 1 - y)]

    def copy(k, block, to, from_src=False):
        rows = dst.at[_dev_index(block)]
        return pltpu.make_async_remote_copy(
            src_ref=src if from_src else rows, dst_ref=rows, send_sem=send_sems.at[k], recv_sem=recv_sems.at[k],
            device_id=to, device_id_type=MESH)

    mine = pltpu.make_async_copy(src, dst.at[_dev_index(me)], local_sem)
    first = [copy(0, me, sibling, True)] + [copy(1 + j, me, (*chip, c), True) for j, chip in enumerate(chips)]
    passed = [copy(4 + j, (*chip, c), sibling) for j, chip in enumerate(chips)]

    def start():
        mine.start()
        for cp in first:
            cp.start()

    def pass_on():
        for j, chip in enumerate(chips):
            copy(1 + j, (*chip, c), me).wait_recv()
            passed[j].start()

    def finish():
        copy(0, sibling, me).wait_recv()
        for j, chip in enumerate(chips):
            copy(4 + j, (*chip, 1 - c), me).wait_recv()
        for cp in first + passed:
            cp.wait_send()
        mine.wait()

    return start, pass_on, finish


def _peer_copies(ins, outs, send_sems, recv_sems, local_sems, by_chunk):
    x, y, c = _place()
    my = _dev_index((x, y, c))
    copies = []
    for w in range(len(ins)):
        copies.append(pltpu.make_async_copy(ins[w].at[my] if by_chunk else ins[w], outs[w].at[my], local_sems.at[w]))
        for k in range(1, N_DEV):
            peer = _flip(x, y, c, k)
            copies.append(pltpu.make_async_remote_copy(
                src_ref=ins[w].at[_dev_index(peer)] if by_chunk else ins[w], dst_ref=outs[w].at[my],
                send_sem=send_sems.at[7 * w + k - 1], recv_sem=recv_sems.at[7 * w + k - 1],
                device_id=peer, device_id_type=MESH))
    return copies


def _peer_sems(n_w):
    return [pltpu.SemaphoreType.DMA((7 * n_w,)), pltpu.SemaphoreType.DMA((7 * n_w,)), pltpu.SemaphoreType.DMA((n_w,))]


def _chip_index(p):
    return 2 * p[0] + p[1]


def _chip_copies(ins, outs, send_sems, recv_sems, local_sems):
    x, y, c = _place()
    mine = _chip_index((x, y))
    copies = []
    for w in range(len(ins)):
        copies.append(pltpu.make_async_copy(ins[w].at[mine], outs[w].at[mine], local_sems.at[w]))
        for k in (2, 4, 6):
            peer = _flip(x, y, c, k)
            copies.append(pltpu.make_async_remote_copy(
                src_ref=ins[w].at[_chip_index(peer)], dst_ref=outs[w].at[mine],
                send_sem=send_sems.at[7 * w + k - 1], recv_sem=recv_sems.at[7 * w + k - 1],
                device_id=peer, device_id_type=MESH))
    return copies


def _sibling_exchange(dw):
    n_chips = N_DEV // 2

    def body(dw_ref, out_ref, send_sems, recv_sems):
        x, y, c = _place()
        sibling = (x, y, 1 - c)
        copies = []
        for ch in range(n_chips):
            copies.append(pltpu.make_async_remote_copy(
                src_ref=dw_ref.at[2 * ch + (1 - c)], dst_ref=out_ref.at[ch],
                send_sem=send_sems.at[ch], recv_sem=recv_sems.at[ch], device_id=sibling, device_id_type=MESH))
        for cp in copies:
            cp.start()
        for cp in copies:
            cp.wait()

    return pl.pallas_call(
        body, name="sibling_exchange", out_shape=SDS((n_chips,) + dw.shape[1:], dw.dtype),
        in_specs=[HBM_SPEC], out_specs=HBM_SPEC,
        scratch_shapes=[pltpu.SemaphoreType.DMA((n_chips,)), pltpu.SemaphoreType.DMA((n_chips,))],
    )(dw)


def _sibling_sum(name, dw, got, core):
    n_chips, rows, cols = got.shape
    tr = min(rows, 256)

    def body(core_ref, a_ref, b_ref, o_ref):
        o_ref[...] = (a_ref[...].astype(F32) + b_ref[...].astype(F32)).astype(BF16)

    return pl.pallas_call(
        body, name=name,
        grid_spec=pltpu.PrefetchScalarGridSpec(
            num_scalar_prefetch=1, grid=(n_chips, rows // tr),
            in_specs=[pl.BlockSpec((None, tr, cols), lambda ch, i, core_ref: (2 * ch + core_ref[0], i, 0)),
                      pl.BlockSpec((None, tr, cols), lambda ch, i, core_ref: (ch, i, 0))],
            out_specs=pl.BlockSpec((None, tr, cols), lambda ch, i, core_ref: (ch, i, 0))),
        out_shape=SDS(got.shape, BF16), compiler_params=_params(2),
    )(core, dw, got)


ADA_COLS = 3 * D_MODEL // N_DEV


def _ada_forward(c_row, conv_w, conv_b, ada_w, ada_b, w_shard):
    cw_cols = conv_w.shape[1]

    def body(c_ref, cw_ref, cb_ref, aw_ref, ab_ref, w_ref, m_ref, cs_ref, cwf_ref, cbf_ref, wg_ref,
             slab, gath, part, land, send_sems, recv_sems, w_send, w_recv, w_local):
        x, y, c = _place()
        my = _dev_index((x, y, c))
        w_start, w_pass_on, w_finish = _two_level_gather(w_ref, wg_ref, w_send, w_recv, w_local)
        w_start()
        slab[...] = jnp.zeros_like(slab)
        slab[0:1, :] = c_ref[...]
        slab[1:4, 0:cw_cols] = cw_ref[...]
        slab[4:5, 0:cw_cols] = cb_ref[...]
        gath[my] = slab[...]
        sends = []
        for k in range(1, N_DEV):
            peer = _flip(x, y, c, k)
            cp = pltpu.make_async_remote_copy(
                src_ref=slab, dst_ref=gath.at[my], send_sem=send_sems.at[k - 1], recv_sem=recv_sems.at[k - 1],
                device_id=peer, device_id_type=MESH)
            cp.start()
            sends.append(cp)
        for cp in sends:
            cp.wait()
        for d in range(N_DEV):
            c_d = gath[d, 0:1, :]
            cs_ref[d:d + 1, :] = c_d * _sigmoid(c_d)
            cwf_ref[:, d * cw_cols:(d + 1) * cw_cols] = gath[d, 1:4, 0:cw_cols]
            cbf_ref[:, d * cw_cols:(d + 1) * cw_cols] = gath[d, 4:5, 0:cw_cols]
        cs = cs_ref[...]
        part[...] = jnp.zeros_like(part)
        for layer in range(2):
            m_part = jnp.dot(cs, aw_ref[layer], preferred_element_type=F32, precision=lax.Precision.HIGHEST)
            for d in range(N_DEV):
                part[d, layer:layer + 1, :] = m_part[d:d + 1, :]
        land[my] = part[my]
        sends = []
        for k in range(1, N_DEV):
            peer = _flip(x, y, c, k)
            cp = pltpu.make_async_remote_copy(
                src_ref=part.at[_dev_index(peer)], dst_ref=land.at[my],
                send_sem=send_sems.at[6 + k], recv_sem=recv_sems.at[6 + k],
                device_id=peer, device_id_type=MESH)
            cp.start()
            sends.append(cp)
        for cp in sends:
            cp.wait()
        for d in range(N_DEV):
            cols = slice(d * ADA_COLS, (d + 1) * ADA_COLS)
            m_ref[:, cols] = land[d, 0:2, :] + ab_ref[:, cols]
        w_pass_on()
        w_finish()

    return pl.pallas_call(
        body, name="ada_forward",
        out_shape=[SDS((2, 3 * D_MODEL), F32), SDS((N_DEV, D_MODEL), F32), SDS((3, N_DEV * cw_cols), F32),
                   SDS((1, N_DEV * cw_cols), F32), SDS((N_DEV,) + w_shard.shape, w_shard.dtype)],
        in_specs=[VMEM_SPEC] * 5 + [HBM_SPEC], out_specs=[VMEM_SPEC] * 4 + [HBM_SPEC],
        scratch_shapes=[pltpu.VMEM((8, D_MODEL), F32), pltpu.VMEM((N_DEV, 8, D_MODEL), F32),
                        pltpu.VMEM((N_DEV, 8, ADA_COLS), F32), pltpu.VMEM((N_DEV, 8, ADA_COLS), F32),
                        pltpu.SemaphoreType.DMA((14,)), pltpu.SemaphoreType.DMA((14,)),
                        pltpu.SemaphoreType.DMA((7,)), pltpu.SemaphoreType.DMA((7,)), pltpu.SemaphoreType.DMA],
        compiler_params=pltpu.CompilerParams(vmem_limit_bytes=VMEM_LIMIT),
    )(c_row, conv_w, conv_b, ada_w, ada_b, w_shard)


def _small_grads(slab):
    def body(slab_ref, gath_ref, tot_ref, send_sems, recv_sems):
        x, y, c = _place()
        my = _dev_index((x, y, c))
        gath_ref[my] = slab_ref[...]
        sends = []
        for k in range(1, N_DEV):
            peer = _flip(x, y, c, k)
            cp = pltpu.make_async_remote_copy(
                src_ref=slab_ref, dst_ref=gath_ref.at[my], send_sem=send_sems.at[k - 1], recv_sem=recv_sems.at[k - 1],
                device_id=peer, device_id_type=MESH)
            cp.start()
            sends.append(cp)
        for cp in sends:
            cp.wait()
        tot = gath_ref[0]
        for d in range(1, N_DEV):
            tot = tot + gath_ref[d]
        tot_ref[...] = tot

    return pl.pallas_call(
        body, name="small_grads",
        out_shape=[SDS((N_DEV, SLAB_ROWS, D_MODEL), F32), SDS((SLAB_ROWS, D_MODEL), F32)],
        in_specs=[VMEM_SPEC], out_specs=[VMEM_SPEC] * 2,
        scratch_shapes=[pltpu.SemaphoreType.DMA((7,)), pltpu.SemaphoreType.DMA((7,))],
    )(slab)


def _adamw_math(w, g, m, v):
    m = ADAM_B1 * m + (1.0 - ADAM_B1) * g
    v = ADAM_B2 * v + (1.0 - ADAM_B2) * jnp.square(g)
    m_hat = m / (1.0 - ADAM_B1 ** ADAM_STEP)
    v_hat = v / (1.0 - ADAM_B2 ** ADAM_STEP)
    delta = -ADAM_LR * (m_hat / (jnp.sqrt(v_hat) + ADAM_EPS) + ADAM_WD * w)
    return delta, m, v


def _sum_adamw(name, recv, w, m, v):
    rows, cols = w.shape
    tr = min(rows, 256)
    n_slots = recv.shape[0]

    def body(r_ref, w_ref, m_ref, v_ref, g_ref, d_ref, nm_ref, nv_ref):
        g = r_ref[0].astype(F32)
        for d in range(1, n_slots):
            g = g + r_ref[d].astype(F32)
        g_ref[...] = g
        d_ref[...], nm_ref[...], nv_ref[...] = _adamw_math(w_ref[...], g, m_ref[...], v_ref[...])

    blk = pl.BlockSpec((tr, cols), lambda i: (i, 0))
    return pl.pallas_call(
        body, name=name, grid=(rows // tr,),
        in_specs=[pl.BlockSpec((n_slots, tr, cols), lambda i: (0, i, 0)), blk, blk, blk],
        out_specs=[blk] * 4, out_shape=[SDS((rows, cols), F32)] * 4, compiler_params=_params(1),
    )(recv, w, m, v)


def _ada_w_adamw(name, cs_t, dm_cols, w, m, v):
    def body(cs_ref, dm_ref, w_ref, m_ref, v_ref, g_ref, d_ref, nm_ref, nv_ref):
        cs = cs_ref[...]
        dm = dm_ref[...]
        g = cs[:, 0:1] * dm[0:1, :]
        for b in range(1, N_DEV):
            g = g + cs[:, b:b + 1] * dm[b:b + 1, :]
        g_ref[...] = g
        d_ref[...], nm_ref[...], nv_ref[...] = _adamw_math(w_ref[...], g, m_ref[...], v_ref[...])

    blk = pl.BlockSpec((None, D_MODEL, ADA_COLS), lambda l: (l, 0, 0))
    return pl.pallas_call(
        body, name=name, grid=(2,),
        in_specs=[pl.BlockSpec((D_MODEL, N_DEV), lambda l: (0, 0)),
                  pl.BlockSpec((None, N_DEV, ADA_COLS), lambda l: (l, 0, 0)), blk, blk, blk],
        out_specs=[blk] * 4, out_shape=[SDS((2, D_MODEL, ADA_COLS), F32)] * 4, compiler_params=_params(1),
    )(cs_t, dm_cols, w, m, v)


def _small_adamw(name, triples):
    n = len(triples)

    def body(*refs):
        ins, outs = refs[:4 * n], refs[4 * n:]
        for j in range(n):
            w_ref, g_ref, m_ref, v_ref = ins[4 * j:4 * j + 4]
            d, nm, nv = _adamw_math(w_ref[...], g_ref[...], m_ref[...], v_ref[...])
            outs[3 * j][...] = d
            outs[3 * j + 1][...] = nm
            outs[3 * j + 2][...] = nv

    flat = [a for t in triples for a in t]
    return pl.pallas_call(
        body, name=name,
        out_shape=[SDS(t[0].shape, F32) for t in triples for _ in range(3)],
        in_specs=[VMEM_SPEC] * (4 * n), out_specs=[VMEM_SPEC] * (3 * n),
    )(*flat)


def kernel(x, c, norm_g, ada_w, ada_b, even_w_in, pool_w, pool_scale, even_w_out, odd_w_in, conv_w, conv_b, odd_w_out, final_g, loss_target, m_norm_g, m_ada_w, m_ada_b, m_even_w_in, m_pool_w, m_pool_scale, m_even_w_out, m_odd_w_in, m_conv_w, m_conv_b, m_odd_w_out, m_final_g, v_norm_g, v_ada_w, v_ada_b, v_even_w_in, v_pool_w, v_pool_scale, v_even_w_out, v_odd_w_in, v_conv_w, v_conv_b, v_odd_w_out, v_final_g):
    seq = x.shape[1]
    x0 = x[0]
    target = loss_target[0]
    final_g2 = final_g.reshape(1, D_MODEL)

    w_in_e = even_w_in[0]
    w_out_e = even_w_out[0]
    w_in_o = odd_w_in[0]
    w_out_o = odd_w_out[0]
    w_pool = pool_w[0].reshape(N_GROUPS * 32, POOL_GROUP)
    shards = [w.astype(BF16) for w in (w_in_e, w_out_e, w_in_o, w_out_o, w_pool)]

    m_vec, cs_all, conv_w_full, conv_b_full, wg_in_e = _ada_forward(c, conv_w[0], conv_b, ada_w, ada_b, shards[0])
    shift = [m_vec[l:l + 1, 0:D_MODEL] for l in range(2)]
    scale = [m_vec[l:l + 1, D_MODEL:2 * D_MODEL] for l in range(2)]
    gate = [m_vec[l:l + 1, 2 * D_MODEL:] for l in range(2)]
    ng = [norm_g[l:l + 1] for l in range(2)]

    h0, h0_t = _ln_mod("ln_mod0", x0, ng[0], scale[0], shift[0])
    proj0 = _proj_in("proj_in0", h0, wg_in_e)
    o, (wg_out_e, wg_in_o, wg_out_o, wg_pool, wt_in_o) = _attn_fwd("attn_fwd", proj0, shards[1:] + [shards[2].T])
    wf_out_e = wg_out_e.reshape(D_INNER, D_MODEL)
    wf_out_o = wg_out_o.reshape(D_INNER, D_MODEL)
    wf_pool = wg_pool.reshape(N_DEV, N_GROUPS, 32, POOL_GROUP).transpose(1, 0, 2, 3).reshape(N_GROUPS, POOL_GROUP, POOL_GROUP)
    p, yp_raw = _pool_fwd("pool_fwd", proj0, wf_pool)
    ymix0 = _gate_fwd0("gate_fwd0", yp_raw, o, proj0, pool_scale)
    yo0 = _proj_out("proj_out0", ymix0, wf_out_e)

    x1, h1, h1_t = _resid_ln_mod("resid_ln_mod1", x0, yo0, gate[0], ng[1], scale[1], shift[1])
    proj1 = _proj_in("proj_in1", h1, wg_in_o)
    ymix1 = _conv_fwd("conv_fwd", proj1, conv_w_full, conv_b_full)
    yo1 = _proj_out("proj_out1", ymix1, wf_out_o)

    dx2, dyo1, loss_acc, d_final_g, d_gate1 = _final_loss("final_loss", x1, yo1, gate[1], final_g2, target)
    loss = lax.psum(loss_acc[0, 0], ("x", "y", "c"))

    dymix1 = _proj_out_bwd("proj_out1_bwd", dyo1, wf_out_o)
    dw_out_o = _wgrad_out("wgrad_out1", ymix1, dyo1)
    dproj1, d_conv_w, d_conv_b = _conv_bwd("conv_bwd", dymix1, proj1, conv_w_full, conv_b_full)
    dw_in_o = _wgrad_in("wgrad_in1", h1_t, [dproj1], D_IN_ODD // N_DEV)
    (dx1, d_shift1, d_scale1, d_ng1, dyo0, d_gate0), _ = _proj_in_bwd_ln(
        "proj_in1_bwd", [dproj1], wt_in_o, x1, dx2, ng[1], scale[1], resid=(yo0, gate[0]))

    dymix0 = _proj_out_bwd("proj_out0_bwd", dyo0, wf_out_e)
    dw_out_e = _wgrad_out("wgrad_out0", ymix0, dyo0)
    dyp, do, dgt0, d_pool_scale = _gate_bwd0("gate_bwd0", dymix0, yp_raw, o, proj0, pool_scale)
    du_pool, dw_pool = _pool_bwd("pool_bwd", dyp, p, wf_pool)
    dw_pool_c = dw_pool.reshape(N_GROUPS, N_DEV, 32, POOL_GROUP).transpose(1, 0, 2, 3).reshape(N_DEV, N_GROUPS * 32, POOL_GROUP).astype(BF16)
    ready = [dw_out_e.reshape(N_DEV, D_INNER // N_DEV, D_MODEL), dw_in_o,
             dw_out_o.reshape(N_DEV, D_INNER // N_DEV, D_MODEL), dw_pool_c]
    dq, dk, dv, (r_out_e, r_in_o, r_out_o, r_pool), (wt_in_e,) = _attn_bwd(
        "attn_bwd", proj0, o, do, ready, [shards[0].T])
    dparts0 = [du_pool, dq, dk, dv, dgt0]
    dw_in_e = _wgrad_in("wgrad_in0", h0_t, dparts0, WGRAD_BLOCK)
    core = lax.axis_index("c").astype(jnp.int32).reshape(1)
    chip_sums = _sibling_sum("sibling_sum", dw_in_e, _sibling_exchange(dw_in_e), core)
    (dx0, d_shift0, d_scale0, d_ng0), (r_in_e,) = _proj_in_bwd_ln(
        "proj_in0_bwd", dparts0, wt_in_e, x0, dx1, ng[0], scale[0], dws=[chip_sums])
    grad_x = dx0[None]

    big = {}
    big["even_w_in"] = _sum_adamw("adamw_even_w_in", r_in_e, w_in_e, m_even_w_in[0], v_even_w_in[0])
    big["even_w_out"] = _sum_adamw("adamw_even_w_out", r_out_e, w_out_e, m_even_w_out[0], v_even_w_out[0])
    big["odd_w_in"] = _sum_adamw("adamw_odd_w_in", r_in_o, w_in_o, m_odd_w_in[0], v_odd_w_in[0])
    big["odd_w_out"] = _sum_adamw("adamw_odd_w_out", r_out_o, w_out_o, m_odd_w_out[0], v_odd_w_out[0])
    big["pool_w"] = _sum_adamw("adamw_pool_w", r_pool, w_pool, m_pool_w[0].reshape(N_GROUPS * 32, POOL_GROUP),
                               v_pool_w[0].reshape(N_GROUPS * 32, POOL_GROUP))
    big = {k: [a.reshape(shape) for a in v] for (k, v), shape in zip(
        big.items(), [even_w_in.shape, even_w_out.shape, odd_w_in.shape, odd_w_out.shape, pool_w.shape])}

    dm = jnp.concatenate([jnp.concatenate([d_shift0, d_scale0, d_gate0], axis=1),
                          jnp.concatenate([d_shift1, d_scale1, d_gate1], axis=1)], axis=0)
    slab = jnp.zeros((SLAB_ROWS, D_MODEL), F32)
    slab = slab.at[0:6].set(dm.reshape(6, D_MODEL))
    slab = slab.at[8:9].set(d_ng0).at[9:10].set(d_ng1).at[10:11].set(d_pool_scale).at[11:12].set(d_final_g)
    slab = slab.at[16:22].set(d_conv_w.reshape(6, D_MODEL)).at[24:26].set(d_conv_b.reshape(2, D_MODEL))
    gathered, total = _small_grads(slab)
    my = 4 * lax.axis_index("x") + 2 * lax.axis_index("y") + lax.axis_index("c")
    g_ada_b = total[0:6].reshape(2, 3 * D_MODEL)
    g_norm_g = total[8:10]
    g_pool_scale = total[10:11]
    g_final_g = total[11:12]
    cw_cols = conv_w.shape[2]
    g_conv_w = lax.dynamic_slice_in_dim(total[16:22].reshape(3, D_INNER), my * cw_cols, cw_cols, axis=1)
    g_conv_b = lax.dynamic_slice_in_dim(total[24:26].reshape(1, D_INNER), my * cw_cols, cw_cols, axis=1)
    dm_all = gathered[:, 0:6, :].reshape(N_DEV, 2, 3 * D_MODEL)
    dm_cols = lax.dynamic_slice_in_dim(dm_all, my * ADA_COLS, ADA_COLS, axis=2).transpose(1, 0, 2)
    ada = _ada_w_adamw("adamw_ada_w", cs_all.T, dm_cols, ada_w, m_ada_w, v_ada_w)

    small = _small_adamw("adamw_small", [
        (norm_g, g_norm_g, m_norm_g, v_norm_g),
        (ada_b, g_ada_b, m_ada_b, v_ada_b),
        (pool_scale, g_pool_scale, m_pool_scale, v_pool_scale),
        (conv_w[0], g_conv_w, m_conv_w[0], v_conv_w[0]),
        (conv_b, g_conv_b, m_conv_b, v_conv_b),
        (final_g2, g_final_g, m_final_g.reshape(1, D_MODEL), v_final_g.reshape(1, D_MODEL)),
    ])
    small = [small[3 * j:3 * j + 3] for j in range(6)]

    grads = {
        "norm_g": g_norm_g, "ada_w": ada[0], "ada_b": g_ada_b, "even_w_in": big["even_w_in"][0],
        "pool_w": big["pool_w"][0], "pool_scale": g_pool_scale, "even_w_out": big["even_w_out"][0],
        "odd_w_in": big["odd_w_in"][0], "conv_w": g_conv_w.reshape(conv_w.shape), "conv_b": g_conv_b,
        "odd_w_out": big["odd_w_out"][0], "final_g": g_final_g.reshape(D_MODEL),
    }
    rest = []
    for idx in range(3):
        rest += [
            small[0][idx], ada[1 + idx], small[1][idx], big["even_w_in"][1 + idx], big["pool_w"][1 + idx],
            small[2][idx], big["even_w_out"][1 + idx], big["odd_w_in"][1 + idx],
            small[3][idx].reshape(conv_w.shape), small[4][idx], big["odd_w_out"][1 + idx],
            small[5][idx].reshape(D_MODEL),
        ]
    order = ["norm_g", "ada_w", "ada_b", "even_w_in", "pool_w", "pool_scale", "even_w_out", "odd_w_in",
             "conv_w", "conv_b", "odd_w_out", "final_g"]
    return (loss, grad_x, *[grads[n] for n in order], *rest)
```

```python
import jax
import jax.numpy as jnp
from jax import lax
from jax.experimental import pallas as pl
from jax.experimental.pallas import tpu as pltpu

F32 = jnp.float32
BF16 = jnp.bfloat16
SDS = jax.ShapeDtypeStruct
MESH = pl.DeviceIdType.MESH

N_DEV = 8
D_MODEL = 1024
D_INNER = 2048
D_POOL = 1024
D_SB = 1024
N_GROUPS = 4
POOL_GROUP = 256
HEAD_DIM = 64
LANES = 128
D_IN_EVEN = 6144
D_IN_ODD = 8192
EPS = 1e-6
ADAM_LR = 0.001
ADAM_B1 = 0.9
ADAM_B2 = 0.999
ADAM_EPS = 1e-08
ADAM_WD = 0.01
ADAM_STEP = 10

ROW_TILE = 256
ATT_TILE = 256
HALO = 16
VMEM_LIMIT = 48 * 1024 * 1024
SLAB_ROWS = 32
WGRAD_BLOCK = 256


def _params(n_axes):
    return pltpu.CompilerParams(dimension_semantics=("arbitrary",) * n_axes, vmem_limit_bytes=VMEM_LIMIT)


def _sigmoid(x):
    return 1.0 / (1.0 + jnp.exp(-x))


def _dot(a, b):
    return jnp.dot(a, b, preferred_element_type=F32)


def _dot_nt(a, b):
    return lax.dot_general(a, b, (((1,), (1,)), ((), ())), preferred_element_type=F32)


def _dot_tn(a, b):
    return lax.dot_general(a, b, (((0,), (0,)), ((), ())), preferred_element_type=F32)


def _mm(name, a, b, *, grid, a_spec, b_spec, o_spec, o_shape, o_dtype, dot, acc_axis=None, acc_shape=None):
    n_acc = grid[acc_axis] if acc_axis is not None else 1

    def body(a_ref, b_ref, o_ref, *scratch):
        prod = dot(a_ref[...], b_ref[...])
        if acc_axis is None:
            o_ref[...] = prod.astype(o_dtype)
        else:
            acc = scratch[0]
            k = pl.program_id(acc_axis)

            @pl.when(k == 0)
            def _():
                acc[...] = prod

            @pl.when(k > 0)
            def _():
                acc[...] += prod

            @pl.when(k == n_acc - 1)
            def _():
                o_ref[...] = acc[...].astype(o_dtype)

    scratch = [] if acc_axis is None else [pltpu.VMEM(acc_shape, F32)]
    return pl.pallas_call(
        body, name=name, grid=grid, in_specs=[a_spec, b_spec], out_specs=o_spec,
        out_shape=SDS(o_shape, o_dtype), scratch_shapes=scratch, compiler_params=_params(len(grid)),
    )(a, b)


def _proj_in(name, h, wg):
    s = h.shape[0]
    cn = wg.shape[2]
    tm = min(s, ROW_TILE)

    def body(a_ref, w_ref, o_ref):
        a = a_ref[...]
        for d in range(N_DEV):
            o_ref[:, d * cn:(d + 1) * cn] = _dot(a, w_ref[d]).astype(BF16)

    return pl.pallas_call(
        body, name=name, grid=(s // tm,),
        in_specs=[pl.BlockSpec((tm, D_MODEL), lambda i: (i, 0)),
                  pl.BlockSpec((N_DEV, D_MODEL, cn), lambda i: (0, 0, 0), pipeline_mode=pl.Buffered(1))],
        out_specs=pl.BlockSpec((tm, N_DEV * cn), lambda i: (i, 0)),
        out_shape=SDS((s, N_DEV * cn), BF16), compiler_params=_params(1),
    )(h, wg)


GATHER_ROWS = 1024


def _proj_in_gather(name, h, w_shard, order):
    s = h.shape[0]
    k_dim, cn = w_shard.shape
    tm = min(s, GATHER_ROWS)
    n_i = s // tm

    def body(ord_ref, a_ref, w_ref, o_ref, w_buf, send_sems, recv_sems, local_sem):
        dd = pl.program_id(0)
        i = pl.program_id(1)
        x, y, c = _place()
        me, sibling = (x, y, c), (x, y, 1 - c)
        chips = [(1 - x, y), (x, 1 - y), (1 - x, 1 - y)]

        def copy(k, block, to, from_src=False):
            slot = w_buf.at[_dev_index(block)]
            return pltpu.make_async_remote_copy(
                src_ref=w_ref if from_src else slot, dst_ref=slot, send_sem=send_sems.at[k], recv_sem=recv_sems.at[k],
                device_id=to, device_id_type=MESH)

        mine = pltpu.make_async_copy(w_ref, w_buf.at[_dev_index(me)], local_sem)
        first = [copy(0, me, sibling, True)] + [copy(1 + j, me, (*chip, c), True) for j, chip in enumerate(chips)]
        passed = [copy(4 + j, (*chip, c), sibling) for j, chip in enumerate(chips)]

        @pl.when((dd == 0) & (i == 0))
        def _():
            mine.start()
            for cp in first:
                cp.start()
            mine.wait()

        @pl.when((dd == 1) & (i == 0))
        def _():
            copy(0, sibling, me).wait_recv()

        for j, chip in enumerate(chips):
            @pl.when((dd == 2 + j) & (i == 0))
            def _(j=j, chip=chip):
                copy(1 + j, (*chip, c), me).wait_recv()
                passed[j].start()

            @pl.when((dd == 5 + j) & (i == 0))
            def _(j=j, chip=chip):
                copy(4 + j, (*chip, 1 - c), me).wait_recv()

        o_ref[...] = _dot(a_ref[...], w_buf[ord_ref[dd]]).astype(BF16)

        @pl.when((dd == N_DEV - 1) & (i == n_i - 1))
        def _():
            for cp in first + passed:
                cp.wait_send()

    return pl.pallas_call(
        body, name=name,
        grid_spec=pltpu.PrefetchScalarGridSpec(
            num_scalar_prefetch=1, grid=(N_DEV, n_i),
            in_specs=[pl.BlockSpec((tm, k_dim), lambda dd, i, ord_ref: (i, 0)), HBM_SPEC],
            out_specs=pl.BlockSpec((tm, cn), lambda dd, i, ord_ref: (i, ord_ref[dd])),
            scratch_shapes=[pltpu.VMEM((N_DEV, k_dim, cn), BF16), pltpu.SemaphoreType.DMA((7,)),
                            pltpu.SemaphoreType.DMA((7,)), pltpu.SemaphoreType.DMA]),
        out_shape=SDS((s, N_DEV * cn), BF16), compiler_params=_params(2),
    )(order, h, w_shard)


def _proj_out(name, y, w):
    s = y.shape[0]
    tm = min(s, 512)
    return _mm(name, y, w, grid=(s // tm,),
               a_spec=pl.BlockSpec((tm, D_INNER), lambda i: (i, 0)),
               b_spec=pl.BlockSpec((D_INNER, D_MODEL), lambda i: (0, 0)),
               o_spec=pl.BlockSpec((tm, D_MODEL), lambda i: (i, 0)),
               o_shape=(s, D_MODEL), o_dtype=F32, dot=_dot)


def _proj_out_bwd(name, dyo, w):
    s = dyo.shape[0]
    tm = min(s, 512)
    return _mm(name, dyo, w, grid=(s // tm,),
               a_spec=pl.BlockSpec((tm, D_MODEL), lambda i: (i, 0)),
               b_spec=pl.BlockSpec((D_INNER, D_MODEL), lambda i: (0, 0)),
               o_spec=pl.BlockSpec((tm, D_INNER), lambda i: (i, 0)),
               o_shape=(s, D_INNER), o_dtype=F32, dot=_dot_nt)


def _wgrad_out(name, y, dyo):
    s = y.shape[0]
    ts = min(s, 512)
    return _mm(name, y, dyo, grid=(s // ts,),
               a_spec=pl.BlockSpec((ts, D_INNER), lambda k: (k, 0)),
               b_spec=pl.BlockSpec((ts, D_MODEL), lambda k: (k, 0)),
               o_spec=pl.BlockSpec((D_INNER, D_MODEL), lambda k: (0, 0)),
               o_shape=(D_INNER, D_MODEL), o_dtype=BF16, dot=_dot_tn, acc_axis=0, acc_shape=(D_INNER, D_MODEL))


def _proj_in_bwd_ln(name, parts, wt, x, dx_next, g, scale, resid=None, dws=()):
    s = x.shape[0]
    widths = [p.shape[1] for p in parts]
    offs = [sum(widths[:k]) for k in range(len(parts))]
    k_all = sum(widths)
    n_i = s // ROW_TILE
    n_p, n_r, n_side = len(parts), (2 if resid else 0), len(dws)
    w_all = wt.reshape(k_all, D_MODEL)

    def body(*refs):
        part_refs = refs[:n_p]
        w_ref, x_ref, dxn_ref, g_ref, sc_ref = refs[n_p:n_p + 5]
        resid_refs = refs[n_p + 5:n_p + 5 + n_r]
        srcs = refs[n_p + 5 + n_r:n_p + 5 + n_r + n_side]
        outs = refs[n_p + 5 + n_r + n_side:]
        dx_ref, dsh_ref, dsc_ref, dg_ref = outs[:4]
        resid_outs = outs[4:4 + n_r]
        dsts = outs[4 + n_r:4 + n_r + n_side]
        sems = outs[4 + n_r + n_side:]
        i = pl.program_id(0)
        if n_side:
            _side_exchange(srcs + dsts + sems, n_side, True, i == 0, i == n_i - 1, chips=True)

        @pl.when(i == 0)
        def _():
            dsh_ref[...] = jnp.zeros_like(dsh_ref)
            dsc_ref[...] = jnp.zeros_like(dsc_ref)
            dg_ref[...] = jnp.zeros_like(dg_ref)
            if resid:
                resid_outs[1][...] = jnp.zeros_like(resid_outs[1])

        dh_v = _dot(part_refs[0][...], w_ref[offs[0]:offs[0] + widths[0], :])
        for k in range(1, n_p):
            dh_v = dh_v + _dot(part_refs[k][...], w_ref[offs[k]:offs[k] + widths[k], :])
        xv = x_ref[...]
        g_v = g_ref[...]
        r = lax.rsqrt(jnp.mean(xv * xv, axis=-1, keepdims=True) + EPS)
        xn = xv * r
        dsh_ref[...] += jnp.sum(dh_v, axis=0, keepdims=True)
        dsc_ref[...] += jnp.sum(dh_v * (xn * g_v), axis=0, keepdims=True)
        dn = dh_v * (1.0 + sc_ref[...])
        dg_ref[...] += jnp.sum(dn * xn, axis=0, keepdims=True)
        dxh = dn * g_v
        dx = dxn_ref[...] + r * (dxh - xn * jnp.mean(dxh * xn, axis=-1, keepdims=True))
        dx_ref[...] = dx
        if resid:
            yo_ref, gt_ref = resid_refs
            resid_outs[0][...] = (dx * (1.0 + gt_ref[...])).astype(BF16)
            resid_outs[1][...] += jnp.sum(dx * yo_ref[...], axis=0, keepdims=True)

    row, vec = _row_spec(), _vec_spec()
    out = pl.pallas_call(
        body, name=name, grid=(n_i,),
        in_specs=[_row_spec(w) for w in widths]
        + [pl.BlockSpec((k_all, D_MODEL), lambda i: (0, 0), pipeline_mode=pl.Buffered(1)), row, row, vec, vec]
        + ([row, vec] if resid else []) + [HBM_SPEC] * n_side,
        out_specs=[row, vec, vec, vec] + ([row, vec] if resid else []) + [HBM_SPEC] * n_side,
        out_shape=[SDS((s, D_MODEL), F32)] + [SDS((1, D_MODEL), F32)] * 3
        + ([SDS((s, D_MODEL), BF16), SDS((1, D_MODEL), F32)] if resid else [])
        + [SDS(dw.shape, dw.dtype) for dw in dws],
        scratch_shapes=_peer_sems(n_side) if n_side else [],
        compiler_params=_params(1),
    )(*parts, w_all, x, dx_next, g, scale, *(resid or ()), *dws)
    return out[:4 + n_r], out[4 + n_r:]


def _wgrad_in(name, h_t, parts, blk):
    s = h_t.shape[1]
    widths = [p.shape[1] for p in parts]
    cn = sum(widths) // N_DEV
    per_dev = cn // blk
    starts = [sum(widths[:k]) // blk for k in range(len(parts))]
    counts = [w // blk for w in widths]
    n_blk = sum(counts)

    def body(a_ref, *rest):
        o_ref = rest[len(parts)]
        b = pl.program_id(0)
        for k in range(len(parts)):
            @pl.when((b >= starts[k]) & (b < starts[k] + counts[k]))
            def _(k=k):
                o_ref[...] = _dot(a_ref[...], rest[k][...]).astype(BF16)

    def part_spec(k):
        return pl.BlockSpec((s, blk), lambda b: (0, jnp.clip(b - starts[k], 0, counts[k] - 1)))

    return pl.pallas_call(
        body, name=name, grid=(n_blk,),
        in_specs=[pl.BlockSpec((D_MODEL, s), lambda b: (0, 0), pipeline_mode=pl.Buffered(1))]
        + [part_spec(k) for k in range(len(parts))],
        out_specs=pl.BlockSpec((None, D_MODEL, blk), lambda b: (b // per_dev, 0, b % per_dev)),
        out_shape=SDS((N_DEV, D_MODEL, cn), BF16), compiler_params=_params(1),
    )(h_t, *parts)


def _vec_spec():
    return pl.BlockSpec((1, D_MODEL), lambda i: (0, 0))


def _row_spec(width=D_MODEL, col=0):
    return pl.BlockSpec((ROW_TILE, width), lambda i: (i, col))


def _col_spec():
    return pl.BlockSpec((D_MODEL, ROW_TILE), lambda i: (0, i))


def _ln_mod(name, x, g, scale, shift):
    s = x.shape[0]

    def body(x_ref, g_ref, sc_ref, sh_ref, h_ref, ht_ref):
        xv = x_ref[...]
        r = lax.rsqrt(jnp.mean(xv * xv, axis=-1, keepdims=True) + EPS)
        n = (xv * r) * g_ref[...]
        h = (n * (1.0 + sc_ref[...]) + sh_ref[...]).astype(BF16)
        h_ref[...] = h
        ht_ref[...] = h.T

    return pl.pallas_call(
        body, name=name, grid=(s // ROW_TILE,),
        in_specs=[_row_spec(), _vec_spec(), _vec_spec(), _vec_spec()], out_specs=[_row_spec(), _col_spec()],
        out_shape=[SDS((s, D_MODEL), BF16), SDS((D_MODEL, s), BF16)], compiler_params=_params(1),
    )(x, g, scale, shift)


def _resid_ln_mod(name, x, yo, gate, g, scale, shift):
    s = x.shape[0]

    def body(x_ref, yo_ref, gt_ref, g_ref, sc_ref, sh_ref, xn_ref, h_ref, ht_ref):
        xv = x_ref[...] + (1.0 + gt_ref[...]) * yo_ref[...]
        xn_ref[...] = xv
        r = lax.rsqrt(jnp.mean(xv * xv, axis=-1, keepdims=True) + EPS)
        n = (xv * r) * g_ref[...]
        h = (n * (1.0 + sc_ref[...]) + sh_ref[...]).astype(BF16)
        h_ref[...] = h
        ht_ref[...] = h.T

    return pl.pallas_call(
        body, name=name, grid=(s // ROW_TILE,),
        in_specs=[_row_spec(), _row_spec(), _vec_spec(), _vec_spec(), _vec_spec(), _vec_spec()],
        out_specs=[_row_spec(), _row_spec(), _col_spec()],
        out_shape=[SDS((s, D_MODEL), F32), SDS((s, D_MODEL), BF16), SDS((D_MODEL, s), BF16)],
        compiler_params=_params(1),
    )(x, yo, gate, g, scale, shift)


def _final_loss(name, x1, yo1, gate1, gf, target):
    s = x1.shape[0]

    def body(x_ref, yo_ref, gt_ref, gf_ref, t_ref, dx_ref, dyo_ref, loss_ref, dgf_ref, dgt_ref):
        i = pl.program_id(0)

        @pl.when(i == 0)
        def _():
            loss_ref[...] = jnp.zeros_like(loss_ref)
            dgf_ref[...] = jnp.zeros_like(dgf_ref)
            dgt_ref[...] = jnp.zeros_like(dgt_ref)

        yo = yo_ref[...]
        one_gate = 1.0 + gt_ref[...]
        x2 = x_ref[...] + one_gate * yo
        r = lax.rsqrt(jnp.mean(x2 * x2, axis=-1, keepdims=True) + EPS)
        xn = x2 * r
        gf_v = gf_ref[...]
        err = xn * gf_v - t_ref[...]
        loss_ref[...] += 0.5 * jnp.sum(jnp.mean(err * err, axis=-1, keepdims=True))
        dout = err * (1.0 / D_MODEL)
        dgf_ref[...] += jnp.sum(dout * xn, axis=0, keepdims=True)
        dxn = dout * gf_v
        dx2 = r * (dxn - xn * jnp.mean(dxn * xn, axis=-1, keepdims=True))
        dx_ref[...] = dx2
        dyo_ref[...] = (dx2 * one_gate).astype(BF16)
        dgt_ref[...] += jnp.sum(dx2 * yo, axis=0, keepdims=True)

    return pl.pallas_call(
        body, name=name, grid=(s // ROW_TILE,),
        in_specs=[_row_spec(), _row_spec(), _vec_spec(), _vec_spec(), _row_spec()],
        out_specs=[_row_spec(), _row_spec(), pl.BlockSpec((1, LANES), lambda i: (0, 0)), _vec_spec(), _vec_spec()],
        out_shape=[SDS((s, D_MODEL), F32), SDS((s, D_MODEL), BF16), SDS((1, LANES), F32),
                   SDS((1, D_MODEL), F32), SDS((1, D_MODEL), F32)],
        compiler_params=_params(1),
    )(x1, yo1, gate1, gf, target)


POOL_WINDOWS = (2, 4, 8, 16)


def _window_sum(x, window, rows, backward):
    acc, step = x, 1
    while step < window:
        acc = acc + pltpu.roll(acc, step if backward else rows - step, axis=0)
        step *= 2
    return acc


def _pool_fwd(name, proj0, wp):
    s = proj0.shape[0]
    hb = ROW_TILE // HALO
    ext_rows = ROW_TILE + HALO

    def body(u_ref, halo_ref, w_ref, p_ref, y_ref):
        i = pl.program_id(0)
        t = i * ROW_TILE + lax.broadcasted_iota(jnp.int32, (ROW_TILE, 1), 0)
        for g, window in enumerate(POOL_WINDOWS):
            cols = slice(g * POOL_GROUP, (g + 1) * POOL_GROUP)
            u = u_ref[:, cols].astype(F32)
            halo = jnp.where(i == 0, 0.0, halo_ref[:, cols].astype(F32))
            ext = jnp.concatenate([halo, u], axis=0)
            win = _window_sum(ext, window, ext_rows, True)[HALO:, :]
            cnt = jnp.minimum(t + 1, window).astype(F32)
            p = (win / cnt - u).astype(BF16)
            p_ref[:, cols] = p
            y_ref[:, cols] = _dot(p, w_ref[g])

    return pl.pallas_call(
        body, name=name, grid=(s // ROW_TILE,),
        in_specs=[pl.BlockSpec((ROW_TILE, D_POOL), lambda i: (i, 0)),
                  pl.BlockSpec((HALO, D_POOL), lambda i: (jnp.maximum(i * hb - 1, 0), 0)),
                  pl.BlockSpec((N_GROUPS, POOL_GROUP, POOL_GROUP), lambda i: (0, 0, 0))],
        out_specs=[_row_spec(D_POOL), _row_spec(D_POOL)],
        out_shape=[SDS((s, D_POOL), BF16), SDS((s, D_POOL), F32)], compiler_params=_params(1),
    )(proj0, proj0, wp)


def _pool_bwd(name, dyp, p, wp):
    s = dyp.shape[0]
    hb = ROW_TILE // HALO
    n_hb = s // HALO
    n_tiles = s // ROW_TILE
    ext_rows = ROW_TILE + HALO

    def body(dy_ref, nxt_ref, p_ref, w_ref, du_ref, dw_ref):
        i = pl.program_id(0)

        @pl.when(i == 0)
        def _():
            dw_ref[...] = jnp.zeros_like(dw_ref)

        t = i * ROW_TILE + lax.broadcasted_iota(jnp.int32, (ext_rows, 1), 0)
        for g, window in enumerate(POOL_WINDOWS):
            cols = slice(g * POOL_GROUP, (g + 1) * POOL_GROUP)
            dy = dy_ref[:, cols]
            nxt = nxt_ref[:, cols]
            nxt = jnp.where(i == n_tiles - 1, jnp.zeros_like(nxt), nxt)
            dp = _dot_nt(jnp.concatenate([dy, nxt], axis=0), w_ref[g])
            cnt = jnp.minimum(t + 1, window).astype(F32)
            win = _window_sum(dp / cnt, window, ext_rows, False)[:ROW_TILE, :]
            du_ref[:, cols] = (win - dp[:ROW_TILE, :]).astype(BF16)
            dw_ref[g] += _dot_tn(p_ref[:, cols], dy)

    return pl.pallas_call(
        body, name=name, grid=(n_tiles,),
        in_specs=[_row_spec(D_POOL),
                  pl.BlockSpec((HALO, D_POOL), lambda i: (jnp.minimum((i + 1) * hb, n_hb - 1), 0)),
                  _row_spec(D_POOL),
                  pl.BlockSpec((N_GROUPS, POOL_GROUP, POOL_GROUP), lambda i: (0, 0, 0))],
        out_specs=[_row_spec(D_POOL), pl.BlockSpec((N_GROUPS, POOL_GROUP, POOL_GROUP), lambda i: (0, 0, 0))],
        out_shape=[SDS((s, D_POOL), BF16), SDS((N_GROUPS, POOL_GROUP, POOL_GROUP), F32)],
        compiler_params=_params(1),
    )(dyp, dyp, p, wp)


FWD_HEADS_PER_STEP = 8
BWD_HEADS_PER_STEP = 4
ATT_SCALE = 0.125
FWD_SKEW = 1


def _att_groups(nh):
    lanes = nh * HEAD_DIM
    return lanes, D_SB // lanes, D_POOL // lanes, (D_POOL + D_SB) // lanes, (D_POOL + 2 * D_SB) // lanes


def _att_consts():
    r = lax.broadcasted_iota(jnp.int32, (ATT_TILE, ATT_TILE), 0)
    c = lax.broadcasted_iota(jnp.int32, (ATT_TILE, ATT_TILE), 1)
    first = lax.broadcasted_iota(jnp.int32, (1, LANES), 1) < HEAD_DIM
    return r, c, first


def _pair(x, p):
    return x[:, p * LANES:(p + 1) * LANES]


def _one_head(x, first, hh):
    zero = jnp.zeros_like(x)
    return jnp.where(first, x, zero) if hh == 0 else jnp.where(first, zero, x)


def _neg_softplus(z):
    return -(jnp.maximum(z, 0.0) + jnp.log(1.0 + jnp.exp(-jnp.abs(z))))


def _side_exchange(side_refs, n_side, by_chunk, is_first, is_last, chips=False):
    ins, outs = side_refs[:n_side], side_refs[n_side:2 * n_side]
    sems = side_refs[2 * n_side:2 * n_side + 3]

    def copies():
        return _chip_copies(ins, outs, *sems) if chips else _peer_copies(ins, outs, *sems, by_chunk=by_chunk)

    @pl.when(is_first)
    def _():
        for cp in copies():
            cp.start()

    @pl.when(is_last)
    def _():
        for cp in copies():
            cp.wait()


def _attn_fwd(name, proj0, shards):
    s = proj0.shape[0]
    nq = s // ATT_TILE
    nh = FWD_HEADS_PER_STEP
    ATT_GROUP, N_ATT_GROUPS, Q_GRP, K_GRP, V_GRP = _att_groups(nh)
    n_side = len(shards)

    def body(q_ref, k_ref, v_ref, *rest):
        o_ref = rest[n_side]
        side = rest[:n_side] + rest[n_side + 1:]
        j = pl.program_id(0)
        i = pl.program_id(1)
        _side_exchange(side, n_side, False, (j == 0) & (i == 0), (j == N_ATT_GROUPS - 1) & (i == nq - 1))
        r, c, first = _att_consts()
        tri = (r >= c).astype(BF16)
        below = c < r
        q = q_ref[...] * ATT_SCALE
        qh = [_one_head(_pair(q, h // 2), first, h % 2) for h in range(nh)]

        def tile(kb, carry, diagonal):
            k0 = pl.multiple_of(kb * ATT_TILE, ATT_TILE)
            kt = k_ref[pl.ds(k0, ATT_TILE), :]
            vt = v_ref[pl.ds(k0, ATT_TILE), :]
            z, lf_b, a_b = [None] * nh, [None] * nh, [None] * nh
            out_c, out_acc = [None] * nh, [None] * nh
            for t in range(nh + 2 * FWD_SKEW):
                if t < nh:
                    z[t] = _dot_nt(qh[t], _pair(kt, t // 2))
                    lf = _neg_softplus(z[t])
                    if diagonal:
                        lf = jnp.where(below, lf, 0.0)
                    lf_b[t] = lf.astype(BF16)
                    out_c[t] = carry[t] + jnp.sum(lf, axis=1, keepdims=True)
                u = t - FWD_SKEW
                if 0 <= u < nh:
                    a = jnp.exp(z[u] + _dot(lf_b[u], tri) + carry[u])
                    if diagonal:
                        a = jnp.where(below, a, 0.0)
                    a_b[u] = a.astype(BF16)
                w = t - 2 * FWD_SKEW
                if 0 <= w < nh:
                    out_acc[w] = carry[nh + w] + _dot(a_b[w], _pair(vt, w // 2))
            return tuple(out_c + out_acc)

        init = tuple([jnp.zeros((ATT_TILE, 1), F32)] * nh + [jnp.zeros((ATT_TILE, LANES), F32)] * nh)
        carry = tile(i, init, True)
        carry = lax.fori_loop(1, i + 1, lambda n, cr: tile(i - n, cr, False), carry)
        for p in range(nh // 2):
            o_ref[:, p * LANES:(p + 1) * LANES] = jnp.where(first, carry[nh + 2 * p], carry[nh + 2 * p + 1])

    out = pl.pallas_call(
        body, name=name, grid=(N_ATT_GROUPS, nq),
        in_specs=[pl.BlockSpec((ATT_TILE, ATT_GROUP), lambda j, i: (i, Q_GRP + j)),
                  pl.BlockSpec((s, ATT_GROUP), lambda j, i: (0, K_GRP + j)),
                  pl.BlockSpec((s, ATT_GROUP), lambda j, i: (0, V_GRP + j))] + [HBM_SPEC] * n_side,
        out_specs=[pl.BlockSpec((ATT_TILE, ATT_GROUP), lambda j, i: (i, j))] + [HBM_SPEC] * n_side,
        out_shape=[SDS((s, D_SB), F32)] + [SDS((N_DEV,) + sh.shape, sh.dtype) for sh in shards],
        scratch_shapes=_peer_sems(n_side), compiler_params=_params(2),
    )(proj0, proj0, proj0, *shards)
    return out[0], out[1:]


def _attn_bwd(name, proj0, o, do, dws, shards):
    s = proj0.shape[0]
    nq = s // ATT_TILE
    nh = BWD_HEADS_PER_STEP
    ATT_GROUP, N_ATT_GROUPS, Q_GRP, K_GRP, V_GRP = _att_groups(nh)
    n1, n2 = len(dws), len(shards)
    n_side = n1 + n2

    def body(q_ref, k_ref, v_ref, o_ref, do_ref, *rest):
        dq_ref, dk_ref, dv_ref = rest[n_side:n_side + 3]
        dk_acc, dv_acc = rest[2 * n_side + 3:2 * n_side + 5]
        srcs, dsts, sems = rest[:n_side], rest[n_side + 3:2 * n_side + 3], rest[2 * n_side + 5:]
        j = pl.program_id(0)
        i = pl.program_id(1)
        is_first, is_last = (j == 0) & (i == 0), (j == N_ATT_GROUPS - 1) & (i == nq - 1)
        _side_exchange(srcs[:n1] + dsts[:n1] + sems[:3], n1, True, is_first, is_last)
        _side_exchange(srcs[n1:] + dsts[n1:] + sems[3:], n2, False, is_first, is_last)

        @pl.when(i == 0)
        def _():
            dk_acc[...] = jnp.zeros_like(dk_acc)
            dv_acc[...] = jnp.zeros_like(dv_acc)

        r, c, first = _att_consts()
        tri = (r >= c).astype(BF16)
        tri_p = (r <= c).astype(BF16)
        below = c < r
        q = q_ref[...] * ATT_SCALE
        do_b = do_ref[...].astype(BF16)
        do_o = do_b.astype(F32) * o_ref[...]
        qh = [_one_head(_pair(q, h // 2), first, h % 2) for h in range(nh)]
        doh = [_one_head(_pair(do_b, h // 2), first, h % 2) for h in range(nh)]
        dsum = [jnp.sum(_one_head(_pair(do_o, h // 2), first, h % 2), axis=1, keepdims=True) for h in range(nh)]

        def tile(kb, carry, diagonal):
            k0 = pl.multiple_of(kb * ATT_TILE, ATT_TILE)
            kt = k_ref[pl.ds(k0, ATT_TILE), :]
            vt = v_ref[pl.ds(k0, ATT_TILE), :]
            none = lambda: [None] * nh
            z, d_a, sig, lf_b, a_b, g, early, dz = none(), none(), none(), none(), none(), none(), none(), none()
            out_c1, out_c2, out_dq, dk_t, dv_t = none(), none(), none(), none(), none()
            for t in range(nh + 3):
                if t < nh:
                    z[t] = _dot_nt(qh[t], _pair(kt, t // 2))
                    d_a[t] = _dot_nt(doh[t], _pair(vt, t // 2))
                    lf = _neg_softplus(z[t])
                    sig[t] = jnp.exp(z[t] + lf)
                    if diagonal:
                        lf = jnp.where(below, lf, 0.0)
                    lf_b[t] = lf.astype(BF16)
                    out_c1[t] = carry[t] + jnp.sum(lf, axis=1, keepdims=True)
                u = t - 1
                if 0 <= u < nh:
                    a = jnp.exp(z[u] + _dot(lf_b[u], tri) + carry[u])
                    if diagonal:
                        a = jnp.where(below, a, 0.0)
                    a_b[u] = a.astype(BF16)
                    g[u] = a_b[u].astype(F32) * d_a[u]
                    g_sum = jnp.sum(g[u], axis=1, keepdims=True)
                    early[u] = dsum[u] - carry[nh + u] - g_sum
                    out_c2[u] = carry[nh + u] + g_sum
                w = t - 2
                if 0 <= w < nh:
                    upto = _dot(g[w].astype(BF16), tri_p)
                    dv_t[w] = _dot_tn(a_b[w], doh[w])
                    d = g[w] - sig[w] * (early[w] + upto)
                    if diagonal:
                        d = jnp.where(below, d, 0.0)
                    dz[w] = d.astype(BF16)
                y = t - 3
                if 0 <= y < nh:
                    out_dq[y] = carry[2 * nh + y] + _dot(dz[y], _pair(kt, y // 2))
                    dk_t[y] = _dot_tn(dz[y], qh[y])
            for p in range(nh // 2):
                dk_acc[pl.ds(k0, ATT_TILE), p * LANES:(p + 1) * LANES] += dk_t[2 * p] + dk_t[2 * p + 1]
                dv_acc[pl.ds(k0, ATT_TILE), p * LANES:(p + 1) * LANES] += dv_t[2 * p] + dv_t[2 * p + 1]
            return tuple(out_c1 + out_c2 + out_dq)

        init = tuple([jnp.zeros((ATT_TILE, 1), F32)] * (2 * nh) + [jnp.zeros((ATT_TILE, LANES), F32)] * nh)
        carry = tile(i, init, True)
        carry = lax.fori_loop(1, i + 1, lambda n, cr: tile(i - n, cr, False), carry)
        for p in range(nh // 2):
            dq_p = jnp.where(first, carry[2 * nh + 2 * p], carry[2 * nh + 2 * p + 1]) * ATT_SCALE
            dq_ref[:, p * LANES:(p + 1) * LANES] = dq_p.astype(BF16)

        @pl.when(i == nq - 1)
        def _():
            dk_ref[...] = dk_acc[...].astype(BF16)
            dv_ref[...] = dv_acc[...].astype(BF16)

    tile_spec = pl.BlockSpec((ATT_TILE, ATT_GROUP), lambda j, i: (i, j))
    full = pl.BlockSpec((s, ATT_GROUP), lambda j, i: (0, j))
    out = pl.pallas_call(
        body, name=name, grid=(N_ATT_GROUPS, nq),
        in_specs=[pl.BlockSpec((ATT_TILE, ATT_GROUP), lambda j, i: (i, Q_GRP + j)),
                  pl.BlockSpec((s, ATT_GROUP), lambda j, i: (0, K_GRP + j)),
                  pl.BlockSpec((s, ATT_GROUP), lambda j, i: (0, V_GRP + j)),
                  tile_spec, tile_spec] + [HBM_SPEC] * n_side,
        out_specs=[tile_spec, full, full] + [HBM_SPEC] * n_side,
        out_shape=[SDS((s, D_SB), BF16)] * 3 + [SDS(dw.shape, dw.dtype) for dw in dws]
        + [SDS((N_DEV,) + sh.shape, sh.dtype) for sh in shards],
        scratch_shapes=[pltpu.VMEM((s, ATT_GROUP), F32), pltpu.VMEM((s, ATT_GROUP), F32)] + _peer_sems(n1) + _peer_sems(n2),
        compiler_params=_params(2),
    )(proj0, proj0, proj0, o, do, *dws, *shards)
    return out[0], out[1], out[2], out[3:3 + n1], out[3 + n1:]


GATE0_COL = (D_POOL + 3 * D_SB) // D_INNER


def _gate_fwd0(name, yp_raw, o, proj0, ps):
    s = o.shape[0]

    def body(yp_ref, o_ref, gt_ref, ps_ref, y_ref):
        gt = gt_ref[...].astype(F32)
        sg = gt * _sigmoid(gt)
        y_ref[:, :D_POOL] = (yp_ref[...] * ps_ref[...] * sg[:, :D_POOL]).astype(BF16)
        y_ref[:, D_POOL:] = (o_ref[...] * sg[:, D_POOL:]).astype(BF16)

    return pl.pallas_call(
        body, name=name, grid=(s // ROW_TILE,),
        in_specs=[_row_spec(), _row_spec(), _row_spec(D_INNER, GATE0_COL), _vec_spec()],
        out_specs=_row_spec(D_INNER),
        out_shape=SDS((s, D_INNER), BF16), compiler_params=_params(1),
    )(yp_raw, o, proj0, ps)


def _dsilu(x):
    sg = _sigmoid(x)
    return sg * (1.0 + x * (1.0 - sg))


def _gate_bwd0(name, dymix, yp_raw, o, proj0, ps):
    s = o.shape[0]

    def body(dy_ref, yp_ref, o_ref, gt_ref, ps_ref, dyp_ref, do_ref, dgt_ref, dps_ref):
        i = pl.program_id(0)

        @pl.when(i == 0)
        def _():
            dps_ref[...] = jnp.zeros_like(dps_ref)

        gt = gt_ref[...].astype(F32)
        dy = dy_ref[...]
        sg = gt * _sigmoid(gt)
        dsg = _dsilu(gt)
        dcat = dy * sg
        yp = yp_ref[...]
        ps_v = ps_ref[...]
        dyp_ref[...] = (dcat[:, :D_POOL] * ps_v).astype(BF16)
        do_ref[...] = dcat[:, D_POOL:]
        dps_ref[...] += jnp.sum(dcat[:, :D_POOL] * yp, axis=0, keepdims=True)
        dgt_ref[:, :D_POOL] = (dy[:, :D_POOL] * (yp * ps_v) * dsg[:, :D_POOL]).astype(BF16)
        dgt_ref[:, D_POOL:] = (dy[:, D_POOL:] * o_ref[...] * dsg[:, D_POOL:]).astype(BF16)

    return pl.pallas_call(
        body, name=name, grid=(s // ROW_TILE,),
        in_specs=[_row_spec(D_INNER), _row_spec(), _row_spec(), _row_spec(D_INNER, GATE0_COL), _vec_spec()],
        out_specs=[_row_spec(), _row_spec(), _row_spec(D_INNER), _vec_spec()],
        out_shape=[SDS((s, D_POOL), BF16), SDS((s, D_SB), F32), SDS((s, D_INNER), BF16), SDS((1, D_POOL), F32)],
        compiler_params=_params(1),
    )(dymix, yp_raw, o, proj0, ps)


CONV_HALO = 16


def _conv_fwd(name, proj1, cw, cb):
    s = proj1.shape[0]
    hb = ROW_TILE // CONV_HALO
    ext_rows = ROW_TILE + CONV_HALO

    def body(gb_ref, gc_ref, u_ref, gt_ref, gch_ref, uh_ref, cw_ref, cb_ref, y_ref):
        i = pl.program_id(0)
        uc = gc_ref[...].astype(F32) * u_ref[...].astype(F32)
        halo = jnp.where(i == 0, 0.0, gch_ref[...].astype(F32) * uh_ref[...].astype(F32))
        ext = jnp.concatenate([halo, uc], axis=0)
        uc1 = pltpu.roll(ext, 1, axis=0)[CONV_HALO:, :]
        uc2 = pltpu.roll(ext, 2, axis=0)[CONV_HALO:, :]
        cw_v = cw_ref[...]
        conv = cb_ref[...] + cw_v[0:1, :] * uc2 + cw_v[1:2, :] * uc1 + cw_v[2:3, :] * uc
        gt = gt_ref[...].astype(F32)
        y_ref[...] = (gb_ref[...].astype(F32) * conv * (gt * _sigmoid(gt))).astype(BF16)

    def tile(part):
        return pl.BlockSpec((ROW_TILE, D_INNER), lambda i: (i, part))

    def halo(part):
        return pl.BlockSpec((CONV_HALO, D_INNER), lambda i: (jnp.maximum(i * hb - 1, 0), part))

    return pl.pallas_call(
        body, name=name, grid=(s // ROW_TILE,),
        in_specs=[tile(0), tile(1), tile(2), tile(3), halo(1), halo(2),
                  pl.BlockSpec((3, D_INNER), lambda i: (0, 0)), pl.BlockSpec((1, D_INNER), lambda i: (0, 0))],
        out_specs=pl.BlockSpec((ROW_TILE, D_INNER), lambda i: (i, 0)),
        out_shape=SDS((s, D_INNER), BF16), compiler_params=_params(1),
    )(proj1, proj1, proj1, proj1, proj1, proj1, cw, cb)


def _conv_bwd(name, dymix, proj1, cw, cb):
    s = proj1.shape[0]
    hb = ROW_TILE // CONV_HALO
    n_hb = s // CONV_HALO
    n_tiles = s // ROW_TILE
    ext_rows = ROW_TILE + CONV_HALO

    def body(dy_ref, gb_ref, gc_ref, u_ref, gt_ref, gch_ref, uh_ref, dyn_ref, gbn_ref, gtn_ref, cw_ref, cb_ref,
             dproj_ref, dcw_ref, dcb_ref):
        i = pl.program_id(0)

        @pl.when(i == 0)
        def _():
            dcw_ref[...] = jnp.zeros_like(dcw_ref)
            dcb_ref[...] = jnp.zeros_like(dcb_ref)

        gc = gc_ref[...].astype(F32)
        u = u_ref[...].astype(F32)
        gb = gb_ref[...].astype(F32)
        gt = gt_ref[...].astype(F32)
        dy = dy_ref[...]
        uc = gc * u
        halo = jnp.where(i == 0, 0.0, gch_ref[...].astype(F32) * uh_ref[...].astype(F32))
        ext = jnp.concatenate([halo, uc], axis=0)
        uc1 = pltpu.roll(ext, 1, axis=0)[CONV_HALO:, :]
        uc2 = pltpu.roll(ext, 2, axis=0)[CONV_HALO:, :]
        cw_v = cw_ref[...]
        w0, w1, w2 = cw_v[0:1, :], cw_v[1:2, :], cw_v[2:3, :]
        conv = cb_ref[...] + w0 * uc2 + w1 * uc1 + w2 * uc
        sig = _sigmoid(gt)
        sg = gt * sig
        dconv = dy * gb * sg
        gtn = gtn_ref[...].astype(F32)
        dconv_next = jnp.where(i == n_tiles - 1, 0.0, dyn_ref[...] * gbn_ref[...].astype(F32) * (gtn * _sigmoid(gtn)))
        dext = jnp.concatenate([dconv, dconv_next], axis=0)
        dconv_p1 = pltpu.roll(dext, ext_rows - 1, axis=0)[:ROW_TILE, :]
        dconv_p2 = pltpu.roll(dext, ext_rows - 2, axis=0)[:ROW_TILE, :]
        duc = w2 * dconv + w1 * dconv_p1 + w0 * dconv_p2
        dproj_ref[:, 0:D_INNER] = (dy * conv * sg).astype(BF16)
        dproj_ref[:, D_INNER:2 * D_INNER] = (duc * u).astype(BF16)
        dproj_ref[:, 2 * D_INNER:3 * D_INNER] = (duc * gc).astype(BF16)
        dproj_ref[:, 3 * D_INNER:] = (dy * gb * conv * (sig + sg * (1.0 - sig))).astype(BF16)
        dcw_ref[0:1, :] += jnp.sum(dconv * uc2, axis=0, keepdims=True)
        dcw_ref[1:2, :] += jnp.sum(dconv * uc1, axis=0, keepdims=True)
        dcw_ref[2:3, :] += jnp.sum(dconv * uc, axis=0, keepdims=True)
        dcb_ref[...] += jnp.sum(dconv, axis=0, keepdims=True)

    def tile(part):
        return pl.BlockSpec((ROW_TILE, D_INNER), lambda i: (i, part))

    def prev(part):
        return pl.BlockSpec((CONV_HALO, D_INNER), lambda i: (jnp.maximum(i * hb - 1, 0), part))

    def nxt(part):
        return pl.BlockSpec((CONV_HALO, D_INNER), lambda i: (jnp.minimum((i + 1) * hb, n_hb - 1), part))

    whole = lambda rows: pl.BlockSpec((rows, D_INNER), lambda i: (0, 0))
    return pl.pallas_call(
        body, name=name, grid=(n_tiles,),
        in_specs=[tile(0), tile(0), tile(1), tile(2), tile(3), prev(1), prev(2), nxt(0), nxt(0), nxt(3),
                  whole(3), whole(1)],
        out_specs=[pl.BlockSpec((ROW_TILE, 4 * D_INNER), lambda i: (i, 0)), whole(3), whole(1)],
        out_shape=[SDS((s, 4 * D_INNER), BF16), SDS((3, D_INNER), F32), SDS((1, D_INNER), F32)],
        compiler_params=_params(1),
    )(dymix, proj1, proj1, proj1, proj1, proj1, proj1, dymix, proj1, proj1, cw, cb)


def _place():
    x, y, c = lax.axis_index("x"), lax.axis_index("y"), lax.axis_index("c")
    return x, y, c


def _flip(x, y, c, k):
    fx, fy, fc = (k >> 2) & 1, (k >> 1) & 1, k & 1
    return (1 - x if fx else x, 1 - y if fy else y, 1 - c if fc else c)


def _dev_index(p):
    return 4 * p[0] + 2 * p[1] + p[2]


HBM_SPEC = pl.BlockSpec(memory_space=pltpu.HBM)
VMEM_SPEC = pl.BlockSpec(memory_space=pltpu.VMEM)


def _peer_copies(ins, outs, send_sems, recv_sems, local_sems, by_chunk):
    x, y, c = _place()
    my = _dev_index((x, y, c))
    copies = []
    for w in range(len(ins)):
        copies.append(pltpu.make_async_copy(ins[w].at[my] if by_chunk else ins[w], outs[w].at[my], local_sems.at[w]))
        for k in range(1, N_DEV):
            peer = _flip(x, y, c, k)
            copies.append(pltpu.make_async_remote_copy(
                src_ref=ins[w].at[_dev_index(peer)] if by_chunk else ins[w], dst_ref=outs[w].at[my],
                send_sem=send_sems.at[7 * w + k - 1], recv_sem=recv_sems.at[7 * w + k - 1],
                device_id=peer, device_id_type=MESH))
    return copies


def _peer_sems(n_w):
    return [pltpu.SemaphoreType.DMA((7 * n_w,)), pltpu.SemaphoreType.DMA((7 * n_w,)), pltpu.SemaphoreType.DMA((n_w,))]


def _chip_index(p):
    return 2 * p[0] + p[1]


def _chip_copies(ins, outs, send_sems, recv_sems, local_sems):
    x, y, c = _place()
    mine = _chip_index((x, y))
    copies = []
    for w in range(len(ins)):
        copies.append(pltpu.make_async_copy(ins[w].at[mine], outs[w].at[mine], local_sems.at[w]))
        for k in (2, 4, 6):
            peer = _flip(x, y, c, k)
            copies.append(pltpu.make_async_remote_copy(
                src_ref=ins[w].at[_chip_index(peer)], dst_ref=outs[w].at[mine],
                send_sem=send_sems.at[7 * w + k - 1], recv_sem=recv_sems.at[7 * w + k - 1],
                device_id=peer, device_id_type=MESH))
    return copies


def _sibling_exchange(dw):
    n_chips = N_DEV // 2

    def body(dw_ref, out_ref, send_sems, recv_sems):
        x, y, c = _place()
        sibling = (x, y, 1 - c)
        copies = []
        for ch in range(n_chips):
            copies.append(pltpu.make_async_remote_copy(
                src_ref=dw_ref.at[2 * ch + (1 - c)], dst_ref=out_ref.at[ch],
                send_sem=send_sems.at[ch], recv_sem=recv_sems.at[ch], device_id=sibling, device_id_type=MESH))
        for cp in copies:
            cp.start()
        for cp in copies:
            cp.wait()

    return pl.pallas_call(
        body, name="sibling_exchange", out_shape=SDS((n_chips,) + dw.shape[1:], dw.dtype),
        in_specs=[HBM_SPEC], out_specs=HBM_SPEC,
        scratch_shapes=[pltpu.SemaphoreType.DMA((n_chips,)), pltpu.SemaphoreType.DMA((n_chips,))],
    )(dw)


def _sibling_sum(name, dw, got, core):
    n_chips, rows, cols = got.shape
    tr = min(rows, 256)

    def body(core_ref, a_ref, b_ref, o_ref):
        o_ref[...] = (a_ref[...].astype(F32) + b_ref[...].astype(F32)).astype(BF16)

    return pl.pallas_call(
        body, name=name,
        grid_spec=pltpu.PrefetchScalarGridSpec(
            num_scalar_prefetch=1, grid=(n_chips, rows // tr),
            in_specs=[pl.BlockSpec((None, tr, cols), lambda ch, i, core_ref: (2 * ch + core_ref[0], i, 0)),
                      pl.BlockSpec((None, tr, cols), lambda ch, i, core_ref: (ch, i, 0))],
            out_specs=pl.BlockSpec((None, tr, cols), lambda ch, i, core_ref: (ch, i, 0))),
        out_shape=SDS(got.shape, BF16), compiler_params=_params(2),
    )(core, dw, got)


ADA_COLS = 3 * D_MODEL // N_DEV


def _ada_forward(c_row, conv_w, conv_b, ada_w, ada_b):
    cw_cols = conv_w.shape[1]

    def body(c_ref, cw_ref, cb_ref, aw_ref, ab_ref, m_ref, cs_ref, cwf_ref, cbf_ref,
             slab, gath, part, land, send_sems, recv_sems):
        x, y, c = _place()
        my = _dev_index((x, y, c))
        slab[...] = jnp.zeros_like(slab)
        slab[0:1, :] = c_ref[...]
        slab[1:4, 0:cw_cols] = cw_ref[...]
        slab[4:5, 0:cw_cols] = cb_ref[...]
        gath[my] = slab[...]
        sends = []
        for k in range(1, N_DEV):
            peer = _flip(x, y, c, k)
            cp = pltpu.make_async_remote_copy(
                src_ref=slab, dst_ref=gath.at[my], send_sem=send_sems.at[k - 1], recv_sem=recv_sems.at[k - 1],
                device_id=peer, device_id_type=MESH)
            cp.start()
            sends.append(cp)
        for cp in sends:
            cp.wait()
        for d in range(N_DEV):
            c_d = gath[d, 0:1, :]
            cs_ref[d:d + 1, :] = c_d * _sigmoid(c_d)
            cwf_ref[:, d * cw_cols:(d + 1) * cw_cols] = gath[d, 1:4, 0:cw_cols]
            cbf_ref[:, d * cw_cols:(d + 1) * cw_cols] = gath[d, 4:5, 0:cw_cols]
        cs = cs_ref[...]
        part[...] = jnp.zeros_like(part)
        for layer in range(2):
            m_part = jnp.dot(cs, aw_ref[layer], preferred_element_type=F32, precision=lax.Precision.HIGHEST)
            for d in range(N_DEV):
                part[d, layer:layer + 1, :] = m_part[d:d + 1, :]
        land[my] = part[my]
        sends = []
        for k in range(1, N_DEV):
            peer = _flip(x, y, c, k)
            cp = pltpu.make_async_remote_copy(
                src_ref=part.at[_dev_index(peer)], dst_ref=land.at[my],
                send_sem=send_sems.at[6 + k], recv_sem=recv_sems.at[6 + k],
                device_id=peer, device_id_type=MESH)
            cp.start()
            sends.append(cp)
        for cp in sends:
            cp.wait()
        for d in range(N_DEV):
            cols = slice(d * ADA_COLS, (d + 1) * ADA_COLS)
            m_ref[:, cols] = land[d, 0:2, :] + ab_ref[:, cols]

    return pl.pallas_call(
        body, name="ada_forward",
        out_shape=[SDS((2, 3 * D_MODEL), F32), SDS((N_DEV, D_MODEL), F32), SDS((3, N_DEV * cw_cols), F32),
                   SDS((1, N_DEV * cw_cols), F32)],
        in_specs=[VMEM_SPEC] * 5, out_specs=[VMEM_SPEC] * 4,
        scratch_shapes=[pltpu.VMEM((8, D_MODEL), F32), pltpu.VMEM((N_DEV, 8, D_MODEL), F32),
                        pltpu.VMEM((N_DEV, 8, ADA_COLS), F32), pltpu.VMEM((N_DEV, 8, ADA_COLS), F32),
                        pltpu.SemaphoreType.DMA((14,)), pltpu.SemaphoreType.DMA((14,))],
        compiler_params=pltpu.CompilerParams(vmem_limit_bytes=VMEM_LIMIT),
    )(c_row, conv_w, conv_b, ada_w, ada_b)


def _small_grads(slab):
    def body(slab_ref, gath_ref, tot_ref, send_sems, recv_sems):
        x, y, c = _place()
        my = _dev_index((x, y, c))
        gath_ref[my] = slab_ref[...]
        sends = []
        for k in range(1, N_DEV):
            peer = _flip(x, y, c, k)
            cp = pltpu.make_async_remote_copy(
                src_ref=slab_ref, dst_ref=gath_ref.at[my], send_sem=send_sems.at[k - 1], recv_sem=recv_sems.at[k - 1],
                device_id=peer, device_id_type=MESH)
            cp.start()
            sends.append(cp)
        for cp in sends:
            cp.wait()
        tot = gath_ref[0]
        for d in range(1, N_DEV):
            tot = tot + gath_ref[d]
        tot_ref[...] = tot

    return pl.pallas_call(
        body, name="small_grads",
        out_shape=[SDS((N_DEV, SLAB_ROWS, D_MODEL), F32), SDS((SLAB_ROWS, D_MODEL), F32)],
        in_specs=[VMEM_SPEC], out_specs=[VMEM_SPEC] * 2,
        scratch_shapes=[pltpu.SemaphoreType.DMA((7,)), pltpu.SemaphoreType.DMA((7,))],
    )(slab)


def _adamw_math(w, g, m, v):
    m = ADAM_B1 * m + (1.0 - ADAM_B1) * g
    v = ADAM_B2 * v + (1.0 - ADAM_B2) * jnp.square(g)
    m_hat = m / (1.0 - ADAM_B1 ** ADAM_STEP)
    v_hat = v / (1.0 - ADAM_B2 ** ADAM_STEP)
    delta = -ADAM_LR * (m_hat / (jnp.sqrt(v_hat) + ADAM_EPS) + ADAM_WD * w)
    return delta, m, v


def _sum_adamw(name, recv, w, m, v):
    rows, cols = w.shape
    tr = min(rows, 256)
    n_slots = recv.shape[0]

    def body(r_ref, w_ref, m_ref, v_ref, g_ref, d_ref, nm_ref, nv_ref):
        g = r_ref[0].astype(F32)
        for d in range(1, n_slots):
            g = g + r_ref[d].astype(F32)
        g_ref[...] = g
        d_ref[...], nm_ref[...], nv_ref[...] = _adamw_math(w_ref[...], g, m_ref[...], v_ref[...])

    blk = pl.BlockSpec((tr, cols), lambda i: (i, 0))
    return pl.pallas_call(
        body, name=name, grid=(rows // tr,),
        in_specs=[pl.BlockSpec((n_slots, tr, cols), lambda i: (0, i, 0)), blk, blk, blk],
        out_specs=[blk] * 4, out_shape=[SDS((rows, cols), F32)] * 4, compiler_params=_params(1),
    )(recv, w, m, v)


def _ada_w_adamw(name, cs_t, dm_cols, w, m, v):
    def body(cs_ref, dm_ref, w_ref, m_ref, v_ref, g_ref, d_ref, nm_ref, nv_ref):
        cs = cs_ref[...]
        dm = dm_ref[...]
        g = cs[:, 0:1] * dm[0:1, :]
        for b in range(1, N_DEV):
            g = g + cs[:, b:b + 1] * dm[b:b + 1, :]
        g_ref[...] = g
        d_ref[...], nm_ref[...], nv_ref[...] = _adamw_math(w_ref[...], g, m_ref[...], v_ref[...])

    blk = pl.BlockSpec((None, D_MODEL, ADA_COLS), lambda l: (l, 0, 0))
    return pl.pallas_call(
        body, name=name, grid=(2,),
        in_specs=[pl.BlockSpec((D_MODEL, N_DEV), lambda l: (0, 0)),
                  pl.BlockSpec((None, N_DEV, ADA_COLS), lambda l: (l, 0, 0)), blk, blk, blk],
        out_specs=[blk] * 4, out_shape=[SDS((2, D_MODEL, ADA_COLS), F32)] * 4, compiler_params=_params(1),
    )(cs_t, dm_cols, w, m, v)


def _small_adamw(name, triples):
    n = len(triples)

    def body(*refs):
        ins, outs = refs[:4 * n], refs[4 * n:]
        for j in range(n):
            w_ref, g_ref, m_ref, v_ref = ins[4 * j:4 * j + 4]
            d, nm, nv = _adamw_math(w_ref[...], g_ref[...], m_ref[...], v_ref[...])
            outs[3 * j][...] = d
            outs[3 * j + 1][...] = nm
            outs[3 * j + 2][...] = nv

    flat = [a for t in triples for a in t]
    return pl.pallas_call(
        body, name=name,
        out_shape=[SDS(t[0].shape, F32) for t in triples for _ in range(3)],
        in_specs=[VMEM_SPEC] * (4 * n), out_specs=[VMEM_SPEC] * (3 * n),
    )(*flat)


def kernel(x, c, norm_g, ada_w, ada_b, even_w_in, pool_w, pool_scale, even_w_out, odd_w_in, conv_w, conv_b, odd_w_out, final_g, loss_target, m_norm_g, m_ada_w, m_ada_b, m_even_w_in, m_pool_w, m_pool_scale, m_even_w_out, m_odd_w_in, m_conv_w, m_conv_b, m_odd_w_out, m_final_g, v_norm_g, v_ada_w, v_ada_b, v_even_w_in, v_pool_w, v_pool_scale, v_even_w_out, v_odd_w_in, v_conv_w, v_conv_b, v_odd_w_out, v_final_g):
    seq = x.shape[1]
    x0 = x[0]
    target = loss_target[0]
    final_g2 = final_g.reshape(1, D_MODEL)

    w_in_e = even_w_in[0]
    w_out_e = even_w_out[0]
    w_in_o = odd_w_in[0]
    w_out_o = odd_w_out[0]
    w_pool = pool_w[0].reshape(N_GROUPS * 32, POOL_GROUP)
    shards = [w.astype(BF16) for w in (w_in_e, w_out_e, w_in_o, w_out_o, w_pool)]

    m_vec, cs_all, conv_w_full, conv_b_full = _ada_forward(c, conv_w[0], conv_b, ada_w, ada_b)
    shift = [m_vec[l:l + 1, 0:D_MODEL] for l in range(2)]
    scale = [m_vec[l:l + 1, D_MODEL:2 * D_MODEL] for l in range(2)]
    gate = [m_vec[l:l + 1, 2 * D_MODEL:] for l in range(2)]
    ng = [norm_g[l:l + 1] for l in range(2)]

    h0, h0_t = _ln_mod("ln_mod0", x0, ng[0], scale[0], shift[0])
    ax, ay, ac = lax.axis_index("x"), lax.axis_index("y"), lax.axis_index("c")
    arrival = [(ax, ay, ac), (ax, ay, 1 - ac)]
    arrival += [(cx, cy, cc) for cc in (ac, 1 - ac) for cx, cy in ((1 - ax, ay), (ax, 1 - ay), (1 - ax, 1 - ay))]
    order = jnp.stack([_dev_index(p) for p in arrival]).astype(jnp.int32)
    proj0 = _proj_in_gather("proj_in0", h0, shards[0], order)
    o, (wg_out_e, wg_in_o, wg_out_o, wg_pool, wt_in_o) = _attn_fwd("attn_fwd", proj0, shards[1:] + [shards[2].T])
    wf_out_e = wg_out_e.reshape(D_INNER, D_MODEL)
    wf_out_o = wg_out_o.reshape(D_INNER, D_MODEL)
    wf_pool = wg_pool.reshape(N_DEV, N_GROUPS, 32, POOL_GROUP).transpose(1, 0, 2, 3).reshape(N_GROUPS, POOL_GROUP, POOL_GROUP)
    p, yp_raw = _pool_fwd("pool_fwd", proj0, wf_pool)
    ymix0 = _gate_fwd0("gate_fwd0", yp_raw, o, proj0, pool_scale)
    yo0 = _proj_out("proj_out0", ymix0, wf_out_e)

    x1, h1, h1_t = _resid_ln_mod("resid_ln_mod1", x0, yo0, gate[0], ng[1], scale[1], shift[1])
    proj1 = _proj_in("proj_in1", h1, wg_in_o)
    ymix1 = _conv_fwd("conv_fwd", proj1, conv_w_full, conv_b_full)
    yo1 = _proj_out("proj_out1", ymix1, wf_out_o)

    dx2, dyo1, loss_acc, d_final_g, d_gate1 = _final_loss("final_loss", x1, yo1, gate[1], final_g2, target)
    loss = lax.psum(loss_acc[0, 0], ("x", "y", "c"))

    dymix1 = _proj_out_bwd("proj_out1_bwd", dyo1, wf_out_o)
    dw_out_o = _wgrad_out("wgrad_out1", ymix1, dyo1)
    dproj1, d_conv_w, d_conv_b = _conv_bwd("conv_bwd", dymix1, proj1, conv_w_full, conv_b_full)
    dw_in_o = _wgrad_in("wgrad_in1", h1_t, [dproj1], D_IN_ODD // N_DEV)
    (dx1, d_shift1, d_scale1, d_ng1, dyo0, d_gate0), _ = _proj_in_bwd_ln(
        "proj_in1_bwd", [dproj1], wt_in_o, x1, dx2, ng[1], scale[1], resid=(yo0, gate[0]))

    dymix0 = _proj_out_bwd("proj_out0_bwd", dyo0, wf_out_e)
    dw_out_e = _wgrad_out("wgrad_out0", ymix0, dyo0)
    dyp, do, dgt0, d_pool_scale = _gate_bwd0("gate_bwd0", dymix0, yp_raw, o, proj0, pool_scale)
    du_pool, dw_pool = _pool_bwd("pool_bwd", dyp, p, wf_pool)
    dw_pool_c = dw_pool.reshape(N_GROUPS, N_DEV, 32, POOL_GROUP).transpose(1, 0, 2, 3).reshape(N_DEV, N_GROUPS * 32, POOL_GROUP).astype(BF16)
    ready = [dw_out_e.reshape(N_DEV, D_INNER // N_DEV, D_MODEL), dw_in_o,
             dw_out_o.reshape(N_DEV, D_INNER // N_DEV, D_MODEL), dw_pool_c]
    dq, dk, dv, (r_out_e, r_in_o, r_out_o, r_pool), (wt_in_e,) = _attn_bwd(
        "attn_bwd", proj0, o, do, ready, [shards[0].T])
    dparts0 = [du_pool, dq, dk, dv, dgt0]
    dw_in_e = _wgrad_in("wgrad_in0", h0_t, dparts0, WGRAD_BLOCK)
    core = lax.axis_index("c").astype(jnp.int32).reshape(1)
    chip_sums = _sibling_sum("sibling_sum", dw_in_e, _sibling_exchange(dw_in_e), core)
    (dx0, d_shift0, d_scale0, d_ng0), (r_in_e,) = _proj_in_bwd_ln(
        "proj_in0_bwd", dparts0, wt_in_e, x0, dx1, ng[0], scale[0], dws=[chip_sums])
    grad_x = dx0[None]

    big = {}
    big["even_w_in"] = _sum_adamw("adamw_even_w_in", r_in_e, w_in_e, m_even_w_in[0], v_even_w_in[0])
    big["even_w_out"] = _sum_adamw("adamw_even_w_out", r_out_e, w_out_e, m_even_w_out[0], v_even_w_out[0])
    big["odd_w_in"] = _sum_adamw("adamw_odd_w_in", r_in_o, w_in_o, m_odd_w_in[0], v_odd_w_in[0])
    big["odd_w_out"] = _sum_adamw("adamw_odd_w_out", r_out_o, w_out_o, m_odd_w_out[0], v_odd_w_out[0])
    big["pool_w"] = _sum_adamw("adamw_pool_w", r_pool, w_pool, m_pool_w[0].reshape(N_GROUPS * 32, POOL_GROUP),
                               v_pool_w[0].reshape(N_GROUPS * 32, POOL_GROUP))
    big = {k: [a.reshape(shape) for a in v] for (k, v), shape in zip(
        big.items(), [even_w_in.shape, even_w_out.shape, odd_w_in.shape, odd_w_out.shape, pool_w.shape])}

    dm = jnp.concatenate([jnp.concatenate([d_shift0, d_scale0, d_gate0], axis=1),
                          jnp.concatenate([d_shift1, d_scale1, d_gate1], axis=1)], axis=0)
    slab = jnp.zeros((SLAB_ROWS, D_MODEL), F32)
    slab = slab.at[0:6].set(dm.reshape(6, D_MODEL))
    slab = slab.at[8:9].set(d_ng0).at[9:10].set(d_ng1).at[10:11].set(d_pool_scale).at[11:12].set(d_final_g)
    slab = slab.at[16:22].set(d_conv_w.reshape(6, D_MODEL)).at[24:26].set(d_conv_b.reshape(2, D_MODEL))
    gathered, total = _small_grads(slab)
    my = 4 * lax.axis_index("x") + 2 * lax.axis_index("y") + lax.axis_index("c")
    g_ada_b = total[0:6].reshape(2, 3 * D_MODEL)
    g_norm_g = total[8:10]
    g_pool_scale = total[10:11]
    g_final_g = total[11:12]
    cw_cols = conv_w.shape[2]
    g_conv_w = lax.dynamic_slice_in_dim(total[16:22].reshape(3, D_INNER), my * cw_cols, cw_cols, axis=1)
    g_conv_b = lax.dynamic_slice_in_dim(total[24:26].reshape(1, D_INNER), my * cw_cols, cw_cols, axis=1)
    dm_all = gathered[:, 0:6, :].reshape(N_DEV, 2, 3 * D_MODEL)
    dm_cols = lax.dynamic_slice_in_dim(dm_all, my * ADA_COLS, ADA_COLS, axis=2).transpose(1, 0, 2)
    ada = _ada_w_adamw("adamw_ada_w", cs_all.T, dm_cols, ada_w, m_ada_w, v_ada_w)

    small = _small_adamw("adamw_small", [
        (norm_g, g_norm_g, m_norm_g, v_norm_g),
        (ada_b, g_ada_b, m_ada_b, v_ada_b),
        (pool_scale, g_pool_scale, m_pool_scale, v_pool_scale),
        (conv_w[0], g_conv_w, m_conv_w[0], v_conv_w[0]),
        (conv_b, g_conv_b, m_conv_b, v_conv_b),
        (final_g2, g_final_g, m_final_g.reshape(1, D_MODEL), v_final_g.reshape(1, D_MODEL)),
    ])
    small = [small[3 * j:3 * j + 3] for j in range(6)]

    grads = {
        "norm_g": g_norm_g, "ada_w": ada[0], "ada_b": g_ada_b, "even_w_in": big["even_w_in"][0],
        "pool_w": big["pool_w"][0], "pool_scale": g_pool_scale, "even_w_out": big["even_w_out"][0],
        "odd_w_in": big["odd_w_in"][0], "conv_w": g_conv_w.reshape(conv_w.shape), "conv_b": g_conv_b,
        "odd_w_out": big["odd_w_out"][0], "final_g": g_final_g.reshape(D_MODEL),
    }
    rest = []
    for idx in range(3):
        rest += [
            small[0][idx], ada[1 + idx], small[1][idx], big["even_w_in"][1 + idx], big["pool_w"][1 + idx],
            small[2][idx], big["even_w_out"][1 + idx], big["odd_w_in"][1 + idx],
            small[3][idx].reshape(conv_w.shape), small[4][idx], big["odd_w_out"][1 + idx],
            small[5][idx].reshape(D_MODEL),
        ]
    order = ["norm_g", "ada_w", "ada_b", "even_w_in", "pool_w", "pool_scale", "even_w_out", "odd_w_in",
             "conv_w", "conv_b", "odd_w_out", "final_g"]
    return (loss, grad_x, *[grads[n] for n in order], *rest)
```

```python
import jax
import jax.numpy as jnp
from jax import lax
from jax.experimental import pallas as pl
from jax.experimental.pallas import tpu as pltpu

F32 = jnp.float32
BF16 = jnp.bfloat16
SDS = jax.ShapeDtypeStruct
MESH = pl.DeviceIdType.MESH

N_DEV = 8
D_MODEL = 1024
D_INNER = 2048
D_POOL = 1024
D_SB = 1024
N_GROUPS = 4
POOL_GROUP = 256
HEAD_DIM = 64
LANES = 128
D_IN_EVEN = 6144
D_IN_ODD = 8192
EPS = 1e-6
ADAM_LR = 0.001
ADAM_B1 = 0.9
ADAM_B2 = 0.999
ADAM_EPS = 1e-08
ADAM_WD = 0.01
ADAM_STEP = 10

ROW_TILE = 256
ATT_TILE = 256
HALO = 16
VMEM_LIMIT = 48 * 1024 * 1024
SLAB_ROWS = 32
WGRAD_BLOCK = 256


def _params(n_axes):
    return pltpu.CompilerParams(dimension_semantics=("arbitrary",) * n_axes, vmem_limit_bytes=VMEM_LIMIT)


def _sigmoid(x):
    return 1.0 / (1.0 + jnp.exp(-x))


def _dot(a, b):
    return jnp.dot(a, b, preferred_element_type=F32)


def _dot_nt(a, b):
    return lax.dot_general(a, b, (((1,), (1,)), ((), ())), preferred_element_type=F32)


def _dot_tn(a, b):
    return lax.dot_general(a, b, (((0,), (0,)), ((), ())), preferred_element_type=F32)


def _mm(name, a, b, *, grid, a_spec, b_spec, o_spec, o_shape, o_dtype, dot, acc_axis=None, acc_shape=None):
    n_acc = grid[acc_axis] if acc_axis is not None else 1

    def body(a_ref, b_ref, o_ref, *scratch):
        prod = dot(a_ref[...], b_ref[...])
        if acc_axis is None:
            o_ref[...] = prod.astype(o_dtype)
        else:
            acc = scratch[0]
            k = pl.program_id(acc_axis)

            @pl.when(k == 0)
            def _():
                acc[...] = prod

            @pl.when(k > 0)
            def _():
                acc[...] += prod

            @pl.when(k == n_acc - 1)
            def _():
                o_ref[...] = acc[...].astype(o_dtype)

    scratch = [] if acc_axis is None else [pltpu.VMEM(acc_shape, F32)]
    return pl.pallas_call(
        body, name=name, grid=grid, in_specs=[a_spec, b_spec], out_specs=o_spec,
        out_shape=SDS(o_shape, o_dtype), scratch_shapes=scratch, compiler_params=_params(len(grid)),
    )(a, b)


def _proj_in(name, h, wg):
    s = h.shape[0]
    cn = wg.shape[2]
    tm = min(s, ROW_TILE)

    def body(a_ref, w_ref, o_ref):
        a = a_ref[...]
        for d in range(N_DEV):
            o_ref[:, d * cn:(d + 1) * cn] = _dot(a, w_ref[d]).astype(BF16)

    return pl.pallas_call(
        body, name=name, grid=(s // tm,),
        in_specs=[pl.BlockSpec((tm, D_MODEL), lambda i: (i, 0)),
                  pl.BlockSpec((N_DEV, D_MODEL, cn), lambda i: (0, 0, 0), pipeline_mode=pl.Buffered(1))],
        out_specs=pl.BlockSpec((tm, N_DEV * cn), lambda i: (i, 0)),
        out_shape=SDS((s, N_DEV * cn), BF16), compiler_params=_params(1),
    )(h, wg)


GATHER_ROWS = 1024
GATHER_ARRIVALS = ((2, 4, 0), (3, 5, 1), (6, 7, 2))


def _proj_in_gather(name, h, w_shard, order):
    s = h.shape[0]
    k_dim, cn = w_shard.shape
    tm = min(s, GATHER_ROWS)
    n_i = s // tm

    def body(ord_ref, a_ref, w_ref, o_ref, w_buf, send_sems, recv_sems, local_sem):
        dd = pl.program_id(0)
        i = pl.program_id(1)
        x, y, c = _place()
        me, sibling = (x, y, c), (x, y, 1 - c)
        chips = [(1 - x, y), (x, 1 - y), (1 - x, 1 - y)]

        def copy(k, block, to, from_src=False):
            slot = w_buf.at[_dev_index(block)]
            return pltpu.make_async_remote_copy(
                src_ref=w_ref if from_src else slot, dst_ref=slot, send_sem=send_sems.at[k], recv_sem=recv_sems.at[k],
                device_id=to, device_id_type=MESH)

        mine = pltpu.make_async_copy(w_ref, w_buf.at[_dev_index(me)], local_sem)
        first = [copy(0, me, sibling, True)] + [copy(1 + j, me, (*chip, c), True) for j, chip in enumerate(chips)]
        passed = [copy(4 + j, (*chip, c), sibling) for j, chip in enumerate(chips)]

        @pl.when((dd == 0) & (i == 0))
        def _():
            mine.start()
            for cp in first:
                cp.start()
            mine.wait()

        @pl.when((dd == 1) & (i == 0))
        def _():
            copy(0, sibling, me).wait_recv()

        for dd_direct, dd_passed, j in GATHER_ARRIVALS:
            @pl.when((dd == dd_direct) & (i == 0))
            def _(j=j):
                copy(1 + j, (*chips[j], c), me).wait_recv()
                passed[j].start()

            @pl.when((dd == dd_passed) & (i == 0))
            def _(j=j):
                copy(4 + j, (*chips[j], 1 - c), me).wait_recv()

        o_ref[...] = _dot(a_ref[...], w_buf[ord_ref[dd]]).astype(BF16)

        @pl.when((dd == N_DEV - 1) & (i == n_i - 1))
        def _():
            for cp in first + passed:
                cp.wait_send()

    return pl.pallas_call(
        body, name=name,
        grid_spec=pltpu.PrefetchScalarGridSpec(
            num_scalar_prefetch=1, grid=(N_DEV, n_i),
            in_specs=[pl.BlockSpec((tm, k_dim), lambda dd, i, ord_ref: (i, 0)), HBM_SPEC],
            out_specs=pl.BlockSpec((tm, cn), lambda dd, i, ord_ref: (i, ord_ref[dd])),
            scratch_shapes=[pltpu.VMEM((N_DEV, k_dim, cn), BF16), pltpu.SemaphoreType.DMA((7,)),
                            pltpu.SemaphoreType.DMA((7,)), pltpu.SemaphoreType.DMA]),
        out_shape=SDS((s, N_DEV * cn), BF16), compiler_params=_params(2),
    )(order, h, w_shard)


def _proj_out(name, y, w):
    s = y.shape[0]
    tm = min(s, 512)
    return _mm(name, y, w, grid=(s // tm,),
               a_spec=pl.BlockSpec((tm, D_INNER), lambda i: (i, 0)),
               b_spec=pl.BlockSpec((D_INNER, D_MODEL), lambda i: (0, 0)),
               o_spec=pl.BlockSpec((tm, D_MODEL), lambda i: (i, 0)),
               o_shape=(s, D_MODEL), o_dtype=F32, dot=_dot)


def _proj_out_bwd(name, dyo, w):
    s = dyo.shape[0]
    tm = min(s, 512)
    return _mm(name, dyo, w, grid=(s // tm,),
               a_spec=pl.BlockSpec((tm, D_MODEL), lambda i: (i, 0)),
               b_spec=pl.BlockSpec((D_INNER, D_MODEL), lambda i: (0, 0)),
               o_spec=pl.BlockSpec((tm, D_INNER), lambda i: (i, 0)),
               o_shape=(s, D_INNER), o_dtype=F32, dot=_dot_nt)


def _wgrad_out(name, y_t, dyo):
    s = y_t.shape[1]
    tm = 512
    return _mm(name, y_t, dyo, grid=(D_INNER // tm,),
               a_spec=pl.BlockSpec((tm, s), lambda r: (r, 0)),
               b_spec=pl.BlockSpec((s, D_MODEL), lambda r: (0, 0), pipeline_mode=pl.Buffered(1)),
               o_spec=pl.BlockSpec((tm, D_MODEL), lambda r: (r, 0)),
               o_shape=(D_INNER, D_MODEL), o_dtype=BF16, dot=_dot)


def _proj_in_bwd_ln(name, parts, wt, x, dx_next, g, scale, resid=None, dws=()):
    s = x.shape[0]
    widths = [p.shape[1] for p in parts]
    offs = [sum(widths[:k]) for k in range(len(parts))]
    k_all = sum(widths)
    n_i = s // ROW_TILE
    n_p, n_r, n_side = len(parts), (2 if resid else 0), len(dws)
    w_all = wt.reshape(k_all, D_MODEL)

    def body(*refs):
        part_refs = refs[:n_p]
        w_ref, x_ref, dxn_ref, g_ref, sc_ref = refs[n_p:n_p + 5]
        resid_refs = refs[n_p + 5:n_p + 5 + n_r]
        srcs = refs[n_p + 5 + n_r:n_p + 5 + n_r + n_side]
        outs = refs[n_p + 5 + n_r + n_side:]
        dx_ref, dsh_ref, dsc_ref, dg_ref = outs[:4]
        resid_outs = outs[4:4 + n_r]
        dsts = outs[4 + n_r:4 + n_r + n_side]
        sems = outs[4 + n_r + n_side:]
        i = pl.program_id(0)
        if n_side:
            _side_exchange(srcs + dsts + sems, n_side, True, i == 0, i == n_i - 1, chips=True)

        @pl.when(i == 0)
        def _():
            dsh_ref[...] = jnp.zeros_like(dsh_ref)
            dsc_ref[...] = jnp.zeros_like(dsc_ref)
            dg_ref[...] = jnp.zeros_like(dg_ref)
            if resid:
                resid_outs[1][...] = jnp.zeros_like(resid_outs[1])

        dh_v = _dot(part_refs[0][...], w_ref[offs[0]:offs[0] + widths[0], :])
        for k in range(1, n_p):
            dh_v = dh_v + _dot(part_refs[k][...], w_ref[offs[k]:offs[k] + widths[k], :])
        xv = x_ref[...]
        g_v = g_ref[...]
        r = lax.rsqrt(jnp.mean(xv * xv, axis=-1, keepdims=True) + EPS)
        xn = xv * r
        dsh_ref[...] += jnp.sum(dh_v, axis=0, keepdims=True)
        dsc_ref[...] += jnp.sum(dh_v * (xn * g_v), axis=0, keepdims=True)
        dn = dh_v * (1.0 + sc_ref[...])
        dg_ref[...] += jnp.sum(dn * xn, axis=0, keepdims=True)
        dxh = dn * g_v
        dx = dxn_ref[...] + r * (dxh - xn * jnp.mean(dxh * xn, axis=-1, keepdims=True))
        dx_ref[...] = dx
        if resid:
            yo_ref, gt_ref = resid_refs
            resid_outs[0][...] = (dx * (1.0 + gt_ref[...])).astype(BF16)
            resid_outs[1][...] += jnp.sum(dx * yo_ref[...], axis=0, keepdims=True)

    row, vec = _row_spec(), _vec_spec()
    out = pl.pallas_call(
        body, name=name, grid=(n_i,),
        in_specs=[_row_spec(w) for w in widths]
        + [pl.BlockSpec((k_all, D_MODEL), lambda i: (0, 0), pipeline_mode=pl.Buffered(1)), row, row, vec, vec]
        + ([row, vec] if resid else []) + [HBM_SPEC] * n_side,
        out_specs=[row, vec, vec, vec] + ([row, vec] if resid else []) + [HBM_SPEC] * n_side,
        out_shape=[SDS((s, D_MODEL), F32)] + [SDS((1, D_MODEL), F32)] * 3
        + ([SDS((s, D_MODEL), BF16), SDS((1, D_MODEL), F32)] if resid else [])
        + [SDS(dw.shape, dw.dtype) for dw in dws],
        scratch_shapes=_peer_sems(n_side) if n_side else [],
        compiler_params=_params(1),
    )(*parts, w_all, x, dx_next, g, scale, *(resid or ()), *dws)
    return out[:4 + n_r], out[4 + n_r:]


def _wgrad_in(name, h_t, parts, blk):
    s = h_t.shape[1]
    widths = [p.shape[1] for p in parts]
    cn = sum(widths) // N_DEV
    per_dev = cn // blk
    starts = [sum(widths[:k]) // blk for k in range(len(parts))]
    counts = [w // blk for w in widths]
    n_blk = sum(counts)

    def body(a_ref, *rest):
        o_ref = rest[len(parts)]
        b = pl.program_id(0)
        for k in range(len(parts)):
            @pl.when((b >= starts[k]) & (b < starts[k] + counts[k]))
            def _(k=k):
                o_ref[...] = _dot(a_ref[...], rest[k][...]).astype(BF16)

    def part_spec(k):
        return pl.BlockSpec((s, blk), lambda b: (0, jnp.clip(b - starts[k], 0, counts[k] - 1)))

    return pl.pallas_call(
        body, name=name, grid=(n_blk,),
        in_specs=[pl.BlockSpec((D_MODEL, s), lambda b: (0, 0), pipeline_mode=pl.Buffered(1))]
        + [part_spec(k) for k in range(len(parts))],
        out_specs=pl.BlockSpec((None, D_MODEL, blk), lambda b: (b // per_dev, 0, b % per_dev)),
        out_shape=SDS((N_DEV, D_MODEL, cn), BF16), compiler_params=_params(1),
    )(h_t, *parts)


def _vec_spec():
    return pl.BlockSpec((1, D_MODEL), lambda i: (0, 0))


def _row_spec(width=D_MODEL, col=0):
    return pl.BlockSpec((ROW_TILE, width), lambda i: (i, col))


def _col_spec():
    return pl.BlockSpec((D_MODEL, ROW_TILE), lambda i: (0, i))


def _ln_mod(name, x, g, scale, shift):
    s = x.shape[0]

    def body(x_ref, g_ref, sc_ref, sh_ref, h_ref, ht_ref):
        xv = x_ref[...]
        r = lax.rsqrt(jnp.mean(xv * xv, axis=-1, keepdims=True) + EPS)
        n = (xv * r) * g_ref[...]
        h = (n * (1.0 + sc_ref[...]) + sh_ref[...]).astype(BF16)
        h_ref[...] = h
        ht_ref[...] = h.T

    return pl.pallas_call(
        body, name=name, grid=(s // ROW_TILE,),
        in_specs=[_row_spec(), _vec_spec(), _vec_spec(), _vec_spec()], out_specs=[_row_spec(), _col_spec()],
        out_shape=[SDS((s, D_MODEL), BF16), SDS((D_MODEL, s), BF16)], compiler_params=_params(1),
    )(x, g, scale, shift)


def _resid_ln_mod(name, x, yo, gate, g, scale, shift):
    s = x.shape[0]

    def body(x_ref, yo_ref, gt_ref, g_ref, sc_ref, sh_ref, xn_ref, h_ref, ht_ref):
        xv = x_ref[...] + (1.0 + gt_ref[...]) * yo_ref[...]
        xn_ref[...] = xv
        r = lax.rsqrt(jnp.mean(xv * xv, axis=-1, keepdims=True) + EPS)
        n = (xv * r) * g_ref[...]
        h = (n * (1.0 + sc_ref[...]) + sh_ref[...]).astype(BF16)
        h_ref[...] = h
        ht_ref[...] = h.T

    return pl.pallas_call(
        body, name=name, grid=(s // ROW_TILE,),
        in_specs=[_row_spec(), _row_spec(), _vec_spec(), _vec_spec(), _vec_spec(), _vec_spec()],
        out_specs=[_row_spec(), _row_spec(), _col_spec()],
        out_shape=[SDS((s, D_MODEL), F32), SDS((s, D_MODEL), BF16), SDS((D_MODEL, s), BF16)],
        compiler_params=_params(1),
    )(x, yo, gate, g, scale, shift)


def _final_loss(name, x1, yo1, gate1, gf, target):
    s = x1.shape[0]

    def body(x_ref, yo_ref, gt_ref, gf_ref, t_ref, dx_ref, dyo_ref, loss_ref, dgf_ref, dgt_ref):
        i = pl.program_id(0)

        @pl.when(i == 0)
        def _():
            loss_ref[...] = jnp.zeros_like(loss_ref)
            dgf_ref[...] = jnp.zeros_like(dgf_ref)
            dgt_ref[...] = jnp.zeros_like(dgt_ref)

        yo = yo_ref[...]
        one_gate = 1.0 + gt_ref[...]
        x2 = x_ref[...] + one_gate * yo
        r = lax.rsqrt(jnp.mean(x2 * x2, axis=-1, keepdims=True) + EPS)
        xn = x2 * r
        gf_v = gf_ref[...]
        err = xn * gf_v - t_ref[...]
        loss_ref[...] += 0.5 * jnp.sum(jnp.mean(err * err, axis=-1, keepdims=True))
        dout = err * (1.0 / D_MODEL)
        dgf_ref[...] += jnp.sum(dout * xn, axis=0, keepdims=True)
        dxn = dout * gf_v
        dx2 = r * (dxn - xn * jnp.mean(dxn * xn, axis=-1, keepdims=True))
        dx_ref[...] = dx2
        dyo_ref[...] = (dx2 * one_gate).astype(BF16)
        dgt_ref[...] += jnp.sum(dx2 * yo, axis=0, keepdims=True)

    return pl.pallas_call(
        body, name=name, grid=(s // ROW_TILE,),
        in_specs=[_row_spec(), _row_spec(), _vec_spec(), _vec_spec(), _row_spec()],
        out_specs=[_row_spec(), _row_spec(), pl.BlockSpec((1, LANES), lambda i: (0, 0)), _vec_spec(), _vec_spec()],
        out_shape=[SDS((s, D_MODEL), F32), SDS((s, D_MODEL), BF16), SDS((1, LANES), F32),
                   SDS((1, D_MODEL), F32), SDS((1, D_MODEL), F32)],
        compiler_params=_params(1),
    )(x1, yo1, gate1, gf, target)


POOL_WINDOWS = (2, 4, 8, 16)


def _window_sum(x, window, rows, backward):
    acc, step = x, 1
    while step < window:
        acc = acc + pltpu.roll(acc, step if backward else rows - step, axis=0)
        step *= 2
    return acc


def _pool_fwd(name, proj0, wp):
    s = proj0.shape[0]
    hb = ROW_TILE // HALO
    ext_rows = ROW_TILE + HALO

    def body(u_ref, halo_ref, w_ref, p_ref, y_ref):
        i = pl.program_id(0)
        t = i * ROW_TILE + lax.broadcasted_iota(jnp.int32, (ROW_TILE, 1), 0)
        for g, window in enumerate(POOL_WINDOWS):
            cols = slice(g * POOL_GROUP, (g + 1) * POOL_GROUP)
            u = u_ref[:, cols].astype(F32)
            halo = jnp.where(i == 0, 0.0, halo_ref[:, cols].astype(F32))
            ext = jnp.concatenate([halo, u], axis=0)
            win = _window_sum(ext, window, ext_rows, True)[HALO:, :]
            cnt = jnp.minimum(t + 1, window).astype(F32)
            p = (win / cnt - u).astype(BF16)
            p_ref[:, cols] = p
            y_ref[:, cols] = _dot(p, w_ref[g])

    return pl.pallas_call(
        body, name=name, grid=(s // ROW_TILE,),
        in_specs=[pl.BlockSpec((ROW_TILE, D_POOL), lambda i: (i, 0)),
                  pl.BlockSpec((HALO, D_POOL), lambda i: (jnp.maximum(i * hb - 1, 0), 0)),
                  pl.BlockSpec((N_GROUPS, POOL_GROUP, POOL_GROUP), lambda i: (0, 0, 0))],
        out_specs=[_row_spec(D_POOL), _row_spec(D_POOL)],
        out_shape=[SDS((s, D_POOL), BF16), SDS((s, D_POOL), F32)], compiler_params=_params(1),
    )(proj0, proj0, wp)


def _pool_bwd(name, dyp, p, wp):
    s = dyp.shape[0]
    hb = ROW_TILE // HALO
    n_hb = s // HALO
    n_tiles = s // ROW_TILE
    ext_rows = ROW_TILE + HALO

    def body(dy_ref, nxt_ref, p_ref, w_ref, du_ref, dw_ref):
        i = pl.program_id(0)

        @pl.when(i == 0)
        def _():
            dw_ref[...] = jnp.zeros_like(dw_ref)

        t = i * ROW_TILE + lax.broadcasted_iota(jnp.int32, (ext_rows, 1), 0)
        for g, window in enumerate(POOL_WINDOWS):
            cols = slice(g * POOL_GROUP, (g + 1) * POOL_GROUP)
            dy = dy_ref[:, cols]
            nxt = nxt_ref[:, cols]
            nxt = jnp.where(i == n_tiles - 1, jnp.zeros_like(nxt), nxt)
            dp = _dot_nt(jnp.concatenate([dy, nxt], axis=0), w_ref[g])
            cnt = jnp.minimum(t + 1, window).astype(F32)
            win = _window_sum(dp / cnt, window, ext_rows, False)[:ROW_TILE, :]
            du_ref[:, cols] = (win - dp[:ROW_TILE, :]).astype(BF16)
            dw_ref[g] += _dot_tn(p_ref[:, cols], dy)

    return pl.pallas_call(
        body, name=name, grid=(n_tiles,),
        in_specs=[_row_spec(D_POOL),
                  pl.BlockSpec((HALO, D_POOL), lambda i: (jnp.minimum((i + 1) * hb, n_hb - 1), 0)),
                  _row_spec(D_POOL),
                  pl.BlockSpec((N_GROUPS, POOL_GROUP, POOL_GROUP), lambda i: (0, 0, 0))],
        out_specs=[_row_spec(D_POOL), pl.BlockSpec((N_GROUPS, POOL_GROUP, POOL_GROUP), lambda i: (0, 0, 0))],
        out_shape=[SDS((s, D_POOL), BF16), SDS((N_GROUPS, POOL_GROUP, POOL_GROUP), F32)],
        compiler_params=_params(1),
    )(dyp, dyp, p, wp)


FWD_HEADS_PER_STEP = 8
BWD_HEADS_PER_STEP = 4
ATT_SCALE = 0.125
FWD_SKEW = 1


def _att_groups(nh):
    lanes = nh * HEAD_DIM
    return lanes, D_SB // lanes, D_POOL // lanes, (D_POOL + D_SB) // lanes, (D_POOL + 2 * D_SB) // lanes


def _att_consts():
    r = lax.broadcasted_iota(jnp.int32, (ATT_TILE, ATT_TILE), 0)
    c = lax.broadcasted_iota(jnp.int32, (ATT_TILE, ATT_TILE), 1)
    first = lax.broadcasted_iota(jnp.int32, (1, LANES), 1) < HEAD_DIM
    return r, c, first


def _pair(x, p):
    return x[:, p * LANES:(p + 1) * LANES]


def _one_head(x, first, hh):
    zero = jnp.zeros_like(x)
    return jnp.where(first, x, zero) if hh == 0 else jnp.where(first, zero, x)


def _neg_softplus(z):
    return -(jnp.maximum(z, 0.0) + jnp.log(1.0 + jnp.exp(-jnp.abs(z))))


def _side_exchange(side_refs, n_side, by_chunk, is_first, is_last, chips=False):
    ins, outs = side_refs[:n_side], side_refs[n_side:2 * n_side]
    sems = side_refs[2 * n_side:2 * n_side + 3]

    def copies():
        return _chip_copies(ins, outs, *sems) if chips else _peer_copies(ins, outs, *sems, by_chunk=by_chunk)

    @pl.when(is_first)
    def _():
        for cp in copies():
            cp.start()

    @pl.when(is_last)
    def _():
        for cp in copies():
            cp.wait()


def _attn_fwd(name, proj0, shards):
    s = proj0.shape[0]
    nq = s // ATT_TILE
    nh = FWD_HEADS_PER_STEP
    ATT_GROUP, N_ATT_GROUPS, Q_GRP, K_GRP, V_GRP = _att_groups(nh)
    n_side = len(shards)

    def body(q_ref, k_ref, v_ref, *rest):
        o_ref = rest[n_side]
        side = rest[:n_side] + rest[n_side + 1:]
        j = pl.program_id(0)
        i = pl.program_id(1)
        _side_exchange(side, n_side, False, (j == 0) & (i == 0), (j == N_ATT_GROUPS - 1) & (i == nq - 1))
        r, c, first = _att_consts()
        tri = (r >= c).astype(BF16)
        below = c < r
        q = q_ref[...] * ATT_SCALE
        qh = [_one_head(_pair(q, h // 2), first, h % 2) for h in range(nh)]

        def tile(kb, carry, diagonal):
            k0 = pl.multiple_of(kb * ATT_TILE, ATT_TILE)
            kt = k_ref[pl.ds(k0, ATT_TILE), :]
            vt = v_ref[pl.ds(k0, ATT_TILE), :]
            z, lf_b, a_b = [None] * nh, [None] * nh, [None] * nh
            out_c, out_acc = [None] * nh, [None] * nh
            for t in range(nh + 2 * FWD_SKEW):
                if t < nh:
                    z[t] = _dot_nt(qh[t], _pair(kt, t // 2))
                    lf = _neg_softplus(z[t])
                    if diagonal:
                        lf = jnp.where(below, lf, 0.0)
                    lf_b[t] = lf.astype(BF16)
                    out_c[t] = carry[t] + jnp.sum(lf, axis=1, keepdims=True)
                u = t - FWD_SKEW
                if 0 <= u < nh:
                    a = jnp.exp(z[u] + _dot(lf_b[u], tri) + carry[u])
                    if diagonal:
                        a = jnp.where(below, a, 0.0)
                    a_b[u] = a.astype(BF16)
                w = t - 2 * FWD_SKEW
                if 0 <= w < nh:
                    out_acc[w] = carry[nh + w] + _dot(a_b[w], _pair(vt, w // 2))
            return tuple(out_c + out_acc)

        init = tuple([jnp.zeros((ATT_TILE, 1), F32)] * nh + [jnp.zeros((ATT_TILE, LANES), F32)] * nh)
        carry = tile(i, init, True)
        carry = lax.fori_loop(1, i + 1, lambda n, cr: tile(i - n, cr, False), carry)
        for p in range(nh // 2):
            o_ref[:, p * LANES:(p + 1) * LANES] = jnp.where(first, carry[nh + 2 * p], carry[nh + 2 * p + 1])

    out = pl.pallas_call(
        body, name=name, grid=(N_ATT_GROUPS, nq),
        in_specs=[pl.BlockSpec((ATT_TILE, ATT_GROUP), lambda j, i: (i, Q_GRP + j)),
                  pl.BlockSpec((s, ATT_GROUP), lambda j, i: (0, K_GRP + j)),
                  pl.BlockSpec((s, ATT_GROUP), lambda j, i: (0, V_GRP + j))] + [HBM_SPEC] * n_side,
        out_specs=[pl.BlockSpec((ATT_TILE, ATT_GROUP), lambda j, i: (i, j))] + [HBM_SPEC] * n_side,
        out_shape=[SDS((s, D_SB), F32)] + [SDS((N_DEV,) + sh.shape, sh.dtype) for sh in shards],
        scratch_shapes=_peer_sems(n_side), compiler_params=_params(2),
    )(proj0, proj0, proj0, *shards)
    return out[0], out[1:]


def _attn_bwd(name, proj0, o, do, dws, shards):
    s = proj0.shape[0]
    nq = s // ATT_TILE
    nh = BWD_HEADS_PER_STEP
    ATT_GROUP, N_ATT_GROUPS, Q_GRP, K_GRP, V_GRP = _att_groups(nh)
    n1, n2 = len(dws), len(shards)
    n_side = n1 + n2

    def body(q_ref, k_ref, v_ref, o_ref, do_ref, *rest):
        dq_ref, dk_ref, dv_ref = rest[n_side:n_side + 3]
        dk_acc, dv_acc = rest[2 * n_side + 3:2 * n_side + 5]
        srcs, dsts, sems = rest[:n_side], rest[n_side + 3:2 * n_side + 3], rest[2 * n_side + 5:]
        j = pl.program_id(0)
        i = pl.program_id(1)
        is_first, is_last = (j == 0) & (i == 0), (j == N_ATT_GROUPS - 1) & (i == nq - 1)
        _side_exchange(srcs[:n1] + dsts[:n1] + sems[:3], n1, True, is_first, is_last)
        _side_exchange(srcs[n1:] + dsts[n1:] + sems[3:], n2, False, is_first, is_last)

        @pl.when(i == 0)
        def _():
            dk_acc[...] = jnp.zeros_like(dk_acc)
            dv_acc[...] = jnp.zeros_like(dv_acc)

        r, c, first = _att_consts()
        tri = (r >= c).astype(BF16)
        tri_p = (r <= c).astype(BF16)
        below = c < r
        q = q_ref[...] * ATT_SCALE
        do_b = do_ref[...].astype(BF16)
        do_o = do_b.astype(F32) * o_ref[...]
        qh = [_one_head(_pair(q, h // 2), first, h % 2) for h in range(nh)]
        doh = [_one_head(_pair(do_b, h // 2), first, h % 2) for h in range(nh)]
        dsum = [jnp.sum(_one_head(_pair(do_o, h // 2), first, h % 2), axis=1, keepdims=True) for h in range(nh)]

        def tile(kb, carry, diagonal):
            k0 = pl.multiple_of(kb * ATT_TILE, ATT_TILE)
            kt = k_ref[pl.ds(k0, ATT_TILE), :]
            vt = v_ref[pl.ds(k0, ATT_TILE), :]
            none = lambda: [None] * nh
            z, d_a, sig, lf_b, a_b, g, early, dz = none(), none(), none(), none(), none(), none(), none(), none()
            out_c1, out_c2, out_dq, dk_t, dv_t = none(), none(), none(), none(), none()
            for t in range(nh + 3):
                if t < nh:
                    z[t] = _dot_nt(qh[t], _pair(kt, t // 2))
                    d_a[t] = _dot_nt(doh[t], _pair(vt, t // 2))
                    lf = _neg_softplus(z[t])
                    sig[t] = jnp.exp(z[t] + lf)
                    if diagonal:
                        lf = jnp.where(below, lf, 0.0)
                    lf_b[t] = lf.astype(BF16)
                    out_c1[t] = carry[t] + jnp.sum(lf, axis=1, keepdims=True)
                u = t - 1
                if 0 <= u < nh:
                    a = jnp.exp(z[u] + _dot(lf_b[u], tri) + carry[u])
                    if diagonal:
                        a = jnp.where(below, a, 0.0)
                    a_b[u] = a.astype(BF16)
                    g[u] = a_b[u].astype(F32) * d_a[u]
                    g_sum = jnp.sum(g[u], axis=1, keepdims=True)
                    early[u] = dsum[u] - carry[nh + u] - g_sum
                    out_c2[u] = carry[nh + u] + g_sum
                w = t - 2
                if 0 <= w < nh:
                    upto = _dot(g[w].astype(BF16), tri_p)
                    dv_t[w] = _dot_tn(a_b[w], doh[w])
                    d = g[w] - sig[w] * (early[w] + upto)
                    if diagonal:
                        d = jnp.where(below, d, 0.0)
                    dz[w] = d.astype(BF16)
                y = t - 3
                if 0 <= y < nh:
                    out_dq[y] = carry[2 * nh + y] + _dot(dz[y], _pair(kt, y // 2))
                    dk_t[y] = _dot_tn(dz[y], qh[y])
            for p in range(nh // 2):
                dk_acc[pl.ds(k0, ATT_TILE), p * LANES:(p + 1) * LANES] += dk_t[2 * p] + dk_t[2 * p + 1]
                dv_acc[pl.ds(k0, ATT_TILE), p * LANES:(p + 1) * LANES] += dv_t[2 * p] + dv_t[2 * p + 1]
            return tuple(out_c1 + out_c2 + out_dq)

        init = tuple([jnp.zeros((ATT_TILE, 1), F32)] * (2 * nh) + [jnp.zeros((ATT_TILE, LANES), F32)] * nh)
        carry = tile(i, init, True)
        carry = lax.fori_loop(1, i + 1, lambda n, cr: tile(i - n, cr, False), carry)
        for p in range(nh // 2):
            dq_p = jnp.where(first, carry[2 * nh + 2 * p], carry[2 * nh + 2 * p + 1]) * ATT_SCALE
            dq_ref[:, p * LANES:(p + 1) * LANES] = dq_p.astype(BF16)

        @pl.when(i == nq - 1)
        def _():
            dk_ref[...] = dk_acc[...].astype(BF16)
            dv_ref[...] = dv_acc[...].astype(BF16)

    tile_spec = pl.BlockSpec((ATT_TILE, ATT_GROUP), lambda j, i: (i, j))
    full = pl.BlockSpec((s, ATT_GROUP), lambda j, i: (0, j))
    out = pl.pallas_call(
        body, name=name, grid=(N_ATT_GROUPS, nq),
        in_specs=[pl.BlockSpec((ATT_TILE, ATT_GROUP), lambda j, i: (i, Q_GRP + j)),
                  pl.BlockSpec((s, ATT_GROUP), lambda j, i: (0, K_GRP + j)),
                  pl.BlockSpec((s, ATT_GROUP), lambda j, i: (0, V_GRP + j)),
                  tile_spec, tile_spec] + [HBM_SPEC] * n_side,
        out_specs=[tile_spec, full, full] + [HBM_SPEC] * n_side,
        out_shape=[SDS((s, D_SB), BF16)] * 3 + [SDS(dw.shape, dw.dtype) for dw in dws]
        + [SDS((N_DEV,) + sh.shape, sh.dtype) for sh in shards],
        scratch_shapes=[pltpu.VMEM((s, ATT_GROUP), F32), pltpu.VMEM((s, ATT_GROUP), F32)] + _peer_sems(n1) + _peer_sems(n2),
        compiler_params=_params(2),
    )(proj0, proj0, proj0, o, do, *dws, *shards)
    return out[0], out[1], out[2], out[3:3 + n1], out[3 + n1:]


GATE0_COL = (D_POOL + 3 * D_SB) // D_INNER


def _gate_fwd0(name, yp_raw, o, proj0, ps):
    s = o.shape[0]

    def body(yp_ref, o_ref, gt_ref, ps_ref, y_ref, yt_ref):
        gt = gt_ref[...].astype(F32)
        sg = gt * _sigmoid(gt)
        y_pool = (yp_ref[...] * ps_ref[...] * sg[:, :D_POOL]).astype(BF16)
        y_sb = (o_ref[...] * sg[:, D_POOL:]).astype(BF16)
        y_ref[:, :D_POOL] = y_pool
        y_ref[:, D_POOL:] = y_sb
        yt_ref[:D_POOL, :] = y_pool.T
        yt_ref[D_POOL:, :] = y_sb.T

    return pl.pallas_call(
        body, name=name, grid=(s // ROW_TILE,),
        in_specs=[_row_spec(), _row_spec(), _row_spec(D_INNER, GATE0_COL), _vec_spec()],
        out_specs=[_row_spec(D_INNER), pl.BlockSpec((D_INNER, ROW_TILE), lambda i: (0, i))],
        out_shape=[SDS((s, D_INNER), BF16), SDS((D_INNER, s), BF16)], compiler_params=_params(1),
    )(yp_raw, o, proj0, ps)


def _dsilu(x):
    sg = _sigmoid(x)
    return sg * (1.0 + x * (1.0 - sg))


def _gate_bwd0(name, dymix, yp_raw, o, proj0, ps):
    s = o.shape[0]

    def body(dy_ref, yp_ref, o_ref, gt_ref, ps_ref, dyp_ref, do_ref, dgt_ref, dps_ref):
        i = pl.program_id(0)

        @pl.when(i == 0)
        def _():
            dps_ref[...] = jnp.zeros_like(dps_ref)

        gt = gt_ref[...].astype(F32)
        dy = dy_ref[...]
        sg = gt * _sigmoid(gt)
        dsg = _dsilu(gt)
        dcat = dy * sg
        yp = yp_ref[...]
        ps_v = ps_ref[...]
        dyp_ref[...] = (dcat[:, :D_POOL] * ps_v).astype(BF16)
        do_ref[...] = dcat[:, D_POOL:]
        dps_ref[...] += jnp.sum(dcat[:, :D_POOL] * yp, axis=0, keepdims=True)
        dgt_ref[:, :D_POOL] = (dy[:, :D_POOL] * (yp * ps_v) * dsg[:, :D_POOL]).astype(BF16)
        dgt_ref[:, D_POOL:] = (dy[:, D_POOL:] * o_ref[...] * dsg[:, D_POOL:]).astype(BF16)

    return pl.pallas_call(
        body, name=name, grid=(s // ROW_TILE,),
        in_specs=[_row_spec(D_INNER), _row_spec(), _row_spec(), _row_spec(D_INNER, GATE0_COL), _vec_spec()],
        out_specs=[_row_spec(), _row_spec(), _row_spec(D_INNER), _vec_spec()],
        out_shape=[SDS((s, D_POOL), BF16), SDS((s, D_SB), F32), SDS((s, D_INNER), BF16), SDS((1, D_POOL), F32)],
        compiler_params=_params(1),
    )(dymix, yp_raw, o, proj0, ps)


CONV_HALO = 16


def _conv_fwd(name, proj1, cw, cb):
    s = proj1.shape[0]
    hb = ROW_TILE // CONV_HALO
    ext_rows = ROW_TILE + CONV_HALO

    def body(gb_ref, gc_ref, u_ref, gt_ref, gch_ref, uh_ref, cw_ref, cb_ref, y_ref, yt_ref):
        i = pl.program_id(0)
        uc = gc_ref[...].astype(F32) * u_ref[...].astype(F32)
        halo = jnp.where(i == 0, 0.0, gch_ref[...].astype(F32) * uh_ref[...].astype(F32))
        ext = jnp.concatenate([halo, uc], axis=0)
        uc1 = pltpu.roll(ext, 1, axis=0)[CONV_HALO:, :]
        uc2 = pltpu.roll(ext, 2, axis=0)[CONV_HALO:, :]
        cw_v = cw_ref[...]
        conv = cb_ref[...] + cw_v[0:1, :] * uc2 + cw_v[1:2, :] * uc1 + cw_v[2:3, :] * uc
        gt = gt_ref[...].astype(F32)
        y = (gb_ref[...].astype(F32) * conv * (gt * _sigmoid(gt))).astype(BF16)
        y_ref[...] = y
        yt_ref[...] = y.T

    def tile(part):
        return pl.BlockSpec((ROW_TILE, D_INNER), lambda i: (i, part))

    def halo(part):
        return pl.BlockSpec((CONV_HALO, D_INNER), lambda i: (jnp.maximum(i * hb - 1, 0), part))

    return pl.pallas_call(
        body, name=name, grid=(s // ROW_TILE,),
        in_specs=[tile(0), tile(1), tile(2), tile(3), halo(1), halo(2),
                  pl.BlockSpec((3, D_INNER), lambda i: (0, 0)), pl.BlockSpec((1, D_INNER), lambda i: (0, 0))],
        out_specs=[pl.BlockSpec((ROW_TILE, D_INNER), lambda i: (i, 0)), pl.BlockSpec((D_INNER, ROW_TILE), lambda i: (0, i))],
        out_shape=[SDS((s, D_INNER), BF16), SDS((D_INNER, s), BF16)], compiler_params=_params(1),
    )(proj1, proj1, proj1, proj1, proj1, proj1, cw, cb)


def _conv_bwd(name, dymix, proj1, cw, cb):
    s = proj1.shape[0]
    hb = ROW_TILE // CONV_HALO
    n_hb = s // CONV_HALO
    n_tiles = s // ROW_TILE
    ext_rows = ROW_TILE + CONV_HALO

    def body(dy_ref, gb_ref, gc_ref, u_ref, gt_ref, gch_ref, uh_ref, dyn_ref, gbn_ref, gtn_ref, cw_ref, cb_ref,
             dproj_ref, dcw_ref, dcb_ref):
        i = pl.program_id(0)

        @pl.when(i == 0)
        def _():
            dcw_ref[...] = jnp.zeros_like(dcw_ref)
            dcb_ref[...] = jnp.zeros_like(dcb_ref)

        gc = gc_ref[...].astype(F32)
        u = u_ref[...].astype(F32)
        gb = gb_ref[...].astype(F32)
        gt = gt_ref[...].astype(F32)
        dy = dy_ref[...]
        uc = gc * u
        halo = jnp.where(i == 0, 0.0, gch_ref[...].astype(F32) * uh_ref[...].astype(F32))
        ext = jnp.concatenate([halo, uc], axis=0)
        uc1 = pltpu.roll(ext, 1, axis=0)[CONV_HALO:, :]
        uc2 = pltpu.roll(ext, 2, axis=0)[CONV_HALO:, :]
        cw_v = cw_ref[...]
        w0, w1, w2 = cw_v[0:1, :], cw_v[1:2, :], cw_v[2:3, :]
        conv = cb_ref[...] + w0 * uc2 + w1 * uc1 + w2 * uc
        sig = _sigmoid(gt)
        sg = gt * sig
        dconv = dy * gb * sg
        gtn = gtn_ref[...].astype(F32)
        dconv_next = jnp.where(i == n_tiles - 1, 0.0, dyn_ref[...] * gbn_ref[...].astype(F32) * (gtn * _sigmoid(gtn)))
        dext = jnp.concatenate([dconv, dconv_next], axis=0)
        dconv_p1 = pltpu.roll(dext, ext_rows - 1, axis=0)[:ROW_TILE, :]
        dconv_p2 = pltpu.roll(dext, ext_rows - 2, axis=0)[:ROW_TILE, :]
        duc = w2 * dconv + w1 * dconv_p1 + w0 * dconv_p2
        dproj_ref[:, 0:D_INNER] = (dy * conv * sg).astype(BF16)
        dproj_ref[:, D_INNER:2 * D_INNER] = (duc * u).astype(BF16)
        dproj_ref[:, 2 * D_INNER:3 * D_INNER] = (duc * gc).astype(BF16)
        dproj_ref[:, 3 * D_INNER:] = (dy * gb * conv * (sig + sg * (1.0 - sig))).astype(BF16)
        dcw_ref[0:1, :] += jnp.sum(dconv * uc2, axis=0, keepdims=True)
        dcw_ref[1:2, :] += jnp.sum(dconv * uc1, axis=0, keepdims=True)
        dcw_ref[2:3, :] += jnp.sum(dconv * uc, axis=0, keepdims=True)
        dcb_ref[...] += jnp.sum(dconv, axis=0, keepdims=True)

    def tile(part):
        return pl.BlockSpec((ROW_TILE, D_INNER), lambda i: (i, part))

    def prev(part):
        return pl.BlockSpec((CONV_HALO, D_INNER), lambda i: (jnp.maximum(i * hb - 1, 0), part))

    def nxt(part):
        return pl.BlockSpec((CONV_HALO, D_INNER), lambda i: (jnp.minimum((i + 1) * hb, n_hb - 1), part))

    whole = lambda rows: pl.BlockSpec((rows, D_INNER), lambda i: (0, 0))
    return pl.pallas_call(
        body, name=name, grid=(n_tiles,),
        in_specs=[tile(0), tile(0), tile(1), tile(2), tile(3), prev(1), prev(2), nxt(0), nxt(0), nxt(3),
                  whole(3), whole(1)],
        out_specs=[pl.BlockSpec((ROW_TILE, 4 * D_INNER), lambda i: (i, 0)), whole(3), whole(1)],
        out_shape=[SDS((s, 4 * D_INNER), BF16), SDS((3, D_INNER), F32), SDS((1, D_INNER), F32)],
        compiler_params=_params(1),
    )(dymix, proj1, proj1, proj1, proj1, proj1, proj1, dymix, proj1, proj1, cw, cb)


def _place():
    x, y, c = lax.axis_index("x"), lax.axis_index("y"), lax.axis_index("c")
    return x, y, c


def _flip(x, y, c, k):
    fx, fy, fc = (k >> 2) & 1, (k >> 1) & 1, k & 1
    return (1 - x if fx else x, 1 - y if fy else y, 1 - c if fc else c)


def _dev_index(p):
    return 4 * p[0] + 2 * p[1] + p[2]


HBM_SPEC = pl.BlockSpec(memory_space=pltpu.HBM)
VMEM_SPEC = pl.BlockSpec(memory_space=pltpu.VMEM)


def _peer_copies(ins, outs, send_sems, recv_sems, local_sems, by_chunk):
    x, y, c = _place()
    my = _dev_index((x, y, c))
    copies = []
    for w in range(len(ins)):
        copies.append(pltpu.make_async_copy(ins[w].at[my] if by_chunk else ins[w], outs[w].at[my], local_sems.at[w]))
        for k in range(1, N_DEV):
            peer = _flip(x, y, c, k)
            copies.append(pltpu.make_async_remote_copy(
                src_ref=ins[w].at[_dev_index(peer)] if by_chunk else ins[w], dst_ref=outs[w].at[my],
                send_sem=send_sems.at[7 * w + k - 1], recv_sem=recv_sems.at[7 * w + k - 1],
                device_id=peer, device_id_type=MESH))
    return copies


def _peer_sems(n_w):
    return [pltpu.SemaphoreType.DMA((7 * n_w,)), pltpu.SemaphoreType.DMA((7 * n_w,)), pltpu.SemaphoreType.DMA((n_w,))]


def _chip_index(p):
    return 2 * p[0] + p[1]


def _chip_copies(ins, outs, send_sems, recv_sems, local_sems):
    x, y, c = _place()
    mine = _chip_index((x, y))
    copies = []
    for w in range(len(ins)):
        copies.append(pltpu.make_async_copy(ins[w].at[mine], outs[w].at[mine], local_sems.at[w]))
        for k in (2, 4, 6):
            peer = _flip(x, y, c, k)
            copies.append(pltpu.make_async_remote_copy(
                src_ref=ins[w].at[_chip_index(peer)], dst_ref=outs[w].at[mine],
                send_sem=send_sems.at[7 * w + k - 1], recv_sem=recv_sems.at[7 * w + k - 1],
                device_id=peer, device_id_type=MESH))
    return copies


def _sibling_exchange(dw):
    n_chips = N_DEV // 2

    def body(dw_ref, out_ref, send_sems, recv_sems):
        x, y, c = _place()
        sibling = (x, y, 1 - c)
        copies = []
        for ch in range(n_chips):
            copies.append(pltpu.make_async_remote_copy(
                src_ref=dw_ref.at[2 * ch + (1 - c)], dst_ref=out_ref.at[ch],
                send_sem=send_sems.at[ch], recv_sem=recv_sems.at[ch], device_id=sibling, device_id_type=MESH))
        for cp in copies:
            cp.start()
        for cp in copies:
            cp.wait()

    return pl.pallas_call(
        body, name="sibling_exchange", out_shape=SDS((n_chips,) + dw.shape[1:], dw.dtype),
        in_specs=[HBM_SPEC], out_specs=HBM_SPEC,
        scratch_shapes=[pltpu.SemaphoreType.DMA((n_chips,)), pltpu.SemaphoreType.DMA((n_chips,))],
    )(dw)


def _sibling_sum(name, dw, got, core):
    n_chips, rows, cols = got.shape
    tr = min(rows, 256)

    def body(core_ref, a_ref, b_ref, o_ref):
        o_ref[...] = (a_ref[...].astype(F32) + b_ref[...].astype(F32)).astype(BF16)

    return pl.pallas_call(
        body, name=name,
        grid_spec=pltpu.PrefetchScalarGridSpec(
            num_scalar_prefetch=1, grid=(n_chips, rows // tr),
            in_specs=[pl.BlockSpec((None, tr, cols), lambda ch, i, core_ref: (2 * ch + core_ref[0], i, 0)),
                      pl.BlockSpec((None, tr, cols), lambda ch, i, core_ref: (ch, i, 0))],
            out_specs=pl.BlockSpec((None, tr, cols), lambda ch, i, core_ref: (ch, i, 0))),
        out_shape=SDS(got.shape, BF16), compiler_params=_params(2),
    )(core, dw, got)


ADA_COLS = 3 * D_MODEL // N_DEV


def _ada_forward(c_row, conv_w, conv_b, ada_w, ada_b):
    cw_cols = conv_w.shape[1]

    def body(c_ref, cw_ref, cb_ref, aw_ref, ab_ref, m_ref, cs_ref, cwf_ref, cbf_ref,
             slab, gath, part, land, send_sems, recv_sems):
        x, y, c = _place()
        my = _dev_index((x, y, c))
        slab[...] = jnp.zeros_like(slab)
        slab[0:1, :] = c_ref[...]
        slab[1:4, 0:cw_cols] = cw_ref[...]
        slab[4:5, 0:cw_cols] = cb_ref[...]
        gath[my] = slab[...]
        sends = []
        for k in range(1, N_DEV):
            peer = _flip(x, y, c, k)
            cp = pltpu.make_async_remote_copy(
                src_ref=slab, dst_ref=gath.at[my], send_sem=send_sems.at[k - 1], recv_sem=recv_sems.at[k - 1],
                device_id=peer, device_id_type=MESH)
            cp.start()
            sends.append(cp)
        for cp in sends:
            cp.wait()
        for d in range(N_DEV):
            c_d = gath[d, 0:1, :]
            cs_ref[d:d + 1, :] = c_d * _sigmoid(c_d)
            cwf_ref[:, d * cw_cols:(d + 1) * cw_cols] = gath[d, 1:4, 0:cw_cols]
            cbf_ref[:, d * cw_cols:(d + 1) * cw_cols] = gath[d, 4:5, 0:cw_cols]
        cs = cs_ref[...]
        part[...] = jnp.zeros_like(part)
        for layer in range(2):
            m_part = jnp.dot(cs, aw_ref[layer], preferred_element_type=F32, precision=lax.Precision.HIGHEST)
            for d in range(N_DEV):
                part[d, layer:layer + 1, :] = m_part[d:d + 1, :]
        land[my] = part[my]
        sends = []
        for k in range(1, N_DEV):
            peer = _flip(x, y, c, k)
            cp = pltpu.make_async_remote_copy(
                src_ref=part.at[_dev_index(peer)], dst_ref=land.at[my],
                send_sem=send_sems.at[6 + k], recv_sem=recv_sems.at[6 + k],
                device_id=peer, device_id_type=MESH)
            cp.start()
            sends.append(cp)
        for cp in sends:
            cp.wait()
        for d in range(N_DEV):
            cols = slice(d * ADA_COLS, (d + 1) * ADA_COLS)
            m_ref[:, cols] = land[d, 0:2, :] + ab_ref[:, cols]

    return pl.pallas_call(
        body, name="ada_forward",
        out_shape=[SDS((2, 3 * D_MODEL), F32), SDS((N_DEV, D_MODEL), F32), SDS((3, N_DEV * cw_cols), F32),
                   SDS((1, N_DEV * cw_cols), F32)],
        in_specs=[VMEM_SPEC] * 5, out_specs=[VMEM_SPEC] * 4,
        scratch_shapes=[pltpu.VMEM((8, D_MODEL), F32), pltpu.VMEM((N_DEV, 8, D_MODEL), F32),
                        pltpu.VMEM((N_DEV, 8, ADA_COLS), F32), pltpu.VMEM((N_DEV, 8, ADA_COLS), F32),
                        pltpu.SemaphoreType.DMA((14,)), pltpu.SemaphoreType.DMA((14,))],
        compiler_params=pltpu.CompilerParams(vmem_limit_bytes=VMEM_LIMIT),
    )(c_row, conv_w, conv_b, ada_w, ada_b)


def _small_grads(slab):
    def body(slab_ref, gath_ref, tot_ref, send_sems, recv_sems):
        x, y, c = _place()
        my = _dev_index((x, y, c))
        gath_ref[my] = slab_ref[...]
        sends = []
        for k in range(1, N_DEV):
            peer = _flip(x, y, c, k)
            cp = pltpu.make_async_remote_copy(
                src_ref=slab_ref, dst_ref=gath_ref.at[my], send_sem=send_sems.at[k - 1], recv_sem=recv_sems.at[k - 1],
                device_id=peer, device_id_type=MESH)
            cp.start()
            sends.append(cp)
        for cp in sends:
            cp.wait()
        tot = gath_ref[0]
        for d in range(1, N_DEV):
            tot = tot + gath_ref[d]
        tot_ref[...] = tot

    return pl.pallas_call(
        body, name="small_grads",
        out_shape=[SDS((N_DEV, SLAB_ROWS, D_MODEL), F32), SDS((SLAB_ROWS, D_MODEL), F32)],
        in_specs=[VMEM_SPEC], out_specs=[VMEM_SPEC] * 2,
        scratch_shapes=[pltpu.SemaphoreType.DMA((7,)), pltpu.SemaphoreType.DMA((7,))],
    )(slab)


def _adamw_math(w, g, m, v):
    m = ADAM_B1 * m + (1.0 - ADAM_B1) * g
    v = ADAM_B2 * v + (1.0 - ADAM_B2) * jnp.square(g)
    m_hat = m / (1.0 - ADAM_B1 ** ADAM_STEP)
    v_hat = v / (1.0 - ADAM_B2 ** ADAM_STEP)
    delta = -ADAM_LR * (m_hat / (jnp.sqrt(v_hat) + ADAM_EPS) + ADAM_WD * w)
    return delta, m, v


def _sum_adamw(name, recv, w, m, v):
    rows, cols = w.shape
    tr = min(rows, 256)
    n_slots = recv.shape[0]

    def body(r_ref, w_ref, m_ref, v_ref, g_ref, d_ref, nm_ref, nv_ref):
        g = r_ref[0].astype(F32)
        for d in range(1, n_slots):
            g = g + r_ref[d].astype(F32)
        g_ref[...] = g
        d_ref[...], nm_ref[...], nv_ref[...] = _adamw_math(w_ref[...], g, m_ref[...], v_ref[...])

    blk = pl.BlockSpec((tr, cols), lambda i: (i, 0))
    return pl.pallas_call(
        body, name=name, grid=(rows // tr,),
        in_specs=[pl.BlockSpec((n_slots, tr, cols), lambda i: (0, i, 0)), blk, blk, blk],
        out_specs=[blk] * 4, out_shape=[SDS((rows, cols), F32)] * 4, compiler_params=_params(1),
    )(recv, w, m, v)


def _ada_w_adamw(name, cs_t, dm_cols, w, m, v):
    def body(cs_ref, dm_ref, w_ref, m_ref, v_ref, g_ref, d_ref, nm_ref, nv_ref):
        cs = cs_ref[...]
        dm = dm_ref[...]
        g = cs[:, 0:1] * dm[0:1, :]
        for b in range(1, N_DEV):
            g = g + cs[:, b:b + 1] * dm[b:b + 1, :]
        g_ref[...] = g
        d_ref[...], nm_ref[...], nv_ref[...] = _adamw_math(w_ref[...], g, m_ref[...], v_ref[...])

    blk = pl.BlockSpec((None, D_MODEL, ADA_COLS), lambda l: (l, 0, 0))
    return pl.pallas_call(
        body, name=name, grid=(2,),
        in_specs=[pl.BlockSpec((D_MODEL, N_DEV), lambda l: (0, 0)),
                  pl.BlockSpec((None, N_DEV, ADA_COLS), lambda l: (l, 0, 0)), blk, blk, blk],
        out_specs=[blk] * 4, out_shape=[SDS((2, D_MODEL, ADA_COLS), F32)] * 4, compiler_params=_params(1),
    )(cs_t, dm_cols, w, m, v)


def _small_adamw(name, triples):
    n = len(triples)

    def body(*refs):
        ins, outs = refs[:4 * n], refs[4 * n:]
        for j in range(n):
            w_ref, g_ref, m_ref, v_ref = ins[4 * j:4 * j + 4]
            d, nm, nv = _adamw_math(w_ref[...], g_ref[...], m_ref[...], v_ref[...])
            outs[3 * j][...] = d
            outs[3 * j + 1][...] = nm
            outs[3 * j + 2][...] = nv

    flat = [a for t in triples for a in t]
    return pl.pallas_call(
        body, name=name,
        out_shape=[SDS(t[0].shape, F32) for t in triples for _ in range(3)],
        in_specs=[VMEM_SPEC] * (4 * n), out_specs=[VMEM_SPEC] * (3 * n),
    )(*flat)


def kernel(x, c, norm_g, ada_w, ada_b, even_w_in, pool_w, pool_scale, even_w_out, odd_w_in, conv_w, conv_b, odd_w_out, final_g, loss_target, m_norm_g, m_ada_w, m_ada_b, m_even_w_in, m_pool_w, m_pool_scale, m_even_w_out, m_odd_w_in, m_conv_w, m_conv_b, m_odd_w_out, m_final_g, v_norm_g, v_ada_w, v_ada_b, v_even_w_in, v_pool_w, v_pool_scale, v_even_w_out, v_odd_w_in, v_conv_w, v_conv_b, v_odd_w_out, v_final_g):
    seq = x.shape[1]
    x0 = x[0]
    target = loss_target[0]
    final_g2 = final_g.reshape(1, D_MODEL)

    w_in_e = even_w_in[0]
    w_out_e = even_w_out[0]
    w_in_o = odd_w_in[0]
    w_out_o = odd_w_out[0]
    w_pool = pool_w[0].reshape(N_GROUPS * 32, POOL_GROUP)
    shards = [w.astype(BF16) for w in (w_in_e, w_out_e, w_in_o, w_out_o, w_pool)]

    m_vec, cs_all, conv_w_full, conv_b_full = _ada_forward(c, conv_w[0], conv_b, ada_w, ada_b)
    shift = [m_vec[l:l + 1, 0:D_MODEL] for l in range(2)]
    scale = [m_vec[l:l + 1, D_MODEL:2 * D_MODEL] for l in range(2)]
    gate = [m_vec[l:l + 1, 2 * D_MODEL:] for l in range(2)]
    ng = [norm_g[l:l + 1] for l in range(2)]

    h0, h0_t = _ln_mod("ln_mod0", x0, ng[0], scale[0], shift[0])
    ax, ay, ac = lax.axis_index("x"), lax.axis_index("y"), lax.axis_index("c")
    chips = ((1 - ax, ay), (ax, 1 - ay), (1 - ax, 1 - ay))
    arrival = [None] * N_DEV
    arrival[0], arrival[1] = (ax, ay, ac), (ax, ay, 1 - ac)
    for dd_direct, dd_passed, j in GATHER_ARRIVALS:
        arrival[dd_direct], arrival[dd_passed] = (*chips[j], ac), (*chips[j], 1 - ac)
    order = jnp.stack([_dev_index(p) for p in arrival]).astype(jnp.int32)
    proj0 = _proj_in_gather("proj_in0", h0, shards[0], order)
    o, (wg_out_e, wg_in_o, wg_out_o, wg_pool, wt_in_o) = _attn_fwd("attn_fwd", proj0, shards[1:] + [shards[2].T])
    wf_out_e = wg_out_e.reshape(D_INNER, D_MODEL)
    wf_out_o = wg_out_o.reshape(D_INNER, D_MODEL)
    wf_pool = wg_pool.reshape(N_DEV, N_GROUPS, 32, POOL_GROUP).transpose(1, 0, 2, 3).reshape(N_GROUPS, POOL_GROUP, POOL_GROUP)
    p, yp_raw = _pool_fwd("pool_fwd", proj0, wf_pool)
    ymix0, ymix0_t = _gate_fwd0("gate_fwd0", yp_raw, o, proj0, pool_scale)
    yo0 = _proj_out("proj_out0", ymix0, wf_out_e)

    x1, h1, h1_t = _resid_ln_mod("resid_ln_mod1", x0, yo0, gate[0], ng[1], scale[1], shift[1])
    proj1 = _proj_in("proj_in1", h1, wg_in_o)
    ymix1, ymix1_t = _conv_fwd("conv_fwd", proj1, conv_w_full, conv_b_full)
    yo1 = _proj_out("proj_out1", ymix1, wf_out_o)

    dx2, dyo1, loss_acc, d_final_g, d_gate1 = _final_loss("final_loss", x1, yo1, gate[1], final_g2, target)
    loss = lax.psum(loss_acc[0, 0], ("x", "y", "c"))

    dymix1 = _proj_out_bwd("proj_out1_bwd", dyo1, wf_out_o)
    dw_out_o = _wgrad_out("wgrad_out1", ymix1_t, dyo1)
    dproj1, d_conv_w, d_conv_b = _conv_bwd("conv_bwd", dymix1, proj1, conv_w_full, conv_b_full)
    dw_in_o = _wgrad_in("wgrad_in1", h1_t, [dproj1], D_IN_ODD // N_DEV)
    (dx1, d_shift1, d_scale1, d_ng1, dyo0, d_gate0), _ = _proj_in_bwd_ln(
        "proj_in1_bwd", [dproj1], wt_in_o, x1, dx2, ng[1], scale[1], resid=(yo0, gate[0]))

    dymix0 = _proj_out_bwd("proj_out0_bwd", dyo0, wf_out_e)
    dw_out_e = _wgrad_out("wgrad_out0", ymix0_t, dyo0)
    dyp, do, dgt0, d_pool_scale = _gate_bwd0("gate_bwd0", dymix0, yp_raw, o, proj0, pool_scale)
    du_pool, dw_pool = _pool_bwd("pool_bwd", dyp, p, wf_pool)
    dw_pool_c = dw_pool.reshape(N_GROUPS, N_DEV, 32, POOL_GROUP).transpose(1, 0, 2, 3).reshape(N_DEV, N_GROUPS * 32, POOL_GROUP).astype(BF16)
    ready = [dw_out_e.reshape(N_DEV, D_INNER // N_DEV, D_MODEL), dw_in_o,
             dw_out_o.reshape(N_DEV, D_INNER // N_DEV, D_MODEL), dw_pool_c]
    dq, dk, dv, (r_out_e, r_in_o, r_out_o, r_pool), (wt_in_e,) = _attn_bwd(
        "attn_bwd", proj0, o, do, ready, [shards[0].T])
    dparts0 = [du_pool, dq, dk, dv, dgt0]
    dw_in_e = _wgrad_in("wgrad_in0", h0_t, dparts0, WGRAD_BLOCK)
    core = lax.axis_index("c").astype(jnp.int32).reshape(1)
    chip_sums = _sibling_sum("sibling_sum", dw_in_e, _sibling_exchange(dw_in_e), core)
    (dx0, d_shift0, d_scale0, d_ng0), (r_in_e,) = _proj_in_bwd_ln(
        "proj_in0_bwd", dparts0, wt_in_e, x0, dx1, ng[0], scale[0], dws=[chip_sums])
    grad_x = dx0[None]

    big = {}
    big["even_w_in"] = _sum_adamw("adamw_even_w_in", r_in_e, w_in_e, m_even_w_in[0], v_even_w_in[0])
    big["even_w_out"] = _sum_adamw("adamw_even_w_out", r_out_e, w_out_e, m_even_w_out[0], v_even_w_out[0])
    big["odd_w_in"] = _sum_adamw("adamw_odd_w_in", r_in_o, w_in_o, m_odd_w_in[0], v_odd_w_in[0])
    big["odd_w_out"] = _sum_adamw("adamw_odd_w_out", r_out_o, w_out_o, m_odd_w_out[0], v_odd_w_out[0])
    big["pool_w"] = _sum_adamw("adamw_pool_w", r_pool, w_pool, m_pool_w[0].reshape(N_GROUPS * 32, POOL_GROUP),
                               v_pool_w[0].reshape(N_GROUPS * 32, POOL_GROUP))
    big = {k: [a.reshape(shape) for a in v] for (k, v), shape in zip(
        big.items(), [even_w_in.shape, even_w_out.shape, odd_w_in.shape, odd_w_out.shape, pool_w.shape])}

    dm = jnp.concatenate([jnp.concatenate([d_shift0, d_scale0, d_gate0], axis=1),
                          jnp.concatenate([d_shift1, d_scale1, d_gate1], axis=1)], axis=0)
    slab = jnp.zeros((SLAB_ROWS, D_MODEL), F32)
    slab = slab.at[0:6].set(dm.reshape(6, D_MODEL))
    slab = slab.at[8:9].set(d_ng0).at[9:10].set(d_ng1).at[10:11].set(d_pool_scale).at[11:12].set(d_final_g)
    slab = slab.at[16:22].set(d_conv_w.reshape(6, D_MODEL)).at[24:26].set(d_conv_b.reshape(2, D_MODEL))
    gathered, total = _small_grads(slab)
    my = 4 * lax.axis_index("x") + 2 * lax.axis_index("y") + lax.axis_index("c")
    g_ada_b = total[0:6].reshape(2, 3 * D_MODEL)
    g_norm_g = total[8:10]
    g_pool_scale = total[10:11]
    g_final_g = total[11:12]
    cw_cols = conv_w.shape[2]
    g_conv_w = lax.dynamic_slice_in_dim(total[16:22].reshape(3, D_INNER), my * cw_cols, cw_cols, axis=1)
    g_conv_b = lax.dynamic_slice_in_dim(total[24:26].reshape(1, D_INNER), my * cw_cols, cw_cols, axis=1)
    dm_all = gathered[:, 0:6, :].reshape(N_DEV, 2, 3 * D_MODEL)
    dm_cols = lax.dynamic_slice_in_dim(dm_all, my * ADA_COLS, ADA_COLS, axis=2).transpose(1, 0, 2)
    ada = _ada_w_adamw("adamw_ada_w", cs_all.T, dm_cols, ada_w, m_ada_w, v_ada_w)

    small = _small_adamw("adamw_small", [
        (norm_g, g_norm_g, m_norm_g, v_norm_g),
        (ada_b, g_ada_b, m_ada_b, v_ada_b),
        (pool_scale, g_pool_scale, m_pool_scale, v_pool_scale),
        (conv_w[0], g_conv_w, m_conv_w[0], v_conv_w[0]),
        (conv_b, g_conv_b, m_conv_b, v_conv_b),
        (final_g2, g_final_g, m_final_g.reshape(1, D_MODEL), v_final_g.reshape(1, D_MODEL)),
    ])
    small = [small[3 * j:3 * j + 3] for j in range(6)]

    grads = {
        "norm_g": g_norm_g, "ada_w": ada[0], "ada_b": g_ada_b, "even_w_in": big["even_w_in"][0],
        "pool_w": big["pool_w"][0], "pool_scale": g_pool_scale, "even_w_out": big["even_w_out"][0],
        "odd_w_in": big["odd_w_in"][0], "conv_w": g_conv_w.reshape(conv_w.shape), "conv_b": g_conv_b,
        "odd_w_out": big["odd_w_out"][0], "final_g": g_final_g.reshape(D_MODEL),
    }
    rest = []
    for idx in range(3):
        rest += [
            small[0][idx], ada[1 + idx], small[1][idx], big["even_w_in"][1 + idx], big["pool_w"][1 + idx],
            small[2][idx], big["even_w_out"][1 + idx], big["odd_w_in"][1 + idx],
            small[3][idx].reshape(conv_w.shape), small[4][idx], big["odd_w_out"][1 + idx],
            small[5][idx].reshape(D_MODEL),
        ]
    order = ["norm_g", "ada_w", "ada_b", "even_w_in", "pool_w", "pool_scale", "even_w_out", "odd_w_in",
             "conv_w", "conv_b", "odd_w_out", "final_g"]
    return (loss, grad_x, *[grads[n] for n in order], *rest)
```

```python
import jax
import jax.numpy as jnp
from jax import lax
from jax.experimental import pallas as pl
from jax.experimental.pallas import tpu as pltpu

F32 = jnp.float32
BF16 = jnp.bfloat16
SDS = jax.ShapeDtypeStruct
MESH = pl.DeviceIdType.MESH

N_DEV = 8
D_MODEL = 1024
D_INNER = 2048
D_POOL = 1024
D_SB = 1024
N_GROUPS = 4
POOL_GROUP = 256
HEAD_DIM = 64
LANES = 128
D_IN_EVEN = 6144
D_IN_ODD = 8192
EPS = 1e-6
ADAM_LR = 0.001
ADAM_B1 = 0.9
ADAM_B2 = 0.999
ADAM_EPS = 1e-08
ADAM_WD = 0.01
ADAM_STEP = 10

ROW_TILE = 256
ATT_TILE = 256
HALO = 16
VMEM_LIMIT = 48 * 1024 * 1024
ATT_BWD_VMEM_LIMIT = 56 * 1024 * 1024
SLAB_ROWS = 32
WGRAD_BLOCK = 256


def _params(n_axes, vmem_limit=VMEM_LIMIT):
    return pltpu.CompilerParams(dimension_semantics=("arbitrary",) * n_axes, vmem_limit_bytes=vmem_limit)


def _sigmoid(x):
    return 1.0 / (1.0 + jnp.exp(-x))


def _dot(a, b):
    return jnp.dot(a, b, preferred_element_type=F32)


def _dot_nt(a, b):
    return lax.dot_general(a, b, (((1,), (1,)), ((), ())), preferred_element_type=F32)


def _dot_tn(a, b):
    return lax.dot_general(a, b, (((0,), (0,)), ((), ())), preferred_element_type=F32)


def _mm(name, a, b, *, grid, a_spec, b_spec, o_spec, o_shape, o_dtype, dot, acc_axis=None, acc_shape=None):
    n_acc = grid[acc_axis] if acc_axis is not None else 1

    def body(a_ref, b_ref, o_ref, *scratch):
        prod = dot(a_ref[...], b_ref[...])
        if acc_axis is None:
            o_ref[...] = prod.astype(o_dtype)
        else:
            acc = scratch[0]
            k = pl.program_id(acc_axis)

            @pl.when(k == 0)
            def _():
                acc[...] = prod

            @pl.when(k > 0)
            def _():
                acc[...] += prod

            @pl.when(k == n_acc - 1)
            def _():
                o_ref[...] = acc[...].astype(o_dtype)

    scratch = [] if acc_axis is None else [pltpu.VMEM(acc_shape, F32)]
    return pl.pallas_call(
        body, name=name, grid=grid, in_specs=[a_spec, b_spec], out_specs=o_spec,
        out_shape=SDS(o_shape, o_dtype), scratch_shapes=scratch, compiler_params=_params(len(grid)),
    )(a, b)


def _proj_in(name, h, wg):
    s = h.shape[0]
    cn = wg.shape[2]
    tm = min(s, ROW_TILE)

    def body(a_ref, w_ref, o_ref):
        a = a_ref[...]
        for d in range(N_DEV):
            o_ref[:, d * cn:(d + 1) * cn] = _dot(a, w_ref[d]).astype(BF16)

    return pl.pallas_call(
        body, name=name, grid=(s // tm,),
        in_specs=[pl.BlockSpec((tm, D_MODEL), lambda i: (i, 0)),
                  pl.BlockSpec((N_DEV, D_MODEL, cn), lambda i: (0, 0, 0), pipeline_mode=pl.Buffered(1))],
        out_specs=pl.BlockSpec((tm, N_DEV * cn), lambda i: (i, 0)),
        out_shape=SDS((s, N_DEV * cn), BF16), compiler_params=_params(1),
    )(h, wg)


GATHER_ROWS = 1024
GATHER_ARRIVALS = ((2, 4, 0), (3, 5, 1), (6, 7, 2))


def _proj_in_gather(name, h, w_shard, order):
    s = h.shape[0]
    k_dim, cn = w_shard.shape
    tm = min(s, GATHER_ROWS)
    n_i = s // tm

    def body(ord_ref, a_ref, w_ref, o_ref, w_buf, send_sems, recv_sems, local_sem):
        dd = pl.program_id(0)
        i = pl.program_id(1)
        x, y, c = _place()
        me, sibling = (x, y, c), (x, y, 1 - c)
        chips = [(1 - x, y), (x, 1 - y), (1 - x, 1 - y)]

        def copy(k, block, to, from_src=False):
            slot = w_buf.at[_dev_index(block)]
            return pltpu.make_async_remote_copy(
                src_ref=w_ref if from_src else slot, dst_ref=slot, send_sem=send_sems.at[k], recv_sem=recv_sems.at[k],
                device_id=to, device_id_type=MESH)

        mine = pltpu.make_async_copy(w_ref, w_buf.at[_dev_index(me)], local_sem)
        first = [copy(0, me, sibling, True)] + [copy(1 + j, me, (*chip, c), True) for j, chip in enumerate(chips)]
        passed = [copy(4 + j, (*chip, c), sibling) for j, chip in enumerate(chips)]

        @pl.when((dd == 0) & (i == 0))
        def _():
            mine.start()
            for cp in first:
                cp.start()
            mine.wait()

        @pl.when((dd == 1) & (i == 0))
        def _():
            copy(0, sibling, me).wait_recv()

        for dd_direct, dd_passed, j in GATHER_ARRIVALS:
            @pl.when((dd == dd_direct) & (i == 0))
            def _(j=j):
                copy(1 + j, (*chips[j], c), me).wait_recv()
                passed[j].start()

            @pl.when((dd == dd_passed) & (i == 0))
            def _(j=j):
                copy(4 + j, (*chips[j], 1 - c), me).wait_recv()

        o_ref[...] = _dot(a_ref[...], w_buf[ord_ref[dd]]).astype(BF16)

        @pl.when((dd == N_DEV - 1) & (i == n_i - 1))
        def _():
            for cp in first + passed:
                cp.wait_send()

    return pl.pallas_call(
        body, name=name,
        grid_spec=pltpu.PrefetchScalarGridSpec(
            num_scalar_prefetch=1, grid=(N_DEV, n_i),
            in_specs=[pl.BlockSpec((tm, k_dim), lambda dd, i, ord_ref: (i, 0)), HBM_SPEC],
            out_specs=pl.BlockSpec((tm, cn), lambda dd, i, ord_ref: (i, ord_ref[dd])),
            scratch_shapes=[pltpu.VMEM((N_DEV, k_dim, cn), BF16), pltpu.SemaphoreType.DMA((7,)),
                            pltpu.SemaphoreType.DMA((7,)), pltpu.SemaphoreType.DMA]),
        out_shape=SDS((s, N_DEV * cn), BF16), compiler_params=_params(2),
    )(order, h, w_shard)


def _proj_out(name, y, w):
    s = y.shape[0]
    tm = min(s, 512)
    return _mm(name, y, w, grid=(s // tm,),
               a_spec=pl.BlockSpec((tm, D_INNER), lambda i: (i, 0)),
               b_spec=pl.BlockSpec((D_INNER, D_MODEL), lambda i: (0, 0)),
               o_spec=pl.BlockSpec((tm, D_MODEL), lambda i: (i, 0)),
               o_shape=(s, D_MODEL), o_dtype=F32, dot=_dot)


def _proj_out_bwd(name, dyo, w):
    s = dyo.shape[0]
    tm = min(s, 512)
    return _mm(name, dyo, w, grid=(s // tm,),
               a_spec=pl.BlockSpec((tm, D_MODEL), lambda i: (i, 0)),
               b_spec=pl.BlockSpec((D_INNER, D_MODEL), lambda i: (0, 0)),
               o_spec=pl.BlockSpec((tm, D_INNER), lambda i: (i, 0)),
               o_shape=(s, D_INNER), o_dtype=F32, dot=_dot_nt)


def _wgrad_out(name, y_t, dyo):
    s = y_t.shape[1]
    tm = 512
    return _mm(name, y_t, dyo, grid=(D_INNER // tm,),
               a_spec=pl.BlockSpec((tm, s), lambda r: (r, 0)),
               b_spec=pl.BlockSpec((s, D_MODEL), lambda r: (0, 0), pipeline_mode=pl.Buffered(1)),
               o_spec=pl.BlockSpec((tm, D_MODEL), lambda r: (r, 0)),
               o_shape=(D_INNER, D_MODEL), o_dtype=BF16, dot=_dot)


def _proj_in_bwd_ln(name, parts, wt, x, dx_next, g, scale, resid=None, dws=()):
    s = x.shape[0]
    widths = [p.shape[1] for p in parts]
    offs = [sum(widths[:k]) for k in range(len(parts))]
    k_all = sum(widths)
    n_i = s // ROW_TILE
    n_p, n_r, n_side = len(parts), (2 if resid else 0), len(dws)
    w_all = wt.reshape(k_all, D_MODEL)

    def body(*refs):
        part_refs = refs[:n_p]
        w_ref, x_ref, dxn_ref, g_ref, sc_ref = refs[n_p:n_p + 5]
        resid_refs = refs[n_p + 5:n_p + 5 + n_r]
        srcs = refs[n_p + 5 + n_r:n_p + 5 + n_r + n_side]
        outs = refs[n_p + 5 + n_r + n_side:]
        dx_ref, dsh_ref, dsc_ref, dg_ref = outs[:4]
        resid_outs = outs[4:4 + n_r]
        dsts = outs[4 + n_r:4 + n_r + n_side]
        sems = outs[4 + n_r + n_side:]
        i = pl.program_id(0)
        if n_side:
            _side_exchange(srcs + dsts + sems, n_side, True, i == 0, i == n_i - 1, chips=True)

        @pl.when(i == 0)
        def _():
            dsh_ref[...] = jnp.zeros_like(dsh_ref)
            dsc_ref[...] = jnp.zeros_like(dsc_ref)
            dg_ref[...] = jnp.zeros_like(dg_ref)
            if resid:
                resid_outs[1][...] = jnp.zeros_like(resid_outs[1])

        dh_v = _dot(part_refs[0][...], w_ref[offs[0]:offs[0] + widths[0], :])
        for k in range(1, n_p):
            dh_v = dh_v + _dot(part_refs[k][...], w_ref[offs[k]:offs[k] + widths[k], :])
        xv = x_ref[...]
        g_v = g_ref[...]
        r = lax.rsqrt(jnp.mean(xv * xv, axis=-1, keepdims=True) + EPS)
        xn = xv * r
        dsh_ref[...] += jnp.sum(dh_v, axis=0, keepdims=True)
        dsc_ref[...] += jnp.sum(dh_v * (xn * g_v), axis=0, keepdims=True)
        dn = dh_v * (1.0 + sc_ref[...])
        dg_ref[...] += jnp.sum(dn * xn, axis=0, keepdims=True)
        dxh = dn * g_v
        dx = dxn_ref[...] + r * (dxh - xn * jnp.mean(dxh * xn, axis=-1, keepdims=True))
        dx_ref[...] = dx
        if resid:
            yo_ref, gt_ref = resid_refs
            resid_outs[0][...] = (dx * (1.0 + gt_ref[...])).astype(BF16)
            resid_outs[1][...] += jnp.sum(dx * yo_ref[...], axis=0, keepdims=True)

    row, vec = _row_spec(), _vec_spec()
    out = pl.pallas_call(
        body, name=name, grid=(n_i,),
        in_specs=[_row_spec(w) for w in widths]
        + [pl.BlockSpec((k_all, D_MODEL), lambda i: (0, 0), pipeline_mode=pl.Buffered(1)), row, row, vec, vec]
        + ([row, vec] if resid else []) + [HBM_SPEC] * n_side,
        out_specs=[row, vec, vec, vec] + ([row, vec] if resid else []) + [HBM_SPEC] * n_side,
        out_shape=[SDS((s, D_MODEL), F32)] + [SDS((1, D_MODEL), F32)] * 3
        + ([SDS((s, D_MODEL), BF16), SDS((1, D_MODEL), F32)] if resid else [])
        + [SDS(dw.shape, dw.dtype) for dw in dws],
        scratch_shapes=_peer_sems(n_side) if n_side else [],
        compiler_params=_params(1),
    )(*parts, w_all, x, dx_next, g, scale, *(resid or ()), *dws)
    return out[:4 + n_r], out[4 + n_r:]


def _wgrad_in(name, h_t, parts, blk):
    s = h_t.shape[1]
    widths = [p.shape[1] for p in parts]
    cn = sum(widths) // N_DEV
    per_dev = cn // blk
    starts = [sum(widths[:k]) // blk for k in range(len(parts))]
    counts = [w // blk for w in widths]
    n_blk = sum(counts)

    def body(a_ref, *rest):
        o_ref = rest[len(parts)]
        b = pl.program_id(0)
        for k in range(len(parts)):
            @pl.when((b >= starts[k]) & (b < starts[k] + counts[k]))
            def _(k=k):
                o_ref[...] = _dot(a_ref[...], rest[k][...]).astype(BF16)

    def part_spec(k):
        return pl.BlockSpec((s, blk), lambda b: (0, jnp.clip(b - starts[k], 0, counts[k] - 1)))

    return pl.pallas_call(
        body, name=name, grid=(n_blk,),
        in_specs=[pl.BlockSpec((D_MODEL, s), lambda b: (0, 0), pipeline_mode=pl.Buffered(1))]
        + [part_spec(k) for k in range(len(parts))],
        out_specs=pl.BlockSpec((None, D_MODEL, blk), lambda b: (b // per_dev, 0, b % per_dev)),
        out_shape=SDS((N_DEV, D_MODEL, cn), BF16), compiler_params=_params(1),
    )(h_t, *parts)


def _vec_spec():
    return pl.BlockSpec((1, D_MODEL), lambda i: (0, 0))


def _row_spec(width=D_MODEL, col=0):
    return pl.BlockSpec((ROW_TILE, width), lambda i: (i, col))


def _col_spec():
    return pl.BlockSpec((D_MODEL, ROW_TILE), lambda i: (0, i))


def _ln_mod(name, x, g, scale, shift):
    s = x.shape[0]

    def body(x_ref, g_ref, sc_ref, sh_ref, h_ref, ht_ref):
        xv = x_ref[...]
        r = lax.rsqrt(jnp.mean(xv * xv, axis=-1, keepdims=True) + EPS)
        n = (xv * r) * g_ref[...]
        h = (n * (1.0 + sc_ref[...]) + sh_ref[...]).astype(BF16)
        h_ref[...] = h
        ht_ref[...] = h.T

    return pl.pallas_call(
        body, name=name, grid=(s // ROW_TILE,),
        in_specs=[_row_spec(), _vec_spec(), _vec_spec(), _vec_spec()], out_specs=[_row_spec(), _col_spec()],
        out_shape=[SDS((s, D_MODEL), BF16), SDS((D_MODEL, s), BF16)], compiler_params=_params(1),
    )(x, g, scale, shift)


def _resid_ln_mod(name, x, yo, gate, g, scale, shift):
    s = x.shape[0]

    def body(x_ref, yo_ref, gt_ref, g_ref, sc_ref, sh_ref, xn_ref, h_ref, ht_ref):
        xv = x_ref[...] + (1.0 + gt_ref[...]) * yo_ref[...]
        xn_ref[...] = xv
        r = lax.rsqrt(jnp.mean(xv * xv, axis=-1, keepdims=True) + EPS)
        n = (xv * r) * g_ref[...]
        h = (n * (1.0 + sc_ref[...]) + sh_ref[...]).astype(BF16)
        h_ref[...] = h
        ht_ref[...] = h.T

    return pl.pallas_call(
        body, name=name, grid=(s // ROW_TILE,),
        in_specs=[_row_spec(), _row_spec(), _vec_spec(), _vec_spec(), _vec_spec(), _vec_spec()],
        out_specs=[_row_spec(), _row_spec(), _col_spec()],
        out_shape=[SDS((s, D_MODEL), F32), SDS((s, D_MODEL), BF16), SDS((D_MODEL, s), BF16)],
        compiler_params=_params(1),
    )(x, yo, gate, g, scale, shift)


def _final_loss(name, x1, yo1, gate1, gf, target):
    s = x1.shape[0]

    def body(x_ref, yo_ref, gt_ref, gf_ref, t_ref, dx_ref, dyo_ref, loss_ref, dgf_ref, dgt_ref):
        i = pl.program_id(0)

        @pl.when(i == 0)
        def _():
            loss_ref[...] = jnp.zeros_like(loss_ref)
            dgf_ref[...] = jnp.zeros_like(dgf_ref)
            dgt_ref[...] = jnp.zeros_like(dgt_ref)

        yo = yo_ref[...]
        one_gate = 1.0 + gt_ref[...]
        x2 = x_ref[...] + one_gate * yo
        r = lax.rsqrt(jnp.mean(x2 * x2, axis=-1, keepdims=True) + EPS)
        xn = x2 * r
        gf_v = gf_ref[...]
        err = xn * gf_v - t_ref[...]
        loss_ref[...] += 0.5 * jnp.sum(jnp.mean(err * err, axis=-1, keepdims=True))
        dout = err * (1.0 / D_MODEL)
        dgf_ref[...] += jnp.sum(dout * xn, axis=0, keepdims=True)
        dxn = dout * gf_v
        dx2 = r * (dxn - xn * jnp.mean(dxn * xn, axis=-1, keepdims=True))
        dx_ref[...] = dx2
        dyo_ref[...] = (dx2 * one_gate).astype(BF16)
        dgt_ref[...] += jnp.sum(dx2 * yo, axis=0, keepdims=True)

    return pl.pallas_call(
        body, name=name, grid=(s // ROW_TILE,),
        in_specs=[_row_spec(), _row_spec(), _vec_spec(), _vec_spec(), _row_spec()],
        out_specs=[_row_spec(), _row_spec(), pl.BlockSpec((1, LANES), lambda i: (0, 0)), _vec_spec(), _vec_spec()],
        out_shape=[SDS((s, D_MODEL), F32), SDS((s, D_MODEL), BF16), SDS((1, LANES), F32),
                   SDS((1, D_MODEL), F32), SDS((1, D_MODEL), F32)],
        compiler_params=_params(1),
    )(x1, yo1, gate1, gf, target)


POOL_WINDOWS = (2, 4, 8, 16)


def _window_sum(x, window, rows, backward):
    acc, step = x, 1
    while step < window:
        acc = acc + pltpu.roll(acc, step if backward else rows - step, axis=0)
        step *= 2
    return acc


def _pool_fwd(name, proj0, wp):
    s = proj0.shape[0]
    hb = ROW_TILE // HALO
    ext_rows = ROW_TILE + HALO

    def body(u_ref, halo_ref, w_ref, p_ref, y_ref):
        i = pl.program_id(0)
        t = i * ROW_TILE + lax.broadcasted_iota(jnp.int32, (ROW_TILE, 1), 0)
        for g, window in enumerate(POOL_WINDOWS):
            cols = slice(g * POOL_GROUP, (g + 1) * POOL_GROUP)
            u = u_ref[:, cols].astype(F32)
            halo = jnp.where(i == 0, 0.0, halo_ref[:, cols].astype(F32))
            ext = jnp.concatenate([halo, u], axis=0)
            win = _window_sum(ext, window, ext_rows, True)[HALO:, :]
            cnt = jnp.minimum(t + 1, window).astype(F32)
            p = (win / cnt - u).astype(BF16)
            p_ref[:, cols] = p
            y_ref[:, cols] = _dot(p, w_ref[g])

    return pl.pallas_call(
        body, name=name, grid=(s // ROW_TILE,),
        in_specs=[pl.BlockSpec((ROW_TILE, D_POOL), lambda i: (i, 0)),
                  pl.BlockSpec((HALO, D_POOL), lambda i: (jnp.maximum(i * hb - 1, 0), 0)),
                  pl.BlockSpec((N_GROUPS, POOL_GROUP, POOL_GROUP), lambda i: (0, 0, 0))],
        out_specs=[_row_spec(D_POOL), _row_spec(D_POOL)],
        out_shape=[SDS((s, D_POOL), BF16), SDS((s, D_POOL), F32)], compiler_params=_params(1),
    )(proj0, proj0, wp)


def _pool_bwd(name, dyp, p, wp):
    s = dyp.shape[0]
    hb = ROW_TILE // HALO
    n_hb = s // HALO
    n_tiles = s // ROW_TILE
    ext_rows = ROW_TILE + HALO

    def body(dy_ref, nxt_ref, p_ref, w_ref, du_ref, dw_ref):
        i = pl.program_id(0)

        @pl.when(i == 0)
        def _():
            dw_ref[...] = jnp.zeros_like(dw_ref)

        t = i * ROW_TILE + lax.broadcasted_iota(jnp.int32, (ext_rows, 1), 0)
        for g, window in enumerate(POOL_WINDOWS):
            cols = slice(g * POOL_GROUP, (g + 1) * POOL_GROUP)
            dy = dy_ref[:, cols]
            nxt = nxt_ref[:, cols]
            nxt = jnp.where(i == n_tiles - 1, jnp.zeros_like(nxt), nxt)
            dp = _dot_nt(jnp.concatenate([dy, nxt], axis=0), w_ref[g])
            cnt = jnp.minimum(t + 1, window).astype(F32)
            win = _window_sum(dp / cnt, window, ext_rows, False)[:ROW_TILE, :]
            du_ref[:, cols] = (win - dp[:ROW_TILE, :]).astype(BF16)
            dw_ref[g] += _dot_tn(p_ref[:, cols], dy)

    return pl.pallas_call(
        body, name=name, grid=(n_tiles,),
        in_specs=[_row_spec(D_POOL),
                  pl.BlockSpec((HALO, D_POOL), lambda i: (jnp.minimum((i + 1) * hb, n_hb - 1), 0)),
                  _row_spec(D_POOL),
                  pl.BlockSpec((N_GROUPS, POOL_GROUP, POOL_GROUP), lambda i: (0, 0, 0))],
        out_specs=[_row_spec(D_POOL), pl.BlockSpec((N_GROUPS, POOL_GROUP, POOL_GROUP), lambda i: (0, 0, 0))],
        out_shape=[SDS((s, D_POOL), BF16), SDS((N_GROUPS, POOL_GROUP, POOL_GROUP), F32)],
        compiler_params=_params(1),
    )(dyp, dyp, p, wp)


FWD_HEADS_PER_STEP = 16
BWD_HEADS_PER_STEP = 8
ATT_SCALE = 0.125
FWD_SKEW = 1


def _att_groups(nh):
    lanes = nh * HEAD_DIM
    return lanes, D_SB // lanes, D_POOL // lanes, (D_POOL + D_SB) // lanes, (D_POOL + 2 * D_SB) // lanes


def _att_consts():
    r = lax.broadcasted_iota(jnp.int32, (ATT_TILE, ATT_TILE), 0)
    c = lax.broadcasted_iota(jnp.int32, (ATT_TILE, ATT_TILE), 1)
    first = lax.broadcasted_iota(jnp.int32, (1, LANES), 1) < HEAD_DIM
    return r, c, first


def _pair(x, p):
    return x[:, p * LANES:(p + 1) * LANES]


def _one_head(x, first, hh):
    zero = jnp.zeros_like(x)
    return jnp.where(first, x, zero) if hh == 0 else jnp.where(first, zero, x)


def _neg_softplus(z):
    return -(jnp.maximum(z, 0.0) + jnp.log(1.0 + jnp.exp(-jnp.abs(z))))


def _side_exchange(side_refs, n_side, by_chunk, is_first, is_last, chips=False):
    ins, outs = side_refs[:n_side], side_refs[n_side:2 * n_side]
    sems = side_refs[2 * n_side:2 * n_side + 3]

    def copies():
        return _chip_copies(ins, outs, *sems) if chips else _peer_copies(ins, outs, *sems, by_chunk=by_chunk)

    @pl.when(is_first)
    def _():
        for cp in copies():
            cp.start()

    @pl.when(is_last)
    def _():
        for cp in copies():
            cp.wait()


def _attn_fwd(name, proj0, shards):
    s = proj0.shape[0]
    nq = s // ATT_TILE
    nh = FWD_HEADS_PER_STEP
    ATT_GROUP, N_ATT_GROUPS, Q_GRP, K_GRP, V_GRP = _att_groups(nh)
    n_side = len(shards)

    def body(q_ref, k_ref, v_ref, *rest):
        o_ref = rest[n_side]
        side = rest[:n_side] + rest[n_side + 1:]
        j = pl.program_id(0)
        i = pl.program_id(1)
        _side_exchange(side, n_side, False, (j == 0) & (i == 0), (j == N_ATT_GROUPS - 1) & (i == nq - 1))
        r, c, first = _att_consts()
        tri = (r >= c).astype(BF16)
        below = c < r
        q = q_ref[...] * ATT_SCALE
        qh = [_one_head(_pair(q, h // 2), first, h % 2) for h in range(nh)]

        def tile(kb, carry, diagonal):
            k0 = pl.multiple_of(kb * ATT_TILE, ATT_TILE)
            kt = k_ref[pl.ds(k0, ATT_TILE), :]
            vt = v_ref[pl.ds(k0, ATT_TILE), :]
            z, lf_b, a_b = [None] * nh, [None] * nh, [None] * nh
            out_c, out_acc = [None] * nh, [None] * nh
            for t in range(nh + 2 * FWD_SKEW):
                if t < nh:
                    z[t] = _dot_nt(qh[t], _pair(kt, t // 2))
                    lf = _neg_softplus(z[t])
                    if diagonal:
                        lf = jnp.where(below, lf, 0.0)
                    lf_b[t] = lf.astype(BF16)
                    out_c[t] = carry[t] + jnp.sum(lf, axis=1, keepdims=True)
                u = t - FWD_SKEW
                if 0 <= u < nh:
                    a = jnp.exp(z[u] + _dot(lf_b[u], tri) + carry[u])
                    if diagonal:
                        a = jnp.where(below, a, 0.0)
                    a_b[u] = a.astype(BF16)
                w = t - 2 * FWD_SKEW
                if 0 <= w < nh:
                    out_acc[w] = carry[nh + w] + _dot(a_b[w], _pair(vt, w // 2))
            return tuple(out_c + out_acc)

        init = tuple([jnp.zeros((ATT_TILE, 1), F32)] * nh + [jnp.zeros((ATT_TILE, LANES), F32)] * nh)
        carry = tile(i, init, True)
        carry = lax.fori_loop(1, i + 1, lambda n, cr: tile(i - n, cr, False), carry)
        for p in range(nh // 2):
            o_ref[:, p * LANES:(p + 1) * LANES] = jnp.where(first, carry[nh + 2 * p], carry[nh + 2 * p + 1])

    out = pl.pallas_call(
        body, name=name, grid=(N_ATT_GROUPS, nq),
        in_specs=[pl.BlockSpec((ATT_TILE, ATT_GROUP), lambda j, i: (i, Q_GRP + j)),
                  pl.BlockSpec((s, ATT_GROUP), lambda j, i: (0, K_GRP + j), pipeline_mode=pl.Buffered(1)),
                  pl.BlockSpec((s, ATT_GROUP), lambda j, i: (0, V_GRP + j), pipeline_mode=pl.Buffered(1))]
        + [HBM_SPEC] * n_side,
        out_specs=[pl.BlockSpec((ATT_TILE, ATT_GROUP), lambda j, i: (i, j))] + [HBM_SPEC] * n_side,
        out_shape=[SDS((s, D_SB), F32)] + [SDS((N_DEV,) + sh.shape, sh.dtype) for sh in shards],
        scratch_shapes=_peer_sems(n_side), compiler_params=_params(2),
    )(proj0, proj0, proj0, *shards)
    return out[0], out[1:]


def _attn_bwd(name, proj0, o, do, dws, shards):
    s = proj0.shape[0]
    nq = s // ATT_TILE
    nh = BWD_HEADS_PER_STEP
    ATT_GROUP, N_ATT_GROUPS, Q_GRP, K_GRP, V_GRP = _att_groups(nh)
    n1, n2 = len(dws), len(shards)
    n_side = n1 + n2

    def body(q_ref, k_ref, v_ref, o_ref, do_ref, *rest):
        dq_ref, dk_ref, dv_ref = rest[n_side:n_side + 3]
        dk_acc, dv_acc = rest[2 * n_side + 3:2 * n_side + 5]
        srcs, dsts, sems = rest[:n_side], rest[n_side + 3:2 * n_side + 3], rest[2 * n_side + 5:]
        j = pl.program_id(0)
        i = pl.program_id(1)
        is_first, is_last = (j == 0) & (i == 0), (j == N_ATT_GROUPS - 1) & (i == nq - 1)
        _side_exchange(srcs[:n1] + dsts[:n1] + sems[:3], n1, True, is_first, is_last)
        _side_exchange(srcs[n1:] + dsts[n1:] + sems[3:], n2, False, is_first, is_last)

        @pl.when(i == 0)
        def _():
            dk_acc[...] = jnp.zeros_like(dk_acc)
            dv_acc[...] = jnp.zeros_like(dv_acc)

        r, c, first = _att_consts()
        tri = (r >= c).astype(BF16)
        tri_p = (r <= c).astype(BF16)
        below = c < r
        q = q_ref[...] * ATT_SCALE
        do_b = do_ref[...].astype(BF16)
        do_o = do_b.astype(F32) * o_ref[...]
        qh = [_one_head(_pair(q, h // 2), first, h % 2) for h in range(nh)]
        doh = [_one_head(_pair(do_b, h // 2), first, h % 2) for h in range(nh)]
        dsum = [jnp.sum(_one_head(_pair(do_o, h // 2), first, h % 2), axis=1, keepdims=True) for h in range(nh)]

        def tile(kb, carry, diagonal):
            k0 = pl.multiple_of(kb * ATT_TILE, ATT_TILE)
            kt = k_ref[pl.ds(k0, ATT_TILE), :]
            vt = v_ref[pl.ds(k0, ATT_TILE), :]
            none = lambda: [None] * nh
            z, d_a, sig, lf_b, a_b, g, early, dz = none(), none(), none(), none(), none(), none(), none(), none()
            out_c1, out_c2, out_dq, dk_t, dv_t = none(), none(), none(), none(), none()
            for t in range(nh + 3):
                if t < nh:
                    z[t] = _dot_nt(qh[t], _pair(kt, t // 2))
                    d_a[t] = _dot_nt(doh[t], _pair(vt, t // 2))
                    lf = _neg_softplus(z[t])
                    sig[t] = jnp.exp(z[t] + lf)
                    if diagonal:
                        lf = jnp.where(below, lf, 0.0)
                    lf_b[t] = lf.astype(BF16)
                    out_c1[t] = carry[t] + jnp.sum(lf, axis=1, keepdims=True)
                u = t - 1
                if 0 <= u < nh:
                    a = jnp.exp(z[u] + _dot(lf_b[u], tri) + carry[u])
                    if diagonal:
                        a = jnp.where(below, a, 0.0)
                    a_b[u] = a.astype(BF16)
                    g[u] = a_b[u].astype(F32) * d_a[u]
                    g_sum = jnp.sum(g[u], axis=1, keepdims=True)
                    early[u] = dsum[u] - carry[nh + u] - g_sum
                    out_c2[u] = carry[nh + u] + g_sum
                w = t - 2
                if 0 <= w < nh:
                    upto = _dot(g[w].astype(BF16), tri_p)
                    dv_t[w] = _dot_tn(a_b[w], doh[w])
                    d = g[w] - sig[w] * (early[w] + upto)
                    if diagonal:
                        d = jnp.where(below, d, 0.0)
                    dz[w] = d.astype(BF16)
                y = t - 3
                if 0 <= y < nh:
                    out_dq[y] = carry[2 * nh + y] + _dot(dz[y], _pair(kt, y // 2))
                    dk_t[y] = _dot_tn(dz[y], qh[y])
            for p in range(nh // 2):
                dk_acc[pl.ds(k0, ATT_TILE), p * LANES:(p + 1) * LANES] += dk_t[2 * p] + dk_t[2 * p + 1]
                dv_acc[pl.ds(k0, ATT_TILE), p * LANES:(p + 1) * LANES] += dv_t[2 * p] + dv_t[2 * p + 1]
            return tuple(out_c1 + out_c2 + out_dq)

        init = tuple([jnp.zeros((ATT_TILE, 1), F32)] * (2 * nh) + [jnp.zeros((ATT_TILE, LANES), F32)] * nh)
        carry = tile(i, init, True)
        carry = lax.fori_loop(1, i + 1, lambda n, cr: tile(i - n, cr, False), carry)
        for p in range(nh // 2):
            dq_p = jnp.where(first, carry[2 * nh + 2 * p], carry[2 * nh + 2 * p + 1]) * ATT_SCALE
            dq_ref[:, p * LANES:(p + 1) * LANES] = dq_p.astype(BF16)

        @pl.when(i == nq - 1)
        def _():
            dk_ref[...] = dk_acc[...].astype(BF16)
            dv_ref[...] = dv_acc[...].astype(BF16)

    tile_spec = pl.BlockSpec((ATT_TILE, ATT_GROUP), lambda j, i: (i, j))
    full = pl.BlockSpec((s, ATT_GROUP), lambda j, i: (0, j), pipeline_mode=pl.Buffered(1))
    out = pl.pallas_call(
        body, name=name, grid=(N_ATT_GROUPS, nq),
        in_specs=[pl.BlockSpec((ATT_TILE, ATT_GROUP), lambda j, i: (i, Q_GRP + j)),
                  pl.BlockSpec((s, ATT_GROUP), lambda j, i: (0, K_GRP + j), pipeline_mode=pl.Buffered(1)),
                  pl.BlockSpec((s, ATT_GROUP), lambda j, i: (0, V_GRP + j), pipeline_mode=pl.Buffered(1)),
                  tile_spec, tile_spec] + [HBM_SPEC] * n_side,
        out_specs=[tile_spec, full, full] + [HBM_SPEC] * n_side,
        out_shape=[SDS((s, D_SB), BF16)] * 3 + [SDS(dw.shape, dw.dtype) for dw in dws]
        + [SDS((N_DEV,) + sh.shape, sh.dtype) for sh in shards],
        scratch_shapes=[pltpu.VMEM((s, ATT_GROUP), F32), pltpu.VMEM((s, ATT_GROUP), F32)] + _peer_sems(n1) + _peer_sems(n2),
        compiler_params=_params(2, ATT_BWD_VMEM_LIMIT),
    )(proj0, proj0, proj0, o, do, *dws, *shards)
    return out[0], out[1], out[2], out[3:3 + n1], out[3 + n1:]


GATE0_COL = (D_POOL + 3 * D_SB) // D_INNER


def _gate_fwd0(name, yp_raw, o, proj0, ps):
    s = o.shape[0]

    def body(yp_ref, o_ref, gt_ref, ps_ref, y_ref, yt_ref):
        gt = gt_ref[...].astype(F32)
        sg = gt * _sigmoid(gt)
        y_pool = (yp_ref[...] * ps_ref[...] * sg[:, :D_POOL]).astype(BF16)
        y_sb = (o_ref[...] * sg[:, D_POOL:]).astype(BF16)
        y_ref[:, :D_POOL] = y_pool
        y_ref[:, D_POOL:] = y_sb
        yt_ref[:D_POOL, :] = y_pool.T
        yt_ref[D_POOL:, :] = y_sb.T

    return pl.pallas_call(
        body, name=name, grid=(s // ROW_TILE,),
        in_specs=[_row_spec(), _row_spec(), _row_spec(D_INNER, GATE0_COL), _vec_spec()],
        out_specs=[_row_spec(D_INNER), pl.BlockSpec((D_INNER, ROW_TILE), lambda i: (0, i))],
        out_shape=[SDS((s, D_INNER), BF16), SDS((D_INNER, s), BF16)], compiler_params=_params(1),
    )(yp_raw, o, proj0, ps)


def _dsilu(x):
    sg = _sigmoid(x)
    return sg * (1.0 + x * (1.0 - sg))


def _gate_bwd0(name, dymix, yp_raw, o, proj0, ps):
    s = o.shape[0]

    def body(dy_ref, yp_ref, o_ref, gt_ref, ps_ref, dyp_ref, do_ref, dgt_ref, dps_ref):
        i = pl.program_id(0)

        @pl.when(i == 0)
        def _():
            dps_ref[...] = jnp.zeros_like(dps_ref)

        gt = gt_ref[...].astype(F32)
        dy = dy_ref[...]
        sg = gt * _sigmoid(gt)
        dsg = _dsilu(gt)
        dcat = dy * sg
        yp = yp_ref[...]
        ps_v = ps_ref[...]
        dyp_ref[...] = (dcat[:, :D_POOL] * ps_v).astype(BF16)
        do_ref[...] = dcat[:, D_POOL:]
        dps_ref[...] += jnp.sum(dcat[:, :D_POOL] * yp, axis=0, keepdims=True)
        dgt_ref[:, :D_POOL] = (dy[:, :D_POOL] * (yp * ps_v) * dsg[:, :D_POOL]).astype(BF16)
        dgt_ref[:, D_POOL:] = (dy[:, D_POOL:] * o_ref[...] * dsg[:, D_POOL:]).astype(BF16)

    return pl.pallas_call(
        body, name=name, grid=(s // ROW_TILE,),
        in_specs=[_row_spec(D_INNER), _row_spec(), _row_spec(), _row_spec(D_INNER, GATE0_COL), _vec_spec()],
        out_specs=[_row_spec(), _row_spec(), _row_spec(D_INNER), _vec_spec()],
        out_shape=[SDS((s, D_POOL), BF16), SDS((s, D_SB), F32), SDS((s, D_INNER), BF16), SDS((1, D_POOL), F32)],
        compiler_params=_params(1),
    )(dymix, yp_raw, o, proj0, ps)


CONV_HALO = 16


def _conv_fwd(name, proj1, cw, cb):
    s = proj1.shape[0]
    hb = ROW_TILE // CONV_HALO
    ext_rows = ROW_TILE + CONV_HALO

    def body(gb_ref, gc_ref, u_ref, gt_ref, gch_ref, uh_ref, cw_ref, cb_ref, y_ref, yt_ref):
        i = pl.program_id(0)
        uc = gc_ref[...].astype(F32) * u_ref[...].astype(F32)
        halo = jnp.where(i == 0, 0.0, gch_ref[...].astype(F32) * uh_ref[...].astype(F32))
        ext = jnp.concatenate([halo, uc], axis=0)
        uc1 = pltpu.roll(ext, 1, axis=0)[CONV_HALO:, :]
        uc2 = pltpu.roll(ext, 2, axis=0)[CONV_HALO:, :]
        cw_v = cw_ref[...]
        conv = cb_ref[...] + cw_v[0:1, :] * uc2 + cw_v[1:2, :] * uc1 + cw_v[2:3, :] * uc
        gt = gt_ref[...].astype(F32)
        y = (gb_ref[...].astype(F32) * conv * (gt * _sigmoid(gt))).astype(BF16)
        y_ref[...] = y
        yt_ref[...] = y.T

    def tile(part):
        return pl.BlockSpec((ROW_TILE, D_INNER), lambda i: (i, part))

    def halo(part):
        return pl.BlockSpec((CONV_HALO, D_INNER), lambda i: (jnp.maximum(i * hb - 1, 0), part))

    return pl.pallas_call(
        body, name=name, grid=(s // ROW_TILE,),
        in_specs=[tile(0), tile(1), tile(2), tile(3), halo(1), halo(2),
                  pl.BlockSpec((3, D_INNER), lambda i: (0, 0)), pl.BlockSpec((1, D_INNER), lambda i: (0, 0))],
        out_specs=[pl.BlockSpec((ROW_TILE, D_INNER), lambda i: (i, 0)), pl.BlockSpec((D_INNER, ROW_TILE), lambda i: (0, i))],
        out_shape=[SDS((s, D_INNER), BF16), SDS((D_INNER, s), BF16)], compiler_params=_params(1),
    )(proj1, proj1, proj1, proj1, proj1, proj1, cw, cb)


def _conv_bwd(name, dymix, proj1, cw, cb):
    s = proj1.shape[0]
    hb = ROW_TILE // CONV_HALO
    n_hb = s // CONV_HALO
    n_tiles = s // ROW_TILE
    ext_rows = ROW_TILE + CONV_HALO

    def body(dy_ref, gb_ref, gc_ref, u_ref, gt_ref, gch_ref, uh_ref, dyn_ref, gbn_ref, gtn_ref, cw_ref, cb_ref,
             dproj_ref, dcw_ref, dcb_ref):
        i = pl.program_id(0)

        @pl.when(i == 0)
        def _():
            dcw_ref[...] = jnp.zeros_like(dcw_ref)
            dcb_ref[...] = jnp.zeros_like(dcb_ref)

        gc = gc_ref[...].astype(F32)
        u = u_ref[...].astype(F32)
        gb = gb_ref[...].astype(F32)
        gt = gt_ref[...].astype(F32)
        dy = dy_ref[...]
        uc = gc * u
        halo = jnp.where(i == 0, 0.0, gch_ref[...].astype(F32) * uh_ref[...].astype(F32))
        ext = jnp.concatenate([halo, uc], axis=0)
        uc1 = pltpu.roll(ext, 1, axis=0)[CONV_HALO:, :]
        uc2 = pltpu.roll(ext, 2, axis=0)[CONV_HALO:, :]
        cw_v = cw_ref[...]
        w0, w1, w2 = cw_v[0:1, :], cw_v[1:2, :], cw_v[2:3, :]
        conv = cb_ref[...] + w0 * uc2 + w1 * uc1 + w2 * uc
        sig = _sigmoid(gt)
        sg = gt * sig
        dconv = dy * gb * sg
        gtn = gtn_ref[...].astype(F32)
        dconv_next = jnp.where(i == n_tiles - 1, 0.0, dyn_ref[...] * gbn_ref[...].astype(F32) * (gtn * _sigmoid(gtn)))
        dext = jnp.concatenate([dconv, dconv_next], axis=0)
        dconv_p1 = pltpu.roll(dext, ext_rows - 1, axis=0)[:ROW_TILE, :]
        dconv_p2 = pltpu.roll(dext, ext_rows - 2, axis=0)[:ROW_TILE, :]
        duc = w2 * dconv + w1 * dconv_p1 + w0 * dconv_p2
        dproj_ref[:, 0:D_INNER] = (dy * conv * sg).astype(BF16)
        dproj_ref[:, D_INNER:2 * D_INNER] = (duc * u).astype(BF16)
        dproj_ref[:, 2 * D_INNER:3 * D_INNER] = (duc * gc).astype(BF16)
        dproj_ref[:, 3 * D_INNER:] = (dy * gb * conv * (sig + sg * (1.0 - sig))).astype(BF16)
        dcw_ref[0:1, :] += jnp.sum(dconv * uc2, axis=0, keepdims=True)
        dcw_ref[1:2, :] += jnp.sum(dconv * uc1, axis=0, keepdims=True)
        dcw_ref[2:3, :] += jnp.sum(dconv * uc, axis=0, keepdims=True)
        dcb_ref[...] += jnp.sum(dconv, axis=0, keepdims=True)

    def tile(part):
        return pl.BlockSpec((ROW_TILE, D_INNER), lambda i: (i, part))

    def prev(part):
        return pl.BlockSpec((CONV_HALO, D_INNER), lambda i: (jnp.maximum(i * hb - 1, 0), part))

    def nxt(part):
        return pl.BlockSpec((CONV_HALO, D_INNER), lambda i: (jnp.minimum((i + 1) * hb, n_hb - 1), part))

    whole = lambda rows: pl.BlockSpec((rows, D_INNER), lambda i: (0, 0))
    return pl.pallas_call(
        body, name=name, grid=(n_tiles,),
        in_specs=[tile(0), tile(0), tile(1), tile(2), tile(3), prev(1), prev(2), nxt(0), nxt(0), nxt(3),
                  whole(3), whole(1)],
        out_specs=[pl.BlockSpec((ROW_TILE, 4 * D_INNER), lambda i: (i, 0)), whole(3), whole(1)],
        out_shape=[SDS((s, 4 * D_INNER), BF16), SDS((3, D_INNER), F32), SDS((1, D_INNER), F32)],
        compiler_params=_params(1),
    )(dymix, proj1, proj1, proj1, proj1, proj1, proj1, dymix, proj1, proj1, cw, cb)


def _place():
    x, y, c = lax.axis_index("x"), lax.axis_index("y"), lax.axis_index("c")
    return x, y, c


def _flip(x, y, c, k):
    fx, fy, fc = (k >> 2) & 1, (k >> 1) & 1, k & 1
    return (1 - x if fx else x, 1 - y if fy else y, 1 - c if fc else c)


def _dev_index(p):
    return 4 * p[0] + 2 * p[1] + p[2]


HBM_SPEC = pl.BlockSpec(memory_space=pltpu.HBM)
VMEM_SPEC = pl.BlockSpec(memory_space=pltpu.VMEM)


def _peer_copies(ins, outs, send_sems, recv_sems, local_sems, by_chunk):
    x, y, c = _place()
    my = _dev_index((x, y, c))
    copies = []
    for w in range(len(ins)):
        copies.append(pltpu.make_async_copy(ins[w].at[my] if by_chunk else ins[w], outs[w].at[my], local_sems.at[w]))
        for k in range(1, N_DEV):
            peer = _flip(x, y, c, k)
            copies.append(pltpu.make_async_remote_copy(
                src_ref=ins[w].at[_dev_index(peer)] if by_chunk else ins[w], dst_ref=outs[w].at[my],
                send_sem=send_sems.at[7 * w + k - 1], recv_sem=recv_sems.at[7 * w + k - 1],
                device_id=peer, device_id_type=MESH))
    return copies


def _peer_sems(n_w):
    return [pltpu.SemaphoreType.DMA((7 * n_w,)), pltpu.SemaphoreType.DMA((7 * n_w,)), pltpu.SemaphoreType.DMA((n_w,))]


def _chip_index(p):
    return 2 * p[0] + p[1]


def _chip_copies(ins, outs, send_sems, recv_sems, local_sems):
    x, y, c = _place()
    mine = _chip_index((x, y))
    copies = []
    for w in range(len(ins)):
        copies.append(pltpu.make_async_copy(ins[w].at[mine], outs[w].at[mine], local_sems.at[w]))
        for k in (2, 4, 6):
            peer = _flip(x, y, c, k)
            copies.append(pltpu.make_async_remote_copy(
                src_ref=ins[w].at[_chip_index(peer)], dst_ref=outs[w].at[mine],
                send_sem=send_sems.at[7 * w + k - 1], recv_sem=recv_sems.at[7 * w + k - 1],
                device_id=peer, device_id_type=MESH))
    return copies


def _sibling_exchange(dw):
    n_chips = N_DEV // 2

    def body(dw_ref, out_ref, send_sems, recv_sems):
        x, y, c = _place()
        sibling = (x, y, 1 - c)
        copies = []
        for ch in range(n_chips):
            copies.append(pltpu.make_async_remote_copy(
                src_ref=dw_ref.at[2 * ch + (1 - c)], dst_ref=out_ref.at[ch],
                send_sem=send_sems.at[ch], recv_sem=recv_sems.at[ch], device_id=sibling, device_id_type=MESH))
        for cp in copies:
            cp.start()
        for cp in copies:
            cp.wait()

    return pl.pallas_call(
        body, name="sibling_exchange", out_shape=SDS((n_chips,) + dw.shape[1:], dw.dtype),
        in_specs=[HBM_SPEC], out_specs=HBM_SPEC,
        scratch_shapes=[pltpu.SemaphoreType.DMA((n_chips,)), pltpu.SemaphoreType.DMA((n_chips,))],
    )(dw)


def _sibling_sum(name, dw, got, core):
    n_chips, rows, cols = got.shape
    tr = min(rows, 256)

    def body(core_ref, a_ref, b_ref, o_ref):
        o_ref[...] = (a_ref[...].astype(F32) + b_ref[...].astype(F32)).astype(BF16)

    return pl.pallas_call(
        body, name=name,
        grid_spec=pltpu.PrefetchScalarGridSpec(
            num_scalar_prefetch=1, grid=(n_chips, rows // tr),
            in_specs=[pl.BlockSpec((None, tr, cols), lambda ch, i, core_ref: (2 * ch + core_ref[0], i, 0)),
                      pl.BlockSpec((None, tr, cols), lambda ch, i, core_ref: (ch, i, 0))],
            out_specs=pl.BlockSpec((None, tr, cols), lambda ch, i, core_ref: (ch, i, 0))),
        out_shape=SDS(got.shape, BF16), compiler_params=_params(2),
    )(core, dw, got)


ADA_COLS = 3 * D_MODEL // N_DEV


def _ada_forward(c_row, conv_w, conv_b, ada_w, ada_b):
    cw_cols = conv_w.shape[1]

    def body(c_ref, cw_ref, cb_ref, aw_ref, ab_ref, m_ref, cs_ref, cwf_ref, cbf_ref,
             slab, gath, part, land, send_sems, recv_sems):
        x, y, c = _place()
        my = _dev_index((x, y, c))
        slab[...] = jnp.zeros_like(slab)
        slab[0:1, :] = c_ref[...]
        slab[1:4, 0:cw_cols] = cw_ref[...]
        slab[4:5, 0:cw_cols] = cb_ref[...]
        gath[my] = slab[...]
        sends = []
        for k in range(1, N_DEV):
            peer = _flip(x, y, c, k)
            cp = pltpu.make_async_remote_copy(
                src_ref=slab, dst_ref=gath.at[my], send_sem=send_sems.at[k - 1], recv_sem=recv_sems.at[k - 1],
                device_id=peer, device_id_type=MESH)
            cp.start()
            sends.append(cp)
        for cp in sends:
            cp.wait()
        for d in range(N_DEV):
            c_d = gath[d, 0:1, :]
            cs_ref[d:d + 1, :] = c_d * _sigmoid(c_d)
            cwf_ref[:, d * cw_cols:(d + 1) * cw_cols] = gath[d, 1:4, 0:cw_cols]
            cbf_ref[:, d * cw_cols:(d + 1) * cw_cols] = gath[d, 4:5, 0:cw_cols]
        cs = cs_ref[...]
        part[...] = jnp.zeros_like(part)
        for layer in range(2):
            m_part = jnp.dot(cs, aw_ref[layer], preferred_element_type=F32, precision=lax.Precision.HIGHEST)
            for d in range(N_DEV):
                part[d, layer:layer + 1, :] = m_part[d:d + 1, :]
        land[my] = part[my]
        sends = []
        for k in range(1, N_DEV):
            peer = _flip(x, y, c, k)
            cp = pltpu.make_async_remote_copy(
                src_ref=part.at[_dev_index(peer)], dst_ref=land.at[my],
                send_sem=send_sems.at[6 + k], recv_sem=recv_sems.at[6 + k],
                device_id=peer, device_id_type=MESH)
            cp.start()
            sends.append(cp)
        for cp in sends:
            cp.wait()
        for d in range(N_DEV):
            cols = slice(d * ADA_COLS, (d + 1) * ADA_COLS)
            m_ref[:, cols] = land[d, 0:2, :] + ab_ref[:, cols]

    return pl.pallas_call(
        body, name="ada_forward",
        out_shape=[SDS((2, 3 * D_MODEL), F32), SDS((N_DEV, D_MODEL), F32), SDS((3, N_DEV * cw_cols), F32),
                   SDS((1, N_DEV * cw_cols), F32)],
        in_specs=[VMEM_SPEC] * 5, out_specs=[VMEM_SPEC] * 4,
        scratch_shapes=[pltpu.VMEM((8, D_MODEL), F32), pltpu.VMEM((N_DEV, 8, D_MODEL), F32),
                        pltpu.VMEM((N_DEV, 8, ADA_COLS), F32), pltpu.VMEM((N_DEV, 8, ADA_COLS), F32),
                        pltpu.SemaphoreType.DMA((14,)), pltpu.SemaphoreType.DMA((14,))],
        compiler_params=pltpu.CompilerParams(vmem_limit_bytes=VMEM_LIMIT),
    )(c_row, conv_w, conv_b, ada_w, ada_b)


def _small_grads(slab):
    def body(slab_ref, gath_ref, tot_ref, send_sems, recv_sems):
        x, y, c = _place()
        my = _dev_index((x, y, c))
        gath_ref[my] = slab_ref[...]
        sends = []
        for k in range(1, N_DEV):
            peer = _flip(x, y, c, k)
            cp = pltpu.make_async_remote_copy(
                src_ref=slab_ref, dst_ref=gath_ref.at[my], send_sem=send_sems.at[k - 1], recv_sem=recv_sems.at[k - 1],
                device_id=peer, device_id_type=MESH)
            cp.start()
            sends.append(cp)
        for cp in sends:
            cp.wait()
        tot = gath_ref[0]
        for d in range(1, N_DEV):
            tot = tot + gath_ref[d]
        tot_ref[...] = tot

    return pl.pallas_call(
        body, name="small_grads",
        out_shape=[SDS((N_DEV, SLAB_ROWS, D_MODEL), F32), SDS((SLAB_ROWS, D_MODEL), F32)],
        in_specs=[VMEM_SPEC], out_specs=[VMEM_SPEC] * 2,
        scratch_shapes=[pltpu.SemaphoreType.DMA((7,)), pltpu.SemaphoreType.DMA((7,))],
    )(slab)


def _adamw_math(w, g, m, v):
    m = ADAM_B1 * m + (1.0 - ADAM_B1) * g
    v = ADAM_B2 * v + (1.0 - ADAM_B2) * jnp.square(g)
    m_hat = m / (1.0 - ADAM_B1 ** ADAM_STEP)
    v_hat = v / (1.0 - ADAM_B2 ** ADAM_STEP)
    delta = -ADAM_LR * (m_hat / (jnp.sqrt(v_hat) + ADAM_EPS) + ADAM_WD * w)
    return delta, m, v


def _sum_adamw(name, recv, w, m, v):
    rows, cols = w.shape
    tr = min(rows, 256)
    n_slots = recv.shape[0]

    def body(r_ref, w_ref, m_ref, v_ref, g_ref, d_ref, nm_ref, nv_ref):
        g = r_ref[0].astype(F32)
        for d in range(1, n_slots):
            g = g + r_ref[d].astype(F32)
        g_ref[...] = g
        d_ref[...], nm_ref[...], nv_ref[...] = _adamw_math(w_ref[...], g, m_ref[...], v_ref[...])

    blk = pl.BlockSpec((tr, cols), lambda i: (i, 0))
    return pl.pallas_call(
        body, name=name, grid=(rows // tr,),
        in_specs=[pl.BlockSpec((n_slots, tr, cols), lambda i: (0, i, 0)), blk, blk, blk],
        out_specs=[blk] * 4, out_shape=[SDS((rows, cols), F32)] * 4, compiler_params=_params(1),
    )(recv, w, m, v)


def _ada_w_adamw(name, cs_t, dm_cols, w, m, v):
    def body(cs_ref, dm_ref, w_ref, m_ref, v_ref, g_ref, d_ref, nm_ref, nv_ref):
        cs = cs_ref[...]
        dm = dm_ref[...]
        g = cs[:, 0:1] * dm[0:1, :]
        for b in range(1, N_DEV):
            g = g + cs[:, b:b + 1] * dm[b:b + 1, :]
        g_ref[...] = g
        d_ref[...], nm_ref[...], nv_ref[...] = _adamw_math(w_ref[...], g, m_ref[...], v_ref[...])

    blk = pl.BlockSpec((None, D_MODEL, ADA_COLS), lambda l: (l, 0, 0))
    return pl.pallas_call(
        body, name=name, grid=(2,),
        in_specs=[pl.BlockSpec((D_MODEL, N_DEV), lambda l: (0, 0)),
                  pl.BlockSpec((None, N_DEV, ADA_COLS), lambda l: (l, 0, 0)), blk, blk, blk],
        out_specs=[blk] * 4, out_shape=[SDS((2, D_MODEL, ADA_COLS), F32)] * 4, compiler_params=_params(1),
    )(cs_t, dm_cols, w, m, v)


def _small_adamw(name, triples):
    n = len(triples)

    def body(*refs):
        ins, outs = refs[:4 * n], refs[4 * n:]
        for j in range(n):
            w_ref, g_ref, m_ref, v_ref = ins[4 * j:4 * j + 4]
            d, nm, nv = _adamw_math(w_ref[...], g_ref[...], m_ref[...], v_ref[...])
            outs[3 * j][...] = d
            outs[3 * j + 1][...] = nm
            outs[3 * j + 2][...] = nv

    flat = [a for t in triples for a in t]
    return pl.pallas_call(
        body, name=name,
        out_shape=[SDS(t[0].shape, F32) for t in triples for _ in range(3)],
        in_specs=[VMEM_SPEC] * (4 * n), out_specs=[VMEM_SPEC] * (3 * n),
    )(*flat)


def kernel(x, c, norm_g, ada_w, ada_b, even_w_in, pool_w, pool_scale, even_w_out, odd_w_in, conv_w, conv_b, odd_w_out, final_g, loss_target, m_norm_g, m_ada_w, m_ada_b, m_even_w_in, m_pool_w, m_pool_scale, m_even_w_out, m_odd_w_in, m_conv_w, m_conv_b, m_odd_w_out, m_final_g, v_norm_g, v_ada_w, v_ada_b, v_even_w_in, v_pool_w, v_pool_scale, v_even_w_out, v_odd_w_in, v_conv_w, v_conv_b, v_odd_w_out, v_final_g):
    seq = x.shape[1]
    x0 = x[0]
    target = loss_target[0]
    final_g2 = final_g.reshape(1, D_MODEL)

    w_in_e = even_w_in[0]
    w_out_e = even_w_out[0]
    w_in_o = odd_w_in[0]
    w_out_o = odd_w_out[0]
    w_pool = pool_w[0].reshape(N_GROUPS * 32, POOL_GROUP)
    shards = [w.astype(BF16) for w in (w_in_e, w_out_e, w_in_o, w_out_o, w_pool)]

    m_vec, cs_all, conv_w_full, conv_b_full = _ada_forward(c, conv_w[0], conv_b, ada_w, ada_b)
    shift = [m_vec[l:l + 1, 0:D_MODEL] for l in range(2)]
    scale = [m_vec[l:l + 1, D_MODEL:2 * D_MODEL] for l in range(2)]
    gate = [m_vec[l:l + 1, 2 * D_MODEL:] for l in range(2)]
    ng = [norm_g[l:l + 1] for l in range(2)]

    h0, h0_t = _ln_mod("ln_mod0", x0, ng[0], scale[0], shift[0])
    ax, ay, ac = lax.axis_index("x"), lax.axis_index("y"), lax.axis_index("c")
    chips = ((1 - ax, ay), (ax, 1 - ay), (1 - ax, 1 - ay))
    arrival = [None] * N_DEV
    arrival[0], arrival[1] = (ax, ay, ac), (ax, ay, 1 - ac)
    for dd_direct, dd_passed, j in GATHER_ARRIVALS:
        arrival[dd_direct], arrival[dd_passed] = (*chips[j], ac), (*chips[j], 1 - ac)
    order = jnp.stack([_dev_index(p) for p in arrival]).astype(jnp.int32)
    proj0 = _proj_in_gather("proj_in0", h0, shards[0], order)
    o, (wg_out_e, wg_in_o, wg_out_o, wg_pool, wt_in_o) = _attn_fwd("attn_fwd", proj0, shards[1:] + [shards[2].T])
    wf_out_e = wg_out_e.reshape(D_INNER, D_MODEL)
    wf_out_o = wg_out_o.reshape(D_INNER, D_MODEL)
    wf_pool = wg_pool.reshape(N_DEV, N_GROUPS, 32, POOL_GROUP).transpose(1, 0, 2, 3).reshape(N_GROUPS, POOL_GROUP, POOL_GROUP)
    p, yp_raw = _pool_fwd("pool_fwd", proj0, wf_pool)
    ymix0, ymix0_t = _gate_fwd0("gate_fwd0", yp_raw, o, proj0, pool_scale)
    yo0 = _proj_out("proj_out0", ymix0, wf_out_e)

    x1, h1, h1_t = _resid_ln_mod("resid_ln_mod1", x0, yo0, gate[0], ng[1], scale[1], shift[1])
    proj1 = _proj_in("proj_in1", h1, wg_in_o)
    ymix1, ymix1_t = _conv_fwd("conv_fwd", proj1, conv_w_full, conv_b_full)
    yo1 = _proj_out("proj_out1", ymix1, wf_out_o)

    dx2, dyo1, loss_acc, d_final_g, d_gate1 = _final_loss("final_loss", x1, yo1, gate[1], final_g2, target)
    loss = lax.psum(loss_acc[0, 0], ("x", "y", "c"))

    dymix1 = _proj_out_bwd("proj_out1_bwd", dyo1, wf_out_o)
    dw_out_o = _wgrad_out("wgrad_out1", ymix1_t, dyo1)
    dproj1, d_conv_w, d_conv_b = _conv_bwd("conv_bwd", dymix1, proj1, conv_w_full, conv_b_full)
    dw_in_o = _wgrad_in("wgrad_in1", h1_t, [dproj1], D_IN_ODD // N_DEV)
    (dx1, d_shift1, d_scale1, d_ng1, dyo0, d_gate0), _ = _proj_in_bwd_ln(
        "proj_in1_bwd", [dproj1], wt_in_o, x1, dx2, ng[1], scale[1], resid=(yo0, gate[0]))

    dymix0 = _proj_out_bwd("proj_out0_bwd", dyo0, wf_out_e)
    dw_out_e = _wgrad_out("wgrad_out0", ymix0_t, dyo0)
    dyp, do, dgt0, d_pool_scale = _gate_bwd0("gate_bwd0", dymix0, yp_raw, o, proj0, pool_scale)
    du_pool, dw_pool = _pool_bwd("pool_bwd", dyp, p, wf_pool)
    dw_pool_c = dw_pool.reshape(N_GROUPS, N_DEV, 32, POOL_GROUP).transpose(1, 0, 2, 3).reshape(N_DEV, N_GROUPS * 32, POOL_GROUP).astype(BF16)
    ready = [dw_out_e.reshape(N_DEV, D_INNER // N_DEV, D_MODEL), dw_in_o,
             dw_out_o.reshape(N_DEV, D_INNER // N_DEV, D_MODEL), dw_pool_c]
    dq, dk, dv, (r_out_e, r_in_o, r_out_o, r_pool), (wt_in_e,) = _attn_bwd(
        "attn_bwd", proj0, o, do, ready, [shards[0].T])
    dparts0 = [du_pool, dq, dk, dv, dgt0]
    dw_in_e = _wgrad_in("wgrad_in0", h0_t, dparts0, WGRAD_BLOCK)
    core = lax.axis_index("c").astype(jnp.int32).reshape(1)
    chip_sums = _sibling_sum("sibling_sum", dw_in_e, _sibling_exchange(dw_in_e), core)
    (dx0, d_shift0, d_scale0, d_ng0), (r_in_e,) = _proj_in_bwd_ln(
        "proj_in0_bwd", dparts0, wt_in_e, x0, dx1, ng[0], scale[0], dws=[chip_sums])
    grad_x = dx0[None]

    big = {}
    big["even_w_in"] = _sum_adamw("adamw_even_w_in", r_in_e, w_in_e, m_even_w_in[0], v_even_w_in[0])
    big["even_w_out"] = _sum_adamw("adamw_even_w_out", r_out_e, w_out_e, m_even_w_out[0], v_even_w_out[0])
    big["odd_w_in"] = _sum_adamw("adamw_odd_w_in", r_in_o, w_in_o, m_odd_w_in[0], v_odd_w_in[0])
    big["odd_w_out"] = _sum_adamw("adamw_odd_w_out", r_out_o, w_out_o, m_odd_w_out[0], v_odd_w_out[0])
    big["pool_w"] = _sum_adamw("adamw_pool_w", r_pool, w_pool, m_pool_w[0].reshape(N_GROUPS * 32, POOL_GROUP),
                               v_pool_w[0].reshape(N_GROUPS * 32, POOL_GROUP))
    big = {k: [a.reshape(shape) for a in v] for (k, v), shape in zip(
        big.items(), [even_w_in.shape, even_w_out.shape, odd_w_in.shape, odd_w_out.shape, pool_w.shape])}

    dm = jnp.concatenate([jnp.concatenate([d_shift0, d_scale0, d_gate0], axis=1),
                          jnp.concatenate([d_shift1, d_scale1, d_gate1], axis=1)], axis=0)
    slab = jnp.zeros((SLAB_ROWS, D_MODEL), F32)
    slab = slab.at[0:6].set(dm.reshape(6, D_MODEL))
    slab = slab.at[8:9].set(d_ng0).at[9:10].set(d_ng1).at[10:11].set(d_pool_scale).at[11:12].set(d_final_g)
    slab = slab.at[16:22].set(d_conv_w.reshape(6, D_MODEL)).at[24:26].set(d_conv_b.reshape(2, D_MODEL))
    gathered, total = _small_grads(slab)
    my = 4 * lax.axis_index("x") + 2 * lax.axis_index("y") + lax.axis_index("c")
    g_ada_b = total[0:6].reshape(2, 3 * D_MODEL)
    g_norm_g = total[8:10]
    g_pool_scale = total[10:11]
    g_final_g = total[11:12]
    cw_cols = conv_w.shape[2]
    g_conv_w = lax.dynamic_slice_in_dim(total[16:22].reshape(3, D_INNER), my * cw_cols, cw_cols, axis=1)
    g_conv_b = lax.dynamic_slice_in_dim(total[24:26].reshape(1, D_INNER), my * cw_cols, cw_cols, axis=1)
    dm_all = gathered[:, 0:6, :].reshape(N_DEV, 2, 3 * D_MODEL)
    dm_cols = lax.dynamic_slice_in_dim(dm_all, my * ADA_COLS, ADA_COLS, axis=2).transpose(1, 0, 2)
    ada = _ada_w_adamw("adamw_ada_w", cs_all.T, dm_cols, ada_w, m_ada_w, v_ada_w)

    small = _small_adamw("adamw_small", [
        (norm_g, g_norm_g, m_norm_g, v_norm_g),
        (ada_b, g_ada_b, m_ada_b, v_ada_b),
        (pool_scale, g_pool_scale, m_pool_scale, v_pool_scale),
        (conv_w[0], g_conv_w, m_conv_w[0], v_conv_w[0]),
        (conv_b, g_conv_b, m_conv_b, v_conv_b),
        (final_g2, g_final_g, m_final_g.reshape(1, D_MODEL), v_final_g.reshape(1, D_MODEL)),
    ])
    small = [small[3 * j:3 * j + 3] for j in range(6)]

    grads = {
        "norm_g": g_norm_g, "ada_w": ada[0], "ada_b": g_ada_b, "even_w_in": big["even_w_in"][0],
        "pool_w": big["pool_w"][0], "pool_scale": g_pool_scale, "even_w_out": big["even_w_out"][0],
        "odd_w_in": big["odd_w_in"][0], "conv_w": g_conv_w.reshape(conv_w.shape), "conv_b": g_conv_b,
        "odd_w_out": big["odd_w_out"][0], "final_g": g_final_g.reshape(D_MODEL),
    }
    rest = []
    for idx in range(3):
        rest += [
            small[0][idx], ada[1 + idx], small[1][idx], big["even_w_in"][1 + idx], big["pool_w"][1 + idx],
            small[2][idx], big["even_w_out"][1 + idx], big["odd_w_in"][1 + idx],
            small[3][idx].reshape(conv_w.shape), small[4][idx], big["odd_w_out"][1 + idx],
            small[5][idx].reshape(D_MODEL),
        ]
    order = ["norm_g", "ada_w", "ada_b", "even_w_in", "pool_w", "pool_scale", "even_w_out", "odd_w_in",
             "conv_w", "conv_b", "odd_w_out", "final_g"]
    return (loss, grad_x, *[grads[n] for n in order], *rest)
```

```python
import jax
import jax.numpy as jnp
from jax import lax
from jax.experimental import pallas as pl
from jax.experimental.pallas import tpu as pltpu

F32 = jnp.float32
BF16 = jnp.bfloat16
SDS = jax.ShapeDtypeStruct
MESH = pl.DeviceIdType.MESH

N_DEV = 8
D_MODEL = 1024
D_INNER = 2048
D_POOL = 1024
D_SB = 1024
N_GROUPS = 4
POOL_GROUP = 256
HEAD_DIM = 64
LANES = 128
D_IN_EVEN = 6144
D_IN_ODD = 8192
EPS = 1e-6
ADAM_LR = 0.001
ADAM_B1 = 0.9
ADAM_B2 = 0.999
ADAM_EPS = 1e-08
ADAM_WD = 0.01
ADAM_STEP = 10

ROW_TILE = 256
ATT_TILE = 256
HALO = 16
VMEM_LIMIT = 48 * 1024 * 1024
ATT_BWD_VMEM_LIMIT = 56 * 1024 * 1024
SLAB_ROWS = 32
WGRAD_BLOCK = 256


def _params(n_axes, vmem_limit=VMEM_LIMIT):
    return pltpu.CompilerParams(dimension_semantics=("arbitrary",) * n_axes, vmem_limit_bytes=vmem_limit)


def _sigmoid(x):
    return 1.0 / (1.0 + jnp.exp(-x))


def _dot(a, b):
    return jnp.dot(a, b, preferred_element_type=F32)


def _dot_nt(a, b):
    return lax.dot_general(a, b, (((1,), (1,)), ((), ())), preferred_element_type=F32)


def _dot_tn(a, b):
    return lax.dot_general(a, b, (((0,), (0,)), ((), ())), preferred_element_type=F32)


def _mm(name, a, b, *, grid, a_spec, b_spec, o_spec, o_shape, o_dtype, dot, acc_axis=None, acc_shape=None):
    n_acc = grid[acc_axis] if acc_axis is not None else 1

    def body(a_ref, b_ref, o_ref, *scratch):
        prod = dot(a_ref[...], b_ref[...])
        if acc_axis is None:
            o_ref[...] = prod.astype(o_dtype)
        else:
            acc = scratch[0]
            k = pl.program_id(acc_axis)

            @pl.when(k == 0)
            def _():
                acc[...] = prod

            @pl.when(k > 0)
            def _():
                acc[...] += prod

            @pl.when(k == n_acc - 1)
            def _():
                o_ref[...] = acc[...].astype(o_dtype)

    scratch = [] if acc_axis is None else [pltpu.VMEM(acc_shape, F32)]
    return pl.pallas_call(
        body, name=name, grid=grid, in_specs=[a_spec, b_spec], out_specs=o_spec,
        out_shape=SDS(o_shape, o_dtype), scratch_shapes=scratch, compiler_params=_params(len(grid)),
    )(a, b)


def _proj_in(name, h, wg):
    s = h.shape[0]
    cn = wg.shape[2]
    tm = min(s, ROW_TILE)

    def body(a_ref, w_ref, o_ref):
        a = a_ref[...]
        for d in range(N_DEV):
            o_ref[:, d * cn:(d + 1) * cn] = _dot(a, w_ref[d]).astype(BF16)

    return pl.pallas_call(
        body, name=name, grid=(s // tm,),
        in_specs=[pl.BlockSpec((tm, D_MODEL), lambda i: (i, 0)),
                  pl.BlockSpec((N_DEV, D_MODEL, cn), lambda i: (0, 0, 0), pipeline_mode=pl.Buffered(1))],
        out_specs=pl.BlockSpec((tm, N_DEV * cn), lambda i: (i, 0)),
        out_shape=SDS((s, N_DEV * cn), BF16), compiler_params=_params(1),
    )(h, wg)


GATHER_ROWS = 1024
GATHER_ARRIVALS = ((2, 4, 0), (3, 5, 1), (6, 7, 2))


def _proj_in_gather(name, h, w_shard, order):
    s = h.shape[0]
    k_dim, cn = w_shard.shape
    tm = min(s, GATHER_ROWS)
    n_i = s // tm

    def body(ord_ref, a_ref, w_ref, o_ref, w_buf, send_sems, recv_sems, local_sem):
        dd = pl.program_id(0)
        i = pl.program_id(1)
        x, y, c = _place()
        me, sibling = (x, y, c), (x, y, 1 - c)
        chips = [(1 - x, y), (x, 1 - y), (1 - x, 1 - y)]

        def copy(k, block, to, from_src=False):
            slot = w_buf.at[_dev_index(block)]
            return pltpu.make_async_remote_copy(
                src_ref=w_ref if from_src else slot, dst_ref=slot, send_sem=send_sems.at[k], recv_sem=recv_sems.at[k],
                device_id=to, device_id_type=MESH)

        mine = pltpu.make_async_copy(w_ref, w_buf.at[_dev_index(me)], local_sem)
        first = [copy(0, me, sibling, True)] + [copy(1 + j, me, (*chip, c), True) for j, chip in enumerate(chips)]
        passed = [copy(4 + j, (*chip, c), sibling) for j, chip in enumerate(chips)]

        @pl.when((dd == 0) & (i == 0))
        def _():
            mine.start()
            for cp in first:
                cp.start()
            mine.wait()

        @pl.when((dd == 1) & (i == 0))
        def _():
            copy(0, sibling, me).wait_recv()

        for dd_direct, dd_passed, j in GATHER_ARRIVALS:
            @pl.when((dd == dd_direct) & (i == 0))
            def _(j=j):
                copy(1 + j, (*chips[j], c), me).wait_recv()
                passed[j].start()

            @pl.when((dd == dd_passed) & (i == 0))
            def _(j=j):
                copy(4 + j, (*chips[j], 1 - c), me).wait_recv()

        o_ref[...] = _dot(a_ref[...], w_buf[ord_ref[dd]]).astype(BF16)

        @pl.when((dd == N_DEV - 1) & (i == n_i - 1))
        def _():
            for cp in first + passed:
                cp.wait_send()

    return pl.pallas_call(
        body, name=name,
        grid_spec=pltpu.PrefetchScalarGridSpec(
            num_scalar_prefetch=1, grid=(N_DEV, n_i),
            in_specs=[pl.BlockSpec((tm, k_dim), lambda dd, i, ord_ref: (i, 0)), HBM_SPEC],
            out_specs=pl.BlockSpec((tm, cn), lambda dd, i, ord_ref: (i, ord_ref[dd])),
            scratch_shapes=[pltpu.VMEM((N_DEV, k_dim, cn), BF16), pltpu.SemaphoreType.DMA((7,)),
                            pltpu.SemaphoreType.DMA((7,)), pltpu.SemaphoreType.DMA]),
        out_shape=SDS((s, N_DEV * cn), BF16), compiler_params=_params(2),
    )(order, h, w_shard)


def _transpose_shards(name, wg):
    n, k_dim, cn = wg.shape
    tb = 256

    def body(a_ref, o_ref):
        o_ref[...] = a_ref[...].T

    return pl.pallas_call(
        body, name=name, grid=(n, cn // tb),
        in_specs=[pl.BlockSpec((None, k_dim, tb), lambda d, j: (d, 0, j))],
        out_specs=pl.BlockSpec((None, tb, k_dim), lambda d, j: (d, j, 0)),
        out_shape=SDS((n, cn, k_dim), wg.dtype), compiler_params=_params(2),
    )(wg)


def _proj_out(name, y, w):
    s = y.shape[0]
    tm = min(s, 512)
    return _mm(name, y, w, grid=(s // tm,),
               a_spec=pl.BlockSpec((tm, D_INNER), lambda i: (i, 0)),
               b_spec=pl.BlockSpec((D_INNER, D_MODEL), lambda i: (0, 0)),
               o_spec=pl.BlockSpec((tm, D_MODEL), lambda i: (i, 0)),
               o_shape=(s, D_MODEL), o_dtype=F32, dot=_dot)


def _proj_out_bwd(name, dyo, w):
    s = dyo.shape[0]
    tm = min(s, 512)
    return _mm(name, dyo, w, grid=(s // tm,),
               a_spec=pl.BlockSpec((tm, D_MODEL), lambda i: (i, 0)),
               b_spec=pl.BlockSpec((D_INNER, D_MODEL), lambda i: (0, 0)),
               o_spec=pl.BlockSpec((tm, D_INNER), lambda i: (i, 0)),
               o_shape=(s, D_INNER), o_dtype=F32, dot=_dot_nt)


def _wgrad_out(name, y_t, dyo):
    s = y_t.shape[1]
    tm = 512
    return _mm(name, y_t, dyo, grid=(D_INNER // tm,),
               a_spec=pl.BlockSpec((tm, s), lambda r: (r, 0)),
               b_spec=pl.BlockSpec((s, D_MODEL), lambda r: (0, 0), pipeline_mode=pl.Buffered(1)),
               o_spec=pl.BlockSpec((tm, D_MODEL), lambda r: (r, 0)),
               o_shape=(D_INNER, D_MODEL), o_dtype=BF16, dot=_dot)


def _proj_in_bwd_ln(name, parts, wt, x, dx_next, g, scale, resid=None, dws=()):
    s = x.shape[0]
    widths = [p.shape[1] for p in parts]
    offs = [sum(widths[:k]) for k in range(len(parts))]
    k_all = sum(widths)
    n_i = s // ROW_TILE
    n_p, n_r, n_side = len(parts), (2 if resid else 0), len(dws)
    w_all = wt.reshape(k_all, D_MODEL)

    def body(*refs):
        part_refs = refs[:n_p]
        w_ref, x_ref, dxn_ref, g_ref, sc_ref = refs[n_p:n_p + 5]
        resid_refs = refs[n_p + 5:n_p + 5 + n_r]
        srcs = refs[n_p + 5 + n_r:n_p + 5 + n_r + n_side]
        outs = refs[n_p + 5 + n_r + n_side:]
        dx_ref, dsh_ref, dsc_ref, dg_ref = outs[:4]
        resid_outs = outs[4:4 + n_r]
        dsts = outs[4 + n_r:4 + n_r + n_side]
        sems = outs[4 + n_r + n_side:]
        i = pl.program_id(0)
        if n_side:
            _side_exchange(srcs + dsts + sems, n_side, True, i == 0, i == n_i - 1, chips=True)

        @pl.when(i == 0)
        def _():
            dsh_ref[...] = jnp.zeros_like(dsh_ref)
            dsc_ref[...] = jnp.zeros_like(dsc_ref)
            dg_ref[...] = jnp.zeros_like(dg_ref)
            if resid:
                resid_outs[1][...] = jnp.zeros_like(resid_outs[1])

        dh_v = _dot(part_refs[0][...], w_ref[offs[0]:offs[0] + widths[0], :])
        for k in range(1, n_p):
            dh_v = dh_v + _dot(part_refs[k][...], w_ref[offs[k]:offs[k] + widths[k], :])
        xv = x_ref[...]
        g_v = g_ref[...]
        r = lax.rsqrt(jnp.mean(xv * xv, axis=-1, keepdims=True) + EPS)
        xn = xv * r
        dsh_ref[...] += jnp.sum(dh_v, axis=0, keepdims=True)
        dsc_ref[...] += jnp.sum(dh_v * (xn * g_v), axis=0, keepdims=True)
        dn = dh_v * (1.0 + sc_ref[...])
        dg_ref[...] += jnp.sum(dn * xn, axis=0, keepdims=True)
        dxh = dn * g_v
        dx = dxn_ref[...] + r * (dxh - xn * jnp.mean(dxh * xn, axis=-1, keepdims=True))
        dx_ref[...] = dx
        if resid:
            yo_ref, gt_ref = resid_refs
            resid_outs[0][...] = (dx * (1.0 + gt_ref[...])).astype(BF16)
            resid_outs[1][...] += jnp.sum(dx * yo_ref[...], axis=0, keepdims=True)

    row, vec = _row_spec(), _vec_spec()
    out = pl.pallas_call(
        body, name=name, grid=(n_i,),
        in_specs=[_row_spec(w) for w in widths]
        + [pl.BlockSpec((k_all, D_MODEL), lambda i: (0, 0), pipeline_mode=pl.Buffered(1)), row, row, vec, vec]
        + ([row, vec] if resid else []) + [HBM_SPEC] * n_side,
        out_specs=[row, vec, vec, vec] + ([row, vec] if resid else []) + [HBM_SPEC] * n_side,
        out_shape=[SDS((s, D_MODEL), F32)] + [SDS((1, D_MODEL), F32)] * 3
        + ([SDS((s, D_MODEL), BF16), SDS((1, D_MODEL), F32)] if resid else [])
        + [SDS(dw.shape, dw.dtype) for dw in dws],
        scratch_shapes=_peer_sems(n_side) if n_side else [],
        compiler_params=_params(1),
    )(*parts, w_all, x, dx_next, g, scale, *(resid or ()), *dws)
    return out[:4 + n_r], out[4 + n_r:]


def _wgrad_in(name, h_t, parts, blk):
    s = h_t.shape[1]
    widths = [p.shape[1] for p in parts]
    cn = sum(widths) // N_DEV
    per_dev = cn // blk
    starts = [sum(widths[:k]) // blk for k in range(len(parts))]
    counts = [w // blk for w in widths]
    n_blk = sum(counts)

    def body(a_ref, *rest):
        o_ref = rest[len(parts)]
        b = pl.program_id(0)
        for k in range(len(parts)):
            @pl.when((b >= starts[k]) & (b < starts[k] + counts[k]))
            def _(k=k):
                o_ref[...] = _dot(a_ref[...], rest[k][...]).astype(BF16)

    def part_spec(k):
        return pl.BlockSpec((s, blk), lambda b: (0, jnp.clip(b - starts[k], 0, counts[k] - 1)))

    return pl.pallas_call(
        body, name=name, grid=(n_blk,),
        in_specs=[pl.BlockSpec((D_MODEL, s), lambda b: (0, 0), pipeline_mode=pl.Buffered(1))]
        + [part_spec(k) for k in range(len(parts))],
        out_specs=pl.BlockSpec((None, D_MODEL, blk), lambda b: (b // per_dev, 0, b % per_dev)),
        out_shape=SDS((N_DEV, D_MODEL, cn), BF16), compiler_params=_params(1),
    )(h_t, *parts)


def _vec_spec():
    return pl.BlockSpec((1, D_MODEL), lambda i: (0, 0))


def _row_spec(width=D_MODEL, col=0):
    return pl.BlockSpec((ROW_TILE, width), lambda i: (i, col))


def _col_spec():
    return pl.BlockSpec((D_MODEL, ROW_TILE), lambda i: (0, i))


def _ln_mod(name, x, g, scale, shift):
    s = x.shape[0]

    def body(x_ref, g_ref, sc_ref, sh_ref, h_ref, ht_ref):
        xv = x_ref[...]
        r = lax.rsqrt(jnp.mean(xv * xv, axis=-1, keepdims=True) + EPS)
        n = (xv * r) * g_ref[...]
        h = (n * (1.0 + sc_ref[...]) + sh_ref[...]).astype(BF16)
        h_ref[...] = h
        ht_ref[...] = h.T

    return pl.pallas_call(
        body, name=name, grid=(s // ROW_TILE,),
        in_specs=[_row_spec(), _vec_spec(), _vec_spec(), _vec_spec()], out_specs=[_row_spec(), _col_spec()],
        out_shape=[SDS((s, D_MODEL), BF16), SDS((D_MODEL, s), BF16)], compiler_params=_params(1),
    )(x, g, scale, shift)


def _resid_ln_mod(name, x, yo, gate, g, scale, shift):
    s = x.shape[0]

    def body(x_ref, yo_ref, gt_ref, g_ref, sc_ref, sh_ref, xn_ref, h_ref, ht_ref):
        xv = x_ref[...] + (1.0 + gt_ref[...]) * yo_ref[...]
        xn_ref[...] = xv
        r = lax.rsqrt(jnp.mean(xv * xv, axis=-1, keepdims=True) + EPS)
        n = (xv * r) * g_ref[...]
        h = (n * (1.0 + sc_ref[...]) + sh_ref[...]).astype(BF16)
        h_ref[...] = h
        ht_ref[...] = h.T

    return pl.pallas_call(
        body, name=name, grid=(s // ROW_TILE,),
        in_specs=[_row_spec(), _row_spec(), _vec_spec(), _vec_spec(), _vec_spec(), _vec_spec()],
        out_specs=[_row_spec(), _row_spec(), _col_spec()],
        out_shape=[SDS((s, D_MODEL), F32), SDS((s, D_MODEL), BF16), SDS((D_MODEL, s), BF16)],
        compiler_params=_params(1),
    )(x, yo, gate, g, scale, shift)


def _final_loss(name, x1, yo1, gate1, gf, target):
    s = x1.shape[0]

    def body(x_ref, yo_ref, gt_ref, gf_ref, t_ref, dx_ref, dyo_ref, loss_ref, dgf_ref, dgt_ref):
        i = pl.program_id(0)

        @pl.when(i == 0)
        def _():
            loss_ref[...] = jnp.zeros_like(loss_ref)
            dgf_ref[...] = jnp.zeros_like(dgf_ref)
            dgt_ref[...] = jnp.zeros_like(dgt_ref)

        yo = yo_ref[...]
        one_gate = 1.0 + gt_ref[...]
        x2 = x_ref[...] + one_gate * yo
        r = lax.rsqrt(jnp.mean(x2 * x2, axis=-1, keepdims=True) + EPS)
        xn = x2 * r
        gf_v = gf_ref[...]
        err = xn * gf_v - t_ref[...]
        loss_ref[...] += 0.5 * jnp.sum(jnp.mean(err * err, axis=-1, keepdims=True))
        dout = err * (1.0 / D_MODEL)
        dgf_ref[...] += jnp.sum(dout * xn, axis=0, keepdims=True)
        dxn = dout * gf_v
        dx2 = r * (dxn - xn * jnp.mean(dxn * xn, axis=-1, keepdims=True))
        dx_ref[...] = dx2
        dyo_ref[...] = (dx2 * one_gate).astype(BF16)
        dgt_ref[...] += jnp.sum(dx2 * yo, axis=0, keepdims=True)

    return pl.pallas_call(
        body, name=name, grid=(s // ROW_TILE,),
        in_specs=[_row_spec(), _row_spec(), _vec_spec(), _vec_spec(), _row_spec()],
        out_specs=[_row_spec(), _row_spec(), pl.BlockSpec((1, LANES), lambda i: (0, 0)), _vec_spec(), _vec_spec()],
        out_shape=[SDS((s, D_MODEL), F32), SDS((s, D_MODEL), BF16), SDS((1, LANES), F32),
                   SDS((1, D_MODEL), F32), SDS((1, D_MODEL), F32)],
        compiler_params=_params(1),
    )(x1, yo1, gate1, gf, target)


POOL_WINDOWS = (2, 4, 8, 16)


def _window_sum(x, window, rows, backward):
    acc, step = x, 1
    while step < window:
        acc = acc + pltpu.roll(acc, step if backward else rows - step, axis=0)
        step *= 2
    return acc


def _pool_fwd(name, proj0, wp):
    s = proj0.shape[0]
    hb = ROW_TILE // HALO
    ext_rows = ROW_TILE + HALO

    def body(u_ref, halo_ref, w_ref, p_ref, y_ref):
        i = pl.program_id(0)
        t = i * ROW_TILE + lax.broadcasted_iota(jnp.int32, (ROW_TILE, 1), 0)
        for g, window in enumerate(POOL_WINDOWS):
            cols = slice(g * POOL_GROUP, (g + 1) * POOL_GROUP)
            u = u_ref[:, cols].astype(F32)
            halo = jnp.where(i == 0, 0.0, halo_ref[:, cols].astype(F32))
            ext = jnp.concatenate([halo, u], axis=0)
            win = _window_sum(ext, window, ext_rows, True)[HALO:, :]
            cnt = jnp.minimum(t + 1, window).astype(F32)
            p = (win / cnt - u).astype(BF16)
            p_ref[:, cols] = p
            y_ref[:, cols] = _dot(p, w_ref[g])

    return pl.pallas_call(
        body, name=name, grid=(s // ROW_TILE,),
        in_specs=[pl.BlockSpec((ROW_TILE, D_POOL), lambda i: (i, 0)),
                  pl.BlockSpec((HALO, D_POOL), lambda i: (jnp.maximum(i * hb - 1, 0), 0)),
                  pl.BlockSpec((N_GROUPS, POOL_GROUP, POOL_GROUP), lambda i: (0, 0, 0))],
        out_specs=[_row_spec(D_POOL), _row_spec(D_POOL)],
        out_shape=[SDS((s, D_POOL), BF16), SDS((s, D_POOL), F32)], compiler_params=_params(1),
    )(proj0, proj0, wp)


def _pool_bwd(name, dyp, p, wp):
    s = dyp.shape[0]
    hb = ROW_TILE // HALO
    n_hb = s // HALO
    n_tiles = s // ROW_TILE
    ext_rows = ROW_TILE + HALO

    def body(dy_ref, nxt_ref, p_ref, w_ref, du_ref, dw_ref):
        i = pl.program_id(0)

        @pl.when(i == 0)
        def _():
            dw_ref[...] = jnp.zeros_like(dw_ref)

        t = i * ROW_TILE + lax.broadcasted_iota(jnp.int32, (ext_rows, 1), 0)
        for g, window in enumerate(POOL_WINDOWS):
            cols = slice(g * POOL_GROUP, (g + 1) * POOL_GROUP)
            dy = dy_ref[:, cols]
            nxt = nxt_ref[:, cols]
            nxt = jnp.where(i == n_tiles - 1, jnp.zeros_like(nxt), nxt)
            dp = _dot_nt(jnp.concatenate([dy, nxt], axis=0), w_ref[g])
            cnt = jnp.minimum(t + 1, window).astype(F32)
            win = _window_sum(dp / cnt, window, ext_rows, False)[:ROW_TILE, :]
            du_ref[:, cols] = (win - dp[:ROW_TILE, :]).astype(BF16)
            dw_ref[g] += _dot_tn(p_ref[:, cols], dy)

    return pl.pallas_call(
        body, name=name, grid=(n_tiles,),
        in_specs=[_row_spec(D_POOL),
                  pl.BlockSpec((HALO, D_POOL), lambda i: (jnp.minimum((i + 1) * hb, n_hb - 1), 0)),
                  _row_spec(D_POOL),
                  pl.BlockSpec((N_GROUPS, POOL_GROUP, POOL_GROUP), lambda i: (0, 0, 0))],
        out_specs=[_row_spec(D_POOL), pl.BlockSpec((N_GROUPS, POOL_GROUP, POOL_GROUP), lambda i: (0, 0, 0))],
        out_shape=[SDS((s, D_POOL), BF16), SDS((N_GROUPS, POOL_GROUP, POOL_GROUP), F32)],
        compiler_params=_params(1),
    )(dyp, dyp, p, wp)


FWD_HEADS_PER_STEP = 16
BWD_HEADS_PER_STEP = 8
ATT_SCALE = 0.125
FWD_SKEW = 1


def _att_groups(nh):
    lanes = nh * HEAD_DIM
    return lanes, D_SB // lanes, D_POOL // lanes, (D_POOL + D_SB) // lanes, (D_POOL + 2 * D_SB) // lanes


def _att_consts():
    r = lax.broadcasted_iota(jnp.int32, (ATT_TILE, ATT_TILE), 0)
    c = lax.broadcasted_iota(jnp.int32, (ATT_TILE, ATT_TILE), 1)
    first = lax.broadcasted_iota(jnp.int32, (1, LANES), 1) < HEAD_DIM
    return r, c, first


def _pair(x, p):
    return x[:, p * LANES:(p + 1) * LANES]


def _one_head(x, first, hh):
    zero = jnp.zeros_like(x)
    return jnp.where(first, x, zero) if hh == 0 else jnp.where(first, zero, x)


def _neg_softplus(z):
    return -(jnp.maximum(z, 0.0) + jnp.log(1.0 + jnp.exp(-jnp.abs(z))))


def _side_exchange(side_refs, n_side, by_chunk, is_first, is_last, chips=False):
    ins, outs = side_refs[:n_side], side_refs[n_side:2 * n_side]
    sems = side_refs[2 * n_side:2 * n_side + 3]

    def copies():
        return _chip_copies(ins, outs, *sems) if chips else _peer_copies(ins, outs, *sems, by_chunk=by_chunk)

    @pl.when(is_first)
    def _():
        for cp in copies():
            cp.start()

    @pl.when(is_last)
    def _():
        for cp in copies():
            cp.wait()


def _attn_fwd(name, proj0, shards):
    s = proj0.shape[0]
    nq = s // ATT_TILE
    nh = FWD_HEADS_PER_STEP
    ATT_GROUP, N_ATT_GROUPS, Q_GRP, K_GRP, V_GRP = _att_groups(nh)
    n_side = len(shards)

    def body(q_ref, k_ref, v_ref, *rest):
        o_ref = rest[n_side]
        side = rest[:n_side] + rest[n_side + 1:]
        j = pl.program_id(0)
        i = pl.program_id(1)
        _side_exchange(side, n_side, False, (j == 0) & (i == 0), (j == N_ATT_GROUPS - 1) & (i == nq - 1))
        r, c, first = _att_consts()
        tri = (r >= c).astype(BF16)
        below = c < r
        q = q_ref[...] * ATT_SCALE
        qh = [_one_head(_pair(q, h // 2), first, h % 2) for h in range(nh)]

        def tile(kb, carry, diagonal):
            k0 = pl.multiple_of(kb * ATT_TILE, ATT_TILE)
            kt = k_ref[pl.ds(k0, ATT_TILE), :]
            vt = v_ref[pl.ds(k0, ATT_TILE), :]
            z, lf_b, a_b = [None] * nh, [None] * nh, [None] * nh
            out_c, out_acc = [None] * nh, [None] * nh
            for t in range(nh + 2 * FWD_SKEW):
                if t < nh:
                    z[t] = _dot_nt(qh[t], _pair(kt, t // 2))
                    lf = _neg_softplus(z[t])
                    if diagonal:
                        lf = jnp.where(below, lf, 0.0)
                    lf_b[t] = lf.astype(BF16)
                    out_c[t] = carry[t] + jnp.sum(lf, axis=1, keepdims=True)
                u = t - FWD_SKEW
                if 0 <= u < nh:
                    a = jnp.exp(z[u] + _dot(lf_b[u], tri) + carry[u])
                    if diagonal:
                        a = jnp.where(below, a, 0.0)
                    a_b[u] = a.astype(BF16)
                w = t - 2 * FWD_SKEW
                if 0 <= w < nh:
                    out_acc[w] = carry[nh + w] + _dot(a_b[w], _pair(vt, w // 2))
            return tuple(out_c + out_acc)

        init = tuple([jnp.zeros((ATT_TILE, 1), F32)] * nh + [jnp.zeros((ATT_TILE, LANES), F32)] * nh)
        carry = tile(i, init, True)
        carry = lax.fori_loop(1, i + 1, lambda n, cr: tile(i - n, cr, False), carry)
        for p in range(nh // 2):
            o_ref[:, p * LANES:(p + 1) * LANES] = jnp.where(first, carry[nh + 2 * p], carry[nh + 2 * p + 1])

    out = pl.pallas_call(
        body, name=name, grid=(N_ATT_GROUPS, nq),
        in_specs=[pl.BlockSpec((ATT_TILE, ATT_GROUP), lambda j, i: (i, Q_GRP + j)),
                  pl.BlockSpec((s, ATT_GROUP), lambda j, i: (0, K_GRP + j), pipeline_mode=pl.Buffered(1)),
                  pl.BlockSpec((s, ATT_GROUP), lambda j, i: (0, V_GRP + j), pipeline_mode=pl.Buffered(1))]
        + [HBM_SPEC] * n_side,
        out_specs=[pl.BlockSpec((ATT_TILE, ATT_GROUP), lambda j, i: (i, j))] + [HBM_SPEC] * n_side,
        out_shape=[SDS((s, D_SB), F32)] + [SDS((N_DEV,) + sh.shape, sh.dtype) for sh in shards],
        scratch_shapes=_peer_sems(n_side), compiler_params=_params(2),
    )(proj0, proj0, proj0, *shards)
    return out[0], out[1:]


def _attn_bwd(name, proj0, o, do, dws, shards):
    s = proj0.shape[0]
    nq = s // ATT_TILE
    nh = BWD_HEADS_PER_STEP
    ATT_GROUP, N_ATT_GROUPS, Q_GRP, K_GRP, V_GRP = _att_groups(nh)
    n1, n2 = len(dws), len(shards)
    n_side = n1 + n2

    def body(q_ref, k_ref, v_ref, o_ref, do_ref, *rest):
        dq_ref, dk_ref, dv_ref = rest[n_side:n_side + 3]
        dk_acc, dv_acc = rest[2 * n_side + 3:2 * n_side + 5]
        srcs, dsts, sems = rest[:n_side], rest[n_side + 3:2 * n_side + 3], rest[2 * n_side + 5:]
        j = pl.program_id(0)
        i = pl.program_id(1)
        is_first, is_last = (j == 0) & (i == 0), (j == N_ATT_GROUPS - 1) & (i == nq - 1)
        _side_exchange(srcs[:n1] + dsts[:n1] + sems[:3], n1, True, is_first, is_last)
        _side_exchange(srcs[n1:] + dsts[n1:] + sems[3:], n2, False, is_first, is_last)

        @pl.when(i == 0)
        def _():
            dk_acc[...] = jnp.zeros_like(dk_acc)
            dv_acc[...] = jnp.zeros_like(dv_acc)

        r, c, first = _att_consts()
        tri = (r >= c).astype(BF16)
        tri_p = (r <= c).astype(BF16)
        below = c < r
        q = q_ref[...] * ATT_SCALE
        do_b = do_ref[...].astype(BF16)
        do_o = do_b.astype(F32) * o_ref[...]
        qh = [_one_head(_pair(q, h // 2), first, h % 2) for h in range(nh)]
        doh = [_one_head(_pair(do_b, h // 2), first, h % 2) for h in range(nh)]
        dsum = [jnp.sum(_one_head(_pair(do_o, h // 2), first, h % 2), axis=1, keepdims=True) for h in range(nh)]

        def tile(kb, carry, diagonal):
            k0 = pl.multiple_of(kb * ATT_TILE, ATT_TILE)
            kt = k_ref[pl.ds(k0, ATT_TILE), :]
            vt = v_ref[pl.ds(k0, ATT_TILE), :]
            none = lambda: [None] * nh
            z, d_a, sig, lf_b, a_b, g, early, dz = none(), none(), none(), none(), none(), none(), none(), none()
            out_c1, out_c2, out_dq, dk_t, dv_t = none(), none(), none(), none(), none()
            for t in range(nh + 3):
                if t < nh:
                    z[t] = _dot_nt(qh[t], _pair(kt, t // 2))
                    d_a[t] = _dot_nt(doh[t], _pair(vt, t // 2))
                    lf = _neg_softplus(z[t])
                    sig[t] = jnp.exp(z[t] + lf)
                    if diagonal:
                        lf = jnp.where(below, lf, 0.0)
                    lf_b[t] = lf.astype(BF16)
                    out_c1[t] = carry[t] + jnp.sum(lf, axis=1, keepdims=True)
                u = t - 1
                if 0 <= u < nh:
                    a = jnp.exp(z[u] + _dot(lf_b[u], tri) + carry[u])
                    if diagonal:
                        a = jnp.where(below, a, 0.0)
                    a_b[u] = a.astype(BF16)
                    g[u] = a_b[u].astype(F32) * d_a[u]
                    g_sum = jnp.sum(g[u], axis=1, keepdims=True)
                    early[u] = dsum[u] - carry[nh + u] - g_sum
                    out_c2[u] = carry[nh + u] + g_sum
                w = t - 2
                if 0 <= w < nh:
                    upto = _dot(g[w].astype(BF16), tri_p)
                    dv_t[w] = _dot_tn(a_b[w], doh[w])
                    d = g[w] - sig[w] * (early[w] + upto)
                    if diagonal:
                        d = jnp.where(below, d, 0.0)
                    dz[w] = d.astype(BF16)
                y = t - 3
                if 0 <= y < nh:
                    out_dq[y] = carry[2 * nh + y] + _dot(dz[y], _pair(kt, y // 2))
                    dk_t[y] = _dot_tn(dz[y], qh[y])
            for p in range(nh // 2):
                dk_acc[pl.ds(k0, ATT_TILE), p * LANES:(p + 1) * LANES] += dk_t[2 * p] + dk_t[2 * p + 1]
                dv_acc[pl.ds(k0, ATT_TILE), p * LANES:(p + 1) * LANES] += dv_t[2 * p] + dv_t[2 * p + 1]
            return tuple(out_c1 + out_c2 + out_dq)

        init = tuple([jnp.zeros((ATT_TILE, 1), F32)] * (2 * nh) + [jnp.zeros((ATT_TILE, LANES), F32)] * nh)
        carry = tile(i, init, True)
        carry = lax.fori_loop(1, i + 1, lambda n, cr: tile(i - n, cr, False), carry)
        for p in range(nh // 2):
            dq_p = jnp.where(first, carry[2 * nh + 2 * p], carry[2 * nh + 2 * p + 1]) * ATT_SCALE
            dq_ref[:, p * LANES:(p + 1) * LANES] = dq_p.astype(BF16)

        @pl.when(i == nq - 1)
        def _():
            dk_ref[...] = dk_acc[...].astype(BF16)
            dv_ref[...] = dv_acc[...].astype(BF16)

    tile_spec = pl.BlockSpec((ATT_TILE, ATT_GROUP), lambda j, i: (i, j))
    full = pl.BlockSpec((s, ATT_GROUP), lambda j, i: (0, j), pipeline_mode=pl.Buffered(1))
    out = pl.pallas_call(
        body, name=name, grid=(N_ATT_GROUPS, nq),
        in_specs=[pl.BlockSpec((ATT_TILE, ATT_GROUP), lambda j, i: (i, Q_GRP + j)),
                  pl.BlockSpec((s, ATT_GROUP), lambda j, i: (0, K_GRP + j), pipeline_mode=pl.Buffered(1)),
                  pl.BlockSpec((s, ATT_GROUP), lambda j, i: (0, V_GRP + j), pipeline_mode=pl.Buffered(1)),
                  tile_spec, tile_spec] + [HBM_SPEC] * n_side,
        out_specs=[tile_spec, full, full] + [HBM_SPEC] * n_side,
        out_shape=[SDS((s, D_SB), BF16)] * 3 + [SDS(dw.shape, dw.dtype) for dw in dws]
        + [SDS((N_DEV,) + sh.shape, sh.dtype) for sh in shards],
        scratch_shapes=[pltpu.VMEM((s, ATT_GROUP), F32), pltpu.VMEM((s, ATT_GROUP), F32)] + _peer_sems(n1) + _peer_sems(n2),
        compiler_params=_params(2, ATT_BWD_VMEM_LIMIT),
    )(proj0, proj0, proj0, o, do, *dws, *shards)
    return out[0], out[1], out[2], out[3:3 + n1], out[3 + n1:]


GATE0_COL = (D_POOL + 3 * D_SB) // D_INNER


def _gate_fwd0(name, yp_raw, o, proj0, ps):
    s = o.shape[0]

    def body(yp_ref, o_ref, gt_ref, ps_ref, y_ref, yt_ref):
        gt = gt_ref[...].astype(F32)
        sg = gt * _sigmoid(gt)
        y_pool = (yp_ref[...] * ps_ref[...] * sg[:, :D_POOL]).astype(BF16)
        y_sb = (o_ref[...] * sg[:, D_POOL:]).astype(BF16)
        y_ref[:, :D_POOL] = y_pool
        y_ref[:, D_POOL:] = y_sb
        yt_ref[:D_POOL, :] = y_pool.T
        yt_ref[D_POOL:, :] = y_sb.T

    return pl.pallas_call(
        body, name=name, grid=(s // ROW_TILE,),
        in_specs=[_row_spec(), _row_spec(), _row_spec(D_INNER, GATE0_COL), _vec_spec()],
        out_specs=[_row_spec(D_INNER), pl.BlockSpec((D_INNER, ROW_TILE), lambda i: (0, i))],
        out_shape=[SDS((s, D_INNER), BF16), SDS((D_INNER, s), BF16)], compiler_params=_params(1),
    )(yp_raw, o, proj0, ps)


def _dsilu(x):
    sg = _sigmoid(x)
    return sg * (1.0 + x * (1.0 - sg))


def _gate_bwd0(name, dymix, yp_raw, o, proj0, ps):
    s = o.shape[0]

    def body(dy_ref, yp_ref, o_ref, gt_ref, ps_ref, dyp_ref, do_ref, dgt_ref, dps_ref):
        i = pl.program_id(0)

        @pl.when(i == 0)
        def _():
            dps_ref[...] = jnp.zeros_like(dps_ref)

        gt = gt_ref[...].astype(F32)
        dy = dy_ref[...]
        sg = gt * _sigmoid(gt)
        dsg = _dsilu(gt)
        dcat = dy * sg
        yp = yp_ref[...]
        ps_v = ps_ref[...]
        dyp_ref[...] = (dcat[:, :D_POOL] * ps_v).astype(BF16)
        do_ref[...] = dcat[:, D_POOL:]
        dps_ref[...] += jnp.sum(dcat[:, :D_POOL] * yp, axis=0, keepdims=True)
        dgt_ref[:, :D_POOL] = (dy[:, :D_POOL] * (yp * ps_v) * dsg[:, :D_POOL]).astype(BF16)
        dgt_ref[:, D_POOL:] = (dy[:, D_POOL:] * o_ref[...] * dsg[:, D_POOL:]).astype(BF16)

    return pl.pallas_call(
        body, name=name, grid=(s // ROW_TILE,),
        in_specs=[_row_spec(D_INNER), _row_spec(), _row_spec(), _row_spec(D_INNER, GATE0_COL), _vec_spec()],
        out_specs=[_row_spec(), _row_spec(), _row_spec(D_INNER), _vec_spec()],
        out_shape=[SDS((s, D_POOL), BF16), SDS((s, D_SB), F32), SDS((s, D_INNER), BF16), SDS((1, D_POOL), F32)],
        compiler_params=_params(1),
    )(dymix, yp_raw, o, proj0, ps)


CONV_HALO = 16


def _conv_fwd(name, proj1, cw, cb):
    s = proj1.shape[0]
    hb = ROW_TILE // CONV_HALO
    ext_rows = ROW_TILE + CONV_HALO

    def body(gb_ref, gc_ref, u_ref, gt_ref, gch_ref, uh_ref, cw_ref, cb_ref, y_ref, yt_ref):
        i = pl.program_id(0)
        uc = gc_ref[...].astype(F32) * u_ref[...].astype(F32)
        halo = jnp.where(i == 0, 0.0, gch_ref[...].astype(F32) * uh_ref[...].astype(F32))
        ext = jnp.concatenate([halo, uc], axis=0)
        uc1 = pltpu.roll(ext, 1, axis=0)[CONV_HALO:, :]
        uc2 = pltpu.roll(ext, 2, axis=0)[CONV_HALO:, :]
        cw_v = cw_ref[...]
        conv = cb_ref[...] + cw_v[0:1, :] * uc2 + cw_v[1:2, :] * uc1 + cw_v[2:3, :] * uc
        gt = gt_ref[...].astype(F32)
        y = (gb_ref[...].astype(F32) * conv * (gt * _sigmoid(gt))).astype(BF16)
        y_ref[...] = y
        yt_ref[...] = y.T

    def tile(part):
        return pl.BlockSpec((ROW_TILE, D_INNER), lambda i: (i, part))

    def halo(part):
        return pl.BlockSpec((CONV_HALO, D_INNER), lambda i: (jnp.maximum(i * hb - 1, 0), part))

    return pl.pallas_call(
        body, name=name, grid=(s // ROW_TILE,),
        in_specs=[tile(0), tile(1), tile(2), tile(3), halo(1), halo(2),
                  pl.BlockSpec((3, D_INNER), lambda i: (0, 0)), pl.BlockSpec((1, D_INNER), lambda i: (0, 0))],
        out_specs=[pl.BlockSpec((ROW_TILE, D_INNER), lambda i: (i, 0)), pl.BlockSpec((D_INNER, ROW_TILE), lambda i: (0, i))],
        out_shape=[SDS((s, D_INNER), BF16), SDS((D_INNER, s), BF16)], compiler_params=_params(1),
    )(proj1, proj1, proj1, proj1, proj1, proj1, cw, cb)


def _conv_bwd(name, dymix, proj1, cw, cb):
    s = proj1.shape[0]
    hb = ROW_TILE // CONV_HALO
    n_hb = s // CONV_HALO
    n_tiles = s // ROW_TILE
    ext_rows = ROW_TILE + CONV_HALO

    def body(dy_ref, gb_ref, gc_ref, u_ref, gt_ref, gch_ref, uh_ref, dyn_ref, gbn_ref, gtn_ref, cw_ref, cb_ref,
             dproj_ref, dcw_ref, dcb_ref):
        i = pl.program_id(0)

        @pl.when(i == 0)
        def _():
            dcw_ref[...] = jnp.zeros_like(dcw_ref)
            dcb_ref[...] = jnp.zeros_like(dcb_ref)

        gc = gc_ref[...].astype(F32)
        u = u_ref[...].astype(F32)
        gb = gb_ref[...].astype(F32)
        gt = gt_ref[...].astype(F32)
        dy = dy_ref[...]
        uc = gc * u
        halo = jnp.where(i == 0, 0.0, gch_ref[...].astype(F32) * uh_ref[...].astype(F32))
        ext = jnp.concatenate([halo, uc], axis=0)
        uc1 = pltpu.roll(ext, 1, axis=0)[CONV_HALO:, :]
        uc2 = pltpu.roll(ext, 2, axis=0)[CONV_HALO:, :]
        cw_v = cw_ref[...]
        w0, w1, w2 = cw_v[0:1, :], cw_v[1:2, :], cw_v[2:3, :]
        conv = cb_ref[...] + w0 * uc2 + w1 * uc1 + w2 * uc
        sig = _sigmoid(gt)
        sg = gt * sig
        dconv = dy * gb * sg
        gtn = gtn_ref[...].astype(F32)
        dconv_next = jnp.where(i == n_tiles - 1, 0.0, dyn_ref[...] * gbn_ref[...].astype(F32) * (gtn * _sigmoid(gtn)))
        dext = jnp.concatenate([dconv, dconv_next], axis=0)
        dconv_p1 = pltpu.roll(dext, ext_rows - 1, axis=0)[:ROW_TILE, :]
        dconv_p2 = pltpu.roll(dext, ext_rows - 2, axis=0)[:ROW_TILE, :]
        duc = w2 * dconv + w1 * dconv_p1 + w0 * dconv_p2
        dproj_ref[:, 0:D_INNER] = (dy * conv * sg).astype(BF16)
        dproj_ref[:, D_INNER:2 * D_INNER] = (duc * u).astype(BF16)
        dproj_ref[:, 2 * D_INNER:3 * D_INNER] = (duc * gc).astype(BF16)
        dproj_ref[:, 3 * D_INNER:] = (dy * gb * conv * (sig + sg * (1.0 - sig))).astype(BF16)
        dcw_ref[0:1, :] += jnp.sum(dconv * uc2, axis=0, keepdims=True)
        dcw_ref[1:2, :] += jnp.sum(dconv * uc1, axis=0, keepdims=True)
        dcw_ref[2:3, :] += jnp.sum(dconv * uc, axis=0, keepdims=True)
        dcb_ref[...] += jnp.sum(dconv, axis=0, keepdims=True)

    def tile(part):
        return pl.BlockSpec((ROW_TILE, D_INNER), lambda i: (i, part))

    def prev(part):
        return pl.BlockSpec((CONV_HALO, D_INNER), lambda i: (jnp.maximum(i * hb - 1, 0), part))

    def nxt(part):
        return pl.BlockSpec((CONV_HALO, D_INNER), lambda i: (jnp.minimum((i + 1) * hb, n_hb - 1), part))

    whole = lambda rows: pl.BlockSpec((rows, D_INNER), lambda i: (0, 0))
    return pl.pallas_call(
        body, name=name, grid=(n_tiles,),
        in_specs=[tile(0), tile(0), tile(1), tile(2), tile(3), prev(1), prev(2), nxt(0), nxt(0), nxt(3),
                  whole(3), whole(1)],
        out_specs=[pl.BlockSpec((ROW_TILE, 4 * D_INNER), lambda i: (i, 0)), whole(3), whole(1)],
        out_shape=[SDS((s, 4 * D_INNER), BF16), SDS((3, D_INNER), F32), SDS((1, D_INNER), F32)],
        compiler_params=_params(1),
    )(dymix, proj1, proj1, proj1, proj1, proj1, proj1, dymix, proj1, proj1, cw, cb)


def _place():
    x, y, c = lax.axis_index("x"), lax.axis_index("y"), lax.axis_index("c")
    return x, y, c


def _flip(x, y, c, k):
    fx, fy, fc = (k >> 2) & 1, (k >> 1) & 1, k & 1
    return (1 - x if fx else x, 1 - y if fy else y, 1 - c if fc else c)


def _dev_index(p):
    return 4 * p[0] + 2 * p[1] + p[2]


HBM_SPEC = pl.BlockSpec(memory_space=pltpu.HBM)
VMEM_SPEC = pl.BlockSpec(memory_space=pltpu.VMEM)


def _peer_copies(ins, outs, send_sems, recv_sems, local_sems, by_chunk):
    x, y, c = _place()
    my = _dev_index((x, y, c))
    copies = []
    for w in range(len(ins)):
        copies.append(pltpu.make_async_copy(ins[w].at[my] if by_chunk else ins[w], outs[w].at[my], local_sems.at[w]))
        for k in range(1, N_DEV):
            peer = _flip(x, y, c, k)
            copies.append(pltpu.make_async_remote_copy(
                src_ref=ins[w].at[_dev_index(peer)] if by_chunk else ins[w], dst_ref=outs[w].at[my],
                send_sem=send_sems.at[7 * w + k - 1], recv_sem=recv_sems.at[7 * w + k - 1],
                device_id=peer, device_id_type=MESH))
    return copies


def _peer_sems(n_w):
    return [pltpu.SemaphoreType.DMA((7 * n_w,)), pltpu.SemaphoreType.DMA((7 * n_w,)), pltpu.SemaphoreType.DMA((n_w,))]


def _chip_index(p):
    return 2 * p[0] + p[1]


def _chip_copies(ins, outs, send_sems, recv_sems, local_sems):
    x, y, c = _place()
    mine = _chip_index((x, y))
    copies = []
    for w in range(len(ins)):
        copies.append(pltpu.make_async_copy(ins[w].at[mine], outs[w].at[mine], local_sems.at[w]))
        for k in (2, 4, 6):
            peer = _flip(x, y, c, k)
            copies.append(pltpu.make_async_remote_copy(
                src_ref=ins[w].at[_chip_index(peer)], dst_ref=outs[w].at[mine],
                send_sem=send_sems.at[7 * w + k - 1], recv_sem=recv_sems.at[7 * w + k - 1],
                device_id=peer, device_id_type=MESH))
    return copies


def _sibling_exchange(dw):
    n_chips = N_DEV // 2

    def body(dw_ref, out_ref, send_sems, recv_sems):
        x, y, c = _place()
        sibling = (x, y, 1 - c)
        copies = []
        for ch in range(n_chips):
            copies.append(pltpu.make_async_remote_copy(
                src_ref=dw_ref.at[2 * ch + (1 - c)], dst_ref=out_ref.at[ch],
                send_sem=send_sems.at[ch], recv_sem=recv_sems.at[ch], device_id=sibling, device_id_type=MESH))
        for cp in copies:
            cp.start()
        for cp in copies:
            cp.wait()

    return pl.pallas_call(
        body, name="sibling_exchange", out_shape=SDS((n_chips,) + dw.shape[1:], dw.dtype),
        in_specs=[HBM_SPEC], out_specs=HBM_SPEC,
        scratch_shapes=[pltpu.SemaphoreType.DMA((n_chips,)), pltpu.SemaphoreType.DMA((n_chips,))],
    )(dw)


def _sibling_sum(name, dw, got, core):
    n_chips, rows, cols = got.shape
    tr = min(rows, 256)

    def body(core_ref, a_ref, b_ref, o_ref):
        o_ref[...] = (a_ref[...].astype(F32) + b_ref[...].astype(F32)).astype(BF16)

    return pl.pallas_call(
        body, name=name,
        grid_spec=pltpu.PrefetchScalarGridSpec(
            num_scalar_prefetch=1, grid=(n_chips, rows // tr),
            in_specs=[pl.BlockSpec((None, tr, cols), lambda ch, i, core_ref: (2 * ch + core_ref[0], i, 0)),
                      pl.BlockSpec((None, tr, cols), lambda ch, i, core_ref: (ch, i, 0))],
            out_specs=pl.BlockSpec((None, tr, cols), lambda ch, i, core_ref: (ch, i, 0))),
        out_shape=SDS(got.shape, BF16), compiler_params=_params(2),
    )(core, dw, got)


ADA_COLS = 3 * D_MODEL // N_DEV


def _ada_forward(c_row, conv_w, conv_b, ada_w, ada_b):
    cw_cols = conv_w.shape[1]

    def body(c_ref, cw_ref, cb_ref, aw_ref, ab_ref, m_ref, cs_ref, cwf_ref, cbf_ref,
             slab, gath, part, land, send_sems, recv_sems):
        x, y, c = _place()
        my = _dev_index((x, y, c))
        slab[...] = jnp.zeros_like(slab)
        slab[0:1, :] = c_ref[...]
        slab[1:4, 0:cw_cols] = cw_ref[...]
        slab[4:5, 0:cw_cols] = cb_ref[...]
        gath[my] = slab[...]
        sends = []
        for k in range(1, N_DEV):
            peer = _flip(x, y, c, k)
            cp = pltpu.make_async_remote_copy(
                src_ref=slab, dst_ref=gath.at[my], send_sem=send_sems.at[k - 1], recv_sem=recv_sems.at[k - 1],
                device_id=peer, device_id_type=MESH)
            cp.start()
            sends.append(cp)
        for cp in sends:
            cp.wait()
        for d in range(N_DEV):
            c_d = gath[d, 0:1, :]
            cs_ref[d:d + 1, :] = c_d * _sigmoid(c_d)
            cwf_ref[:, d * cw_cols:(d + 1) * cw_cols] = gath[d, 1:4, 0:cw_cols]
            cbf_ref[:, d * cw_cols:(d + 1) * cw_cols] = gath[d, 4:5, 0:cw_cols]
        cs = cs_ref[...]
        part[...] = jnp.zeros_like(part)
        for layer in range(2):
            m_part = jnp.dot(cs, aw_ref[layer], preferred_element_type=F32, precision=lax.Precision.HIGHEST)
            for d in range(N_DEV):
                part[d, layer:layer + 1, :] = m_part[d:d + 1, :]
        land[my] = part[my]
        sends = []
        for k in range(1, N_DEV):
            peer = _flip(x, y, c, k)
            cp = pltpu.make_async_remote_copy(
                src_ref=part.at[_dev_index(peer)], dst_ref=land.at[my],
                send_sem=send_sems.at[6 + k], recv_sem=recv_sems.at[6 + k],
                device_id=peer, device_id_type=MESH)
            cp.start()
            sends.append(cp)
        for cp in sends:
            cp.wait()
        for d in range(N_DEV):
            cols = slice(d * ADA_COLS, (d + 1) * ADA_COLS)
            m_ref[:, cols] = land[d, 0:2, :] + ab_ref[:, cols]

    return pl.pallas_call(
        body, name="ada_forward",
        out_shape=[SDS((2, 3 * D_MODEL), F32), SDS((N_DEV, D_MODEL), F32), SDS((3, N_DEV * cw_cols), F32),
                   SDS((1, N_DEV * cw_cols), F32)],
        in_specs=[VMEM_SPEC] * 5, out_specs=[VMEM_SPEC] * 4,
        scratch_shapes=[pltpu.VMEM((8, D_MODEL), F32), pltpu.VMEM((N_DEV, 8, D_MODEL), F32),
                        pltpu.VMEM((N_DEV, 8, ADA_COLS), F32), pltpu.VMEM((N_DEV, 8, ADA_COLS), F32),
                        pltpu.SemaphoreType.DMA((14,)), pltpu.SemaphoreType.DMA((14,))],
        compiler_params=pltpu.CompilerParams(vmem_limit_bytes=VMEM_LIMIT),
    )(c_row, conv_w, conv_b, ada_w, ada_b)


def _small_grads(slab):
    def body(slab_ref, gath_ref, tot_ref, send_sems, recv_sems):
        x, y, c = _place()
        my = _dev_index((x, y, c))
        gath_ref[my] = slab_ref[...]
        sends = []
        for k in range(1, N_DEV):
            peer = _flip(x, y, c, k)
            cp = pltpu.make_async_remote_copy(
                src_ref=slab_ref, dst_ref=gath_ref.at[my], send_sem=send_sems.at[k - 1], recv_sem=recv_sems.at[k - 1],
                device_id=peer, device_id_type=MESH)
            cp.start()
            sends.append(cp)
        for cp in sends:
            cp.wait()
        tot = gath_ref[0]
        for d in range(1, N_DEV):
            tot = tot + gath_ref[d]
        tot_ref[...] = tot

    return pl.pallas_call(
        body, name="small_grads",
        out_shape=[SDS((N_DEV, SLAB_ROWS, D_MODEL), F32), SDS((SLAB_ROWS, D_MODEL), F32)],
        in_specs=[VMEM_SPEC], out_specs=[VMEM_SPEC] * 2,
        scratch_shapes=[pltpu.SemaphoreType.DMA((7,)), pltpu.SemaphoreType.DMA((7,))],
    )(slab)


def _adamw_math(w, g, m, v):
    m = ADAM_B1 * m + (1.0 - ADAM_B1) * g
    v = ADAM_B2 * v + (1.0 - ADAM_B2) * jnp.square(g)
    m_hat = m / (1.0 - ADAM_B1 ** ADAM_STEP)
    v_hat = v / (1.0 - ADAM_B2 ** ADAM_STEP)
    delta = -ADAM_LR * (m_hat / (jnp.sqrt(v_hat) + ADAM_EPS) + ADAM_WD * w)
    return delta, m, v


def _sum_adamw(name, recv, w, m, v):
    rows, cols = w.shape
    tr = min(rows, 256)
    n_slots = recv.shape[0]

    def body(r_ref, w_ref, m_ref, v_ref, g_ref, d_ref, nm_ref, nv_ref):
        g = r_ref[0].astype(F32)
        for d in range(1, n_slots):
            g = g + r_ref[d].astype(F32)
        g_ref[...] = g
        d_ref[...], nm_ref[...], nv_ref[...] = _adamw_math(w_ref[...], g, m_ref[...], v_ref[...])

    blk = pl.BlockSpec((tr, cols), lambda i: (i, 0))
    return pl.pallas_call(
        body, name=name, grid=(rows // tr,),
        in_specs=[pl.BlockSpec((n_slots, tr, cols), lambda i: (0, i, 0)), blk, blk, blk],
        out_specs=[blk] * 4, out_shape=[SDS((rows, cols), F32)] * 4, compiler_params=_params(1),
    )(recv, w, m, v)


def _ada_w_adamw(name, cs_t, dm_cols, w, m, v):
    def body(cs_ref, dm_ref, w_ref, m_ref, v_ref, g_ref, d_ref, nm_ref, nv_ref):
        cs = cs_ref[...]
        dm = dm_ref[...]
        g = cs[:, 0:1] * dm[0:1, :]
        for b in range(1, N_DEV):
            g = g + cs[:, b:b + 1] * dm[b:b + 1, :]
        g_ref[...] = g
        d_ref[...], nm_ref[...], nv_ref[...] = _adamw_math(w_ref[...], g, m_ref[...], v_ref[...])

    blk = pl.BlockSpec((None, D_MODEL, ADA_COLS), lambda l: (l, 0, 0))
    return pl.pallas_call(
        body, name=name, grid=(2,),
        in_specs=[pl.BlockSpec((D_MODEL, N_DEV), lambda l: (0, 0)),
                  pl.BlockSpec((None, N_DEV, ADA_COLS), lambda l: (l, 0, 0)), blk, blk, blk],
        out_specs=[blk] * 4, out_shape=[SDS((2, D_MODEL, ADA_COLS), F32)] * 4, compiler_params=_params(1),
    )(cs_t, dm_cols, w, m, v)


def _small_adamw(name, triples):
    n = len(triples)

    def body(*refs):
        ins, outs = refs[:4 * n], refs[4 * n:]
        for j in range(n):
            w_ref, g_ref, m_ref, v_ref = ins[4 * j:4 * j + 4]
            d, nm, nv = _adamw_math(w_ref[...], g_ref[...], m_ref[...], v_ref[...])
            outs[3 * j][...] = d
            outs[3 * j + 1][...] = nm
            outs[3 * j + 2][...] = nv

    flat = [a for t in triples for a in t]
    return pl.pallas_call(
        body, name=name,
        out_shape=[SDS(t[0].shape, F32) for t in triples for _ in range(3)],
        in_specs=[VMEM_SPEC] * (4 * n), out_specs=[VMEM_SPEC] * (3 * n),
    )(*flat)


def kernel(x, c, norm_g, ada_w, ada_b, even_w_in, pool_w, pool_scale, even_w_out, odd_w_in, conv_w, conv_b, odd_w_out, final_g, loss_target, m_norm_g, m_ada_w, m_ada_b, m_even_w_in, m_pool_w, m_pool_scale, m_even_w_out, m_odd_w_in, m_conv_w, m_conv_b, m_odd_w_out, m_final_g, v_norm_g, v_ada_w, v_ada_b, v_even_w_in, v_pool_w, v_pool_scale, v_even_w_out, v_odd_w_in, v_conv_w, v_conv_b, v_odd_w_out, v_final_g):
    seq = x.shape[1]
    x0 = x[0]
    target = loss_target[0]
    final_g2 = final_g.reshape(1, D_MODEL)

    w_in_e = even_w_in[0]
    w_out_e = even_w_out[0]
    w_in_o = odd_w_in[0]
    w_out_o = odd_w_out[0]
    w_pool = pool_w[0].reshape(N_GROUPS * 32, POOL_GROUP)
    shards = [w.astype(BF16) for w in (w_in_e, w_out_e, w_in_o, w_out_o, w_pool)]

    m_vec, cs_all, conv_w_full, conv_b_full = _ada_forward(c, conv_w[0], conv_b, ada_w, ada_b)
    shift = [m_vec[l:l + 1, 0:D_MODEL] for l in range(2)]
    scale = [m_vec[l:l + 1, D_MODEL:2 * D_MODEL] for l in range(2)]
    gate = [m_vec[l:l + 1, 2 * D_MODEL:] for l in range(2)]
    ng = [norm_g[l:l + 1] for l in range(2)]

    h0, h0_t = _ln_mod("ln_mod0", x0, ng[0], scale[0], shift[0])
    ax, ay, ac = lax.axis_index("x"), lax.axis_index("y"), lax.axis_index("c")
    chips = ((1 - ax, ay), (ax, 1 - ay), (1 - ax, 1 - ay))
    arrival = [None] * N_DEV
    arrival[0], arrival[1] = (ax, ay, ac), (ax, ay, 1 - ac)
    for dd_direct, dd_passed, j in GATHER_ARRIVALS:
        arrival[dd_direct], arrival[dd_passed] = (*chips[j], ac), (*chips[j], 1 - ac)
    order = jnp.stack([_dev_index(p) for p in arrival]).astype(jnp.int32)
    proj0 = _proj_in_gather("proj_in0", h0, shards[0], order)
    o, (wg_out_e, wg_in_o, wg_out_o, wg_pool) = _attn_fwd("attn_fwd", proj0, shards[1:])
    wt_in_o = _transpose_shards("transpose_w_in_o", wg_in_o)
    wf_out_e = wg_out_e.reshape(D_INNER, D_MODEL)
    wf_out_o = wg_out_o.reshape(D_INNER, D_MODEL)
    wf_pool = wg_pool.reshape(N_DEV, N_GROUPS, 32, POOL_GROUP).transpose(1, 0, 2, 3).reshape(N_GROUPS, POOL_GROUP, POOL_GROUP)
    p, yp_raw = _pool_fwd("pool_fwd", proj0, wf_pool)
    ymix0, ymix0_t = _gate_fwd0("gate_fwd0", yp_raw, o, proj0, pool_scale)
    yo0 = _proj_out("proj_out0", ymix0, wf_out_e)

    x1, h1, h1_t = _resid_ln_mod("resid_ln_mod1", x0, yo0, gate[0], ng[1], scale[1], shift[1])
    proj1 = _proj_in("proj_in1", h1, wg_in_o)
    ymix1, ymix1_t = _conv_fwd("conv_fwd", proj1, conv_w_full, conv_b_full)
    yo1 = _proj_out("proj_out1", ymix1, wf_out_o)

    dx2, dyo1, loss_acc, d_final_g, d_gate1 = _final_loss("final_loss", x1, yo1, gate[1], final_g2, target)
    loss = lax.psum(loss_acc[0, 0], ("x", "y", "c"))

    dymix1 = _proj_out_bwd("proj_out1_bwd", dyo1, wf_out_o)
    dw_out_o = _wgrad_out("wgrad_out1", ymix1_t, dyo1)
    dproj1, d_conv_w, d_conv_b = _conv_bwd("conv_bwd", dymix1, proj1, conv_w_full, conv_b_full)
    dw_in_o = _wgrad_in("wgrad_in1", h1_t, [dproj1], D_IN_ODD // N_DEV)
    (dx1, d_shift1, d_scale1, d_ng1, dyo0, d_gate0), _ = _proj_in_bwd_ln(
        "proj_in1_bwd", [dproj1], wt_in_o, x1, dx2, ng[1], scale[1], resid=(yo0, gate[0]))

    dymix0 = _proj_out_bwd("proj_out0_bwd", dyo0, wf_out_e)
    dw_out_e = _wgrad_out("wgrad_out0", ymix0_t, dyo0)
    dyp, do, dgt0, d_pool_scale = _gate_bwd0("gate_bwd0", dymix0, yp_raw, o, proj0, pool_scale)
    du_pool, dw_pool = _pool_bwd("pool_bwd", dyp, p, wf_pool)
    dw_pool_c = dw_pool.reshape(N_GROUPS, N_DEV, 32, POOL_GROUP).transpose(1, 0, 2, 3).reshape(N_DEV, N_GROUPS * 32, POOL_GROUP).astype(BF16)
    ready = [dw_out_e.reshape(N_DEV, D_INNER // N_DEV, D_MODEL), dw_in_o,
             dw_out_o.reshape(N_DEV, D_INNER // N_DEV, D_MODEL), dw_pool_c]
    dq, dk, dv, (r_out_e, r_in_o, r_out_o, r_pool), (wt_in_e,) = _attn_bwd(
        "attn_bwd", proj0, o, do, ready, [shards[0].T])
    dparts0 = [du_pool, dq, dk, dv, dgt0]
    dw_in_e = _wgrad_in("wgrad_in0", h0_t, dparts0, WGRAD_BLOCK)
    core = lax.axis_index("c").astype(jnp.int32).reshape(1)
    chip_sums = _sibling_sum("sibling_sum", dw_in_e, _sibling_exchange(dw_in_e), core)
    (dx0, d_shift0, d_scale0, d_ng0), (r_in_e,) = _proj_in_bwd_ln(
        "proj_in0_bwd", dparts0, wt_in_e, x0, dx1, ng[0], scale[0], dws=[chip_sums])
    grad_x = dx0[None]

    big = {}
    big["even_w_in"] = _sum_adamw("adamw_even_w_in", r_in_e, w_in_e, m_even_w_in[0], v_even_w_in[0])
    big["even_w_out"] = _sum_adamw("adamw_even_w_out", r_out_e, w_out_e, m_even_w_out[0], v_even_w_out[0])
    big["odd_w_in"] = _sum_adamw("adamw_odd_w_in", r_in_o, w_in_o, m_odd_w_in[0], v_odd_w_in[0])
    big["odd_w_out"] = _sum_adamw("adamw_odd_w_out", r_out_o, w_out_o, m_odd_w_out[0], v_odd_w_out[0])
    big["pool_w"] = _sum_adamw("adamw_pool_w", r_pool, w_pool, m_pool_w[0].reshape(N_GROUPS * 32, POOL_GROUP),
                               v_pool_w[0].reshape(N_GROUPS * 32, POOL_GROUP))
    big = {k: [a.reshape(shape) for a in v] for (k, v), shape in zip(
        big.items(), [even_w_in.shape, even_w_out.shape, odd_w_in.shape, odd_w_out.shape, pool_w.shape])}

    dm = jnp.concatenate([jnp.concatenate([d_shift0, d_scale0, d_gate0], axis=1),
                          jnp.concatenate([d_shift1, d_scale1, d_gate1], axis=1)], axis=0)
    slab = jnp.zeros((SLAB_ROWS, D_MODEL), F32)
    slab = slab.at[0:6].set(dm.reshape(6, D_MODEL))
    slab = slab.at[8:9].set(d_ng0).at[9:10].set(d_ng1).at[10:11].set(d_pool_scale).at[11:12].set(d_final_g)
    slab = slab.at[16:22].set(d_conv_w.reshape(6, D_MODEL)).at[24:26].set(d_conv_b.reshape(2, D_MODEL))
    gathered, total = _small_grads(slab)
    my = 4 * lax.axis_index("x") + 2 * lax.axis_index("y") + lax.axis_index("c")
    g_ada_b = total[0:6].reshape(2, 3 * D_MODEL)
    g_norm_g = total[8:10]
    g_pool_scale = total[10:11]
    g_final_g = total[11:12]
    cw_cols = conv_w.shape[2]
    g_conv_w = lax.dynamic_slice_in_dim(total[16:22].reshape(3, D_INNER), my * cw_cols, cw_cols, axis=1)
    g_conv_b = lax.dynamic_slice_in_dim(total[24:26].reshape(1, D_INNER), my * cw_cols, cw_cols, axis=1)
    dm_all = gathered[:, 0:6, :].reshape(N_DEV, 2, 3 * D_MODEL)
    dm_cols = lax.dynamic_slice_in_dim(dm_all, my * ADA_COLS, ADA_COLS, axis=2).transpose(1, 0, 2)
    ada = _ada_w_adamw("adamw_ada_w", cs_all.T, dm_cols, ada_w, m_ada_w, v_ada_w)

    small = _small_adamw("adamw_small", [
        (norm_g, g_norm_g, m_norm_g, v_norm_g),
        (ada_b, g_ada_b, m_ada_b, v_ada_b),
        (pool_scale, g_pool_scale, m_pool_scale, v_pool_scale),
        (conv_w[0], g_conv_w, m_conv_w[0], v_conv_w[0]),
        (conv_b, g_conv_b, m_conv_b, v_conv_b),
        (final_g2, g_final_g, m_final_g.reshape(1, D_MODEL), v_final_g.reshape(1, D_MODEL)),
    ])
    small = [small[3 * j:3 * j + 3] for j in range(6)]

    grads = {
        "norm_g": g_norm_g, "ada_w": ada[0], "ada_b": g_ada_b, "even_w_in": big["even_w_in"][0],
        "pool_w": big["pool_w"][0], "pool_scale": g_pool_scale, "even_w_out": big["even_w_out"][0],
        "odd_w_in": big["odd_w_in"][0], "conv_w": g_conv_w.reshape(conv_w.shape), "conv_b": g_conv_b,
        "odd_w_out": big["odd_w_out"][0], "final_g": g_final_g.reshape(D_MODEL),
    }
    rest = []
    for idx in range(3):
        rest += [
            small[0][idx], ada[1 + idx], small[1][idx], big["even_w_in"][1 + idx], big["pool_w"][1 + idx],
            small[2][idx], big["even_w_out"][1 + idx], big["odd_w_in"][1 + idx],
            small[3][idx].reshape(conv_w.shape), small[4][idx], big["odd_w_out"][1 + idx],
            small[5][idx].reshape(D_MODEL),
        ]
    order = ["norm_g", "ada_w", "ada_b", "even_w_in", "pool_w", "pool_scale", "even_w_out", "odd_w_in",
             "conv_w", "conv_b", "odd_w_out", "final_g"]
    return (loss, grad_x, *[grads[n] for n in order], *rest)
```

```python
import jax
import jax.numpy as jnp
from jax import lax
from jax.experimental import pallas as pl
from jax.experimental.pallas import tpu as pltpu

F32 = jnp.float32
BF16 = jnp.bfloat16
SDS = jax.ShapeDtypeStruct
MESH = pl.DeviceIdType.MESH

N_DEV = 8
D_MODEL = 1024
D_INNER = 2048
D_POOL = 1024
D_SB = 1024
N_GROUPS = 4
POOL_GROUP = 256
HEAD_DIM = 64
LANES = 128
D_IN_EVEN = 6144
D_IN_ODD = 8192
EPS = 1e-6
ADAM_LR = 0.001
ADAM_B1 = 0.9
ADAM_B2 = 0.999
ADAM_EPS = 1e-08
ADAM_WD = 0.01
ADAM_STEP = 10

ROW_TILE = 256
ATT_TILE = 256
HALO = 16
VMEM_LIMIT = 48 * 1024 * 1024
ATT_BWD_VMEM_LIMIT = 56 * 1024 * 1024
SLAB_ROWS = 32
WGRAD_BLOCK = 256


def _params(n_axes, vmem_limit=VMEM_LIMIT):
    return pltpu.CompilerParams(dimension_semantics=("arbitrary",) * n_axes, vmem_limit_bytes=vmem_limit)


def _sigmoid(x):
    return 1.0 / (1.0 + jnp.exp(-x))


def _dot(a, b):
    return jnp.dot(a, b, preferred_element_type=F32)


def _dot_nt(a, b):
    return lax.dot_general(a, b, (((1,), (1,)), ((), ())), preferred_element_type=F32)


def _dot_tn(a, b):
    return lax.dot_general(a, b, (((0,), (0,)), ((), ())), preferred_element_type=F32)


def _mm(name, a, b, *, grid, a_spec, b_spec, o_spec, o_shape, o_dtype, dot, acc_axis=None, acc_shape=None):
    n_acc = grid[acc_axis] if acc_axis is not None else 1

    def body(a_ref, b_ref, o_ref, *scratch):
        prod = dot(a_ref[...], b_ref[...])
        if acc_axis is None:
            o_ref[...] = prod.astype(o_dtype)
        else:
            acc = scratch[0]
            k = pl.program_id(acc_axis)

            @pl.when(k == 0)
            def _():
                acc[...] = prod

            @pl.when(k > 0)
            def _():
                acc[...] += prod

            @pl.when(k == n_acc - 1)
            def _():
                o_ref[...] = acc[...].astype(o_dtype)

    scratch = [] if acc_axis is None else [pltpu.VMEM(acc_shape, F32)]
    return pl.pallas_call(
        body, name=name, grid=grid, in_specs=[a_spec, b_spec], out_specs=o_spec,
        out_shape=SDS(o_shape, o_dtype), scratch_shapes=scratch, compiler_params=_params(len(grid)),
    )(a, b)


def _proj_in(name, h, wg):
    s = h.shape[0]
    cn = wg.shape[2]
    tm = min(s, ROW_TILE)

    def body(a_ref, w_ref, o_ref):
        a = a_ref[...]
        for d in range(N_DEV):
            o_ref[:, d * cn:(d + 1) * cn] = _dot(a, w_ref[d]).astype(BF16)

    return pl.pallas_call(
        body, name=name, grid=(s // tm,),
        in_specs=[pl.BlockSpec((tm, D_MODEL), lambda i: (i, 0)),
                  pl.BlockSpec((N_DEV, D_MODEL, cn), lambda i: (0, 0, 0), pipeline_mode=pl.Buffered(1))],
        out_specs=pl.BlockSpec((tm, N_DEV * cn), lambda i: (i, 0)),
        out_shape=SDS((s, N_DEV * cn), BF16), compiler_params=_params(1),
    )(h, wg)


GATHER_ROWS = 1024
GATHER_ARRIVALS = ((2, 4, 0), (3, 5, 1), (6, 7, 2))


def _proj_in_gather(name, h, w_shard, order):
    s = h.shape[0]
    k_dim, cn = w_shard.shape
    tm = min(s, GATHER_ROWS)
    n_i = s // tm

    def body(ord_ref, a_ref, w_ref, o_ref, w_buf, send_sems, recv_sems, local_sem):
        dd = pl.program_id(0)
        i = pl.program_id(1)
        x, y, c = _place()
        me, sibling = (x, y, c), (x, y, 1 - c)
        chips = [(1 - x, y), (x, 1 - y), (1 - x, 1 - y)]

        def copy(k, block, to, from_src=False):
            slot = w_buf.at[_dev_index(block)]
            return pltpu.make_async_remote_copy(
                src_ref=w_ref if from_src else slot, dst_ref=slot, send_sem=send_sems.at[k], recv_sem=recv_sems.at[k],
                device_id=to, device_id_type=MESH)

        mine = pltpu.make_async_copy(w_ref, w_buf.at[_dev_index(me)], local_sem)
        first = [copy(0, me, sibling, True)] + [copy(1 + j, me, (*chip, c), True) for j, chip in enumerate(chips)]
        passed = [copy(4 + j, (*chip, c), sibling) for j, chip in enumerate(chips)]

        @pl.when((dd == 0) & (i == 0))
        def _():
            mine.start()
            for cp in first:
                cp.start()
            mine.wait()

        @pl.when((dd == 1) & (i == 0))
        def _():
            copy(0, sibling, me).wait_recv()

        for dd_direct, dd_passed, j in GATHER_ARRIVALS:
            @pl.when((dd == dd_direct) & (i == 0))
            def _(j=j):
                copy(1 + j, (*chips[j], c), me).wait_recv()
                passed[j].start()

            @pl.when((dd == dd_passed) & (i == 0))
            def _(j=j):
                copy(4 + j, (*chips[j], 1 - c), me).wait_recv()

        o_ref[...] = _dot(a_ref[...], w_buf[ord_ref[dd]]).astype(BF16)

        @pl.when((dd == N_DEV - 1) & (i == n_i - 1))
        def _():
            for cp in first + passed:
                cp.wait_send()

    return pl.pallas_call(
        body, name=name,
        grid_spec=pltpu.PrefetchScalarGridSpec(
            num_scalar_prefetch=1, grid=(N_DEV, n_i),
            in_specs=[pl.BlockSpec((tm, k_dim), lambda dd, i, ord_ref: (i, 0)), HBM_SPEC],
            out_specs=pl.BlockSpec((tm, cn), lambda dd, i, ord_ref: (i, ord_ref[dd])),
            scratch_shapes=[pltpu.VMEM((N_DEV, k_dim, cn), BF16), pltpu.SemaphoreType.DMA((7,)),
                            pltpu.SemaphoreType.DMA((7,)), pltpu.SemaphoreType.DMA]),
        out_shape=SDS((s, N_DEV * cn), BF16), compiler_params=_params(2),
    )(order, h, w_shard)


def _transpose_shards(name, wg):
    n, k_dim, cn = wg.shape
    tb = 256

    def body(a_ref, o_ref):
        o_ref[...] = a_ref[...].T

    return pl.pallas_call(
        body, name=name, grid=(n, cn // tb),
        in_specs=[pl.BlockSpec((None, k_dim, tb), lambda d, j: (d, 0, j))],
        out_specs=pl.BlockSpec((None, tb, k_dim), lambda d, j: (d, j, 0)),
        out_shape=SDS((n, cn, k_dim), wg.dtype), compiler_params=_params(2),
    )(wg)


def _proj_out(name, y, w):
    s = y.shape[0]
    tm = min(s, 512)
    return _mm(name, y, w, grid=(s // tm,),
               a_spec=pl.BlockSpec((tm, D_INNER), lambda i: (i, 0)),
               b_spec=pl.BlockSpec((D_INNER, D_MODEL), lambda i: (0, 0)),
               o_spec=pl.BlockSpec((tm, D_MODEL), lambda i: (i, 0)),
               o_shape=(s, D_MODEL), o_dtype=F32, dot=_dot)


def _proj_out_bwd(name, dyo, w):
    s = dyo.shape[0]
    tm = min(s, 512)
    return _mm(name, dyo, w, grid=(s // tm,),
               a_spec=pl.BlockSpec((tm, D_MODEL), lambda i: (i, 0)),
               b_spec=pl.BlockSpec((D_INNER, D_MODEL), lambda i: (0, 0)),
               o_spec=pl.BlockSpec((tm, D_INNER), lambda i: (i, 0)),
               o_shape=(s, D_INNER), o_dtype=F32, dot=_dot_nt)


def _wgrad_out(name, y_t, dyo):
    s = y_t.shape[1]
    tm = 512
    return _mm(name, y_t, dyo, grid=(D_INNER // tm,),
               a_spec=pl.BlockSpec((tm, s), lambda r: (r, 0)),
               b_spec=pl.BlockSpec((s, D_MODEL), lambda r: (0, 0), pipeline_mode=pl.Buffered(1)),
               o_spec=pl.BlockSpec((tm, D_MODEL), lambda r: (r, 0)),
               o_shape=(D_INNER, D_MODEL), o_dtype=BF16, dot=_dot)


def _proj_in_bwd_ln(name, parts, wt, x, dx_next, g, scale, resid=None, dws=(), untransposed=False):
    s = x.shape[0]
    widths = [p.shape[1] for p in parts]
    offs = [sum(widths[:k]) for k in range(len(parts))]
    k_all = sum(widths)
    n_i = s // ROW_TILE
    n_p, n_r, n_side = len(parts), (2 if resid else 0), len(dws)
    w_all = wt if untransposed else wt.reshape(k_all, D_MODEL)
    w_spec = (pl.BlockSpec(wt.shape, lambda i: (0, 0, 0), pipeline_mode=pl.Buffered(1)) if untransposed
              else pl.BlockSpec((k_all, D_MODEL), lambda i: (0, 0), pipeline_mode=pl.Buffered(1)))

    def body(*refs):
        part_refs = refs[:n_p]
        w_ref, x_ref, dxn_ref, g_ref, sc_ref = refs[n_p:n_p + 5]
        resid_refs = refs[n_p + 5:n_p + 5 + n_r]
        srcs = refs[n_p + 5 + n_r:n_p + 5 + n_r + n_side]
        outs = refs[n_p + 5 + n_r + n_side:]
        dx_ref, dsh_ref, dsc_ref, dg_ref = outs[:4]
        resid_outs = outs[4:4 + n_r]
        dsts = outs[4 + n_r:4 + n_r + n_side]
        sems = outs[4 + n_r + n_side:]
        i = pl.program_id(0)
        if n_side:
            _side_exchange(srcs + dsts + sems, n_side, True, i == 0, i == n_i - 1, chips=True)

        @pl.when(i == 0)
        def _():
            dsh_ref[...] = jnp.zeros_like(dsh_ref)
            dsc_ref[...] = jnp.zeros_like(dsc_ref)
            dg_ref[...] = jnp.zeros_like(dg_ref)
            if resid:
                resid_outs[1][...] = jnp.zeros_like(resid_outs[1])

        if untransposed:
            cn = k_all // N_DEV
            dh_v = _dot_nt(part_refs[0][:, 0:cn], w_ref[0])
            for d in range(1, N_DEV):
                dh_v = dh_v + _dot_nt(part_refs[0][:, d * cn:(d + 1) * cn], w_ref[d])
        else:
            dh_v = _dot(part_refs[0][...], w_ref[offs[0]:offs[0] + widths[0], :])
            for k in range(1, n_p):
                dh_v = dh_v + _dot(part_refs[k][...], w_ref[offs[k]:offs[k] + widths[k], :])
        xv = x_ref[...]
        g_v = g_ref[...]
        r = lax.rsqrt(jnp.mean(xv * xv, axis=-1, keepdims=True) + EPS)
        xn = xv * r
        dsh_ref[...] += jnp.sum(dh_v, axis=0, keepdims=True)
        dsc_ref[...] += jnp.sum(dh_v * (xn * g_v), axis=0, keepdims=True)
        dn = dh_v * (1.0 + sc_ref[...])
        dg_ref[...] += jnp.sum(dn * xn, axis=0, keepdims=True)
        dxh = dn * g_v
        dx = dxn_ref[...] + r * (dxh - xn * jnp.mean(dxh * xn, axis=-1, keepdims=True))
        dx_ref[...] = dx
        if resid:
            yo_ref, gt_ref = resid_refs
            resid_outs[0][...] = (dx * (1.0 + gt_ref[...])).astype(BF16)
            resid_outs[1][...] += jnp.sum(dx * yo_ref[...], axis=0, keepdims=True)

    row, vec = _row_spec(), _vec_spec()
    out = pl.pallas_call(
        body, name=name, grid=(n_i,),
        in_specs=[_row_spec(w) for w in widths]
        + [w_spec, row, row, vec, vec]
        + ([row, vec] if resid else []) + [HBM_SPEC] * n_side,
        out_specs=[row, vec, vec, vec] + ([row, vec] if resid else []) + [HBM_SPEC] * n_side,
        out_shape=[SDS((s, D_MODEL), F32)] + [SDS((1, D_MODEL), F32)] * 3
        + ([SDS((s, D_MODEL), BF16), SDS((1, D_MODEL), F32)] if resid else [])
        + [SDS(dw.shape, dw.dtype) for dw in dws],
        scratch_shapes=_peer_sems(n_side) if n_side else [],
        compiler_params=_params(1),
    )(*parts, w_all, x, dx_next, g, scale, *(resid or ()), *dws)
    return out[:4 + n_r], out[4 + n_r:]


def _wgrad_in(name, h_t, parts, blk):
    s = h_t.shape[1]
    widths = [p.shape[1] for p in parts]
    cn = sum(widths) // N_DEV
    per_dev = cn // blk
    starts = [sum(widths[:k]) // blk for k in range(len(parts))]
    counts = [w // blk for w in widths]
    n_blk = sum(counts)

    def body(a_ref, *rest):
        o_ref = rest[len(parts)]
        b = pl.program_id(0)
        for k in range(len(parts)):
            @pl.when((b >= starts[k]) & (b < starts[k] + counts[k]))
            def _(k=k):
                o_ref[...] = _dot(a_ref[...], rest[k][...]).astype(BF16)

    def part_spec(k):
        return pl.BlockSpec((s, blk), lambda b: (0, jnp.clip(b - starts[k], 0, counts[k] - 1)))

    return pl.pallas_call(
        body, name=name, grid=(n_blk,),
        in_specs=[pl.BlockSpec((D_MODEL, s), lambda b: (0, 0), pipeline_mode=pl.Buffered(1))]
        + [part_spec(k) for k in range(len(parts))],
        out_specs=pl.BlockSpec((None, D_MODEL, blk), lambda b: (b // per_dev, 0, b % per_dev)),
        out_shape=SDS((N_DEV, D_MODEL, cn), BF16), compiler_params=_params(1),
    )(h_t, *parts)


def _vec_spec():
    return pl.BlockSpec((1, D_MODEL), lambda i: (0, 0))


def _row_spec(width=D_MODEL, col=0):
    return pl.BlockSpec((ROW_TILE, width), lambda i: (i, col))


def _col_spec():
    return pl.BlockSpec((D_MODEL, ROW_TILE), lambda i: (0, i))


def _ln_mod(name, x, g, scale, shift):
    s = x.shape[0]

    def body(x_ref, g_ref, sc_ref, sh_ref, h_ref, ht_ref):
        xv = x_ref[...]
        r = lax.rsqrt(jnp.mean(xv * xv, axis=-1, keepdims=True) + EPS)
        n = (xv * r) * g_ref[...]
        h = (n * (1.0 + sc_ref[...]) + sh_ref[...]).astype(BF16)
        h_ref[...] = h
        ht_ref[...] = h.T

    return pl.pallas_call(
        body, name=name, grid=(s // ROW_TILE,),
        in_specs=[_row_spec(), _vec_spec(), _vec_spec(), _vec_spec()], out_specs=[_row_spec(), _col_spec()],
        out_shape=[SDS((s, D_MODEL), BF16), SDS((D_MODEL, s), BF16)], compiler_params=_params(1),
    )(x, g, scale, shift)


def _resid_ln_mod(name, x, yo, gate, g, scale, shift):
    s = x.shape[0]

    def body(x_ref, yo_ref, gt_ref, g_ref, sc_ref, sh_ref, xn_ref, h_ref, ht_ref):
        xv = x_ref[...] + (1.0 + gt_ref[...]) * yo_ref[...]
        xn_ref[...] = xv
        r = lax.rsqrt(jnp.mean(xv * xv, axis=-1, keepdims=True) + EPS)
        n = (xv * r) * g_ref[...]
        h = (n * (1.0 + sc_ref[...]) + sh_ref[...]).astype(BF16)
        h_ref[...] = h
        ht_ref[...] = h.T

    return pl.pallas_call(
        body, name=name, grid=(s // ROW_TILE,),
        in_specs=[_row_spec(), _row_spec(), _vec_spec(), _vec_spec(), _vec_spec(), _vec_spec()],
        out_specs=[_row_spec(), _row_spec(), _col_spec()],
        out_shape=[SDS((s, D_MODEL), F32), SDS((s, D_MODEL), BF16), SDS((D_MODEL, s), BF16)],
        compiler_params=_params(1),
    )(x, yo, gate, g, scale, shift)


def _final_loss(name, x1, yo1, gate1, gf, target):
    s = x1.shape[0]

    def body(x_ref, yo_ref, gt_ref, gf_ref, t_ref, dx_ref, dyo_ref, loss_ref, dgf_ref, dgt_ref):
        i = pl.program_id(0)

        @pl.when(i == 0)
        def _():
            loss_ref[...] = jnp.zeros_like(loss_ref)
            dgf_ref[...] = jnp.zeros_like(dgf_ref)
            dgt_ref[...] = jnp.zeros_like(dgt_ref)

        yo = yo_ref[...]
        one_gate = 1.0 + gt_ref[...]
        x2 = x_ref[...] + one_gate * yo
        r = lax.rsqrt(jnp.mean(x2 * x2, axis=-1, keepdims=True) + EPS)
        xn = x2 * r
        gf_v = gf_ref[...]
        err = xn * gf_v - t_ref[...]
        loss_ref[...] += 0.5 * jnp.sum(jnp.mean(err * err, axis=-1, keepdims=True))
        dout = err * (1.0 / D_MODEL)
        dgf_ref[...] += jnp.sum(dout * xn, axis=0, keepdims=True)
        dxn = dout * gf_v
        dx2 = r * (dxn - xn * jnp.mean(dxn * xn, axis=-1, keepdims=True))
        dx_ref[...] = dx2
        dyo_ref[...] = (dx2 * one_gate).astype(BF16)
        dgt_ref[...] += jnp.sum(dx2 * yo, axis=0, keepdims=True)

    return pl.pallas_call(
        body, name=name, grid=(s // ROW_TILE,),
        in_specs=[_row_spec(), _row_spec(), _vec_spec(), _vec_spec(), _row_spec()],
        out_specs=[_row_spec(), _row_spec(), pl.BlockSpec((1, LANES), lambda i: (0, 0)), _vec_spec(), _vec_spec()],
        out_shape=[SDS((s, D_MODEL), F32), SDS((s, D_MODEL), BF16), SDS((1, LANES), F32),
                   SDS((1, D_MODEL), F32), SDS((1, D_MODEL), F32)],
        compiler_params=_params(1),
    )(x1, yo1, gate1, gf, target)


POOL_WINDOWS = (2, 4, 8, 16)


def _window_sum(x, window, rows, backward):
    acc, step = x, 1
    while step < window:
        acc = acc + pltpu.roll(acc, step if backward else rows - step, axis=0)
        step *= 2
    return acc


def _pool_fwd(name, proj0, wp):
    s = proj0.shape[0]
    hb = ROW_TILE // HALO
    ext_rows = ROW_TILE + HALO

    def body(u_ref, halo_ref, w_ref, p_ref, y_ref):
        i = pl.program_id(0)
        t = i * ROW_TILE + lax.broadcasted_iota(jnp.int32, (ROW_TILE, 1), 0)
        for g, window in enumerate(POOL_WINDOWS):
            cols = slice(g * POOL_GROUP, (g + 1) * POOL_GROUP)
            u = u_ref[:, cols].astype(F32)
            halo = jnp.where(i == 0, 0.0, halo_ref[:, cols].astype(F32))
            ext = jnp.concatenate([halo, u], axis=0)
            win = _window_sum(ext, window, ext_rows, True)[HALO:, :]
            cnt = jnp.minimum(t + 1, window).astype(F32)
            p = (win / cnt - u).astype(BF16)
            p_ref[:, cols] = p
            y_ref[:, cols] = _dot(p, w_ref[g])

    return pl.pallas_call(
        body, name=name, grid=(s // ROW_TILE,),
        in_specs=[pl.BlockSpec((ROW_TILE, D_POOL), lambda i: (i, 0)),
                  pl.BlockSpec((HALO, D_POOL), lambda i: (jnp.maximum(i * hb - 1, 0), 0)),
                  pl.BlockSpec((N_GROUPS, POOL_GROUP, POOL_GROUP), lambda i: (0, 0, 0))],
        out_specs=[_row_spec(D_POOL), _row_spec(D_POOL)],
        out_shape=[SDS((s, D_POOL), BF16), SDS((s, D_POOL), F32)], compiler_params=_params(1),
    )(proj0, proj0, wp)


def _pool_bwd(name, dyp, p, wp):
    s = dyp.shape[0]
    hb = ROW_TILE // HALO
    n_hb = s // HALO
    n_tiles = s // ROW_TILE
    ext_rows = ROW_TILE + HALO

    def body(dy_ref, nxt_ref, p_ref, w_ref, du_ref, dw_ref):
        i = pl.program_id(0)

        @pl.when(i == 0)
        def _():
            dw_ref[...] = jnp.zeros_like(dw_ref)

        t = i * ROW_TILE + lax.broadcasted_iota(jnp.int32, (ext_rows, 1), 0)
        for g, window in enumerate(POOL_WINDOWS):
            cols = slice(g * POOL_GROUP, (g + 1) * POOL_GROUP)
            dy = dy_ref[:, cols]
            nxt = nxt_ref[:, cols]
            nxt = jnp.where(i == n_tiles - 1, jnp.zeros_like(nxt), nxt)
            dp = _dot_nt(jnp.concatenate([dy, nxt], axis=0), w_ref[g])
            cnt = jnp.minimum(t + 1, window).astype(F32)
            win = _window_sum(dp / cnt, window, ext_rows, False)[:ROW_TILE, :]
            du_ref[:, cols] = (win - dp[:ROW_TILE, :]).astype(BF16)
            dw_ref[g] += _dot_tn(p_ref[:, cols], dy)

    return pl.pallas_call(
        body, name=name, grid=(n_tiles,),
        in_specs=[_row_spec(D_POOL),
                  pl.BlockSpec((HALO, D_POOL), lambda i: (jnp.minimum((i + 1) * hb, n_hb - 1), 0)),
                  _row_spec(D_POOL),
                  pl.BlockSpec((N_GROUPS, POOL_GROUP, POOL_GROUP), lambda i: (0, 0, 0))],
        out_specs=[_row_spec(D_POOL), pl.BlockSpec((N_GROUPS, POOL_GROUP, POOL_GROUP), lambda i: (0, 0, 0))],
        out_shape=[SDS((s, D_POOL), BF16), SDS((N_GROUPS, POOL_GROUP, POOL_GROUP), F32)],
        compiler_params=_params(1),
    )(dyp, dyp, p, wp)


FWD_HEADS_PER_STEP = 16
BWD_HEADS_PER_STEP = 8
ATT_SCALE = 0.125
FWD_SKEW = 1


def _att_groups(nh):
    lanes = nh * HEAD_DIM
    return lanes, D_SB // lanes, D_POOL // lanes, (D_POOL + D_SB) // lanes, (D_POOL + 2 * D_SB) // lanes


def _att_consts():
    r = lax.broadcasted_iota(jnp.int32, (ATT_TILE, ATT_TILE), 0)
    c = lax.broadcasted_iota(jnp.int32, (ATT_TILE, ATT_TILE), 1)
    first = lax.broadcasted_iota(jnp.int32, (1, LANES), 1) < HEAD_DIM
    return r, c, first


def _pair(x, p):
    return x[:, p * LANES:(p + 1) * LANES]


def _one_head(x, first, hh):
    zero = jnp.zeros_like(x)
    return jnp.where(first, x, zero) if hh == 0 else jnp.where(first, zero, x)


def _neg_softplus(z):
    return -(jnp.maximum(z, 0.0) + jnp.log(1.0 + jnp.exp(-jnp.abs(z))))


def _side_exchange(side_refs, n_side, by_chunk, is_first, is_last, chips=False):
    ins, outs = side_refs[:n_side], side_refs[n_side:2 * n_side]
    sems = side_refs[2 * n_side:2 * n_side + 3]

    def copies():
        return _chip_copies(ins, outs, *sems) if chips else _peer_copies(ins, outs, *sems, by_chunk=by_chunk)

    @pl.when(is_first)
    def _():
        for cp in copies():
            cp.start()

    @pl.when(is_last)
    def _():
        for cp in copies():
            cp.wait()


def _attn_fwd(name, proj0, shards):
    s = proj0.shape[0]
    nq = s // ATT_TILE
    nh = FWD_HEADS_PER_STEP
    ATT_GROUP, N_ATT_GROUPS, Q_GRP, K_GRP, V_GRP = _att_groups(nh)
    n_side = len(shards)

    def body(q_ref, k_ref, v_ref, *rest):
        o_ref = rest[n_side]
        side = rest[:n_side] + rest[n_side + 1:]
        j = pl.program_id(0)
        i = pl.program_id(1)
        _side_exchange(side, n_side, False, (j == 0) & (i == 0), (j == N_ATT_GROUPS - 1) & (i == nq - 1))
        r, c, first = _att_consts()
        tri = (r >= c).astype(BF16)
        below = c < r
        q = q_ref[...] * ATT_SCALE
        qh = [_one_head(_pair(q, h // 2), first, h % 2) for h in range(nh)]

        def tile(kb, carry, diagonal):
            k0 = pl.multiple_of(kb * ATT_TILE, ATT_TILE)
            kt = k_ref[pl.ds(k0, ATT_TILE), :]
            vt = v_ref[pl.ds(k0, ATT_TILE), :]
            z, lf_b, a_b = [None] * nh, [None] * nh, [None] * nh
            out_c, out_acc = [None] * nh, [None] * nh
            for t in range(nh + 2 * FWD_SKEW):
                if t < nh:
                    z[t] = _dot_nt(qh[t], _pair(kt, t // 2))
                    lf = _neg_softplus(z[t])
                    if diagonal:
                        lf = jnp.where(below, lf, 0.0)
                    lf_b[t] = lf.astype(BF16)
                    out_c[t] = carry[t] + jnp.sum(lf, axis=1, keepdims=True)
                u = t - FWD_SKEW
                if 0 <= u < nh:
                    a = jnp.exp(z[u] + _dot(lf_b[u], tri) + carry[u])
                    if diagonal:
                        a = jnp.where(below, a, 0.0)
                    a_b[u] = a.astype(BF16)
                w = t - 2 * FWD_SKEW
                if 0 <= w < nh:
                    out_acc[w] = carry[nh + w] + _dot(a_b[w], _pair(vt, w // 2))
            return tuple(out_c + out_acc)

        init = tuple([jnp.zeros((ATT_TILE, 1), F32)] * nh + [jnp.zeros((ATT_TILE, LANES), F32)] * nh)
        carry = tile(i, init, True)
        carry = lax.fori_loop(1, i + 1, lambda n, cr: tile(i - n, cr, False), carry)
        for p in range(nh // 2):
            o_ref[:, p * LANES:(p + 1) * LANES] = jnp.where(first, carry[nh + 2 * p], carry[nh + 2 * p + 1])

    out = pl.pallas_call(
        body, name=name, grid=(N_ATT_GROUPS, nq),
        in_specs=[pl.BlockSpec((ATT_TILE, ATT_GROUP), lambda j, i: (i, Q_GRP + j)),
                  pl.BlockSpec((s, ATT_GROUP), lambda j, i: (0, K_GRP + j), pipeline_mode=pl.Buffered(1)),
                  pl.BlockSpec((s, ATT_GROUP), lambda j, i: (0, V_GRP + j), pipeline_mode=pl.Buffered(1))]
        + [HBM_SPEC] * n_side,
        out_specs=[pl.BlockSpec((ATT_TILE, ATT_GROUP), lambda j, i: (i, j))] + [HBM_SPEC] * n_side,
        out_shape=[SDS((s, D_SB), F32)] + [SDS((N_DEV,) + sh.shape, sh.dtype) for sh in shards],
        scratch_shapes=_peer_sems(n_side), compiler_params=_params(2),
    )(proj0, proj0, proj0, *shards)
    return out[0], out[1:]


def _attn_bwd(name, proj0, o, do, dws, shards):
    s = proj0.shape[0]
    nq = s // ATT_TILE
    nh = BWD_HEADS_PER_STEP
    ATT_GROUP, N_ATT_GROUPS, Q_GRP, K_GRP, V_GRP = _att_groups(nh)
    n1, n2 = len(dws), len(shards)
    n_side = n1 + n2

    def body(q_ref, k_ref, v_ref, o_ref, do_ref, *rest):
        dq_ref, dk_ref, dv_ref = rest[n_side:n_side + 3]
        dk_acc, dv_acc = rest[2 * n_side + 3:2 * n_side + 5]
        srcs, dsts, sems = rest[:n_side], rest[n_side + 3:2 * n_side + 3], rest[2 * n_side + 5:]
        j = pl.program_id(0)
        i = pl.program_id(1)
        is_first, is_last = (j == 0) & (i == 0), (j == N_ATT_GROUPS - 1) & (i == nq - 1)
        _side_exchange(srcs[:n1] + dsts[:n1] + sems[:3], n1, True, is_first, is_last)
        _side_exchange(srcs[n1:] + dsts[n1:] + sems[3:], n2, False, is_first, is_last)

        @pl.when(i == 0)
        def _():
            dk_acc[...] = jnp.zeros_like(dk_acc)
            dv_acc[...] = jnp.zeros_like(dv_acc)

        r, c, first = _att_consts()
        tri = (r >= c).astype(BF16)
        tri_p = (r <= c).astype(BF16)
        below = c < r
        q = q_ref[...] * ATT_SCALE
        do_b = do_ref[...].astype(BF16)
        do_o = do_b.astype(F32) * o_ref[...]
        qh = [_one_head(_pair(q, h // 2), first, h % 2) for h in range(nh)]
        doh = [_one_head(_pair(do_b, h // 2), first, h % 2) for h in range(nh)]
        dsum = [jnp.sum(_one_head(_pair(do_o, h // 2), first, h % 2), axis=1, keepdims=True) for h in range(nh)]

        def tile(kb, carry, diagonal):
            k0 = pl.multiple_of(kb * ATT_TILE, ATT_TILE)
            kt = k_ref[pl.ds(k0, ATT_TILE), :]
            vt = v_ref[pl.ds(k0, ATT_TILE), :]
            none = lambda: [None] * nh
            z, d_a, sig, lf_b, a_b, g, early, dz = none(), none(), none(), none(), none(), none(), none(), none()
            out_c1, out_c2, out_dq, dk_t, dv_t = none(), none(), none(), none(), none()
            for t in range(nh + 3):
                if t < nh:
                    z[t] = _dot_nt(qh[t], _pair(kt, t // 2))
                    d_a[t] = _dot_nt(doh[t], _pair(vt, t // 2))
                    lf = _neg_softplus(z[t])
                    sig[t] = jnp.exp(z[t] + lf)
                    if diagonal:
                        lf = jnp.where(below, lf, 0.0)
                    lf_b[t] = lf.astype(BF16)
                    out_c1[t] = carry[t] + jnp.sum(lf, axis=1, keepdims=True)
                u = t - 1
                if 0 <= u < nh:
                    a = jnp.exp(z[u] + _dot(lf_b[u], tri) + carry[u])
                    if diagonal:
                        a = jnp.where(below, a, 0.0)
                    a_b[u] = a.astype(BF16)
                    g[u] = a_b[u].astype(F32) * d_a[u]
                    g_sum = jnp.sum(g[u], axis=1, keepdims=True)
                    early[u] = dsum[u] - carry[nh + u] - g_sum
                    out_c2[u] = carry[nh + u] + g_sum
                w = t - 2
                if 0 <= w < nh:
                    upto = _dot(g[w].astype(BF16), tri_p)
                    dv_t[w] = _dot_tn(a_b[w], doh[w])
                    d = g[w] - sig[w] * (early[w] + upto)
                    if diagonal:
                        d = jnp.where(below, d, 0.0)
                    dz[w] = d.astype(BF16)
                y = t - 3
                if 0 <= y < nh:
                    out_dq[y] = carry[2 * nh + y] + _dot(dz[y], _pair(kt, y // 2))
                    dk_t[y] = _dot_tn(dz[y], qh[y])
            for p in range(nh // 2):
                dk_acc[pl.ds(k0, ATT_TILE), p * LANES:(p + 1) * LANES] += dk_t[2 * p] + dk_t[2 * p + 1]
                dv_acc[pl.ds(k0, ATT_TILE), p * LANES:(p + 1) * LANES] += dv_t[2 * p] + dv_t[2 * p + 1]
            return tuple(out_c1 + out_c2 + out_dq)

        init = tuple([jnp.zeros((ATT_TILE, 1), F32)] * (2 * nh) + [jnp.zeros((ATT_TILE, LANES), F32)] * nh)
        carry = tile(i, init, True)
        carry = lax.fori_loop(1, i + 1, lambda n, cr: tile(i - n, cr, False), carry)
        for p in range(nh // 2):
            dq_p = jnp.where(first, carry[2 * nh + 2 * p], carry[2 * nh + 2 * p + 1]) * ATT_SCALE
            dq_ref[:, p * LANES:(p + 1) * LANES] = dq_p.astype(BF16)

        @pl.when(i == nq - 1)
        def _():
            dk_ref[...] = dk_acc[...].astype(BF16)
            dv_ref[...] = dv_acc[...].astype(BF16)

    tile_spec = pl.BlockSpec((ATT_TILE, ATT_GROUP), lambda j, i: (i, j))
    full = pl.BlockSpec((s, ATT_GROUP), lambda j, i: (0, j), pipeline_mode=pl.Buffered(1))
    out = pl.pallas_call(
        body, name=name, grid=(N_ATT_GROUPS, nq),
        in_specs=[pl.BlockSpec((ATT_TILE, ATT_GROUP), lambda j, i: (i, Q_GRP + j)),
                  pl.BlockSpec((s, ATT_GROUP), lambda j, i: (0, K_GRP + j), pipeline_mode=pl.Buffered(1)),
                  pl.BlockSpec((s, ATT_GROUP), lambda j, i: (0, V_GRP + j), pipeline_mode=pl.Buffered(1)),
                  tile_spec, tile_spec] + [HBM_SPEC] * n_side,
        out_specs=[tile_spec, full, full] + [HBM_SPEC] * n_side,
        out_shape=[SDS((s, D_SB), BF16)] * 3 + [SDS(dw.shape, dw.dtype) for dw in dws]
        + [SDS((N_DEV,) + sh.shape, sh.dtype) for sh in shards],
        scratch_shapes=[pltpu.VMEM((s, ATT_GROUP), F32), pltpu.VMEM((s, ATT_GROUP), F32)] + _peer_sems(n1) + _peer_sems(n2),
        compiler_params=_params(2, ATT_BWD_VMEM_LIMIT),
    )(proj0, proj0, proj0, o, do, *dws, *shards)
    return out[0], out[1], out[2], out[3:3 + n1], out[3 + n1:]


GATE0_COL = (D_POOL + 3 * D_SB) // D_INNER


def _gate_fwd0(name, yp_raw, o, proj0, ps):
    s = o.shape[0]

    def body(yp_ref, o_ref, gt_ref, ps_ref, y_ref, yt_ref):
        gt = gt_ref[...].astype(F32)
        sg = gt * _sigmoid(gt)
        y_pool = (yp_ref[...] * ps_ref[...] * sg[:, :D_POOL]).astype(BF16)
        y_sb = (o_ref[...] * sg[:, D_POOL:]).astype(BF16)
        y_ref[:, :D_POOL] = y_pool
        y_ref[:, D_POOL:] = y_sb
        yt_ref[:D_POOL, :] = y_pool.T
        yt_ref[D_POOL:, :] = y_sb.T

    return pl.pallas_call(
        body, name=name, grid=(s // ROW_TILE,),
        in_specs=[_row_spec(), _row_spec(), _row_spec(D_INNER, GATE0_COL), _vec_spec()],
        out_specs=[_row_spec(D_INNER), pl.BlockSpec((D_INNER, ROW_TILE), lambda i: (0, i))],
        out_shape=[SDS((s, D_INNER), BF16), SDS((D_INNER, s), BF16)], compiler_params=_params(1),
    )(yp_raw, o, proj0, ps)


def _dsilu(x):
    sg = _sigmoid(x)
    return sg * (1.0 + x * (1.0 - sg))


def _gate_bwd0(name, dymix, yp_raw, o, proj0, ps):
    s = o.shape[0]

    def body(dy_ref, yp_ref, o_ref, gt_ref, ps_ref, dyp_ref, do_ref, dgt_ref, dps_ref):
        i = pl.program_id(0)

        @pl.when(i == 0)
        def _():
            dps_ref[...] = jnp.zeros_like(dps_ref)

        gt = gt_ref[...].astype(F32)
        dy = dy_ref[...]
        sg = gt * _sigmoid(gt)
        dsg = _dsilu(gt)
        dcat = dy * sg
        yp = yp_ref[...]
        ps_v = ps_ref[...]
        dyp_ref[...] = (dcat[:, :D_POOL] * ps_v).astype(BF16)
        do_ref[...] = dcat[:, D_POOL:]
        dps_ref[...] += jnp.sum(dcat[:, :D_POOL] * yp, axis=0, keepdims=True)
        dgt_ref[:, :D_POOL] = (dy[:, :D_POOL] * (yp * ps_v) * dsg[:, :D_POOL]).astype(BF16)
        dgt_ref[:, D_POOL:] = (dy[:, D_POOL:] * o_ref[...] * dsg[:, D_POOL:]).astype(BF16)

    return pl.pallas_call(
        body, name=name, grid=(s // ROW_TILE,),
        in_specs=[_row_spec(D_INNER), _row_spec(), _row_spec(), _row_spec(D_INNER, GATE0_COL), _vec_spec()],
        out_specs=[_row_spec(), _row_spec(), _row_spec(D_INNER), _vec_spec()],
        out_shape=[SDS((s, D_POOL), BF16), SDS((s, D_SB), F32), SDS((s, D_INNER), BF16), SDS((1, D_POOL), F32)],
        compiler_params=_params(1),
    )(dymix, yp_raw, o, proj0, ps)


CONV_HALO = 16


def _conv_fwd(name, proj1, cw, cb):
    s = proj1.shape[0]
    hb = ROW_TILE // CONV_HALO
    ext_rows = ROW_TILE + CONV_HALO

    def body(gb_ref, gc_ref, u_ref, gt_ref, gch_ref, uh_ref, cw_ref, cb_ref, y_ref, yt_ref):
        i = pl.program_id(0)
        uc = gc_ref[...].astype(F32) * u_ref[...].astype(F32)
        halo = jnp.where(i == 0, 0.0, gch_ref[...].astype(F32) * uh_ref[...].astype(F32))
        ext = jnp.concatenate([halo, uc], axis=0)
        uc1 = pltpu.roll(ext, 1, axis=0)[CONV_HALO:, :]
        uc2 = pltpu.roll(ext, 2, axis=0)[CONV_HALO:, :]
        cw_v = cw_ref[...]
        conv = cb_ref[...] + cw_v[0:1, :] * uc2 + cw_v[1:2, :] * uc1 + cw_v[2:3, :] * uc
        gt = gt_ref[...].astype(F32)
        y = (gb_ref[...].astype(F32) * conv * (gt * _sigmoid(gt))).astype(BF16)
        y_ref[...] = y
        yt_ref[...] = y.T

    def tile(part):
        return pl.BlockSpec((ROW_TILE, D_INNER), lambda i: (i, part))

    def halo(part):
        return pl.BlockSpec((CONV_HALO, D_INNER), lambda i: (jnp.maximum(i * hb - 1, 0), part))

    return pl.pallas_call(
        body, name=name, grid=(s // ROW_TILE,),
        in_specs=[tile(0), tile(1), tile(2), tile(3), halo(1), halo(2),
                  pl.BlockSpec((3, D_INNER), lambda i: (0, 0)), pl.BlockSpec((1, D_INNER), lambda i: (0, 0))],
        out_specs=[pl.BlockSpec((ROW_TILE, D_INNER), lambda i: (i, 0)), pl.BlockSpec((D_INNER, ROW_TILE), lambda i: (0, i))],
        out_shape=[SDS((s, D_INNER), BF16), SDS((D_INNER, s), BF16)], compiler_params=_params(1),
    )(proj1, proj1, proj1, proj1, proj1, proj1, cw, cb)


def _conv_bwd(name, dymix, proj1, cw, cb):
    s = proj1.shape[0]
    hb = ROW_TILE // CONV_HALO
    n_hb = s // CONV_HALO
    n_tiles = s // ROW_TILE
    ext_rows = ROW_TILE + CONV_HALO

    def body(dy_ref, gb_ref, gc_ref, u_ref, gt_ref, gch_ref, uh_ref, dyn_ref, gbn_ref, gtn_ref, cw_ref, cb_ref,
             dproj_ref, dcw_ref, dcb_ref):
        i = pl.program_id(0)

        @pl.when(i == 0)
        def _():
            dcw_ref[...] = jnp.zeros_like(dcw_ref)
            dcb_ref[...] = jnp.zeros_like(dcb_ref)

        gc = gc_ref[...].astype(F32)
        u = u_ref[...].astype(F32)
        gb = gb_ref[...].astype(F32)
        gt = gt_ref[...].astype(F32)
        dy = dy_ref[...]
        uc = gc * u
        halo = jnp.where(i == 0, 0.0, gch_ref[...].astype(F32) * uh_ref[...].astype(F32))
        ext = jnp.concatenate([halo, uc], axis=0)
        uc1 = pltpu.roll(ext, 1, axis=0)[CONV_HALO:, :]
        uc2 = pltpu.roll(ext, 2, axis=0)[CONV_HALO:, :]
        cw_v = cw_ref[...]
        w0, w1, w2 = cw_v[0:1, :], cw_v[1:2, :], cw_v[2:3, :]
        conv = cb_ref[...] + w0 * uc2 + w1 * uc1 + w2 * uc
        sig = _sigmoid(gt)
        sg = gt * sig
        dconv = dy * gb * sg
        gtn = gtn_ref[...].astype(F32)
        dconv_next = jnp.where(i == n_tiles - 1, 0.0, dyn_ref[...] * gbn_ref[...].astype(F32) * (gtn * _sigmoid(gtn)))
        dext = jnp.concatenate([dconv, dconv_next], axis=0)
        dconv_p1 = pltpu.roll(dext, ext_rows - 1, axis=0)[:ROW_TILE, :]
        dconv_p2 = pltpu.roll(dext, ext_rows - 2, axis=0)[:ROW_TILE, :]
        duc = w2 * dconv + w1 * dconv_p1 + w0 * dconv_p2
        dproj_ref[:, 0:D_INNER] = (dy * conv * sg).astype(BF16)
        dproj_ref[:, D_INNER:2 * D_INNER] = (duc * u).astype(BF16)
        dproj_ref[:, 2 * D_INNER:3 * D_INNER] = (duc * gc).astype(BF16)
        dproj_ref[:, 3 * D_INNER:] = (dy * gb * conv * (sig + sg * (1.0 - sig))).astype(BF16)
        dcw_ref[0:1, :] += jnp.sum(dconv * uc2, axis=0, keepdims=True)
        dcw_ref[1:2, :] += jnp.sum(dconv * uc1, axis=0, keepdims=True)
        dcw_ref[2:3, :] += jnp.sum(dconv * uc, axis=0, keepdims=True)
        dcb_ref[...] += jnp.sum(dconv, axis=0, keepdims=True)

    def tile(part):
        return pl.BlockSpec((ROW_TILE, D_INNER), lambda i: (i, part))

    def prev(part):
        return pl.BlockSpec((CONV_HALO, D_INNER), lambda i: (jnp.maximum(i * hb - 1, 0), part))

    def nxt(part):
        return pl.BlockSpec((CONV_HALO, D_INNER), lambda i: (jnp.minimum((i + 1) * hb, n_hb - 1), part))

    whole = lambda rows: pl.BlockSpec((rows, D_INNER), lambda i: (0, 0))
    return pl.pallas_call(
        body, name=name, grid=(n_tiles,),
        in_specs=[tile(0), tile(0), tile(1), tile(2), tile(3), prev(1), prev(2), nxt(0), nxt(0), nxt(3),
                  whole(3), whole(1)],
        out_specs=[pl.BlockSpec((ROW_TILE, 4 * D_INNER), lambda i: (i, 0)), whole(3), whole(1)],
        out_shape=[SDS((s, 4 * D_INNER), BF16), SDS((3, D_INNER), F32), SDS((1, D_INNER), F32)],
        compiler_params=_params(1),
    )(dymix, proj1, proj1, proj1, proj1, proj1, proj1, dymix, proj1, proj1, cw, cb)


def _place():
    x, y, c = lax.axis_index("x"), lax.axis_index("y"), lax.axis_index("c")
    return x, y, c


def _flip(x, y, c, k):
    fx, fy, fc = (k >> 2) & 1, (k >> 1) & 1, k & 1
    return (1 - x if fx else x, 1 - y if fy else y, 1 - c if fc else c)


def _dev_index(p):
    return 4 * p[0] + 2 * p[1] + p[2]


HBM_SPEC = pl.BlockSpec(memory_space=pltpu.HBM)
VMEM_SPEC = pl.BlockSpec(memory_space=pltpu.VMEM)


def _peer_copies(ins, outs, send_sems, recv_sems, local_sems, by_chunk):
    x, y, c = _place()
    my = _dev_index((x, y, c))
    copies = []
    for w in range(len(ins)):
        copies.append(pltpu.make_async_copy(ins[w].at[my] if by_chunk else ins[w], outs[w].at[my], local_sems.at[w]))
        for k in range(1, N_DEV):
            peer = _flip(x, y, c, k)
            copies.append(pltpu.make_async_remote_copy(
                src_ref=ins[w].at[_dev_index(peer)] if by_chunk else ins[w], dst_ref=outs[w].at[my],
                send_sem=send_sems.at[7 * w + k - 1], recv_sem=recv_sems.at[7 * w + k - 1],
                device_id=peer, device_id_type=MESH))
    return copies


def _peer_sems(n_w):
    return [pltpu.SemaphoreType.DMA((7 * n_w,)), pltpu.SemaphoreType.DMA((7 * n_w,)), pltpu.SemaphoreType.DMA((n_w,))]


def _chip_index(p):
    return 2 * p[0] + p[1]


def _chip_copies(ins, outs, send_sems, recv_sems, local_sems):
    x, y, c = _place()
    mine = _chip_index((x, y))
    copies = []
    for w in range(len(ins)):
        copies.append(pltpu.make_async_copy(ins[w].at[mine], outs[w].at[mine], local_sems.at[w]))
        for k in (2, 4, 6):
            peer = _flip(x, y, c, k)
            copies.append(pltpu.make_async_remote_copy(
                src_ref=ins[w].at[_chip_index(peer)], dst_ref=outs[w].at[mine],
                send_sem=send_sems.at[7 * w + k - 1], recv_sem=recv_sems.at[7 * w + k - 1],
                device_id=peer, device_id_type=MESH))
    return copies


def _sibling_exchange(dw):
    n_chips = N_DEV // 2

    def body(dw_ref, out_ref, send_sems, recv_sems):
        x, y, c = _place()
        sibling = (x, y, 1 - c)
        copies = []
        for ch in range(n_chips):
            copies.append(pltpu.make_async_remote_copy(
                src_ref=dw_ref.at[2 * ch + (1 - c)], dst_ref=out_ref.at[ch],
                send_sem=send_sems.at[ch], recv_sem=recv_sems.at[ch], device_id=sibling, device_id_type=MESH))
        for cp in copies:
            cp.start()
        for cp in copies:
            cp.wait()

    return pl.pallas_call(
        body, name="sibling_exchange", out_shape=SDS((n_chips,) + dw.shape[1:], dw.dtype),
        in_specs=[HBM_SPEC], out_specs=HBM_SPEC,
        scratch_shapes=[pltpu.SemaphoreType.DMA((n_chips,)), pltpu.SemaphoreType.DMA((n_chips,))],
    )(dw)


def _sibling_sum(name, dw, got, core):
    n_chips, rows, cols = got.shape
    tr = min(rows, 256)

    def body(core_ref, a_ref, b_ref, o_ref):
        o_ref[...] = (a_ref[...].astype(F32) + b_ref[...].astype(F32)).astype(BF16)

    return pl.pallas_call(
        body, name=name,
        grid_spec=pltpu.PrefetchScalarGridSpec(
            num_scalar_prefetch=1, grid=(n_chips, rows // tr),
            in_specs=[pl.BlockSpec((None, tr, cols), lambda ch, i, core_ref: (2 * ch + core_ref[0], i, 0)),
                      pl.BlockSpec((None, tr, cols), lambda ch, i, core_ref: (ch, i, 0))],
            out_specs=pl.BlockSpec((None, tr, cols), lambda ch, i, core_ref: (ch, i, 0))),
        out_shape=SDS(got.shape, BF16), compiler_params=_params(2),
    )(core, dw, got)


ADA_COLS = 3 * D_MODEL // N_DEV


def _ada_forward(c_row, conv_w, conv_b, ada_w, ada_b):
    cw_cols = conv_w.shape[1]

    def body(c_ref, cw_ref, cb_ref, aw_ref, ab_ref, m_ref, cs_ref, cwf_ref, cbf_ref,
             slab, gath, part, land, send_sems, recv_sems):
        x, y, c = _place()
        my = _dev_index((x, y, c))
        slab[...] = jnp.zeros_like(slab)
        slab[0:1, :] = c_ref[...]
        slab[1:4, 0:cw_cols] = cw_ref[...]
        slab[4:5, 0:cw_cols] = cb_ref[...]
        gath[my] = slab[...]
        sends = []
        for k in range(1, N_DEV):
            peer = _flip(x, y, c, k)
            cp = pltpu.make_async_remote_copy(
                src_ref=slab, dst_ref=gath.at[my], send_sem=send_sems.at[k - 1], recv_sem=recv_sems.at[k - 1],
                device_id=peer, device_id_type=MESH)
            cp.start()
            sends.append(cp)
        for cp in sends:
            cp.wait()
        for d in range(N_DEV):
            c_d = gath[d, 0:1, :]
            cs_ref[d:d + 1, :] = c_d * _sigmoid(c_d)
            cwf_ref[:, d * cw_cols:(d + 1) * cw_cols] = gath[d, 1:4, 0:cw_cols]
            cbf_ref[:, d * cw_cols:(d + 1) * cw_cols] = gath[d, 4:5, 0:cw_cols]
        cs = cs_ref[...]
        part[...] = jnp.zeros_like(part)
        for layer in range(2):
            m_part = jnp.dot(cs, aw_ref[layer], preferred_element_type=F32, precision=lax.Precision.HIGHEST)
            for d in range(N_DEV):
                part[d, layer:layer + 1, :] = m_part[d:d + 1, :]
        land[my] = part[my]
        sends = []
        for k in range(1, N_DEV):
            peer = _flip(x, y, c, k)
            cp = pltpu.make_async_remote_copy(
                src_ref=part.at[_dev_index(peer)], dst_ref=land.at[my],
                send_sem=send_sems.at[6 + k], recv_sem=recv_sems.at[6 + k],
                device_id=peer, device_id_type=MESH)
            cp.start()
            sends.append(cp)
        for cp in sends:
            cp.wait()
        for d in range(N_DEV):
            cols = slice(d * ADA_COLS, (d + 1) * ADA_COLS)
            m_ref[:, cols] = land[d, 0:2, :] + ab_ref[:, cols]

    return pl.pallas_call(
        body, name="ada_forward",
        out_shape=[SDS((2, 3 * D_MODEL), F32), SDS((N_DEV, D_MODEL), F32), SDS((3, N_DEV * cw_cols), F32),
                   SDS((1, N_DEV * cw_cols), F32)],
        in_specs=[VMEM_SPEC] * 5, out_specs=[VMEM_SPEC] * 4,
        scratch_shapes=[pltpu.VMEM((8, D_MODEL), F32), pltpu.VMEM((N_DEV, 8, D_MODEL), F32),
                        pltpu.VMEM((N_DEV, 8, ADA_COLS), F32), pltpu.VMEM((N_DEV, 8, ADA_COLS), F32),
                        pltpu.SemaphoreType.DMA((14,)), pltpu.SemaphoreType.DMA((14,))],
        compiler_params=pltpu.CompilerParams(vmem_limit_bytes=VMEM_LIMIT),
    )(c_row, conv_w, conv_b, ada_w, ada_b)


def _small_grads(slab):
    def body(slab_ref, gath_ref, tot_ref, send_sems, recv_sems):
        x, y, c = _place()
        my = _dev_index((x, y, c))
        gath_ref[my] = slab_ref[...]
        sends = []
        for k in range(1, N_DEV):
            peer = _flip(x, y, c, k)
            cp = pltpu.make_async_remote_copy(
                src_ref=slab_ref, dst_ref=gath_ref.at[my], send_sem=send_sems.at[k - 1], recv_sem=recv_sems.at[k - 1],
                device_id=peer, device_id_type=MESH)
            cp.start()
            sends.append(cp)
        for cp in sends:
            cp.wait()
        tot = gath_ref[0]
        for d in range(1, N_DEV):
            tot = tot + gath_ref[d]
        tot_ref[...] = tot

    return pl.pallas_call(
        body, name="small_grads",
        out_shape=[SDS((N_DEV, SLAB_ROWS, D_MODEL), F32), SDS((SLAB_ROWS, D_MODEL), F32)],
        in_specs=[VMEM_SPEC], out_specs=[VMEM_SPEC] * 2,
        scratch_shapes=[pltpu.SemaphoreType.DMA((7,)), pltpu.SemaphoreType.DMA((7,))],
    )(slab)


def _adamw_math(w, g, m, v):
    m = ADAM_B1 * m + (1.0 - ADAM_B1) * g
    v = ADAM_B2 * v + (1.0 - ADAM_B2) * jnp.square(g)
    m_hat = m / (1.0 - ADAM_B1 ** ADAM_STEP)
    v_hat = v / (1.0 - ADAM_B2 ** ADAM_STEP)
    delta = -ADAM_LR * (m_hat / (jnp.sqrt(v_hat) + ADAM_EPS) + ADAM_WD * w)
    return delta, m, v


def _sum_adamw(name, recv, w, m, v):
    rows, cols = w.shape
    tr = min(rows, 256)
    n_slots = recv.shape[0]

    def body(r_ref, w_ref, m_ref, v_ref, g_ref, d_ref, nm_ref, nv_ref):
        g = r_ref[0].astype(F32)
        for d in range(1, n_slots):
            g = g + r_ref[d].astype(F32)
        g_ref[...] = g
        d_ref[...], nm_ref[...], nv_ref[...] = _adamw_math(w_ref[...], g, m_ref[...], v_ref[...])

    blk = pl.BlockSpec((tr, cols), lambda i: (i, 0))
    return pl.pallas_call(
        body, name=name, grid=(rows // tr,),
        in_specs=[pl.BlockSpec((n_slots, tr, cols), lambda i: (0, i, 0)), blk, blk, blk],
        out_specs=[blk] * 4, out_shape=[SDS((rows, cols), F32)] * 4, compiler_params=_params(1),
    )(recv, w, m, v)


def _ada_w_adamw(name, cs_t, dm_cols, w, m, v):
    def body(cs_ref, dm_ref, w_ref, m_ref, v_ref, g_ref, d_ref, nm_ref, nv_ref):
        cs = cs_ref[...]
        dm = dm_ref[...]
        g = cs[:, 0:1] * dm[0:1, :]
        for b in range(1, N_DEV):
            g = g + cs[:, b:b + 1] * dm[b:b + 1, :]
        g_ref[...] = g
        d_ref[...], nm_ref[...], nv_ref[...] = _adamw_math(w_ref[...], g, m_ref[...], v_ref[...])

    blk = pl.BlockSpec((None, D_MODEL, ADA_COLS), lambda l: (l, 0, 0))
    return pl.pallas_call(
        body, name=name, grid=(2,),
        in_specs=[pl.BlockSpec((D_MODEL, N_DEV), lambda l: (0, 0)),
                  pl.BlockSpec((None, N_DEV, ADA_COLS), lambda l: (l, 0, 0)), blk, blk, blk],
        out_specs=[blk] * 4, out_shape=[SDS((2, D_MODEL, ADA_COLS), F32)] * 4, compiler_params=_params(1),
    )(cs_t, dm_cols, w, m, v)


def _small_adamw(name, triples):
    n = len(triples)

    def body(*refs):
        ins, outs = refs[:4 * n], refs[4 * n:]
        for j in range(n):
            w_ref, g_ref, m_ref, v_ref = ins[4 * j:4 * j + 4]
            d, nm, nv = _adamw_math(w_ref[...], g_ref[...], m_ref[...], v_ref[...])
            outs[3 * j][...] = d
            outs[3 * j + 1][...] = nm
            outs[3 * j + 2][...] = nv

    flat = [a for t in triples for a in t]
    return pl.pallas_call(
        body, name=name,
        out_shape=[SDS(t[0].shape, F32) for t in triples for _ in range(3)],
        in_specs=[VMEM_SPEC] * (4 * n), out_specs=[VMEM_SPEC] * (3 * n),
    )(*flat)


def kernel(x, c, norm_g, ada_w, ada_b, even_w_in, pool_w, pool_scale, even_w_out, odd_w_in, conv_w, conv_b, odd_w_out, final_g, loss_target, m_norm_g, m_ada_w, m_ada_b, m_even_w_in, m_pool_w, m_pool_scale, m_even_w_out, m_odd_w_in, m_conv_w, m_conv_b, m_odd_w_out, m_final_g, v_norm_g, v_ada_w, v_ada_b, v_even_w_in, v_pool_w, v_pool_scale, v_even_w_out, v_odd_w_in, v_conv_w, v_conv_b, v_odd_w_out, v_final_g):
    seq = x.shape[1]
    x0 = x[0]
    target = loss_target[0]
    final_g2 = final_g.reshape(1, D_MODEL)

    w_in_e = even_w_in[0]
    w_out_e = even_w_out[0]
    w_in_o = odd_w_in[0]
    w_out_o = odd_w_out[0]
    w_pool = pool_w[0].reshape(N_GROUPS * 32, POOL_GROUP)
    shards = [w.astype(BF16) for w in (w_in_e, w_out_e, w_in_o, w_out_o, w_pool)]

    m_vec, cs_all, conv_w_full, conv_b_full = _ada_forward(c, conv_w[0], conv_b, ada_w, ada_b)
    shift = [m_vec[l:l + 1, 0:D_MODEL] for l in range(2)]
    scale = [m_vec[l:l + 1, D_MODEL:2 * D_MODEL] for l in range(2)]
    gate = [m_vec[l:l + 1, 2 * D_MODEL:] for l in range(2)]
    ng = [norm_g[l:l + 1] for l in range(2)]

    h0, h0_t = _ln_mod("ln_mod0", x0, ng[0], scale[0], shift[0])
    ax, ay, ac = lax.axis_index("x"), lax.axis_index("y"), lax.axis_index("c")
    chips = ((1 - ax, ay), (ax, 1 - ay), (1 - ax, 1 - ay))
    arrival = [None] * N_DEV
    arrival[0], arrival[1] = (ax, ay, ac), (ax, ay, 1 - ac)
    for dd_direct, dd_passed, j in GATHER_ARRIVALS:
        arrival[dd_direct], arrival[dd_passed] = (*chips[j], ac), (*chips[j], 1 - ac)
    order = jnp.stack([_dev_index(p) for p in arrival]).astype(jnp.int32)
    proj0 = _proj_in_gather("proj_in0", h0, shards[0], order)
    o, (wg_out_e, wg_in_o, wg_out_o, wg_pool) = _attn_fwd("attn_fwd", proj0, shards[1:])
    wf_out_e = wg_out_e.reshape(D_INNER, D_MODEL)
    wf_out_o = wg_out_o.reshape(D_INNER, D_MODEL)
    wf_pool = wg_pool.reshape(N_DEV, N_GROUPS, 32, POOL_GROUP).transpose(1, 0, 2, 3).reshape(N_GROUPS, POOL_GROUP, POOL_GROUP)
    p, yp_raw = _pool_fwd("pool_fwd", proj0, wf_pool)
    ymix0, ymix0_t = _gate_fwd0("gate_fwd0", yp_raw, o, proj0, pool_scale)
    yo0 = _proj_out("proj_out0", ymix0, wf_out_e)

    x1, h1, h1_t = _resid_ln_mod("resid_ln_mod1", x0, yo0, gate[0], ng[1], scale[1], shift[1])
    proj1 = _proj_in("proj_in1", h1, wg_in_o)
    ymix1, ymix1_t = _conv_fwd("conv_fwd", proj1, conv_w_full, conv_b_full)
    yo1 = _proj_out("proj_out1", ymix1, wf_out_o)

    dx2, dyo1, loss_acc, d_final_g, d_gate1 = _final_loss("final_loss", x1, yo1, gate[1], final_g2, target)
    loss = lax.psum(loss_acc[0, 0], ("x", "y", "c"))

    dymix1 = _proj_out_bwd("proj_out1_bwd", dyo1, wf_out_o)
    dw_out_o = _wgrad_out("wgrad_out1", ymix1_t, dyo1)
    dproj1, d_conv_w, d_conv_b = _conv_bwd("conv_bwd", dymix1, proj1, conv_w_full, conv_b_full)
    dw_in_o = _wgrad_in("wgrad_in1", h1_t, [dproj1], D_IN_ODD // N_DEV)
    (dx1, d_shift1, d_scale1, d_ng1, dyo0, d_gate0), _ = _proj_in_bwd_ln(
        "proj_in1_bwd", [dproj1], wg_in_o, x1, dx2, ng[1], scale[1], resid=(yo0, gate[0]), untransposed=True)

    dymix0 = _proj_out_bwd("proj_out0_bwd", dyo0, wf_out_e)
    dw_out_e = _wgrad_out("wgrad_out0", ymix0_t, dyo0)
    dyp, do, dgt0, d_pool_scale = _gate_bwd0("gate_bwd0", dymix0, yp_raw, o, proj0, pool_scale)
    du_pool, dw_pool = _pool_bwd("pool_bwd", dyp, p, wf_pool)
    dw_pool_c = dw_pool.reshape(N_GROUPS, N_DEV, 32, POOL_GROUP).transpose(1, 0, 2, 3).reshape(N_DEV, N_GROUPS * 32, POOL_GROUP).astype(BF16)
    ready = [dw_out_e.reshape(N_DEV, D_INNER // N_DEV, D_MODEL), dw_in_o,
             dw_out_o.reshape(N_DEV, D_INNER // N_DEV, D_MODEL), dw_pool_c]
    dq, dk, dv, (r_out_e, r_in_o, r_out_o, r_pool), (wt_in_e,) = _attn_bwd(
        "attn_bwd", proj0, o, do, ready, [shards[0].T])
    dparts0 = [du_pool, dq, dk, dv, dgt0]
    dw_in_e = _wgrad_in("wgrad_in0", h0_t, dparts0, WGRAD_BLOCK)
    core = lax.axis_index("c").astype(jnp.int32).reshape(1)
    chip_sums = _sibling_sum("sibling_sum", dw_in_e, _sibling_exchange(dw_in_e), core)
    (dx0, d_shift0, d_scale0, d_ng0), (r_in_e,) = _proj_in_bwd_ln(
        "proj_in0_bwd", dparts0, wt_in_e, x0, dx1, ng[0], scale[0], dws=[chip_sums])
    grad_x = dx0[None]

    big = {}
    big["even_w_in"] = _sum_adamw("adamw_even_w_in", r_in_e, w_in_e, m_even_w_in[0], v_even_w_in[0])
    big["even_w_out"] = _sum_adamw("adamw_even_w_out", r_out_e, w_out_e, m_even_w_out[0], v_even_w_out[0])
    big["odd_w_in"] = _sum_adamw("adamw_odd_w_in", r_in_o, w_in_o, m_odd_w_in[0], v_odd_w_in[0])
    big["odd_w_out"] = _sum_adamw("adamw_odd_w_out", r_out_o, w_out_o, m_odd_w_out[0], v_odd_w_out[0])
    big["pool_w"] = _sum_adamw("adamw_pool_w", r_pool, w_pool, m_pool_w[0].reshape(N_GROUPS * 32, POOL_GROUP),
                               v_pool_w[0].reshape(N_GROUPS * 32, POOL_GROUP))
    big = {k: [a.reshape(shape) for a in v] for (k, v), shape in zip(
        big.items(), [even_w_in.shape, even_w_out.shape, odd_w_in.shape, odd_w_out.shape, pool_w.shape])}

    dm = jnp.concatenate([jnp.concatenate([d_shift0, d_scale0, d_gate0], axis=1),
                          jnp.concatenate([d_shift1, d_scale1, d_gate1], axis=1)], axis=0)
    slab = jnp.zeros((SLAB_ROWS, D_MODEL), F32)
    slab = slab.at[0:6].set(dm.reshape(6, D_MODEL))
    slab = slab.at[8:9].set(d_ng0).at[9:10].set(d_ng1).at[10:11].set(d_pool_scale).at[11:12].set(d_final_g)
    slab = slab.at[16:22].set(d_conv_w.reshape(6, D_MODEL)).at[24:26].set(d_conv_b.reshape(2, D_MODEL))
    gathered, total = _small_grads(slab)
    my = 4 * lax.axis_index("x") + 2 * lax.axis_index("y") + lax.axis_index("c")
    g_ada_b = total[0:6].reshape(2, 3 * D_MODEL)
    g_norm_g = total[8:10]
    g_pool_scale = total[10:11]
    g_final_g = total[11:12]
    cw_cols = conv_w.shape[2]
    g_conv_w = lax.dynamic_slice_in_dim(total[16:22].reshape(3, D_INNER), my * cw_cols, cw_cols, axis=1)
    g_conv_b = lax.dynamic_slice_in_dim(total[24:26].reshape(1, D_INNER), my * cw_cols, cw_cols, axis=1)
    dm_all = gathered[:, 0:6, :].reshape(N_DEV, 2, 3 * D_MODEL)
    dm_cols = lax.dynamic_slice_in_dim(dm_all, my * ADA_COLS, ADA_COLS, axis=2).transpose(1, 0, 2)
    ada = _ada_w_adamw("adamw_ada_w", cs_all.T, dm_cols, ada_w, m_ada_w, v_ada_w)

    small = _small_adamw("adamw_small", [
        (norm_g, g_norm_g, m_norm_g, v_norm_g),
        (ada_b, g_ada_b, m_ada_b, v_ada_b),
        (pool_scale, g_pool_scale, m_pool_scale, v_pool_scale),
        (conv_w[0], g_conv_w, m_conv_w[0], v_conv_w[0]),
        (conv_b, g_conv_b, m_conv_b, v_conv_b),
        (final_g2, g_final_g, m_final_g.reshape(1, D_MODEL), v_final_g.reshape(1, D_MODEL)),
    ])
    small = [small[3 * j:3 * j + 3] for j in range(6)]

    grads = {
        "norm_g": g_norm_g, "ada_w": ada[0], "ada_b": g_ada_b, "even_w_in": big["even_w_in"][0],
        "pool_w": big["pool_w"][0], "pool_scale": g_pool_scale, "even_w_out": big["even_w_out"][0],
        "odd_w_in": big["odd_w_in"][0], "conv_w": g_conv_w.reshape(conv_w.shape), "conv_b": g_conv_b,
        "odd_w_out": big["odd_w_out"][0], "final_g": g_final_g.reshape(D_MODEL),
    }
    rest = []
    for idx in range(3):
        rest += [
            small[0][idx], ada[1 + idx], small[1][idx], big["even_w_in"][1 + idx], big["pool_w"][1 + idx],
            small[2][idx], big["even_w_out"][1 + idx], big["odd_w_in"][1 + idx],
            small[3][idx].reshape(conv_w.shape), small[4][idx], big["odd_w_out"][1 + idx],
            small[5][idx].reshape(D_MODEL),
        ]
    order = ["norm_g", "ada_w", "ada_b", "even_w_in", "pool_w", "pool_scale", "even_w_out", "odd_w_in",
             "conv_w", "conv_b", "odd_w_out", "final_g"]
    return (loss, grad_x, *[grads[n] for n in order], *rest)
```
